```python
import jax, jax.numpy as jnp
from jax import lax
import numpy as np

D_MODEL = 2048
BATCH = 8
SEQ = 8192
DEPTH = 1

MEM_LEN = 256
HEAD_DIM = 128
N_CONV_GROUPS = 4
CONV_W = N_CONV_GROUPS * HEAD_DIM
CONV_K = 31
N_FOX_HEADS = 8
FOX_W = N_FOX_HEADS * HEAD_DIM
N_MEM_HEADS = 4
MEM_W = N_MEM_HEADS * HEAD_DIM
MIX_W = CONV_W + FOX_W + MEM_W
Q_BLOCK = 128
EPS = 1e-6
FORGET_BIAS_INIT = 2.0

SPLITS = (
    CONV_W,
    2 * CONV_W,
    3 * CONV_W,
    3 * CONV_W + FOX_W,
    3 * CONV_W + 2 * FOX_W,
    3 * CONV_W + 3 * FOX_W,
    3 * CONV_W + 3 * FOX_W + N_FOX_HEADS,
    3 * CONV_W + 4 * FOX_W + N_FOX_HEADS,
    3 * CONV_W + 4 * FOX_W + N_FOX_HEADS + MEM_W,
)
D_IN = 3 * CONV_W + 4 * FOX_W + N_FOX_HEADS + 2 * MEM_W

kernel_name = "hybrid_conformer_fox_memory_layer"


def rmsnorm(x, g):
    xf = x.astype(jnp.float32)
    y = xf * lax.rsqrt(jnp.mean(xf * xf, axis=-1, keepdims=True) + EPS)
    return (y * g.astype(jnp.float32)).astype(x.dtype)


def layernorm(x, g, b):
    xf = x.astype(jnp.float32)
    mu = jnp.mean(xf, axis=-1, keepdims=True)
    var = jnp.mean(jnp.square(xf - mu), axis=-1, keepdims=True)
    y = (xf - mu) * lax.rsqrt(var + EPS)
    return (y * g.astype(jnp.float32) + b.astype(jnp.float32)).astype(x.dtype)


def causal_depthwise_conv(u, w, b):
    y = lax.conv_general_dilated(
        u, w[:, None, :], window_strides=(1,), padding=[(CONV_K - 1, 0)],
        dimension_numbers=('NWC', 'WIO', 'NWC'), feature_group_count=u.shape[-1])
    return y + b


def fox_attention(q, k, v, logf):
    B, S, H, Dh = q.shape
    nb = S // Q_BLOCK
    cT = jnp.cumsum(logf, axis=1).transpose(0, 2, 1)
    qb = q.reshape(B, nb, Q_BLOCK, H, Dh).transpose(1, 0, 2, 3, 4)
    cb = cT.reshape(B, H, nb, Q_BLOCK).transpose(2, 0, 1, 3)
    k_pos = jnp.arange(S)
    scale = Dh ** -0.5

    def block(args):
        i, qi, ci = args
        s = jnp.einsum('bqhd,bkhd->bhqk', qi, k).astype(jnp.float32) * scale
        bias = ci[..., :, None] - cT[..., None, :]
        q_pos = i * Q_BLOCK + jnp.arange(Q_BLOCK)
        mask = k_pos[None, :] <= q_pos[:, None]
        s = jnp.where(mask, s + bias, -jnp.inf)
        p = jax.nn.softmax(s, axis=-1).astype(v.dtype)
        return jnp.einsum('bhqk,bkhd->bqhd', p, v)

    out = lax.map(block, (jnp.arange(nb), qb, cb))
    return out.transpose(1, 0, 2, 3, 4).reshape(B, S, H * Dh)


def memory_attention(q, mk, mv):
    B, S, H, Dh = q.shape
    s = jnp.einsum('bshd,bmhd->bhsm', q, mk).astype(jnp.float32) * (Dh ** -0.5)
    p = jax.nn.softmax(s, axis=-1).astype(mv.dtype)
    return jnp.einsum('bhsm,bmhd->bshd', p, mv).reshape(B, S, H * Dh)


def _fwd_setup_inputs(seed: int = 0) -> dict:
    key = jax.random.key(seed)
    ks = jax.random.split(key, 16)
    n = jax.random.normal
    f32 = jnp.float32
    return {
        "x": n(ks[0], (BATCH, SEQ, D_MODEL), f32),
        "mem": n(ks[1], (BATCH, MEM_LEN, D_MODEL), f32),
        "norm_g": 1.0 + 0.05 * n(ks[2], (DEPTH, D_MODEL), f32),
        "mem_norm_g": 1.0 + 0.05 * n(ks[3], (DEPTH, D_MODEL), f32),
        "w_in": n(ks[4], (DEPTH, D_MODEL, D_IN), f32) * D_MODEL ** -0.5,
        "b_f": FORGET_BIAS_INIT + 0.1 * n(ks[5], (DEPTH, N_FOX_HEADS), f32),
        "conv_w": n(ks[6], (DEPTH, CONV_K, CONV_W), f32) * CONV_K ** -0.5,
        "conv_b": 0.02 * n(ks[7], (DEPTH, CONV_W), f32),
        "conv_ln_g": 1.0 + 0.05 * n(ks[8], (DEPTH, CONV_W), f32),
        "conv_ln_b": 0.02 * n(ks[9], (DEPTH, CONV_W), f32),
        "w_conv_pw": n(ks[10], (DEPTH, CONV_W, CONV_W), f32) * CONV_W ** -0.5,
        "w_mem_kv": n(ks[11], (DEPTH, D_MODEL, 2 * MEM_W), f32) * D_MODEL ** -0.5,
        "w_out": n(ks[12], (DEPTH, MIX_W, D_MODEL), f32) * MIX_W ** -0.5,
        "final_g": 1.0 + 0.05 * n(ks[13], (D_MODEL,), f32),
    }


def _fwd_reference(x, mem, norm_g, mem_norm_g, w_in, b_f, conv_w, conv_b, conv_ln_g,
              conv_ln_b, w_conv_pw, w_mem_kv, w_out, final_g):
    B, S, _ = x.shape
    for l in range(DEPTH):
        h = rmsnorm(x, norm_g[l])
        proj = h @ w_in[l]
        (cv_a, cv_b, cv_gate, fq, fk, fv, f_logit, fox_gate,
         mq, mem_gate) = jnp.split(proj, SPLITS, axis=-1)

        u = cv_a * jax.nn.sigmoid(cv_b)
        u = causal_depthwise_conv(u, conv_w[l], conv_b[l])
        u = jax.nn.silu(layernorm(u, conv_ln_g[l], conv_ln_b[l]))
        y_conv = (u @ w_conv_pw[l]) * jax.nn.silu(cv_gate)

        logf = jax.nn.log_sigmoid(f_logit.astype(jnp.float32) + b_f[l].astype(jnp.float32))
        shp = (B, S, N_FOX_HEADS, HEAD_DIM)
        y_fox = fox_attention(fq.reshape(shp), fk.reshape(shp), fv.reshape(shp), logf)
        y_fox = y_fox * jax.nn.silu(fox_gate)

        mkv = rmsnorm(mem, mem_norm_g[l]) @ w_mem_kv[l]
        mk, mv = jnp.split(mkv, 2, axis=-1)
        mshp = (B, mem.shape[1], N_MEM_HEADS, HEAD_DIM)
        y_mem = memory_attention(mq.reshape(B, S, N_MEM_HEADS, HEAD_DIM),
                                 mk.reshape(mshp), mv.reshape(mshp))
        y_mem = y_mem * jax.nn.silu(mem_gate)

        y = jnp.concatenate([y_conv, y_fox, y_mem], axis=-1)
        x = x + y @ w_out[l]
    return rmsnorm(x, final_g)


import jax as _jax
import jax.numpy as _jnp

TWIN_FORMAT = 'train_step'
FWD_PARAMS = ['x', 'mem', 'norm_g', 'mem_norm_g', 'w_in', 'b_f', 'conv_w', 'conv_b', 'conv_ln_g', 'conv_ln_b', 'w_conv_pw', 'w_mem_kv', 'w_out', 'final_g']
TWIN_WEIGHTS = ['norm_g', 'mem_norm_g', 'w_in', 'b_f', 'conv_w', 'conv_b', 'conv_ln_g', 'conv_ln_b', 'w_conv_pw', 'w_mem_kv', 'w_out', 'final_g']
TWIN_DIFF_INPUT = 'x'
TWIN_INPUTS = ['x', 'mem', 'norm_g', 'mem_norm_g', 'w_in', 'b_f', 'conv_w', 'conv_b', 'conv_ln_g', 'conv_ln_b', 'w_conv_pw', 'w_mem_kv', 'w_out', 'final_g', 'loss_target', 'm_norm_g', 'm_mem_norm_g', 'm_w_in', 'm_b_f', 'm_conv_w', 'm_conv_b', 'm_conv_ln_g', 'm_conv_ln_b', 'm_w_conv_pw', 'm_w_mem_kv', 'm_w_out', 'm_final_g', 'v_norm_g', 'v_mem_norm_g', 'v_w_in', 'v_b_f', 'v_conv_w', 'v_conv_b', 'v_conv_ln_g', 'v_conv_ln_b', 'v_w_conv_pw', 'v_w_mem_kv', 'v_w_out', 'v_final_g']
TWIN_OUTPUTS = ['loss', 'grad_x', 'grad_norm_g', 'grad_mem_norm_g', 'grad_w_in', 'grad_b_f', 'grad_conv_w', 'grad_conv_b', 'grad_conv_ln_g', 'grad_conv_ln_b', 'grad_w_conv_pw', 'grad_w_mem_kv', 'grad_w_out', 'grad_final_g', 'delta_norm_g', 'delta_mem_norm_g', 'delta_w_in', 'delta_b_f', 'delta_conv_w', 'delta_conv_b', 'delta_conv_ln_g', 'delta_conv_ln_b', 'delta_w_conv_pw', 'delta_w_mem_kv', 'delta_w_out', 'delta_final_g', 'new_m_norm_g', 'new_m_mem_norm_g', 'new_m_w_in', 'new_m_b_f', 'new_m_conv_w', 'new_m_conv_b', 'new_m_conv_ln_g', 'new_m_conv_ln_b', 'new_m_w_conv_pw', 'new_m_w_mem_kv', 'new_m_w_out', 'new_m_final_g', 'new_v_norm_g', 'new_v_mem_norm_g', 'new_v_w_in', 'new_v_b_f', 'new_v_conv_w', 'new_v_conv_b', 'new_v_conv_ln_g', 'new_v_conv_ln_b', 'new_v_w_conv_pw', 'new_v_w_mem_kv', 'new_v_w_out', 'new_v_final_g']
TWIN_LEAF_KINDS = {'loss': 'loss', 'grad_x': 'grad_x', 'grad_norm_g': 'grad_w', 'grad_mem_norm_g': 'grad_w', 'grad_w_in': 'grad_w', 'grad_b_f': 'grad_w', 'grad_conv_w': 'grad_w', 'grad_conv_b': 'grad_w', 'grad_conv_ln_g': 'grad_w', 'grad_conv_ln_b': 'grad_w', 'grad_w_conv_pw': 'grad_w', 'grad_w_mem_kv': 'grad_w', 'grad_w_out': 'grad_w', 'grad_final_g': 'grad_w', 'delta_norm_g': 'delta_w', 'delta_mem_norm_g': 'delta_w', 'delta_w_in': 'delta_w', 'delta_b_f': 'delta_w', 'delta_conv_w': 'delta_w', 'delta_conv_b': 'delta_w', 'delta_conv_ln_g': 'delta_w', 'delta_conv_ln_b': 'delta_w', 'delta_w_conv_pw': 'delta_w', 'delta_w_mem_kv': 'delta_w', 'delta_w_out': 'delta_w', 'delta_final_g': 'delta_w', 'new_m_norm_g': 'new_m', 'new_m_mem_norm_g': 'new_m', 'new_m_w_in': 'new_m', 'new_m_b_f': 'new_m', 'new_m_conv_w': 'new_m', 'new_m_conv_b': 'new_m', 'new_m_conv_ln_g': 'new_m', 'new_m_conv_ln_b': 'new_m', 'new_m_w_conv_pw': 'new_m', 'new_m_w_mem_kv': 'new_m', 'new_m_w_out': 'new_m', 'new_m_final_g': 'new_m', 'new_v_norm_g': 'new_v', 'new_v_mem_norm_g': 'new_v', 'new_v_w_in': 'new_v', 'new_v_b_f': 'new_v', 'new_v_conv_w': 'new_v', 'new_v_conv_b': 'new_v', 'new_v_conv_ln_g': 'new_v', 'new_v_conv_ln_b': 'new_v', 'new_v_w_conv_pw': 'new_v', 'new_v_w_mem_kv': 'new_v', 'new_v_w_out': 'new_v', 'new_v_final_g': 'new_v'}


def _forward(args):
    return _fwd_reference(*[args[k] for k in FWD_PARAMS])


def _output_shape():
    def fwd():
        inp = _fwd_setup_inputs(0)
        return _fwd_reference(*[inp[k] for k in FWD_PARAMS])
    out = _jax.eval_shape(fwd)
    return out.shape, out.dtype

N_MICROBATCH = 1
ADAM_LR = 0.001
ADAM_B1 = 0.9
ADAM_B2 = 0.999
ADAM_EPS = 1e-08
ADAM_WD = 0.01
ADAM_STEP = 10
PER_EXAMPLE_BATCH_AXIS = {'x': 0, 'mem': 0, 'loss_target': 0}
SHARED_INPUTS = []
_WEIGHT_DTYPES = {'norm_g': _jnp.float32, 'mem_norm_g': _jnp.float32, 'w_in': _jnp.float32, 'b_f': _jnp.float32, 'conv_w': _jnp.float32, 'conv_b': _jnp.float32, 'conv_ln_g': _jnp.float32, 'conv_ln_b': _jnp.float32, 'w_conv_pw': _jnp.float32, 'w_mem_kv': _jnp.float32, 'w_out': _jnp.float32, 'final_g': _jnp.float32}
MOMENT_SCALE = {'norm_g': 5.137452e-02, 'mem_norm_g': 5.510249e-03, 'w_in': 2.842761e-02, 'b_f': 1.729840e-01, 'conv_w': 4.580143e-02, 'conv_b': 9.558714e-02, 'conv_ln_g': 5.081318e-02, 'conv_ln_b': 4.471122e-02, 'w_conv_pw': 4.462533e-02, 'w_mem_kv': 7.645988e-03, 'w_out': 3.090789e-02, 'final_g': 3.202770e+01}


def _to_microbatches(a, axis):
    t = _jnp.moveaxis(a, axis, 0)
    t = t.reshape((N_MICROBATCH, t.shape[0] // N_MICROBATCH) + t.shape[1:])
    return _jnp.moveaxis(t, 1, axis + 1)


def setup_inputs(seed: int = 0) -> dict:
    inp = _fwd_setup_inputs(seed)
    key = _jax.random.fold_in(_jax.random.key(seed), 7919)
    shape, _ = _output_shape()
    out = dict(inp)
    out["loss_target"] = _jax.random.normal(_jax.random.fold_in(key, 0), shape, _jnp.float32)
    for i, name in enumerate(TWIN_WEIGHTS):
        w = inp[name].astype(_jnp.float32)
        if MOMENT_SCALE is None:
            s = _jnp.sqrt(_jnp.mean(_jnp.square(w)) + 1e-30)
        else:
            s = MOMENT_SCALE[name]
        km, kv = _jax.random.split(_jax.random.fold_in(key, i + 1))
        out[name] = w
        out["m_" + name] = s * _jax.random.normal(km, w.shape, _jnp.float32)
        out["v_" + name] = (s * s) * _jax.random.uniform(kv, w.shape, _jnp.float32, 0.5, 1.5)
    if N_MICROBATCH > 1:
        for name, axis in PER_EXAMPLE_BATCH_AXIS.items():
            out[name] = _to_microbatches(out[name], axis)
    return {'x': out['x'], 'mem': out['mem'], 'norm_g': out['norm_g'], 'mem_norm_g': out['mem_norm_g'], 'w_in': out['w_in'], 'b_f': out['b_f'], 'conv_w': out['conv_w'], 'conv_b': out['conv_b'], 'conv_ln_g': out['conv_ln_g'], 'conv_ln_b': out['conv_ln_b'], 'w_conv_pw': out['w_conv_pw'], 'w_mem_kv': out['w_mem_kv'], 'w_out': out['w_out'], 'final_g': out['final_g'], 'loss_target': out['loss_target'], 'm_norm_g': out['m_norm_g'], 'm_mem_norm_g': out['m_mem_norm_g'], 'm_w_in': out['m_w_in'], 'm_b_f': out['m_b_f'], 'm_conv_w': out['m_conv_w'], 'm_conv_b': out['m_conv_b'], 'm_conv_ln_g': out['m_conv_ln_g'], 'm_conv_ln_b': out['m_conv_ln_b'], 'm_w_conv_pw': out['m_w_conv_pw'], 'm_w_mem_kv': out['m_w_mem_kv'], 'm_w_out': out['m_w_out'], 'm_final_g': out['m_final_g'], 'v_norm_g': out['v_norm_g'], 'v_mem_norm_g': out['v_mem_norm_g'], 'v_w_in': out['v_w_in'], 'v_b_f': out['v_b_f'], 'v_conv_w': out['v_conv_w'], 'v_conv_b': out['v_conv_b'], 'v_conv_ln_g': out['v_conv_ln_g'], 'v_conv_ln_b': out['v_conv_ln_b'], 'v_w_conv_pw': out['v_w_conv_pw'], 'v_w_mem_kv': out['v_w_mem_kv'], 'v_w_out': out['v_w_out'], 'v_final_g': out['v_final_g']}


def _loss(weights, diff, rest, loss_target):
    with _jax.named_scope("forward"):
        args = {**rest, TWIN_DIFF_INPUT: diff, **{k: w.astype(_WEIGHT_DTYPES[k]) for k, w in weights.items()}}
        y = _forward(args)
    with _jax.named_scope("loss_head"):
        err = _jnp.square(y.astype(_jnp.float32) - loss_target)
        return 0.5 * _jnp.sum(_jnp.mean(err, axis=-1)) if err.ndim else 0.5 * err


def _adamw(w, g, m, v):
    m = ADAM_B1 * m + (1.0 - ADAM_B1) * g
    v = ADAM_B2 * v + (1.0 - ADAM_B2) * _jnp.square(g)
    m_hat = m / (1.0 - ADAM_B1 ** ADAM_STEP)
    v_hat = v / (1.0 - ADAM_B2 ** ADAM_STEP)
    delta = -ADAM_LR * (m_hat / (_jnp.sqrt(v_hat) + ADAM_EPS) + ADAM_WD * w)
    return delta, m, v


def reference(x, mem, norm_g, mem_norm_g, w_in, b_f, conv_w, conv_b, conv_ln_g, conv_ln_b, w_conv_pw, w_mem_kv, w_out, final_g, loss_target, m_norm_g, m_mem_norm_g, m_w_in, m_b_f, m_conv_w, m_conv_b, m_conv_ln_g, m_conv_ln_b, m_w_conv_pw, m_w_mem_kv, m_w_out, m_final_g, v_norm_g, v_mem_norm_g, v_w_in, v_b_f, v_conv_w, v_conv_b, v_conv_ln_g, v_conv_ln_b, v_w_conv_pw, v_w_mem_kv, v_w_out, v_final_g):
    given = dict(x=x, mem=mem, norm_g=norm_g, mem_norm_g=mem_norm_g, w_in=w_in, b_f=b_f, conv_w=conv_w, conv_b=conv_b, conv_ln_g=conv_ln_g, conv_ln_b=conv_ln_b, w_conv_pw=w_conv_pw, w_mem_kv=w_mem_kv, w_out=w_out, final_g=final_g, loss_target=loss_target, m_norm_g=m_norm_g, m_mem_norm_g=m_mem_norm_g, m_w_in=m_w_in, m_b_f=m_b_f, m_conv_w=m_conv_w, m_conv_b=m_conv_b, m_conv_ln_g=m_conv_ln_g, m_conv_ln_b=m_conv_ln_b, m_w_conv_pw=m_w_conv_pw, m_w_mem_kv=m_w_mem_kv, m_w_out=m_w_out, m_final_g=m_final_g, v_norm_g=v_norm_g, v_mem_norm_g=v_mem_norm_g, v_w_in=v_w_in, v_b_f=v_b_f, v_conv_w=v_conv_w, v_conv_b=v_conv_b, v_conv_ln_g=v_conv_ln_g, v_conv_ln_b=v_conv_ln_b, v_w_conv_pw=v_w_conv_pw, v_w_mem_kv=v_w_mem_kv, v_w_out=v_w_out, v_final_g=v_final_g)
    weights = {n: given[n] for n in TWIN_WEIGHTS}
    shared = {n: given[n] for n in SHARED_INPUTS}
    per_example = {n: given[n] for n in ['x', 'mem']}
    grad_fn = _jax.value_and_grad(_loss, argnums=(0, 1))

    def one_microbatch(ex, loss_target):
        ex = dict(ex)
        diff = ex.pop(TWIN_DIFF_INPUT)
        return grad_fn(weights, diff, {**shared, **ex}, loss_target)

    if N_MICROBATCH == 1:
        loss, (grad_w, grad_x) = one_microbatch(per_example, given["loss_target"])
    else:
        def body(carry, xs):
            loss_sum, grad_sum = carry
            l_k, (gw_k, gx_k) = one_microbatch(xs[0], xs[1])
            with _jax.named_scope("update"):
                return (loss_sum + l_k, _jax.tree.map(_jnp.add, grad_sum, gw_k)), gx_k

        init = (_jnp.zeros((), _jnp.float32), _jax.tree.map(_jnp.zeros_like, weights))
        (loss, grad_w), grad_x = _jax.lax.scan(body, init, (per_example, given["loss_target"]))
    with _jax.named_scope("update"):
        delta_w, new_m, new_v = {}, {}, {}
        for n in TWIN_WEIGHTS:
            delta_w[n], new_m[n], new_v[n] = _adamw(weights[n], grad_w[n], given["m_" + n], given["v_" + n])
    return (loss, grad_x, *[grad_w[n] for n in TWIN_WEIGHTS], *[delta_w[n] for n in TWIN_WEIGHTS],
            *[new_m[n] for n in TWIN_WEIGHTS], *[new_v[n] for n in TWIN_WEIGHTS])
```

```python
import functools

import jax
import jax.numpy as jnp
from jax import lax
from jax.experimental import pallas as pl
from jax.experimental.pallas import tpu as pltpu

F32 = jnp.float32
_BF = jnp.bfloat16
SDS = jax.ShapeDtypeStruct
MESH = pl.DeviceIdType.MESH

N_DEV = 8
HEAD = 128
N_FOX = 8
N_MEM = 4
CONV_W = 512
CONV_K = 31
FOX_W = N_FOX * HEAD
MEM_W = N_MEM * HEAD
D_IN = 3 * CONV_W + 4 * FOX_W + N_FOX + 2 * MEM_W
EPS = 1e-6
SCALE = HEAD ** -0.5
NEG = -1e30

ADAM_LR = 0.001
ADAM_B1 = 0.9
ADAM_B2 = 0.999
ADAM_EPS = 1e-08
ADAM_WD = 0.01
ADAM_STEP = 10

PART_W = 3584
F_A, F_B, F_GC, F_FG, F_MG, F_FL = 0, 512, 1024, 1536, 2560, 3072
B_Q, B_K, B_V, B_MQ = 0, 1024, 2048, 3072
O_A, O_B, O_GC = 0, 512, 1024
O_Q, O_K, O_V = 1536, 2560, 3584
O_FL, O_FG, O_MQ, O_MG = 4608, 4616, 5640, 6152

HALO = 32
T_ATT = 512
T_ROW = 256
T_CONV = 512
VMEM_LIMIT = 56 * 1024 * 1024
PACK_W = 2048


def _cp(*sem):
    return pltpu.CompilerParams(dimension_semantics=sem, vmem_limit_bytes=VMEM_LIMIT)


def _sigmoid(x):
    return jax.nn.sigmoid(x)


def _dsilu(x, sg):
    return sg * (1.0 + x * (1.0 - sg))


def _dot(a, b, dims):
    return lax.dot_general(a, b, (dims, ((), ())), preferred_element_type=F32)


NN = ((1,), (0,))
NT = ((1,), (1,))
TN = ((0,), (0,))


def _pick(n, pref):
    if n <= pref:
        return n
    t = pref - pref % 128
    while n % t:
        t -= 128
    return t


def _matmul(a, b, mode, out_dtype, name, tm=1024, tn=1024, tk=512):
    if mode == "nn":
        (M, K), (K2, N) = a.shape, b.shape
    elif mode == "nt":
        (M, K), (N, K2) = a.shape, b.shape
    else:
        (K, M), (K2, N) = a.shape, b.shape
    assert K == K2
    tm, tn, tk = _pick(M, tm), _pick(N, tn), _pick(K, tk)
    assert M % tm == 0 and N % tn == 0 and K % tk == 0, (name, M, N, K)
    nk = K // tk
    dims = {"nn": NN, "nt": NT, "tn": TN}[mode]

    def body(a_ref, b_ref, o_ref, acc_ref):
        k = pl.program_id(2)
        p = _dot(a_ref[...].astype(_BF), b_ref[...].astype(_BF), dims)

        @pl.when(k == 0)
        def _():
            acc_ref[...] = p

        @pl.when(k > 0)
        def _():
            acc_ref[...] += p

        @pl.when(k == nk - 1)
        def _():
            o_ref[...] = acc_ref[...].astype(out_dtype)

    if mode == "nn":
        a_spec = pl.BlockSpec((tm, tk), lambda i, j, k: (i, k))
        b_spec = pl.BlockSpec((tk, tn), lambda i, j, k: (k, j))
    elif mode == "nt":
        a_spec = pl.BlockSpec((tm, tk), lambda i, j, k: (i, k))
        b_spec = pl.BlockSpec((tn, tk), lambda i, j, k: (j, k))
    else:
        a_spec = pl.BlockSpec((tk, tm), lambda i, j, k: (k, i))
        b_spec = pl.BlockSpec((tk, tn), lambda i, j, k: (k, j))
    return pl.pallas_call(
        body, grid=(M // tm, N // tn, nk), in_specs=[a_spec, b_spec],
        out_specs=pl.BlockSpec((tm, tn), lambda i, j, k: (i, j)),
        out_shape=SDS((M, N), out_dtype), scratch_shapes=[pltpu.VMEM((tm, tn), F32)],
        compiler_params=_cp("parallel", "parallel", "arbitrary"), name=name)(a, b)


def _rms_fwd(x, g):
    S, D = x.shape
    tr = min(T_ROW, S)

    def body(x_ref, g_ref, h_ref, r_ref):
        xv = x_ref[...]
        r = lax.rsqrt(jnp.mean(xv * xv, axis=-1, keepdims=True) + EPS)
        h_ref[...] = (xv * r * g_ref[...]).astype(_BF)
        r_ref[...] = r

    return pl.pallas_call(
        body, grid=(S // tr,),
        in_specs=[pl.BlockSpec((tr, D), lambda i: (i, 0)), pl.BlockSpec((1, D), lambda i: (0, 0))],
        out_specs=[pl.BlockSpec((tr, D), lambda i: (i, 0)), pl.BlockSpec((tr, 1), lambda i: (i, 0))],
        out_shape=[SDS((S, D), _BF), SDS((S, 1), F32)],
        compiler_params=_cp("parallel"), name="rms_fwd")(x, g)


def _head_loss(x, z, target, g):
    S, D = x.shape
    tr = min(T_ROW, S)

    def body(x_ref, z_ref, t_ref, g_ref, dx2_ref, gg_ref, loss_ref):
        i = pl.program_id(0)
        x2 = x_ref[...] + z_ref[...]
        r = lax.rsqrt(jnp.mean(x2 * x2, axis=-1, keepdims=True) + EPS)
        xh = x2 * r
        gv = g_ref[...]
        diff = xh * gv - t_ref[...]
        lsum = 0.5 * jnp.sum(jnp.mean(diff * diff, axis=-1, keepdims=True), axis=0, keepdims=True)
        dout = diff * (1.0 / D)
        gd = dout * gv
        dx2_ref[...] = r * (gd - xh * jnp.mean(gd * xh, axis=-1, keepdims=True))
        gg = jnp.sum(dout * xh, axis=0, keepdims=True)

        @pl.when(i == 0)
        def _():
            gg_ref[...] = gg
            loss_ref[...] = jnp.broadcast_to(lsum, (1, 128))

        @pl.when(i > 0)
        def _():
            gg_ref[...] += gg
            loss_ref[...] += jnp.broadcast_to(lsum, (1, 128))

    row = pl.BlockSpec((tr, D), lambda i: (i, 0))
    return pl.pallas_call(
        body, grid=(S // tr,), in_specs=[row, row, row, pl.BlockSpec((1, D), lambda i: (0, 0))],
        out_specs=[row, pl.BlockSpec((1, D), lambda i: (0, 0)), pl.BlockSpec((1, 128), lambda i: (0, 0))],
        out_shape=[SDS((S, D), F32), SDS((1, D), F32), SDS((1, 128), F32)],
        compiler_params=_cp("arbitrary"), name="head_loss")(x, z, target, g)


def _rms_bwd(x, r, dh, dx2, g):
    S, D = x.shape
    tr = min(T_ROW, S)

    def body(x_ref, r_ref, dh_ref, dx2_ref, g_ref, gx_ref, gg_ref):
        i = pl.program_id(0)
        rv = r_ref[...]
        xh = x_ref[...] * rv
        dh_ = dh_ref[...]
        gd = dh_ * g_ref[...]
        gx_ref[...] = dx2_ref[...] + rv * (gd - xh * jnp.mean(gd * xh, axis=-1, keepdims=True))
        gg = jnp.sum(dh_ * xh, axis=0, keepdims=True)

        @pl.when(i == 0)
        def _():
            gg_ref[...] = gg

        @pl.when(i > 0)
        def _():
            gg_ref[...] += gg

    row = pl.BlockSpec((tr, D), lambda i: (i, 0))
    return pl.pallas_call(
        body, grid=(S // tr,),
        in_specs=[row, pl.BlockSpec((tr, 1), lambda i: (i, 0)), row, row, pl.BlockSpec((1, D), lambda i: (0, 0))],
        out_specs=[row, pl.BlockSpec((1, D), lambda i: (0, 0))],
        out_shape=[SDS((S, D), F32), SDS((1, D), F32)],
        compiler_params=_cp("arbitrary"), name="rms_bwd")(x, r, dh, dx2, g)


def _mem_kv_fwd(mem, g, w_kv):
    M, D = mem.shape

    def body(mem_ref, g_ref, w_ref, mkv_ref, mhat_ref):
        mv = mem_ref[...]
        mhat = mv * lax.rsqrt(jnp.mean(mv * mv, axis=-1, keepdims=True) + EPS)
        mhat_ref[...] = mhat
        mkv_ref[...] = _dot((mhat * g_ref[...]).astype(_BF), w_ref[...], NN).astype(_BF)

    return pl.pallas_call(
        body, out_shape=[SDS((M, 2 * MEM_W), _BF), SDS((M, D), F32)],
        compiler_params=pltpu.CompilerParams(vmem_limit_bytes=VMEM_LIMIT), name="mem_kv_fwd")(mem, g, w_kv)


def _mem_kv_bwd(dmkv, mhat, g, w_kv):
    M, D = mhat.shape

    def body(d_ref, mhat_ref, g_ref, w_ref, gw_ref, gg_ref):
        d = d_ref[...].astype(_BF)
        mhat = mhat_ref[...]
        gw_ref[...] = _dot((mhat * g_ref[...]).astype(_BF), d, TN).astype(_BF)
        dmh = _dot(d, w_ref[...], NT)
        gg_ref[...] = jnp.sum(dmh * mhat, axis=0, keepdims=True)

    return pl.pallas_call(
        body, out_shape=[SDS((D, 2 * MEM_W), _BF), SDS((1, D), F32)],
        compiler_params=pltpu.CompilerParams(vmem_limit_bytes=VMEM_LIMIT), name="mem_kv_bwd")(dmkv, mhat, g, w_kv)


def _mem_attn_fwd(p_b, p_f, mkv):
    S = p_b.shape[0]
    M = mkv.shape[0]
    tq = min(T_ATT, S)
    qb, gb = B_MQ // HEAD, F_MG // HEAD

    def body(q_ref, k_ref, v_ref, g_ref, y_ref):
        s = _dot(q_ref[...], k_ref[...], NT) * SCALE
        m = jnp.max(s, axis=-1, keepdims=True)
        e = jnp.exp(s - m)
        p = e / jnp.sum(e, axis=-1, keepdims=True)
        o = _dot(p.astype(_BF), v_ref[...], NN)
        gv = g_ref[...]
        y_ref[...] = (o * (gv * _sigmoid(gv))).astype(_BF)

    return pl.pallas_call(
        body, grid=(S // tq, N_MEM),
        in_specs=[pl.BlockSpec((tq, HEAD), lambda i, h: (i, qb + h)),
                  pl.BlockSpec((M, HEAD), lambda i, h: (0, h)),
                  pl.BlockSpec((M, HEAD), lambda i, h: (0, N_MEM + h)),
                  pl.BlockSpec((tq, HEAD), lambda i, h: (i, gb + h))],
        out_specs=pl.BlockSpec((tq, HEAD), lambda i, h: (i, h)),
        out_shape=SDS((S, MEM_W), _BF), compiler_params=_cp("parallel", "parallel"), name="mem_attn_fwd")(p_b, mkv, mkv, p_f)


def _mem_attn_bwd(p_b, p_f, mkv, dy):
    S = p_b.shape[0]
    M = mkv.shape[0]
    tq = min(T_ATT, S)
    qb, gb, yb = B_MQ // HEAD, F_MG // HEAD, (CONV_W + FOX_W) // HEAD

    def body(q_ref, k_ref, v_ref, g_ref, dy_ref, dq_ref, dg_ref, dk_ref, dv_ref):
        i = pl.program_id(1)
        q, k, v = q_ref[...], k_ref[...], v_ref[...]
        s = _dot(q, k, NT) * SCALE
        m = jnp.max(s, axis=-1, keepdims=True)
        e = jnp.exp(s - m)
        p = e / jnp.sum(e, axis=-1, keepdims=True)
        pb = p.astype(_BF)
        o = _dot(pb, v, NN)
        gv = g_ref[...]
        sg = _sigmoid(gv)
        dyv = dy_ref[...]
        do = dyv * (gv * sg)
        dg_ref[...] = (dyv * o * _dsilu(gv, sg)).astype(_BF)
        dob = do.astype(_BF)
        dp = _dot(dob, v, NT)
        ds = p * (dp - jnp.sum(dp * p, axis=-1, keepdims=True)) * SCALE
        dsb = ds.astype(_BF)
        dq_ref[...] = _dot(dsb, k, NN).astype(_BF)
        dk = _dot(dsb, q, TN)
        dv = _dot(pb, dob, TN)

        @pl.when(i == 0)
        def _():
            dk_ref[...] = dk
            dv_ref[...] = dv

        @pl.when(i > 0)
        def _():
            dk_ref[...] += dk
            dv_ref[...] += dv

    tile = pl.BlockSpec((tq, HEAD), lambda h, i: (i, h))
    kv = pl.BlockSpec((M, HEAD), lambda h, i: (0, h))
    return pl.pallas_call(
        body, grid=(N_MEM, S // tq),
        in_specs=[pl.BlockSpec((tq, HEAD), lambda h, i: (i, qb + h)), kv,
                  pl.BlockSpec((M, HEAD), lambda h, i: (0, N_MEM + h)),
                  pl.BlockSpec((tq, HEAD), lambda h, i: (i, gb + h)),
                  pl.BlockSpec((tq, HEAD), lambda h, i: (i, yb + h))],
        out_specs=[tile, tile, kv, kv],
        out_shape=[SDS((S, MEM_W), _BF), SDS((S, MEM_W), _BF), SDS((M, MEM_W), F32), SDS((M, MEM_W), F32)],
        compiler_params=_cp("parallel", "arbitrary"), name="mem_attn_bwd")(p_b, mkv, mkv, p_f, dy)


def _fox_cumsum(p_f, b_f_pad):
    S = p_f.shape[0]
    tr = min(256, S)
    fb = F_FL // 128

    def body(z_ref, b_ref, c_ref, carry_ref):
        i = pl.program_id(0)

        @pl.when(i == 0)
        def _():
            carry_ref[...] = jnp.zeros_like(carry_ref)

        z = z_ref[...] + b_ref[...]
        lf = jnp.minimum(z, 0.0) - jnp.log1p(jnp.exp(-jnp.abs(z)))
        r = lax.broadcasted_iota(jnp.int32, (tr, tr), 0)
        c = lax.broadcasted_iota(jnp.int32, (tr, tr), 1)
        tri = (c <= r).astype(F32)
        cs = jnp.dot(tri, lf, precision=lax.Precision.HIGHEST, preferred_element_type=F32) + carry_ref[...]
        c_ref[...] = cs
        carry_ref[...] = cs[tr - 1:tr, :]

    return pl.pallas_call(
        body, grid=(S // tr,),
        in_specs=[pl.BlockSpec((tr, 128), lambda i: (i, fb)), pl.BlockSpec((1, 128), lambda i: (0, 0))],
        out_specs=pl.BlockSpec((tr, 128), lambda i: (i, 0)),
        out_shape=SDS((S, 128), F32), scratch_shapes=[pltpu.VMEM((1, 128), F32)],
        compiler_params=_cp("arbitrary"), name="fox_cumsum")(p_f, b_f_pad)


def _fox_dlogf(dc_key, dc_query, p_f, b_f_pad):
    S = p_f.shape[0]
    tr = min(256, S)
    nb = S // tr
    fb = F_FL // 128
    wide = PART_W - F_FL

    def body(dc_ref, dq_ref, z_ref, b_ref, dz_ref, gb_ref, carry_ref):
        i = pl.program_id(0)

        @pl.when(i == 0)
        def _():
            carry_ref[...] = jnp.zeros_like(carry_ref)
            gb_ref[...] = jnp.zeros_like(gb_ref)

        dc = dc_ref[...] + dq_ref[...]
        r = lax.broadcasted_iota(jnp.int32, (tr, tr), 0)
        c = lax.broadcasted_iota(jnp.int32, (tr, tr), 1)
        tri = (c >= r).astype(F32)
        rc = jnp.dot(tri, dc, precision=lax.Precision.HIGHEST, preferred_element_type=F32) + carry_ref[...]
        carry_ref[...] = rc[0:1, :]
        z = z_ref[...] + b_ref[...]
        dz = rc * _sigmoid(-z)
        gb_ref[...] += jnp.sum(dz, axis=0, keepdims=True)
        dz_ref[...] = jnp.concatenate([dz.astype(_BF), jnp.zeros((tr, wide - 128), _BF)], axis=1)

    return pl.pallas_call(
        body, grid=(nb,),
        in_specs=[pl.BlockSpec((tr, 128), lambda i: (nb - 1 - i, 0)), pl.BlockSpec((tr, 128), lambda i: (nb - 1 - i, 0)),
                  pl.BlockSpec((tr, 128), lambda i: (nb - 1 - i, fb)), pl.BlockSpec((1, 128), lambda i: (0, 0))],
        out_specs=[pl.BlockSpec((tr, wide), lambda i: (nb - 1 - i, 0)), pl.BlockSpec((1, 128), lambda i: (0, 0))],
        out_shape=[SDS((S, wide), _BF), SDS((1, 128), F32)], scratch_shapes=[pltpu.VMEM((1, 128), F32)],
        compiler_params=_cp("arbitrary"), name="fox_dlogf")(dc_key, dc_query, p_f, b_f_pad)


def _fox_fwd(p_b, p_f, c_col, c_row):
    S = p_b.shape[0]
    t = min(T_ATT, S)
    nq = S // t
    qb, kb, vb, gb = B_Q // HEAD, B_K // HEAD, B_V // HEAD, F_FG // HEAD

    def body(q_ref, k_ref, v_ref, cc_ref, cr_ref, g_ref, o_ref, y_ref, lse_ref, m_ref, l_ref, acc_ref):
        i = pl.program_id(1)
        q = q_ref[...]
        cc = cc_ref[...]
        m_ref[...] = jnp.full_like(m_ref, NEG)
        l_ref[...] = jnp.zeros_like(l_ref)
        acc_ref[...] = jnp.zeros_like(acc_ref)

        def block(j, masked):
            off = pl.multiple_of(j * t, t)
            k = k_ref[pl.ds(off, t), :]
            v = v_ref[pl.ds(off, t), :]
            s = _dot(q, k, NT) * SCALE + (cc - cr_ref[:, pl.ds(off, t)])
            if masked:
                row = lax.broadcasted_iota(jnp.int32, (t, t), 0)
                col = lax.broadcasted_iota(jnp.int32, (t, t), 1)
                s = jnp.where(col <= row, s, NEG)
            m_old = m_ref[...]
            m_new = jnp.maximum(m_old, jnp.max(s, axis=-1, keepdims=True))
            alpha = jnp.exp(m_old - m_new)
            p = jnp.exp(s - m_new)
            l_ref[...] = alpha * l_ref[...] + jnp.sum(p, axis=-1, keepdims=True)
            acc_ref[...] = alpha * acc_ref[...] + _dot(p.astype(_BF), v, NN)
            m_ref[...] = m_new

        def step(j, carry):
            block(j, False)
            return carry

        lax.fori_loop(0, i, step, 0)
        block(i, True)
        l = l_ref[...]
        o = acc_ref[...] / l
        gv = g_ref[...]
        o_ref[...] = o.astype(_BF)
        y_ref[...] = (o * (gv * _sigmoid(gv))).astype(_BF)
        lse_ref[...] = m_ref[...] + jnp.log(l)

    tile = pl.BlockSpec((t, HEAD), lambda h, i: (i, h))
    return pl.pallas_call(
        body, grid=(N_FOX, nq),
        in_specs=[pl.BlockSpec((t, HEAD), lambda h, i: (i, qb + h)),
                  pl.BlockSpec((S, HEAD), lambda h, i: (0, kb + h)),
                  pl.BlockSpec((S, HEAD), lambda h, i: (0, vb + h)),
                  pl.BlockSpec((None, t, 1), lambda h, i: (h, i, 0)),
                  pl.BlockSpec((None, 1, S), lambda h, i: (h, 0, 0)),
                  pl.BlockSpec((t, HEAD), lambda h, i: (i, gb + h))],
        out_specs=[tile, tile, pl.BlockSpec((None, t, 1), lambda h, i: (h, i, 0))],
        out_shape=[SDS((S, FOX_W), _BF), SDS((S, FOX_W), _BF), SDS((N_FOX, S, 1), F32)],
        scratch_shapes=[pltpu.VMEM((t, 1), F32), pltpu.VMEM((t, 1), F32), pltpu.VMEM((t, HEAD), F32)],
        compiler_params=_cp("parallel", "parallel"), name="fox_fwd")(p_b, p_b, p_b, c_col, c_row, p_f)


def _fox_bwd_prep(dy, p_f, o, lse, c_col):
    S = dy.shape[0]
    t = min(T_ATT, S)
    yb, gb = CONV_W // HEAD, F_FG // HEAD

    def body(dy_ref, g_ref, o_ref, lse_ref, cc_ref, do_ref, dg_ref, rows_ref):
        gv = g_ref[...]
        sg = _sigmoid(gv)
        dyv = dy_ref[...]
        ov = o_ref[...].astype(F32)
        do = dyv * (gv * sg)
        do_ref[...] = do.astype(_BF)
        dg_ref[...] = (dyv * ov * _dsilu(gv, sg)).astype(_BF)
        delta = jnp.sum(do * ov, axis=-1, keepdims=True)
        a = cc_ref[...] - lse_ref[...]
        lane = lax.broadcasted_iota(jnp.int32, (t, 128), 1)
        mat = jnp.where(lane == 0, a, jnp.where(lane == 1, delta, 0.0))
        rows_ref[...] = mat.T[0:8, :]

    tile = pl.BlockSpec((t, HEAD), lambda h, i: (i, h))
    col = pl.BlockSpec((None, t, 1), lambda h, i: (h, i, 0))
    return pl.pallas_call(
        body, grid=(N_FOX, S // t),
        in_specs=[pl.BlockSpec((t, HEAD), lambda h, i: (i, yb + h)), pl.BlockSpec((t, HEAD), lambda h, i: (i, gb + h)),
                  tile, col, col],
        out_specs=[tile, tile, pl.BlockSpec((None, 8, t), lambda h, i: (h, 0, i))],
        out_shape=[SDS((S, FOX_W), _BF), SDS((S, FOX_W), _BF), SDS((N_FOX, 8, S), F32)],
        compiler_params=_cp("parallel", "parallel"), name="fox_bwd_prep")(dy, p_f, o, lse, c_col)


def _fox_bwd(p_b, do, rows, c_col):
    S = p_b.shape[0]
    t = min(T_ATT, S)
    nk = S // t
    qb, kb, vb = B_Q // HEAD, B_K // HEAD, B_V // HEAD

    def body(k_ref, v_ref, q_ref, do_ref, rows_ref, cc_ref, dq_ref, dk_ref, dv_ref, dc_ref, dr_ref,
             dqt_ref, dka_ref, dva_ref, dca_ref, dra_ref):
        j = pl.program_id(1)

        @pl.when(j == 0)
        def _():
            dqt_ref[...] = jnp.zeros_like(dqt_ref)
            dra_ref[...] = jnp.zeros_like(dra_ref)

        k = k_ref[...]
        v = v_ref[...]
        kt = k.astype(F32).T.astype(_BF)
        cc = cc_ref[...]
        dka_ref[...] = jnp.zeros_like(dka_ref)
        dva_ref[...] = jnp.zeros_like(dva_ref)
        dca_ref[...] = jnp.zeros_like(dca_ref)

        def block(i, masked):
            off = pl.multiple_of(i * t, t)
            q = q_ref[pl.ds(off, t), :]
            dov = do_ref[pl.ds(off, t), :]
            a_row = rows_ref[0:1, pl.ds(off, t)]
            delta_row = rows_ref[1:2, pl.ds(off, t)]
            st = _dot(k, q, NT) * SCALE + (a_row - cc)
            if masked:
                srow = lax.broadcasted_iota(jnp.int32, (t, t), 0)
                tcol = lax.broadcasted_iota(jnp.int32, (t, t), 1)
                st = jnp.where(srow <= tcol, st, NEG)
            pt = jnp.exp(st)
            dva_ref[...] += _dot(pt.astype(_BF), dov, NN)
            dpt = _dot(v, dov, NT)
            dst = pt * (dpt - delta_row)
            part = dst[:, 0:128]
            for gidx in range(1, t // 128):
                part = part + dst[:, gidx * 128:(gidx + 1) * 128]
            dca_ref[...] += part
            dra_ref[0:1, pl.ds(off, t)] += jnp.sum(dst, axis=0, keepdims=True)
            dsb = dst.astype(_BF)
            dka_ref[...] += _dot(dsb, q, NN)
            dqt_ref[:, pl.ds(off, t)] += _dot(kt, dsb, NN)

        block(j, True)

        def step(i, carry):
            block(i, False)
            return carry

        lax.fori_loop(j + 1, nk, step, 0)
        dk_ref[...] = (dka_ref[...] * SCALE).astype(_BF)
        dv_ref[...] = dva_ref[...].astype(_BF)
        dc_ref[...] = -jnp.sum(dca_ref[...], axis=-1, keepdims=True)

        @pl.when(j == nk - 1)
        def _():
            dr_ref[...] = dra_ref[...]
            for ci in range(nk):
                dq_ref[ci * t:(ci + 1) * t, :] = (dqt_ref[:, ci * t:(ci + 1) * t].T * SCALE).astype(_BF)

    tile = pl.BlockSpec((t, HEAD), lambda h, j: (j, h))
    return pl.pallas_call(
        body, grid=(N_FOX, nk),
        in_specs=[pl.BlockSpec((t, HEAD), lambda h, j: (j, kb + h)),
                  pl.BlockSpec((t, HEAD), lambda h, j: (j, vb + h)),
                  pl.BlockSpec((S, HEAD), lambda h, j: (0, qb + h)),
                  pl.BlockSpec((S, HEAD), lambda h, j: (0, h)),
                  pl.BlockSpec((None, 8, S), lambda h, j: (h, 0, 0)),
                  pl.BlockSpec((None, t, 1), lambda h, j: (h, j, 0))],
        out_specs=[pl.BlockSpec((S, HEAD), lambda h, j: (0, h)), tile, tile,
                   pl.BlockSpec((None, t, 1), lambda h, j: (h, j, 0)), pl.BlockSpec((None, 8, S), lambda h, j: (h, 0, 0))],
        out_shape=[SDS((S, FOX_W), _BF), SDS((S, FOX_W), _BF), SDS((S, FOX_W), _BF), SDS((N_FOX, S, 1), F32),
                   SDS((N_FOX, 8, S), F32)],
        scratch_shapes=[pltpu.VMEM((HEAD, S), F32), pltpu.VMEM((t, HEAD), F32), pltpu.VMEM((t, HEAD), F32),
                        pltpu.VMEM((t, 128), F32), pltpu.VMEM((8, S), F32)],
        compiler_params=_cp("parallel", "arbitrary"), name="fox_bwd")(p_b, p_b, p_b, do, rows, c_col)


CHUNK = 32


def _conv_taps(ext_ref, w_ref, first, out_fn, n_rows):
    def chunk(c, carry):
        r0 = pl.multiple_of(c * CHUNK, CHUNK)
        win = ext_ref[pl.ds(r0, 2 * CHUNK), :]
        acc = jnp.zeros((CHUNK, CONV_W), F32)
        for k in range(CONV_K):
            f = first(k)
            acc = acc + w_ref[k:k + 1, :] * win[f:f + CHUNK, :]
        out_fn(r0, acc)
        return carry

    lax.fori_loop(0, n_rows // CHUNK, chunk, 0)


def _conv_fwd(p_f, conv_w, conv_b, ln_g, ln_b, w_pw):
    S = p_f.shape[0]
    tc = min(T_CONV, S)
    hb = tc // HALO

    def body(a_ref, b_ref, gc_ref, ap_ref, bp_ref, w_ref, cb_ref, lg_ref, lb_ref, pw_ref, y_ref, u1_ref, ext_ref):
        i = pl.program_id(0)
        prev = ap_ref[...] * _sigmoid(bp_ref[...])
        ext_ref[0:HALO, :] = jnp.where(i > 0, prev, 0.0)
        ext_ref[HALO:HALO + tc, :] = a_ref[...] * _sigmoid(b_ref[...])
        cb = cb_ref[...]

        def put(r0, acc):
            u1_ref[pl.ds(r0, CHUNK), :] = acc + cb

        _conv_taps(ext_ref, w_ref, lambda k: HALO - (CONV_K - 1) + k, put, tc)
        u1 = u1_ref[...]
        mu = jnp.mean(u1, axis=-1, keepdims=True)
        d = u1 - mu
        rstd = lax.rsqrt(jnp.mean(d * d, axis=-1, keepdims=True) + EPS)
        u2 = d * rstd * lg_ref[...] + lb_ref[...]
        u3 = u2 * _sigmoid(u2)
        pw = _dot(u3.astype(_BF), pw_ref[...], NN)
        gc = gc_ref[...]
        y_ref[...] = (pw * (gc * _sigmoid(gc))).astype(_BF)

    blk = lambda cb_: pl.BlockSpec((tc, CONV_W), lambda i: (i, cb_))
    halo = lambda cb_: pl.BlockSpec((HALO, CONV_W), lambda i: (jnp.maximum(i * hb - 1, 0), cb_))
    vec = pl.BlockSpec((1, CONV_W), lambda i: (0, 0))
    return pl.pallas_call(
        body, grid=(S // tc,),
        in_specs=[blk(0), blk(1), blk(2), halo(0), halo(1), pl.BlockSpec((CONV_K, CONV_W), lambda i: (0, 0)),
                  vec, vec, vec, pl.BlockSpec((CONV_W, CONV_W), lambda i: (0, 0))],
        out_specs=[pl.BlockSpec((tc, CONV_W), lambda i: (i, 0)), pl.BlockSpec((tc, CONV_W), lambda i: (i, 0))],
        out_shape=[SDS((S, CONV_W), _BF), SDS((S, CONV_W), F32)],
        scratch_shapes=[pltpu.VMEM((tc + 2 * HALO, CONV_W), F32)],
        compiler_params=_cp("parallel"), name="conv_fwd")(p_f, p_f, p_f, p_f, p_f, conv_w, conv_b, ln_g, ln_b, w_pw)


def _conv_bwd1(u1, p_f, dy, ln_g, ln_b, w_pw):
    S = u1.shape[0]
    tc = min(T_CONV, S)

    def body(u1_ref, gc_ref, dy_ref, lg_ref, lb_ref, pw_ref, du1_ref, dgc_ref, gpw_ref, glg_ref, glb_ref, gcb_ref):
        i = pl.program_id(0)
        u1v = u1_ref[...]
        mu = jnp.mean(u1v, axis=-1, keepdims=True)
        d = u1v - mu
        rstd = lax.rsqrt(jnp.mean(d * d, axis=-1, keepdims=True) + EPS)
        xh = d * rstd
        lg = lg_ref[...]
        u2 = xh * lg + lb_ref[...]
        sg2 = _sigmoid(u2)
        u3b = (u2 * sg2).astype(_BF)
        w = pw_ref[...]
        pw = _dot(u3b, w, NN)
        gc = gc_ref[...]
        sgc = _sigmoid(gc)
        dyv = dy_ref[...]
        dpw = (dyv * (gc * sgc)).astype(_BF)
        dgc_ref[...] = (dyv * pw * _dsilu(gc, sgc)).astype(_BF)
        gpw = _dot(u3b, dpw, TN)
        du2 = _dot(dpw, w, NT) * _dsilu(u2, sg2)
        glg = jnp.sum(du2 * xh, axis=0, keepdims=True)
        glb = jnp.sum(du2, axis=0, keepdims=True)
        dxh = du2 * lg
        du1 = rstd * (dxh - jnp.mean(dxh, axis=-1, keepdims=True) - xh * jnp.mean(dxh * xh, axis=-1, keepdims=True))
        du1_ref[...] = du1
        gcb = jnp.sum(du1, axis=0, keepdims=True)

        @pl.when(i == 0)
        def _():
            gpw_ref[...] = gpw
            glg_ref[...] = glg
            glb_ref[...] = glb
            gcb_ref[...] = gcb

        @pl.when(i > 0)
        def _():
            gpw_ref[...] += gpw
            glg_ref[...] += glg
            glb_ref[...] += glb
            gcb_ref[...] += gcb

    row = pl.BlockSpec((tc, CONV_W), lambda i: (i, 0))
    vec = pl.BlockSpec((1, CONV_W), lambda i: (0, 0))
    sq = pl.BlockSpec((CONV_W, CONV_W), lambda i: (0, 0))
    return pl.pallas_call(
        body, grid=(S // tc,),
        in_specs=[row, pl.BlockSpec((tc, CONV_W), lambda i: (i, F_GC // CONV_W)), row, vec, vec, sq],
        out_specs=[row, row, sq, vec, vec, vec],
        out_shape=[SDS((S, CONV_W), F32), SDS((S, CONV_W), _BF), SDS((CONV_W, CONV_W), F32),
                   SDS((1, CONV_W), F32), SDS((1, CONV_W), F32), SDS((1, CONV_W), F32)],
        compiler_params=_cp("arbitrary"), name="conv_bwd1")(u1, p_f, dy, ln_g, ln_b, w_pw)


def _conv_bwd2(du1, p_f, conv_w):
    S = du1.shape[0]
    tc = min(T_CONV, S)
    hb = tc // HALO
    nblk = S // tc
    last_halo = S // HALO - 1

    def body(d_ref, dn_ref, a_ref, b_ref, ap_ref, bp_ref, w_ref, da_ref, db_ref, gw_ref, ext_ref, dext_ref, du0_ref):
        i = pl.program_id(0)
        av = a_ref[...]
        sb = _sigmoid(b_ref[...])
        prev = ap_ref[...] * _sigmoid(bp_ref[...])
        ext_ref[0:HALO, :] = jnp.where(i > 0, prev, 0.0)
        ext_ref[HALO:HALO + tc, :] = av * sb
        dext_ref[0:tc, :] = d_ref[...]
        dext_ref[tc:tc + HALO, :] = jnp.where(i < nblk - 1, dn_ref[...], 0.0)

        def put(r0, acc):
            du0_ref[pl.ds(r0, CHUNK), :] = acc

        _conv_taps(dext_ref, w_ref, lambda k: CONV_K - 1 - k, put, tc)
        du0 = du0_ref[...]
        da_ref[...] = (du0 * sb).astype(_BF)
        db_ref[...] = (du0 * av * sb * (1.0 - sb)).astype(_BF)

        def chunk(c, carry):
            r0 = pl.multiple_of(c * CHUNK, CHUNK)
            win = ext_ref[pl.ds(r0, 2 * CHUNK), :]
            dv = dext_ref[pl.ds(r0, CHUNK), :]
            rows = [jnp.sum(dv * win[HALO - (CONV_K - 1) + k:HALO - (CONV_K - 1) + k + CHUNK, :], axis=0, keepdims=True)
                    for k in range(CONV_K)]
            rows.append(jnp.zeros((1, CONV_W), F32))
            return carry + jnp.concatenate(rows, axis=0)

        gw = lax.fori_loop(0, tc // CHUNK, chunk, jnp.zeros((CONV_K + 1, CONV_W), F32))

        @pl.when(i == 0)
        def _():
            gw_ref[...] = gw

        @pl.when(i > 0)
        def _():
            gw_ref[...] += gw

    row = pl.BlockSpec((tc, CONV_W), lambda i: (i, 0))
    blk = lambda cb_: pl.BlockSpec((tc, CONV_W), lambda i: (i, cb_))
    halo = lambda cb_: pl.BlockSpec((HALO, CONV_W), lambda i: (jnp.maximum(i * hb - 1, 0), cb_))
    nxt = pl.BlockSpec((HALO, CONV_W), lambda i: (jnp.minimum((i + 1) * hb, last_halo), 0))
    return pl.pallas_call(
        body, grid=(nblk,),
        in_specs=[row, nxt, blk(0), blk(1), halo(0), halo(1), pl.BlockSpec((CONV_K, CONV_W), lambda i: (0, 0))],
        out_specs=[row, row, pl.BlockSpec((CONV_K + 1, CONV_W), lambda i: (0, 0))],
        out_shape=[SDS((S, CONV_W), _BF), SDS((S, CONV_W), _BF), SDS((CONV_K + 1, CONV_W), F32)],
        scratch_shapes=[pltpu.VMEM((tc + 2 * HALO, CONV_W), F32), pltpu.VMEM((tc + 2 * HALO, CONV_W), F32),
                        pltpu.VMEM((tc, CONV_W), F32)],
        compiler_params=_cp("arbitrary"), name="conv_bwd2")(du1, du1, p_f, p_f, p_f, p_f, conv_w)


def _exchange(srcs, scatter, name):
    n = len(srcs)
    out_shape = [SDS((N_DEV,) + (s.shape[1:] if sc else s.shape), s.dtype) for s, sc in zip(srcs, scatter)]

    def body(*refs):
        src_refs, dst_refs = refs[:n], refs[n:2 * n]
        send_sems, recv_sems, local_sems = refs[2 * n:]
        x, y, c = lax.axis_index("x"), lax.axis_index("y"), lax.axis_index("c")
        me = 4 * x + 2 * y + c
        copies = []
        for a in range(n):
            for f in range(1, N_DEV):
                px = 1 - x if f & 4 else x
                py = 1 - y if f & 2 else y
                pc = 1 - c if f & 1 else c
                peer = 4 * px + 2 * py + pc
                src = src_refs[a].at[peer] if scatter[a] else src_refs[a]
                cp = pltpu.make_async_remote_copy(
                    src_ref=src, dst_ref=dst_refs[a].at[me], send_sem=send_sems.at[a, f - 1],
                    recv_sem=recv_sems.at[a, f - 1], device_id=(px, py, pc), device_id_type=MESH)
                cp.start()
                copies.append(cp)
            own = src_refs[a].at[me] if scatter[a] else src_refs[a]
            lc = pltpu.make_async_copy(own, dst_refs[a].at[me], local_sems.at[a])
            lc.start()
            copies.append(lc)
        for cp in copies:
            cp.wait()

    anyspec = pl.BlockSpec(memory_space=pl.ANY)
    return pl.pallas_call(
        body, in_specs=[anyspec] * n, out_specs=[anyspec] * n, out_shape=out_shape,
        scratch_shapes=[pltpu.SemaphoreType.DMA((n, N_DEV - 1)), pltpu.SemaphoreType.DMA((n, N_DEV - 1)),
                        pltpu.SemaphoreType.DMA((n,))],
        name=name)(*srcs)


def _adamw(parts, w, m, v, name, tr=256):
    R, C = w.shape
    tr = tr if R % tr == 0 else R

    def body(p_ref, w_ref, m_ref, v_ref, g_ref, d_ref, nm_ref, nv_ref):
        g = p_ref[0].astype(F32)
        for dev in range(1, N_DEV):
            g = g + p_ref[dev].astype(F32)
        mn = ADAM_B1 * m_ref[...] + (1.0 - ADAM_B1) * g
        vn = ADAM_B2 * v_ref[...] + (1.0 - ADAM_B2) * (g * g)
        m_hat = mn / (1.0 - ADAM_B1 ** ADAM_STEP)
        v_hat = vn / (1.0 - ADAM_B2 ** ADAM_STEP)
        g_ref[...] = g
        d_ref[...] = -ADAM_LR * (m_hat / (jnp.sqrt(v_hat) + ADAM_EPS) + ADAM_WD * w_ref[...])
        nm_ref[...] = mn
        nv_ref[...] = vn

    blk = pl.BlockSpec((tr, C), lambda i: (i, 0))
    return pl.pallas_call(
        body, grid=(R // tr,), in_specs=[pl.BlockSpec((N_DEV, tr, C), lambda i: (0, i, 0)), blk, blk, blk],
        out_specs=[blk] * 4, out_shape=[SDS((R, C), F32)] * 4,
        compiler_params=_cp("parallel"), name=name)(parts, w, m, v)


def _split_w_in(w_cat, D):
    z = jnp.zeros((D, PART_W - F_FL - N_FOX), w_cat.dtype)
    w_f = jnp.concatenate([w_cat[:, O_A:O_Q], w_cat[:, O_FG:O_MQ], w_cat[:, O_MG:D_IN], w_cat[:, O_FL:O_FG], z], axis=1)
    w_b = jnp.concatenate([w_cat[:, O_Q:O_FL], w_cat[:, O_MQ:O_MG]], axis=1)
    return w_f, w_b


def _merge_w_in(g_f, g_b):
    return jnp.concatenate([g_f[:, F_A:F_FG], g_b[:, B_Q:B_MQ], g_f[:, F_FL:F_FL + N_FOX], g_f[:, F_FG:F_MG],
                            g_b[:, B_MQ:PART_W], g_f[:, F_MG:F_FL]], axis=1)


def _pack_small(norm, mem_norm, final, conv_b, ln_g, ln_b, b_f, extra, D):
    width = max(D, PACK_W)
    row3 = jnp.concatenate([conv_b.reshape(-1), ln_g.reshape(-1), ln_b.reshape(-1), b_f.reshape(-1)])
    rows = [norm.reshape(-1), mem_norm.reshape(-1), final.reshape(-1), row3, extra.reshape(-1)]
    rows = [jnp.pad(r, (0, width - r.shape[0])) for r in rows]
    return jnp.concatenate([jnp.stack(rows), jnp.zeros((3, width), F32)], axis=0)


def _unpack_small(p, D):
    c = CONV_W
    return dict(norm_g=p[0:1, :D], mem_norm_g=p[1:2, :D], final_g=p[2, :D], conv_b=p[3:4, 0:c],
                conv_ln_g=p[3:4, c:2 * c], conv_ln_b=p[3:4, 2 * c:3 * c], b_f=p[3:4, 3 * c:3 * c + N_FOX])


def kernel(x, mem, norm_g, mem_norm_g, w_in, b_f, conv_w, conv_b, conv_ln_g, conv_ln_b, w_conv_pw, w_mem_kv, w_out, final_g, loss_target, m_norm_g, m_mem_norm_g, m_w_in, m_b_f, m_conv_w, m_conv_b, m_conv_ln_g, m_conv_ln_b, m_w_conv_pw, m_w_mem_kv, m_w_out, m_final_g, v_norm_g, v_mem_norm_g, v_w_in, v_b_f, v_conv_w, v_conv_b, v_conv_ln_g, v_conv_ln_b, v_w_conv_pw, v_w_mem_kv, v_w_out, v_final_g):
    S, D = x.shape[1], x.shape[2]
    M = mem.shape[1]
    xs, ms, tgt = x[0], mem[0], loss_target[0]
    cols = w_in.shape[2]

    g_in, g_cw, g_pw, g_kv, g_out = _exchange(
        [w_in[0].astype(_BF), conv_w[0], w_conv_pw[0].astype(_BF), w_mem_kv[0].astype(_BF), w_out[0].astype(_BF)],
        [False] * 5, "gather_weights")
    w_f, w_b = _split_w_in(jnp.transpose(g_in, (1, 0, 2)).reshape(D, N_DEV * cols), D)
    conv_w_full = jnp.transpose(g_cw, (1, 0, 2)).reshape(CONV_K, CONV_W)
    w_pw_full = g_pw.reshape(CONV_W, CONV_W)
    w_kv_full = g_kv.reshape(D, 2 * MEM_W)
    w_out_full = g_out.reshape(CONV_W + FOX_W + MEM_W, D)
    b_f_pad = jnp.pad(b_f, ((0, 0), (0, 128 - N_FOX)))

    h, r1 = _rms_fwd(xs, norm_g)
    p_f = _matmul(h, w_f, "nn", F32, "proj_f")
    p_b = _matmul(h, w_b, "nn", _BF, "proj_b")
    mkv, mhat = _mem_kv_fwd(ms, mem_norm_g, w_kv_full)
    y_conv, u1 = _conv_fwd(p_f, conv_w_full, conv_b, conv_ln_g, conv_ln_b, w_pw_full)
    c_tok = _fox_cumsum(p_f, b_f_pad)
    c_heads = jnp.transpose(c_tok[:, :N_FOX])
    c_col, c_row = c_heads.reshape(N_FOX, S, 1), c_heads.reshape(N_FOX, 1, S)
    o_fox, y_fox, lse = _fox_fwd(p_b, p_f, c_col, c_row)
    y_mem = _mem_attn_fwd(p_b, p_f, mkv)
    y = jnp.concatenate([y_conv, y_fox, y_mem], axis=1)
    z = _matmul(y, w_out_full, "nn", F32, "out_proj")
    dx2, g_final, loss_part = _head_loss(xs, z, tgt, final_g.reshape(1, D))

    dy = _matmul(dx2, w_out_full, "nt", F32, "d_y")
    gw_out = _matmul(y, dx2, "tn", _BF, "gw_out")
    d_mq, d_mg, d_mk, d_mv = _mem_attn_bwd(p_b, p_f, mkv, dy)
    gw_kv, g_mem_norm = _mem_kv_bwd(jnp.concatenate([d_mk, d_mv], axis=1), mhat, mem_norm_g, w_kv_full)
    d_o, d_fg, rows = _fox_bwd_prep(dy, p_f, o_fox, lse, c_col)
    d_q, d_k, d_v, dc, dr = _fox_bwd(p_b, d_o, rows, c_col)
    to_tok = lambda a: jnp.pad(jnp.transpose(a), ((0, 0), (0, 128 - N_FOX)))
    d_fl, g_bf = _fox_dlogf(to_tok(dc.reshape(N_FOX, S)), to_tok(dr[:, 0, :]), p_f, b_f_pad)
    du1, d_gc, gw_pw, g_ln_g, g_ln_b, g_cb = _conv_bwd1(u1, p_f, dy, conv_ln_g, conv_ln_b, w_pw_full)
    d_a, d_b, gw_cw = _conv_bwd2(du1, p_f, conv_w_full)
    dp = jnp.concatenate([d_a, d_b, d_gc, d_fg, d_mg, d_fl, d_q, d_k, d_v, d_mq], axis=1)
    w_all = jnp.concatenate([w_f, w_b], axis=1)
    gw_all = _matmul(h, dp, "tn", _BF, "gw_in")
    dh = _matmul(dp, w_all, "nt", F32, "d_h")
    grad_x, g_norm = _rms_bwd(xs, r1, dh, dx2, norm_g)

    gw_in_cat = _merge_w_in(gw_all[:, :PART_W], gw_all[:, PART_W:])
    send_in = jnp.transpose(gw_in_cat.reshape(D, N_DEV, cols), (1, 0, 2))
    send_cw = jnp.transpose(gw_cw[:CONV_K].reshape(CONV_K, N_DEV, CONV_W // N_DEV), (1, 0, 2))
    send_pw = gw_pw.astype(_BF).reshape(N_DEV, CONV_W // N_DEV, CONV_W)
    send_kv = gw_kv.reshape(N_DEV, D // N_DEV, 2 * MEM_W)
    send_out = gw_out.reshape(N_DEV, (CONV_W + FOX_W + MEM_W) // N_DEV, D)
    small = _pack_small(g_norm, g_mem_norm, g_final, g_cb, g_ln_g, g_ln_b, g_bf[:, :N_FOX], loss_part[0, 0:1], D)
    r_in, r_cw, r_pw, r_kv, r_out, r_small = _exchange(
        [send_in, send_cw, send_pw, send_kv, send_out, small], [True] * 5 + [False], "exchange_grads")

    res = {}
    res["w_in"] = _adamw(r_in, w_in[0], m_w_in[0], v_w_in[0], "adamw_w_in")
    res["conv_w"] = _adamw(r_cw, conv_w[0], m_conv_w[0], v_conv_w[0], "adamw_conv_w")
    res["w_conv_pw"] = _adamw(r_pw, w_conv_pw[0], m_w_conv_pw[0], v_w_conv_pw[0], "adamw_w_pw")
    res["w_mem_kv"] = _adamw(r_kv, w_mem_kv[0], m_w_mem_kv[0], v_w_mem_kv[0], "adamw_w_kv")
    res["w_out"] = _adamw(r_out, w_out[0], m_w_out[0], v_w_out[0], "adamw_w_out")
    zero = jnp.zeros((1,), F32)
    pk = lambda a, b_, c_, d_, e, f_, g_: _pack_small(a, b_, c_, d_, e, f_, g_, zero, D)
    sm = _adamw(r_small,
                pk(norm_g, mem_norm_g, final_g, conv_b, conv_ln_g, conv_ln_b, b_f),
                pk(m_norm_g, m_mem_norm_g, m_final_g, m_conv_b, m_conv_ln_g, m_conv_ln_b, m_b_f),
                pk(v_norm_g, v_mem_norm_g, v_final_g, v_conv_b, v_conv_ln_g, v_conv_ln_b, v_b_f), "adamw_small")
    loss = sm[0][4, 0]
    small_out = [_unpack_small(a, D) for a in sm]
    names = ["norm_g", "mem_norm_g", "w_in", "b_f", "conv_w", "conv_b", "conv_ln_g", "conv_ln_b", "w_conv_pw",
             "w_mem_kv", "w_out", "final_g"]
    outs = [loss, grad_x[None]]
    for kind in range(4):
        for nme in names:
            outs.append(res[nme][kind][None] if nme in res else small_out[kind][nme])
    return tuple(outs)
```

```python
import functools

import jax
import jax.numpy as jnp
from jax import lax
from jax.experimental import pallas as pl
from jax.experimental.pallas import tpu as pltpu

F32 = jnp.float32
_BF = jnp.bfloat16
SDS = jax.ShapeDtypeStruct
MESH = pl.DeviceIdType.MESH

N_DEV = 8
HEAD = 128
N_FOX = 8
N_MEM = 4
CONV_W = 512
CONV_K = 31
FOX_W = N_FOX * HEAD
MEM_W = N_MEM * HEAD
D_IN = 3 * CONV_W + 4 * FOX_W + N_FOX + 2 * MEM_W
EPS = 1e-6
SCALE = HEAD ** -0.5
NEG = -1e30

ADAM_LR = 0.001
ADAM_B1 = 0.9
ADAM_B2 = 0.999
ADAM_EPS = 1e-08
ADAM_WD = 0.01
ADAM_STEP = 10

PART_W = 3584
F_A, F_B, F_GC, F_FG, F_MG, F_FL = 0, 512, 1024, 1536, 2560, 3072
B_Q, B_K, B_V, B_MQ = 0, 1024, 2048, 3072
O_A, O_B, O_GC = 0, 512, 1024
O_Q, O_K, O_V = 1536, 2560, 3584
O_FL, O_FG, O_MQ, O_MG = 4608, 4616, 5640, 6152

HALO = 32
T_ATT = 512
T_ROW = 256
T_CONV = 512
VMEM_LIMIT = 56 * 1024 * 1024
PACK_W = 2048


def _cp(*sem):
    return pltpu.CompilerParams(dimension_semantics=sem, vmem_limit_bytes=VMEM_LIMIT)


def _sigmoid(x):
    return jax.nn.sigmoid(x)


def _dsilu(x, sg):
    return sg * (1.0 + x * (1.0 - sg))


def _dot(a, b, dims):
    return lax.dot_general(a, b, (dims, ((), ())), preferred_element_type=F32)


NN = ((1,), (0,))
NT = ((1,), (1,))
TN = ((0,), (0,))


def _pick(n, pref):
    if n <= pref:
        return n
    t = pref - pref % 128
    while n % t:
        t -= 128
    return t


def _matmul(a, b, mode, out_dtype, name, tm=512, tn=1024, tk=2048, after=None):
    if mode == "nn":
        (M, K), (K2, N) = a.shape, b.shape
    elif mode == "nt":
        (M, K), (N, K2) = a.shape, b.shape
    else:
        (K, M), (K2, N) = a.shape, b.shape
    assert K == K2
    tm, tn, tk = _pick(M, tm), _pick(N, tn), _pick(K, tk)
    assert M % tm == 0 and N % tn == 0 and K % tk == 0, (name, M, N, K)
    nk = K // tk
    dims = {"nn": NN, "nt": NT, "tn": TN}[mode]
    n_in = 2 if after is None else 3

    def body(*refs):
        a_ref, b_ref = refs[0], refs[1]
        o_ref = refs[n_in]
        p = _dot(a_ref[...].astype(_BF), b_ref[...].astype(_BF), dims)
        if nk == 1:
            o_ref[...] = p.astype(out_dtype)
            return
        acc_ref = refs[n_in + 1]
        k = pl.program_id(2)

        @pl.when(k == 0)
        def _():
            acc_ref[...] = p

        @pl.when(jnp.logical_and(k > 0, k < nk - 1))
        def _():
            acc_ref[...] += p

        @pl.when(k == nk - 1)
        def _():
            o_ref[...] = (acc_ref[...] + p).astype(out_dtype)

    if mode == "nn":
        a_spec = pl.BlockSpec((tm, tk), lambda i, j, k: (i, k))
        b_spec = pl.BlockSpec((tk, tn), lambda i, j, k: (k, j))
    elif mode == "nt":
        a_spec = pl.BlockSpec((tm, tk), lambda i, j, k: (i, k))
        b_spec = pl.BlockSpec((tn, tk), lambda i, j, k: (j, k))
    else:
        a_spec = pl.BlockSpec((tk, tm), lambda i, j, k: (k, i))
        b_spec = pl.BlockSpec((tk, tn), lambda i, j, k: (k, j))
    in_specs, args = [a_spec, b_spec], [a, b]
    if after is not None:
        in_specs.append(pl.BlockSpec(memory_space=pl.ANY))
        args.append(after)
    return pl.pallas_call(
        body, grid=(M // tm, N // tn, nk), in_specs=in_specs,
        out_specs=pl.BlockSpec((tm, tn), lambda i, j, k: (i, j)),
        out_shape=SDS((M, N), out_dtype), scratch_shapes=[] if nk == 1 else [pltpu.VMEM((tm, tn), F32)],
        compiler_params=_cp("parallel", "parallel", "arbitrary"), name=name)(*args)


def _rms_fwd(x, g):
    S, D = x.shape
    tr = min(T_ROW, S)

    def body(x_ref, g_ref, h_ref, r_ref):
        xv = x_ref[...]
        r = lax.rsqrt(jnp.mean(xv * xv, axis=-1, keepdims=True) + EPS)
        h_ref[...] = (xv * r * g_ref[...]).astype(_BF)
        r_ref[...] = r

    return pl.pallas_call(
        body, grid=(S // tr,),
        in_specs=[pl.BlockSpec((tr, D), lambda i: (i, 0)), pl.BlockSpec((1, D), lambda i: (0, 0))],
        out_specs=[pl.BlockSpec((tr, D), lambda i: (i, 0)), pl.BlockSpec((tr, 1), lambda i: (i, 0))],
        out_shape=[SDS((S, D), _BF), SDS((S, 1), F32)],
        compiler_params=_cp("parallel"), name="rms_fwd")(x, g)


def _head_loss(x, z, target, g):
    S, D = x.shape
    tr = min(T_ROW, S)

    def body(x_ref, z_ref, t_ref, g_ref, dx2_ref, gg_ref, loss_ref):
        i = pl.program_id(0)
        x2 = x_ref[...] + z_ref[...]
        r = lax.rsqrt(jnp.mean(x2 * x2, axis=-1, keepdims=True) + EPS)
        xh = x2 * r
        gv = g_ref[...]
        diff = xh * gv - t_ref[...]
        lsum = 0.5 * jnp.sum(jnp.mean(diff * diff, axis=-1, keepdims=True), axis=0, keepdims=True)
        dout = diff * (1.0 / D)
        gd = dout * gv
        dx2_ref[...] = r * (gd - xh * jnp.mean(gd * xh, axis=-1, keepdims=True))
        gg = jnp.sum(dout * xh, axis=0, keepdims=True)

        @pl.when(i == 0)
        def _():
            gg_ref[...] = gg
            loss_ref[...] = jnp.broadcast_to(lsum, (1, 128))

        @pl.when(i > 0)
        def _():
            gg_ref[...] += gg
            loss_ref[...] += jnp.broadcast_to(lsum, (1, 128))

    row = pl.BlockSpec((tr, D), lambda i: (i, 0))
    return pl.pallas_call(
        body, grid=(S // tr,), in_specs=[row, row, row, pl.BlockSpec((1, D), lambda i: (0, 0))],
        out_specs=[row, pl.BlockSpec((1, D), lambda i: (0, 0)), pl.BlockSpec((1, 128), lambda i: (0, 0))],
        out_shape=[SDS((S, D), F32), SDS((1, D), F32), SDS((1, 128), F32)],
        compiler_params=_cp("arbitrary"), name="head_loss")(x, z, target, g)


def _rms_bwd(x, r, dh, dx2, g):
    S, D = x.shape
    tr = min(T_ROW, S)

    def body(x_ref, r_ref, dh_ref, dx2_ref, g_ref, gx_ref, gg_ref):
        i = pl.program_id(0)
        rv = r_ref[...]
        xh = x_ref[...] * rv
        dh_ = dh_ref[...]
        gd = dh_ * g_ref[...]
        gx_ref[...] = dx2_ref[...] + rv * (gd - xh * jnp.mean(gd * xh, axis=-1, keepdims=True))
        gg = jnp.sum(dh_ * xh, axis=0, keepdims=True)

        @pl.when(i == 0)
        def _():
            gg_ref[...] = gg

        @pl.when(i > 0)
        def _():
            gg_ref[...] += gg

    row = pl.BlockSpec((tr, D), lambda i: (i, 0))
    return pl.pallas_call(
        body, grid=(S // tr,),
        in_specs=[row, pl.BlockSpec((tr, 1), lambda i: (i, 0)), row, row, pl.BlockSpec((1, D), lambda i: (0, 0))],
        out_specs=[row, pl.BlockSpec((1, D), lambda i: (0, 0))],
        out_shape=[SDS((S, D), F32), SDS((1, D), F32)],
        compiler_params=_cp("arbitrary"), name="rms_bwd")(x, r, dh, dx2, g)


def _mem_kv_fwd(mem, g, w_kv):
    M, D = mem.shape

    def body(mem_ref, g_ref, w_ref, mkv_ref, mhat_ref):
        mv = mem_ref[...]
        mhat = mv * lax.rsqrt(jnp.mean(mv * mv, axis=-1, keepdims=True) + EPS)
        mhat_ref[...] = mhat
        mkv_ref[...] = _dot((mhat * g_ref[...]).astype(_BF), w_ref[...], NN).astype(_BF)

    return pl.pallas_call(
        body, out_shape=[SDS((M, 2 * MEM_W), _BF), SDS((M, D), F32)],
        compiler_params=pltpu.CompilerParams(vmem_limit_bytes=VMEM_LIMIT), name="mem_kv_fwd")(mem, g, w_kv)


def _mem_kv_bwd(dmkv, mhat, g, w_kv):
    M, D = mhat.shape

    def body(d_ref, mhat_ref, g_ref, w_ref, gw_ref, gg_ref):
        d = d_ref[...].astype(_BF)
        mhat = mhat_ref[...]
        gw_ref[...] = _dot((mhat * g_ref[...]).astype(_BF), d, TN).astype(_BF)
        dmh = _dot(d, w_ref[...], NT)
        gg_ref[...] = jnp.sum(dmh * mhat, axis=0, keepdims=True)

    return pl.pallas_call(
        body, out_shape=[SDS((D, 2 * MEM_W), _BF), SDS((1, D), F32)],
        compiler_params=pltpu.CompilerParams(vmem_limit_bytes=VMEM_LIMIT), name="mem_kv_bwd")(dmkv, mhat, g, w_kv)


def _mem_attn_fwd(p_b, p_f, mkv):
    S = p_b.shape[0]
    M = mkv.shape[0]
    tq = min(T_ATT, S)
    qb, gb = B_MQ // HEAD, F_MG // HEAD

    def body(q_ref, k_ref, v_ref, g_ref, y_ref):
        s = _dot(q_ref[...], k_ref[...], NT) * SCALE
        m = jnp.max(s, axis=-1, keepdims=True)
        e = jnp.exp(s - m)
        p = e / jnp.sum(e, axis=-1, keepdims=True)
        o = _dot(p.astype(_BF), v_ref[...], NN)
        gv = g_ref[...]
        y_ref[...] = (o * (gv * _sigmoid(gv))).astype(_BF)

    return pl.pallas_call(
        body, grid=(S // tq, N_MEM),
        in_specs=[pl.BlockSpec((tq, HEAD), lambda i, h: (i, qb + h)),
                  pl.BlockSpec((M, HEAD), lambda i, h: (0, h)),
                  pl.BlockSpec((M, HEAD), lambda i, h: (0, N_MEM + h)),
                  pl.BlockSpec((tq, HEAD), lambda i, h: (i, gb + h))],
        out_specs=pl.BlockSpec((tq, HEAD), lambda i, h: (i, h)),
        out_shape=SDS((S, MEM_W), _BF), compiler_params=_cp("parallel", "parallel"), name="mem_attn_fwd")(p_b, mkv, mkv, p_f)


def _mem_attn_bwd(p_b, p_f, mkv, dy):
    S = p_b.shape[0]
    M = mkv.shape[0]
    tq = min(T_ATT, S)
    qb, gb, yb = B_MQ // HEAD, F_MG // HEAD, (CONV_W + FOX_W) // HEAD

    def body(q_ref, k_ref, v_ref, g_ref, dy_ref, dq_ref, dg_ref, dk_ref, dv_ref):
        i = pl.program_id(1)
        q, k, v = q_ref[...], k_ref[...], v_ref[...]
        s = _dot(q, k, NT) * SCALE
        m = jnp.max(s, axis=-1, keepdims=True)
        e = jnp.exp(s - m)
        p = e / jnp.sum(e, axis=-1, keepdims=True)
        pb = p.astype(_BF)
        o = _dot(pb, v, NN)
        gv = g_ref[...]
        sg = _sigmoid(gv)
        dyv = dy_ref[...]
        do = dyv * (gv * sg)
        dg_ref[...] = (dyv * o * _dsilu(gv, sg)).astype(_BF)
        dob = do.astype(_BF)
        dp = _dot(dob, v, NT)
        ds = p * (dp - jnp.sum(dp * p, axis=-1, keepdims=True)) * SCALE
        dsb = ds.astype(_BF)
        dq_ref[...] = _dot(dsb, k, NN).astype(_BF)
        dk = _dot(dsb, q, TN)
        dv = _dot(pb, dob, TN)

        @pl.when(i == 0)
        def _():
            dk_ref[...] = dk
            dv_ref[...] = dv

        @pl.when(i > 0)
        def _():
            dk_ref[...] += dk
            dv_ref[...] += dv

    tile = pl.BlockSpec((tq, HEAD), lambda h, i: (i, h))
    kv = pl.BlockSpec((M, HEAD), lambda h, i: (0, h))
    return pl.pallas_call(
        body, grid=(N_MEM, S // tq),
        in_specs=[pl.BlockSpec((tq, HEAD), lambda h, i: (i, qb + h)), kv,
                  pl.BlockSpec((M, HEAD), lambda h, i: (0, N_MEM + h)),
                  pl.BlockSpec((tq, HEAD), lambda h, i: (i, gb + h)),
                  pl.BlockSpec((tq, HEAD), lambda h, i: (i, yb + h))],
        out_specs=[tile, tile, kv, kv],
        out_shape=[SDS((S, MEM_W), _BF), SDS((S, MEM_W), _BF), SDS((M, MEM_W), F32), SDS((M, MEM_W), F32)],
        compiler_params=_cp("parallel", "arbitrary"), name="mem_attn_bwd")(p_b, mkv, mkv, p_f, dy)


def _fox_cumsum(p_f, b_f_pad):
    S = p_f.shape[0]
    tr = min(256, S)
    fb = F_FL // 128

    def body(z_ref, b_ref, col_ref, row_ref, carry_ref):
        i = pl.program_id(0)

        @pl.when(i == 0)
        def _():
            carry_ref[...] = jnp.zeros_like(carry_ref)

        z = z_ref[...] + b_ref[...]
        lf = jnp.minimum(z, 0.0) - jnp.log1p(jnp.exp(-jnp.abs(z)))
        r = lax.broadcasted_iota(jnp.int32, (tr, tr), 0)
        c = lax.broadcasted_iota(jnp.int32, (tr, tr), 1)
        tri = (c <= r).astype(F32)
        cs = jnp.dot(tri, lf, precision=lax.Precision.HIGHEST, preferred_element_type=F32) + carry_ref[...]
        carry_ref[...] = cs[tr - 1:tr, :]
        cst = cs.T
        for hd in range(N_FOX):
            col_ref[hd] = cs[:, hd:hd + 1]
            row_ref[hd] = cst[hd:hd + 1, :]

    return pl.pallas_call(
        body, grid=(S // tr,),
        in_specs=[pl.BlockSpec((tr, 128), lambda i: (i, fb)), pl.BlockSpec((1, 128), lambda i: (0, 0))],
        out_specs=[pl.BlockSpec((N_FOX, tr, 1), lambda i: (0, i, 0)), pl.BlockSpec((N_FOX, 1, tr), lambda i: (0, 0, i))],
        out_shape=[SDS((N_FOX, S, 1), F32), SDS((N_FOX, 1, S), F32)], scratch_shapes=[pltpu.VMEM((1, 128), F32)],
        compiler_params=_cp("arbitrary"), name="fox_cumsum")(p_f, b_f_pad)


def _fox_dlogf(dr, p_f, b_f_pad):
    S = p_f.shape[0]
    tr = min(256, S)
    nb = S // tr
    fb = F_FL // 128
    wide = PART_W - F_FL

    def body(dr_ref, z_ref, b_ref, dz_ref, gb_ref, carry_ref):
        i = pl.program_id(0)

        @pl.when(i == 0)
        def _():
            carry_ref[...] = jnp.zeros_like(carry_ref)
            gb_ref[...] = jnp.zeros_like(gb_ref)

        heads = [dr_ref[hd, 0:1, :] + dr_ref[hd, 1:2, :] for hd in range(N_FOX)]
        dc = jnp.concatenate(heads + [jnp.zeros((128 - N_FOX, tr), F32)], axis=0).T
        r = lax.broadcasted_iota(jnp.int32, (tr, tr), 0)
        c = lax.broadcasted_iota(jnp.int32, (tr, tr), 1)
        tri = (c >= r).astype(F32)
        rc = jnp.dot(tri, dc, precision=lax.Precision.HIGHEST, preferred_element_type=F32) + carry_ref[...]
        carry_ref[...] = rc[0:1, :]
        z = z_ref[...] + b_ref[...]
        dz = rc * _sigmoid(-z)
        gb_ref[...] += jnp.sum(dz, axis=0, keepdims=True)
        dz_ref[...] = jnp.concatenate([dz.astype(_BF), jnp.zeros((tr, wide - 128), _BF)], axis=1)

    return pl.pallas_call(
        body, grid=(nb,),
        in_specs=[pl.BlockSpec((N_FOX, 8, tr), lambda i: (0, 0, nb - 1 - i)),
                  pl.BlockSpec((tr, 128), lambda i: (nb - 1 - i, fb)), pl.BlockSpec((1, 128), lambda i: (0, 0))],
        out_specs=[pl.BlockSpec((tr, wide), lambda i: (nb - 1 - i, 0)), pl.BlockSpec((1, 128), lambda i: (0, 0))],
        out_shape=[SDS((S, wide), _BF), SDS((1, 128), F32)], scratch_shapes=[pltpu.VMEM((1, 128), F32)],
        compiler_params=_cp("arbitrary"), name="fox_dlogf")(dr, p_f, b_f_pad)


def _fox_fwd(p_b, p_f, c_col, c_row):
    S = p_b.shape[0]
    t = min(T_ATT, S)
    nq = S // t
    wide = 2 * t
    hq = t // 2
    qb, kb, vb, gb = B_Q // HEAD, B_K // HEAD, B_V // HEAD, F_FG // HEAD
    kq = SCALE * 1.4426950408889634

    def body(q_ref, k_ref, v_ref, cc_ref, cr_ref, g_ref, o_ref, y_ref, lse_ref, va_ref, m_ref, acc_ref):
        i = pl.program_id(1)

        @pl.when(i == 0)
        def _():
            va_ref[:, 0:HEAD] = v_ref[...]
            lane = lax.broadcasted_iota(jnp.int32, (S, HEAD), 1)
            va_ref[:, HEAD:2 * HEAD] = jnp.where(lane == 0, 1.0, 0.0).astype(_BF)

        m_ref[...] = jnp.full_like(m_ref, NEG)
        acc_ref[...] = jnp.zeros_like(acc_ref)

        def block(off, width, masked):
            k = k_ref[pl.ds(off, width), :]
            va = va_ref[pl.ds(off, width), :]
            csr = cr_ref[:, pl.ds(off, width)] * (1.0 / SCALE)
            for half in range(2):
                rows = slice(half * hq, (half + 1) * hq)
                u = _dot(q_ref[rows, :], k, NT) - csr
                if masked:
                    row = lax.broadcasted_iota(jnp.int32, (hq, width), 0) + half * hq
                    col = lax.broadcasted_iota(jnp.int32, (hq, width), 1)
                    u = jnp.where(col <= row, u, NEG)
                m_old = m_ref[rows, :]
                m_new = jnp.maximum(m_old, jnp.max(u, axis=-1, keepdims=True))
                alpha = jnp.exp2((m_old - m_new) * kq)
                p = jnp.exp2((u - m_new) * kq)
                acc_ref[rows, :] = alpha * acc_ref[rows, :] + _dot(p.astype(_BF), va, NN)
                m_ref[rows, :] = m_new

        def step(j, carry):
            block(pl.multiple_of(j * wide, wide), wide, False)
            return carry

        lax.fori_loop(0, i // 2, step, 0)

        @pl.when(i % 2 == 1)
        def _():
            block(pl.multiple_of((i - 1) * t, t), t, False)

        block(pl.multiple_of(i * t, t), t, True)
        l = acc_ref[:, HEAD:HEAD + 1]
        o = acc_ref[:, 0:HEAD] / l
        gv = g_ref[...]
        o_ref[...] = o.astype(_BF)
        y_ref[...] = (o * (gv * _sigmoid(gv))).astype(_BF)
        lse_ref[...] = cc_ref[...] + SCALE * m_ref[...] + jnp.log(l)

    tile = pl.BlockSpec((t, HEAD), lambda h, i: (i, h))
    return pl.pallas_call(
        body, grid=(N_FOX, nq),
        in_specs=[pl.BlockSpec((t, HEAD), lambda h, i: (i, qb + h)),
                  pl.BlockSpec((S, HEAD), lambda h, i: (0, kb + h)),
                  pl.BlockSpec((S, HEAD), lambda h, i: (0, vb + h)),
                  pl.BlockSpec((None, t, 1), lambda h, i: (h, i, 0)),
                  pl.BlockSpec((None, 1, S), lambda h, i: (h, 0, 0)),
                  pl.BlockSpec((t, HEAD), lambda h, i: (i, gb + h))],
        out_specs=[tile, tile, pl.BlockSpec((None, t, 1), lambda h, i: (h, i, 0))],
        out_shape=[SDS((S, FOX_W), _BF), SDS((S, FOX_W), _BF), SDS((N_FOX, S, 1), F32)],
        scratch_shapes=[pltpu.VMEM((S, 2 * HEAD), _BF), pltpu.VMEM((t, 1), F32), pltpu.VMEM((t, 2 * HEAD), F32)],
        compiler_params=_cp("parallel", "arbitrary"), name="fox_fwd")(p_b, p_b, p_b, c_col, c_row, p_f)


def _fox_bwd_prep(dy, p_f, o, lse, c_col):
    S = dy.shape[0]
    t = min(T_ATT, S)
    yb, gb = CONV_W // HEAD, F_FG // HEAD

    def body(dy_ref, g_ref, o_ref, lse_ref, cc_ref, do_ref, dg_ref, rows_ref):
        gv = g_ref[...]
        sg = _sigmoid(gv)
        dyv = dy_ref[...]
        ov = o_ref[...].astype(F32)
        do = dyv * (gv * sg)
        do_ref[...] = do.astype(_BF)
        dg_ref[...] = (dyv * ov * _dsilu(gv, sg)).astype(_BF)
        delta = jnp.sum(do * ov, axis=-1, keepdims=True)
        a = cc_ref[...] - lse_ref[...]
        lane = lax.broadcasted_iota(jnp.int32, (t, 128), 1)
        mat = jnp.where(lane == 0, a, jnp.where(lane == 1, delta, 0.0))
        rows_ref[...] = mat.T[0:8, :]

    tile = pl.BlockSpec((t, HEAD), lambda h, i: (i, h))
    col = pl.BlockSpec((None, t, 1), lambda h, i: (h, i, 0))
    return pl.pallas_call(
        body, grid=(N_FOX, S // t),
        in_specs=[pl.BlockSpec((t, HEAD), lambda h, i: (i, yb + h)), pl.BlockSpec((t, HEAD), lambda h, i: (i, gb + h)),
                  tile, col, col],
        out_specs=[tile, tile, pl.BlockSpec((None, 8, t), lambda h, i: (h, 0, i))],
        out_shape=[SDS((S, FOX_W), _BF), SDS((S, FOX_W), _BF), SDS((N_FOX, 8, S), F32)],
        compiler_params=_cp("parallel", "parallel"), name="fox_bwd_prep")(dy, p_f, o, lse, c_col)


def _fox_bwd(p_b, do, rows, c_col):
    S = p_b.shape[0]
    t = min(T_ATT, S)
    nk = S // t
    qb, kb, vb = B_Q // HEAD, B_K // HEAD, B_V // HEAD

    def body(k_ref, v_ref, q_ref, do_ref, rows_ref, cc_ref, dq_ref, dk_ref, dv_ref, dr_ref,
             dqt_ref, dka_ref, dva_ref, dca_ref, dra_ref):
        j = pl.program_id(1)

        @pl.when(j == 0)
        def _():
            dqt_ref[...] = jnp.zeros_like(dqt_ref)
            dra_ref[...] = jnp.zeros_like(dra_ref)

        k = k_ref[...]
        v = v_ref[...]
        kt = k.astype(F32).T.astype(_BF)
        cc = cc_ref[...]
        dka_ref[...] = jnp.zeros_like(dka_ref)
        dva_ref[...] = jnp.zeros_like(dva_ref)
        dca_ref[...] = jnp.zeros_like(dca_ref)

        def block(i, masked):
            off = pl.multiple_of(i * t, t)
            q = q_ref[pl.ds(off, t), :]
            dov = do_ref[pl.ds(off, t), :]
            a_row = rows_ref[0:1, pl.ds(off, t)]
            delta_row = rows_ref[1:2, pl.ds(off, t)]
            st = _dot(k, q, NT) * SCALE + (a_row - cc)
            if masked:
                srow = lax.broadcasted_iota(jnp.int32, (t, t), 0)
                tcol = lax.broadcasted_iota(jnp.int32, (t, t), 1)
                st = jnp.where(srow <= tcol, st, NEG)
            pt = jnp.exp(st)
            dva_ref[...] += _dot(pt.astype(_BF), dov, NN)
            dpt = _dot(v, dov, NT)
            dst = pt * (dpt - delta_row)
            part = dst[:, 0:128]
            for gidx in range(1, t // 128):
                part = part + dst[:, gidx * 128:(gidx + 1) * 128]
            dca_ref[...] += part
            dra_ref[0:1, pl.ds(off, t)] += jnp.sum(dst, axis=0, keepdims=True)
            dsb = dst.astype(_BF)
            dka_ref[...] += _dot(dsb, q, NN)
            dqt_ref[:, pl.ds(off, t)] += _dot(kt, dsb, NN)

        block(j, True)

        def step(i, carry):
            block(i, False)
            return carry

        lax.fori_loop(j + 1, nk, step, 0)
        dk_ref[...] = (dka_ref[...] * SCALE).astype(_BF)
        dv_ref[...] = dva_ref[...].astype(_BF)
        dra_ref[1:2, pl.ds(pl.multiple_of(j * t, t), t)] = -jnp.sum(dca_ref[...].T, axis=0, keepdims=True)

        @pl.when(j == nk - 1)
        def _():
            dr_ref[...] = dra_ref[...]
            for ci in range(nk):
                dq_ref[ci * t:(ci + 1) * t, :] = (dqt_ref[:, ci * t:(ci + 1) * t].T * SCALE).astype(_BF)

    tile = pl.BlockSpec((t, HEAD), lambda h, j: (j, h))
    return pl.pallas_call(
        body, grid=(N_FOX, nk),
        in_specs=[pl.BlockSpec((t, HEAD), lambda h, j: (j, kb + h)),
                  pl.BlockSpec((t, HEAD), lambda h, j: (j, vb + h)),
                  pl.BlockSpec((S, HEAD), lambda h, j: (0, qb + h)),
                  pl.BlockSpec((S, HEAD), lambda h, j: (0, h)),
                  pl.BlockSpec((None, 8, S), lambda h, j: (h, 0, 0)),
                  pl.BlockSpec((None, t, 1), lambda h, j: (h, j, 0))],
        out_specs=[pl.BlockSpec((S, HEAD), lambda h, j: (0, h)), tile, tile,
                   pl.BlockSpec((None, 8, S), lambda h, j: (h, 0, 0))],
        out_shape=[SDS((S, FOX_W), _BF), SDS((S, FOX_W), _BF), SDS((S, FOX_W), _BF), SDS((N_FOX, 8, S), F32)],
        scratch_shapes=[pltpu.VMEM((HEAD, S), F32), pltpu.VMEM((t, HEAD), F32), pltpu.VMEM((t, HEAD), F32),
                        pltpu.VMEM((t, 128), F32), pltpu.VMEM((8, S), F32)],
        compiler_params=_cp("parallel", "arbitrary"), name="fox_bwd")(p_b, p_b, p_b, do, rows, c_col)


CHUNK = 32


def _conv_taps(ext_ref, w_ref, first, out_fn, n_rows):
    def chunk(c, carry):
        r0 = pl.multiple_of(c * CHUNK, CHUNK)
        win = ext_ref[pl.ds(r0, 2 * CHUNK), :]
        acc = jnp.zeros((CHUNK, CONV_W), F32)
        for k in range(CONV_K):
            f = first(k)
            acc = acc + w_ref[k:k + 1, :] * win[f:f + CHUNK, :]
        out_fn(r0, acc)
        return carry

    lax.fori_loop(0, n_rows // CHUNK, chunk, 0)


def _conv_fwd(p_f, conv_w, conv_b, ln_g, ln_b, w_pw):
    S = p_f.shape[0]
    tc = min(T_CONV, S)
    hb = tc // HALO

    def body(a_ref, b_ref, gc_ref, ap_ref, bp_ref, w_ref, cb_ref, lg_ref, lb_ref, pw_ref, y_ref, u1_ref, ext_ref):
        i = pl.program_id(0)
        prev = ap_ref[...] * _sigmoid(bp_ref[...])
        ext_ref[0:HALO, :] = jnp.where(i > 0, prev, 0.0)
        ext_ref[HALO:HALO + tc, :] = a_ref[...] * _sigmoid(b_ref[...])
        cb = cb_ref[...]

        def put(r0, acc):
            u1_ref[pl.ds(r0, CHUNK), :] = acc + cb

        _conv_taps(ext_ref, w_ref, lambda k: HALO - (CONV_K - 1) + k, put, tc)
        u1 = u1_ref[...]
        mu = jnp.mean(u1, axis=-1, keepdims=True)
        d = u1 - mu
        rstd = lax.rsqrt(jnp.mean(d * d, axis=-1, keepdims=True) + EPS)
        u2 = d * rstd * lg_ref[...] + lb_ref[...]
        u3 = u2 * _sigmoid(u2)
        pw = _dot(u3.astype(_BF), pw_ref[...], NN)
        gc = gc_ref[...]
        y_ref[...] = (pw * (gc * _sigmoid(gc))).astype(_BF)

    blk = lambda cb_: pl.BlockSpec((tc, CONV_W), lambda i: (i, cb_))
    halo = lambda cb_: pl.BlockSpec((HALO, CONV_W), lambda i: (jnp.maximum(i * hb - 1, 0), cb_))
    vec = pl.BlockSpec((1, CONV_W), lambda i: (0, 0))
    return pl.pallas_call(
        body, grid=(S // tc,),
        in_specs=[blk(0), blk(1), blk(2), halo(0), halo(1), pl.BlockSpec((CONV_K, CONV_W), lambda i: (0, 0)),
                  vec, vec, vec, pl.BlockSpec((CONV_W, CONV_W), lambda i: (0, 0))],
        out_specs=[pl.BlockSpec((tc, CONV_W), lambda i: (i, 0)), pl.BlockSpec((tc, CONV_W), lambda i: (i, 0))],
        out_shape=[SDS((S, CONV_W), _BF), SDS((S, CONV_W), F32)],
        scratch_shapes=[pltpu.VMEM((tc + 2 * HALO, CONV_W), F32)],
        compiler_params=_cp("parallel"), name="conv_fwd")(p_f, p_f, p_f, p_f, p_f, conv_w, conv_b, ln_g, ln_b, w_pw)


def _conv_bwd1(u1, p_f, dy, ln_g, ln_b, w_pw):
    S = u1.shape[0]
    tc = min(T_CONV, S)

    def body(u1_ref, gc_ref, dy_ref, lg_ref, lb_ref, pw_ref, du1_ref, dgc_ref, gpw_ref, glg_ref, glb_ref, gcb_ref):
        i = pl.program_id(0)
        u1v = u1_ref[...]
        mu = jnp.mean(u1v, axis=-1, keepdims=True)
        d = u1v - mu
        rstd = lax.rsqrt(jnp.mean(d * d, axis=-1, keepdims=True) + EPS)
        xh = d * rstd
        lg = lg_ref[...]
        u2 = xh * lg + lb_ref[...]
        sg2 = _sigmoid(u2)
        u3b = (u2 * sg2).astype(_BF)
        w = pw_ref[...]
        pw = _dot(u3b, w, NN)
        gc = gc_ref[...]
        sgc = _sigmoid(gc)
        dyv = dy_ref[...]
        dpw = (dyv * (gc * sgc)).astype(_BF)
        dgc_ref[...] = (dyv * pw * _dsilu(gc, sgc)).astype(_BF)
        gpw = _dot(u3b, dpw, TN)
        du2 = _dot(dpw, w, NT) * _dsilu(u2, sg2)
        glg = jnp.sum(du2 * xh, axis=0, keepdims=True)
        glb = jnp.sum(du2, axis=0, keepdims=True)
        dxh = du2 * lg
        du1 = rstd * (dxh - jnp.mean(dxh, axis=-1, keepdims=True) - xh * jnp.mean(dxh * xh, axis=-1, keepdims=True))
        du1_ref[...] = du1
        gcb = jnp.sum(du1, axis=0, keepdims=True)

        @pl.when(i == 0)
        def _():
            gpw_ref[...] = gpw
            glg_ref[...] = glg
            glb_ref[...] = glb
            gcb_ref[...] = gcb

        @pl.when(i > 0)
        def _():
            gpw_ref[...] += gpw
            glg_ref[...] += glg
            glb_ref[...] += glb
            gcb_ref[...] += gcb

    row = pl.BlockSpec((tc, CONV_W), lambda i: (i, 0))
    vec = pl.BlockSpec((1, CONV_W), lambda i: (0, 0))
    sq = pl.BlockSpec((CONV_W, CONV_W), lambda i: (0, 0))
    return pl.pallas_call(
        body, grid=(S // tc,),
        in_specs=[row, pl.BlockSpec((tc, CONV_W), lambda i: (i, F_GC // CONV_W)), row, vec, vec, sq],
        out_specs=[row, row, sq, vec, vec, vec],
        out_shape=[SDS((S, CONV_W), F32), SDS((S, CONV_W), _BF), SDS((CONV_W, CONV_W), F32),
                   SDS((1, CONV_W), F32), SDS((1, CONV_W), F32), SDS((1, CONV_W), F32)],
        compiler_params=_cp("arbitrary"), name="conv_bwd1")(u1, p_f, dy, ln_g, ln_b, w_pw)


def _conv_bwd2(du1, p_f, conv_w):
    S = du1.shape[0]
    tc = min(T_CONV, S)
    hb = tc // HALO
    nblk = S // tc
    last_halo = S // HALO - 1

    def body(d_ref, dn_ref, a_ref, b_ref, ap_ref, bp_ref, w_ref, da_ref, db_ref, gw_ref, ext_ref, dext_ref, du0_ref):
        i = pl.program_id(0)
        av = a_ref[...]
        sb = _sigmoid(b_ref[...])
        prev = ap_ref[...] * _sigmoid(bp_ref[...])
        ext_ref[0:HALO, :] = jnp.where(i > 0, prev, 0.0)
        ext_ref[HALO:HALO + tc, :] = av * sb
        dext_ref[0:tc, :] = d_ref[...]
        dext_ref[tc:tc + HALO, :] = jnp.where(i < nblk - 1, dn_ref[...], 0.0)

        def put(r0, acc):
            du0_ref[pl.ds(r0, CHUNK), :] = acc

        _conv_taps(dext_ref, w_ref, lambda k: CONV_K - 1 - k, put, tc)
        du0 = du0_ref[...]
        da_ref[...] = (du0 * sb).astype(_BF)
        db_ref[...] = (du0 * av * sb * (1.0 - sb)).astype(_BF)

        def chunk(c, carry):
            r0 = pl.multiple_of(c * CHUNK, CHUNK)
            win = ext_ref[pl.ds(r0, 2 * CHUNK), :]
            dv = dext_ref[pl.ds(r0, CHUNK), :]
            rows = [jnp.sum(dv * win[HALO - (CONV_K - 1) + k:HALO - (CONV_K - 1) + k + CHUNK, :], axis=0, keepdims=True)
                    for k in range(CONV_K)]
            rows.append(jnp.zeros((1, CONV_W), F32))
            return carry + jnp.concatenate(rows, axis=0)

        gw = lax.fori_loop(0, tc // CHUNK, chunk, jnp.zeros((CONV_K + 1, CONV_W), F32))

        @pl.when(i == 0)
        def _():
            gw_ref[...] = gw

        @pl.when(i > 0)
        def _():
            gw_ref[...] += gw

    row = pl.BlockSpec((tc, CONV_W), lambda i: (i, 0))
    blk = lambda cb_: pl.BlockSpec((tc, CONV_W), lambda i: (i, cb_))
    halo = lambda cb_: pl.BlockSpec((HALO, CONV_W), lambda i: (jnp.maximum(i * hb - 1, 0), cb_))
    nxt = pl.BlockSpec((HALO, CONV_W), lambda i: (jnp.minimum((i + 1) * hb, last_halo), 0))
    return pl.pallas_call(
        body, grid=(nblk,),
        in_specs=[row, nxt, blk(0), blk(1), halo(0), halo(1), pl.BlockSpec((CONV_K, CONV_W), lambda i: (0, 0))],
        out_specs=[row, row, pl.BlockSpec((CONV_K + 1, CONV_W), lambda i: (0, 0))],
        out_shape=[SDS((S, CONV_W), _BF), SDS((S, CONV_W), _BF), SDS((CONV_K + 1, CONV_W), F32)],
        scratch_shapes=[pltpu.VMEM((tc + 2 * HALO, CONV_W), F32), pltpu.VMEM((tc + 2 * HALO, CONV_W), F32),
                        pltpu.VMEM((tc, CONV_W), F32)],
        compiler_params=_cp("arbitrary"), name="conv_bwd2")(du1, du1, p_f, p_f, p_f, p_f, conv_w)


def _exchange(srcs, scatter, name):
    n = len(srcs)
    out_shape = [SDS((N_DEV,) + (s.shape[1:] if sc else s.shape), s.dtype) for s, sc in zip(srcs, scatter)]

    def body(*refs):
        src_refs, dst_refs = refs[:n], refs[n:2 * n]
        send_sems, recv_sems, local_sems = refs[2 * n:]
        x, y, c = lax.axis_index("x"), lax.axis_index("y"), lax.axis_index("c")
        me = 4 * x + 2 * y + c
        copies = []
        for a in range(n):
            for f in range(1, N_DEV):
                px = 1 - x if f & 4 else x
                py = 1 - y if f & 2 else y
                pc = 1 - c if f & 1 else c
                peer = 4 * px + 2 * py + pc
                src = src_refs[a].at[peer] if scatter[a] else src_refs[a]
                cp = pltpu.make_async_remote_copy(
                    src_ref=src, dst_ref=dst_refs[a].at[me], send_sem=send_sems.at[a, f - 1],
                    recv_sem=recv_sems.at[a, f - 1], device_id=(px, py, pc), device_id_type=MESH)
                cp.start()
                copies.append(cp)
            own = src_refs[a].at[me] if scatter[a] else src_refs[a]
            lc = pltpu.make_async_copy(own, dst_refs[a].at[me], local_sems.at[a])
            lc.start()
            copies.append(lc)
        for cp in copies:
            cp.wait()

    anyspec = pl.BlockSpec(memory_space=pl.ANY)
    return pl.pallas_call(
        body, in_specs=[anyspec] * n, out_specs=[anyspec] * n, out_shape=out_shape,
        scratch_shapes=[pltpu.SemaphoreType.DMA((n, N_DEV - 1)), pltpu.SemaphoreType.DMA((n, N_DEV - 1)),
                        pltpu.SemaphoreType.DMA((n,))],
        name=name)(*srcs)


def _gather_two_level(srcs, name):
    n = len(srcs)
    out_shape = [SDS((N_DEV,) + s.shape, s.dtype) for s in srcs]

    def body(*refs):
        src_refs, dst_refs = refs[:n], refs[n:2 * n]
        send_sems, recv_sems, local_sems = refs[2 * n:]
        x, y, c = lax.axis_index("x"), lax.axis_index("y"), lax.axis_index("c")
        sibling = (x, y, 1 - c)
        chips = [(1 - x, y), (x, 1 - y), (1 - x, 1 - y)]

        def slot(a, px, py, pc):
            return dst_refs[a].at[4 * px + 2 * py + pc]

        def copy(a, k, block, to, src=None):
            return pltpu.make_async_remote_copy(
                src_ref=slot(a, *block) if src is None else src, dst_ref=slot(a, *block),
                send_sem=send_sems.at[a, k], recv_sem=recv_sems.at[a, k], device_id=to, device_id_type=MESH)

        own, sends = [], []
        for a in range(n):
            mine = pltpu.make_async_copy(src_refs[a], slot(a, x, y, c), local_sems.at[a])
            mine.start()
            own.append(mine)
            first = [copy(a, 1 + j, (x, y, c), (*chip, c), src=src_refs[a]) for j, chip in enumerate(chips)]
            first.append(copy(a, 0, (x, y, c), sibling, src=src_refs[a]))
            for cp in first:
                cp.start()
            sends += first
        for a in range(n):
            for j, chip in enumerate(chips):
                copy(a, 1 + j, (*chip, c), (x, y, c)).wait_recv()
                fwd = copy(a, 4 + j, (*chip, c), sibling)
                fwd.start()
                sends.append(fwd)
        for a in range(n):
            copy(a, 0, (x, y, 1 - c), (x, y, c)).wait_recv()
            for j, chip in enumerate(chips):
                copy(a, 4 + j, (*chip, 1 - c), (x, y, c)).wait_recv()
        for cp in sends:
            cp.wait_send()
        for cp in own:
            cp.wait()

    anyspec = pl.BlockSpec(memory_space=pl.ANY)
    return pl.pallas_call(
        body, in_specs=[anyspec] * n, out_specs=[anyspec] * n, out_shape=out_shape,
        scratch_shapes=[pltpu.SemaphoreType.DMA((n, N_DEV - 1)), pltpu.SemaphoreType.DMA((n, N_DEV - 1)),
                        pltpu.SemaphoreType.DMA((n,))],
        name=name)(*srcs)


def _adamw(parts, w, m, v, name, tr=256):
    lead = w.ndim == 3
    R, C = w.shape[-2:]
    tr = tr if R % tr == 0 else R

    def body(p_ref, w_ref, m_ref, v_ref, g_ref, d_ref, nm_ref, nv_ref):
        g = p_ref[0].astype(F32)
        for dev in range(1, N_DEV):
            g = g + p_ref[dev].astype(F32)
        mn = ADAM_B1 * m_ref[...] + (1.0 - ADAM_B1) * g
        vn = ADAM_B2 * v_ref[...] + (1.0 - ADAM_B2) * (g * g)
        m_hat = mn / (1.0 - ADAM_B1 ** ADAM_STEP)
        v_hat = vn / (1.0 - ADAM_B2 ** ADAM_STEP)
        g_ref[...] = g
        d_ref[...] = -ADAM_LR * (m_hat / (jnp.sqrt(v_hat) + ADAM_EPS) + ADAM_WD * w_ref[...])
        nm_ref[...] = mn
        nv_ref[...] = vn

    blk = pl.BlockSpec((None, tr, C), lambda i: (0, i, 0)) if lead else pl.BlockSpec((tr, C), lambda i: (i, 0))
    return pl.pallas_call(
        body, grid=(R // tr,), in_specs=[pl.BlockSpec((N_DEV, tr, C), lambda i: (0, i, 0)), blk, blk, blk],
        out_specs=[blk] * 4, out_shape=[SDS(w.shape, F32)] * 4,
        compiler_params=_cp("parallel"), name=name)(parts, w, m, v)


def _split_w_in(w_cat, D):
    z = jnp.zeros((D, PART_W - F_FL - N_FOX), w_cat.dtype)
    w_f = jnp.concatenate([w_cat[:, O_A:O_Q], w_cat[:, O_FG:O_MQ], w_cat[:, O_MG:D_IN], w_cat[:, O_FL:O_FG], z], axis=1)
    w_b = jnp.concatenate([w_cat[:, O_Q:O_FL], w_cat[:, O_MQ:O_MG]], axis=1)
    return w_f, w_b


def _merge_w_in(g_f, g_b):
    return jnp.concatenate([g_f[:, F_A:F_FG], g_b[:, B_Q:B_MQ], g_f[:, F_FL:F_FL + N_FOX], g_f[:, F_FG:F_MG],
                            g_b[:, B_MQ:PART_W], g_f[:, F_MG:F_FL]], axis=1)


def _pack_small(norm, mem_norm, final, conv_b, ln_g, ln_b, b_f, extra, D):
    width = max(D, PACK_W)
    row3 = jnp.concatenate([conv_b.reshape(-1), ln_g.reshape(-1), ln_b.reshape(-1), b_f.reshape(-1)])
    rows = [norm.reshape(-1), mem_norm.reshape(-1), final.reshape(-1), row3, extra.reshape(-1)]
    rows = [jnp.pad(r, (0, width - r.shape[0])) for r in rows]
    return jnp.concatenate([jnp.stack(rows), jnp.zeros((3, width), F32)], axis=0)


def _unpack_small(p, D):
    c = CONV_W
    return dict(norm_g=p[0:1, :D], mem_norm_g=p[1:2, :D], final_g=p[2, :D], conv_b=p[3:4, 0:c],
                conv_ln_g=p[3:4, c:2 * c], conv_ln_b=p[3:4, 2 * c:3 * c], b_f=p[3:4, 3 * c:3 * c + N_FOX])


def kernel(x, mem, norm_g, mem_norm_g, w_in, b_f, conv_w, conv_b, conv_ln_g, conv_ln_b, w_conv_pw, w_mem_kv, w_out, final_g, loss_target, m_norm_g, m_mem_norm_g, m_w_in, m_b_f, m_conv_w, m_conv_b, m_conv_ln_g, m_conv_ln_b, m_w_conv_pw, m_w_mem_kv, m_w_out, m_final_g, v_norm_g, v_mem_norm_g, v_w_in, v_b_f, v_conv_w, v_conv_b, v_conv_ln_g, v_conv_ln_b, v_w_conv_pw, v_w_mem_kv, v_w_out, v_final_g):
    S, D = x.shape[1], x.shape[2]
    M = mem.shape[1]
    xs, ms, tgt = x[0], mem[0], loss_target[0]
    cols = w_in.shape[2]

    g_in, g_cw, g_pw, g_kv, g_out = _gather_two_level(
        [w_in[0].astype(_BF), conv_w[0], w_conv_pw[0].astype(_BF), w_mem_kv[0].astype(_BF), w_out[0].astype(_BF)],
        "gather_weights")
    w_f, w_b = _split_w_in(jnp.transpose(g_in, (1, 0, 2)).reshape(D, N_DEV * cols), D)
    conv_w_full = jnp.transpose(g_cw, (1, 0, 2)).reshape(CONV_K, CONV_W)
    w_pw_full = g_pw.reshape(CONV_W, CONV_W)
    w_kv_full = g_kv.reshape(D, 2 * MEM_W)
    w_out_full = g_out.reshape(CONV_W + FOX_W + MEM_W, D)
    b_f_pad = jnp.pad(b_f, ((0, 0), (0, 128 - N_FOX)))

    h, r1 = _rms_fwd(xs, norm_g)
    p_f = _matmul(h, w_f, "nn", F32, "proj_f")
    p_b = _matmul(h, w_b, "nn", _BF, "proj_b")
    mkv, mhat = _mem_kv_fwd(ms, mem_norm_g, w_kv_full)
    y_conv, u1 = _conv_fwd(p_f, conv_w_full, conv_b, conv_ln_g, conv_ln_b, w_pw_full)
    c_col, c_row = _fox_cumsum(p_f, b_f_pad)
    o_fox, y_fox, lse = _fox_fwd(p_b, p_f, c_col, c_row)
    y_mem = _mem_attn_fwd(p_b, p_f, mkv)
    y = jnp.concatenate([y_conv, y_fox, y_mem], axis=1)
    z = _matmul(y, w_out_full, "nn", F32, "out_proj")
    dx2, g_final, loss_part = _head_loss(xs, z, tgt, final_g.reshape(1, D))

    dy = _matmul(dx2, w_out_full, "nt", F32, "d_y")
    gw_out = _matmul(y, dx2, "tn", _BF, "gw_out")
    d_mq, d_mg, d_mk, d_mv = _mem_attn_bwd(p_b, p_f, mkv, dy)
    gw_kv, g_mem_norm = _mem_kv_bwd(jnp.concatenate([d_mk, d_mv], axis=1), mhat, mem_norm_g, w_kv_full)
    d_o, d_fg, rows = _fox_bwd_prep(dy, p_f, o_fox, lse, c_col)
    d_q, d_k, d_v, dr = _fox_bwd(p_b, d_o, rows, c_col)
    d_fl, g_bf = _fox_dlogf(dr, p_f, b_f_pad)
    du1, d_gc, gw_pw, g_ln_g, g_ln_b, g_cb = _conv_bwd1(u1, p_f, dy, conv_ln_g, conv_ln_b, w_pw_full)
    d_a, d_b, gw_cw = _conv_bwd2(du1, p_f, conv_w_full)
    dp = jnp.concatenate([d_a, d_b, d_gc, d_fg, d_mg, d_fl, d_q, d_k, d_v, d_mq], axis=1)
    w_all = jnp.concatenate([w_f, w_b], axis=1)
    gw_all = _matmul(h, dp, "tn", _BF, "gw_in")
    dh = _matmul(dp, w_all, "nt", F32, "d_h")
    grad_x, g_norm = _rms_bwd(xs, r1, dh, dx2, norm_g)

    gw_in_cat = _merge_w_in(gw_all[:, :PART_W], gw_all[:, PART_W:])
    send_in = jnp.transpose(gw_in_cat.reshape(D, N_DEV, cols), (1, 0, 2))
    send_cw = jnp.transpose(gw_cw[:CONV_K].reshape(CONV_K, N_DEV, CONV_W // N_DEV), (1, 0, 2))
    send_pw = gw_pw.astype(_BF).reshape(N_DEV, CONV_W // N_DEV, CONV_W)
    send_kv = gw_kv.reshape(N_DEV, D // N_DEV, 2 * MEM_W)
    send_out = gw_out.reshape(N_DEV, (CONV_W + FOX_W + MEM_W) // N_DEV, D)
    small = _pack_small(g_norm, g_mem_norm, g_final, g_cb, g_ln_g, g_ln_b, g_bf[:, :N_FOX], loss_part[0, 0:1], D)
    r_in, r_cw, r_pw, r_kv, r_out, r_small = _exchange(
        [send_in, send_cw, send_pw, send_kv, send_out, small], [True] * 5 + [False], "exchange_grads")

    res = {}
    res["w_in"] = _adamw(r_in, w_in, m_w_in, v_w_in, "adamw_w_in")
    res["conv_w"] = _adamw(r_cw, conv_w, m_conv_w, v_conv_w, "adamw_conv_w")
    res["w_conv_pw"] = _adamw(r_pw, w_conv_pw, m_w_conv_pw, v_w_conv_pw, "adamw_w_pw")
    res["w_mem_kv"] = _adamw(r_kv, w_mem_kv, m_w_mem_kv, v_w_mem_kv, "adamw_w_kv")
    res["w_out"] = _adamw(r_out, w_out, m_w_out, v_w_out, "adamw_w_out")
    zero = jnp.zeros((1,), F32)
    pk = lambda a, b_, c_, d_, e, f_, g_: _pack_small(a, b_, c_, d_, e, f_, g_, zero, D)
    sm = _adamw(r_small,
                pk(norm_g, mem_norm_g, final_g, conv_b, conv_ln_g, conv_ln_b, b_f),
                pk(m_norm_g, m_mem_norm_g, m_final_g, m_conv_b, m_conv_ln_g, m_conv_ln_b, m_b_f),
                pk(v_norm_g, v_mem_norm_g, v_final_g, v_conv_b, v_conv_ln_g, v_conv_ln_b, v_b_f), "adamw_small")
    loss = sm[0][4, 0]
    small_out = [_unpack_small(a, D) for a in sm]
    names = ["norm_g", "mem_norm_g", "w_in", "b_f", "conv_w", "conv_b", "conv_ln_g", "conv_ln_b", "w_conv_pw",
             "w_mem_kv", "w_out", "final_g"]
    outs = [loss, grad_x[None]]
    for kind in range(4):
        for nme in names:
            outs.append(res[nme][kind] if nme in res else small_out[kind][nme])
    return tuple(outs)
```

```python
import functools

import jax
import jax.numpy as jnp
from jax import lax
from jax.experimental import pallas as pl
from jax.experimental.pallas import tpu as pltpu

F32 = jnp.float32
_BF = jnp.bfloat16
SDS = jax.ShapeDtypeStruct
MESH = pl.DeviceIdType.MESH

N_DEV = 8
HEAD = 128
N_FOX = 8
N_MEM = 4
CONV_W = 512
CONV_K = 31
FOX_W = N_FOX * HEAD
MEM_W = N_MEM * HEAD
D_IN = 3 * CONV_W + 4 * FOX_W + N_FOX + 2 * MEM_W
EPS = 1e-6
SCALE = HEAD ** -0.5
NEG = -1e30

ADAM_LR = 0.001
ADAM_B1 = 0.9
ADAM_B2 = 0.999
ADAM_EPS = 1e-08
ADAM_WD = 0.01
ADAM_STEP = 10

PART_W = 3584
F_A, F_B, F_GC, F_FG, F_MG, F_FL = 0, 512, 1024, 1536, 2560, 3072
B_Q, B_K, B_V, B_MQ = 0, 1024, 2048, 3072
O_A, O_B, O_GC = 0, 512, 1024
O_Q, O_K, O_V = 1536, 2560, 3584
O_FL, O_FG, O_MQ, O_MG = 4608, 4616, 5640, 6152

HALO = 32
T_ATT = 512
T_ROW = 256
T_CONV = 512
VMEM_LIMIT = 56 * 1024 * 1024
PACK_W = 2048


def _cp(*sem):
    return pltpu.CompilerParams(dimension_semantics=sem, vmem_limit_bytes=VMEM_LIMIT)


def _sigmoid(x):
    return jax.nn.sigmoid(x)


def _dsilu(x, sg):
    return sg * (1.0 + x * (1.0 - sg))


def _dot(a, b, dims):
    return lax.dot_general(a, b, (dims, ((), ())), preferred_element_type=F32)


NN = ((1,), (0,))
NT = ((1,), (1,))
TN = ((0,), (0,))


def _pick(n, pref):
    if n <= pref:
        return n
    t = pref - pref % 128
    while n % t:
        t -= 128
    return t


def _matmul(a, b, mode, out_dtype, name, tm=512, tn=1024, tk=2048, after=None):
    if mode == "nn":
        (M, K), (K2, N) = a.shape, b.shape
    elif mode == "nt":
        (M, K), (N, K2) = a.shape, b.shape
    else:
        (K, M), (K2, N) = a.shape, b.shape
    assert K == K2
    tm, tn, tk = _pick(M, tm), _pick(N, tn), _pick(K, tk)
    assert M % tm == 0 and N % tn == 0 and K % tk == 0, (name, M, N, K)
    nk = K // tk
    dims = {"nn": NN, "nt": NT, "tn": TN}[mode]
    n_in = 2 if after is None else 3

    def body(*refs):
        a_ref, b_ref = refs[0], refs[1]
        o_ref = refs[n_in]
        p = _dot(a_ref[...].astype(_BF), b_ref[...].astype(_BF), dims)
        if nk == 1:
            o_ref[...] = p.astype(out_dtype)
            return
        acc_ref = refs[n_in + 1]
        k = pl.program_id(2)

        @pl.when(k == 0)
        def _():
            acc_ref[...] = p

        @pl.when(jnp.logical_and(k > 0, k < nk - 1))
        def _():
            acc_ref[...] += p

        @pl.when(k == nk - 1)
        def _():
            o_ref[...] = (acc_ref[...] + p).astype(out_dtype)

    if mode == "nn":
        a_spec = pl.BlockSpec((tm, tk), lambda i, j, k: (i, k))
        b_spec = pl.BlockSpec((tk, tn), lambda i, j, k: (k, j))
    elif mode == "nt":
        a_spec = pl.BlockSpec((tm, tk), lambda i, j, k: (i, k))
        b_spec = pl.BlockSpec((tn, tk), lambda i, j, k: (j, k))
    else:
        a_spec = pl.BlockSpec((tk, tm), lambda i, j, k: (k, i))
        b_spec = pl.BlockSpec((tk, tn), lambda i, j, k: (k, j))
    in_specs, args = [a_spec, b_spec], [a, b]
    if after is not None:
        in_specs.append(pl.BlockSpec(memory_space=pl.ANY))
        args.append(after)
    return pl.pallas_call(
        body, grid=(M // tm, N // tn, nk), in_specs=in_specs,
        out_specs=pl.BlockSpec((tm, tn), lambda i, j, k: (i, j)),
        out_shape=SDS((M, N), out_dtype), scratch_shapes=[] if nk == 1 else [pltpu.VMEM((tm, tn), F32)],
        compiler_params=_cp("parallel", "parallel", "arbitrary"), name=name)(*args)


def _rms_fwd(x, g):
    S, D = x.shape
    tr = min(T_ROW, S)

    def body(x_ref, g_ref, h_ref, r_ref):
        xv = x_ref[...]
        r = lax.rsqrt(jnp.mean(xv * xv, axis=-1, keepdims=True) + EPS)
        h_ref[...] = (xv * r * g_ref[...]).astype(_BF)
        r_ref[...] = r

    return pl.pallas_call(
        body, grid=(S // tr,),
        in_specs=[pl.BlockSpec((tr, D), lambda i: (i, 0)), pl.BlockSpec((1, D), lambda i: (0, 0))],
        out_specs=[pl.BlockSpec((tr, D), lambda i: (i, 0)), pl.BlockSpec((tr, 1), lambda i: (i, 0))],
        out_shape=[SDS((S, D), _BF), SDS((S, 1), F32)],
        compiler_params=_cp("parallel"), name="rms_fwd")(x, g)


def _head_loss(x, z, target, g):
    S, D = x.shape
    tr = min(T_ROW, S)

    def body(x_ref, z_ref, t_ref, g_ref, dx2_ref, gg_ref, loss_ref):
        i = pl.program_id(0)
        x2 = x_ref[...] + z_ref[...]
        r = lax.rsqrt(jnp.mean(x2 * x2, axis=-1, keepdims=True) + EPS)
        xh = x2 * r
        gv = g_ref[...]
        diff = xh * gv - t_ref[...]
        lsum = 0.5 * jnp.sum(jnp.mean(diff * diff, axis=-1, keepdims=True), axis=0, keepdims=True)
        dout = diff * (1.0 / D)
        gd = dout * gv
        dx2_ref[...] = r * (gd - xh * jnp.mean(gd * xh, axis=-1, keepdims=True))
        gg = jnp.sum(dout * xh, axis=0, keepdims=True)

        @pl.when(i == 0)
        def _():
            gg_ref[...] = gg
            loss_ref[...] = jnp.broadcast_to(lsum, (1, 128))

        @pl.when(i > 0)
        def _():
            gg_ref[...] += gg
            loss_ref[...] += jnp.broadcast_to(lsum, (1, 128))

    row = pl.BlockSpec((tr, D), lambda i: (i, 0))
    return pl.pallas_call(
        body, grid=(S // tr,), in_specs=[row, row, row, pl.BlockSpec((1, D), lambda i: (0, 0))],
        out_specs=[row, pl.BlockSpec((1, D), lambda i: (0, 0)), pl.BlockSpec((1, 128), lambda i: (0, 0))],
        out_shape=[SDS((S, D), F32), SDS((1, D), F32), SDS((1, 128), F32)],
        compiler_params=_cp("arbitrary"), name="head_loss")(x, z, target, g)


def _rms_bwd(x, r, dh, dx2, g):
    S, D = x.shape
    tr = min(T_ROW, S)

    def body(x_ref, r_ref, dh_ref, dx2_ref, g_ref, gx_ref, gg_ref):
        i = pl.program_id(0)
        rv = r_ref[...]
        xh = x_ref[...] * rv
        dh_ = dh_ref[...]
        gd = dh_ * g_ref[...]
        gx_ref[...] = dx2_ref[...] + rv * (gd - xh * jnp.mean(gd * xh, axis=-1, keepdims=True))
        gg = jnp.sum(dh_ * xh, axis=0, keepdims=True)

        @pl.when(i == 0)
        def _():
            gg_ref[...] = gg

        @pl.when(i > 0)
        def _():
            gg_ref[...] += gg

    row = pl.BlockSpec((tr, D), lambda i: (i, 0))
    return pl.pallas_call(
        body, grid=(S // tr,),
        in_specs=[row, pl.BlockSpec((tr, 1), lambda i: (i, 0)), row, row, pl.BlockSpec((1, D), lambda i: (0, 0))],
        out_specs=[row, pl.BlockSpec((1, D), lambda i: (0, 0))],
        out_shape=[SDS((S, D), F32), SDS((1, D), F32)],
        compiler_params=_cp("arbitrary"), name="rms_bwd")(x, r, dh, dx2, g)


def _mem_kv_fwd(mem, g, w_kv):
    M, D = mem.shape

    def body(mem_ref, g_ref, w_ref, mkv_ref, mhat_ref):
        mv = mem_ref[...]
        mhat = mv * lax.rsqrt(jnp.mean(mv * mv, axis=-1, keepdims=True) + EPS)
        mhat_ref[...] = mhat
        mkv_ref[...] = _dot((mhat * g_ref[...]).astype(_BF), w_ref[...], NN).astype(_BF)

    return pl.pallas_call(
        body, out_shape=[SDS((M, 2 * MEM_W), _BF), SDS((M, D), F32)],
        compiler_params=pltpu.CompilerParams(vmem_limit_bytes=VMEM_LIMIT), name="mem_kv_fwd")(mem, g, w_kv)


def _mem_kv_bwd(dmkv, mhat, g, w_kv):
    M, D = mhat.shape

    def body(d_ref, mhat_ref, g_ref, w_ref, gw_ref, gg_ref):
        d = d_ref[...].astype(_BF)
        mhat = mhat_ref[...]
        gw_ref[...] = _dot((mhat * g_ref[...]).astype(_BF), d, TN).astype(_BF)
        dmh = _dot(d, w_ref[...], NT)
        gg_ref[...] = jnp.sum(dmh * mhat, axis=0, keepdims=True)

    return pl.pallas_call(
        body, out_shape=[SDS((D, 2 * MEM_W), _BF), SDS((1, D), F32)],
        compiler_params=pltpu.CompilerParams(vmem_limit_bytes=VMEM_LIMIT), name="mem_kv_bwd")(dmkv, mhat, g, w_kv)


def _mem_attn_fwd(p_b, p_f, mkv):
    S = p_b.shape[0]
    M = mkv.shape[0]
    tq = min(T_ATT, S)
    qb, gb = B_MQ // HEAD, F_MG // HEAD

    def body(q_ref, k_ref, v_ref, g_ref, y_ref):
        s = _dot(q_ref[...], k_ref[...], NT) * SCALE
        m = jnp.max(s, axis=-1, keepdims=True)
        e = jnp.exp(s - m)
        p = e / jnp.sum(e, axis=-1, keepdims=True)
        o = _dot(p.astype(_BF), v_ref[...], NN)
        gv = g_ref[...]
        y_ref[...] = (o * (gv * _sigmoid(gv))).astype(_BF)

    return pl.pallas_call(
        body, grid=(S // tq, N_MEM),
        in_specs=[pl.BlockSpec((tq, HEAD), lambda i, h: (i, qb + h)),
                  pl.BlockSpec((M, HEAD), lambda i, h: (0, h)),
                  pl.BlockSpec((M, HEAD), lambda i, h: (0, N_MEM + h)),
                  pl.BlockSpec((tq, HEAD), lambda i, h: (i, gb + h))],
        out_specs=pl.BlockSpec((tq, HEAD), lambda i, h: (i, h)),
        out_shape=SDS((S, MEM_W), _BF), compiler_params=_cp("parallel", "parallel"), name="mem_attn_fwd")(p_b, mkv, mkv, p_f)


def _mem_attn_bwd(p_b, p_f, mkv, dy):
    S = p_b.shape[0]
    M = mkv.shape[0]
    tq = min(T_ATT, S)
    qb, gb, yb = B_MQ // HEAD, F_MG // HEAD, (CONV_W + FOX_W) // HEAD

    def body(q_ref, k_ref, v_ref, g_ref, dy_ref, dq_ref, dg_ref, dk_ref, dv_ref):
        i = pl.program_id(1)
        q, k, v = q_ref[...], k_ref[...], v_ref[...]
        s = _dot(q, k, NT) * SCALE
        m = jnp.max(s, axis=-1, keepdims=True)
        e = jnp.exp(s - m)
        p = e / jnp.sum(e, axis=-1, keepdims=True)
        pb = p.astype(_BF)
        o = _dot(pb, v, NN)
        gv = g_ref[...]
        sg = _sigmoid(gv)
        dyv = dy_ref[...]
        do = dyv * (gv * sg)
        dg_ref[...] = (dyv * o * _dsilu(gv, sg)).astype(_BF)
        dob = do.astype(_BF)
        dp = _dot(dob, v, NT)
        ds = p * (dp - jnp.sum(dp * p, axis=-1, keepdims=True)) * SCALE
        dsb = ds.astype(_BF)
        dq_ref[...] = _dot(dsb, k, NN).astype(_BF)
        dk = _dot(dsb, q, TN)
        dv = _dot(pb, dob, TN)

        @pl.when(i == 0)
        def _():
            dk_ref[...] = dk
            dv_ref[...] = dv

        @pl.when(i > 0)
        def _():
            dk_ref[...] += dk
            dv_ref[...] += dv

    tile = pl.BlockSpec((tq, HEAD), lambda h, i: (i, h))
    kv = pl.BlockSpec((M, HEAD), lambda h, i: (0, h))
    return pl.pallas_call(
        body, grid=(N_MEM, S // tq),
        in_specs=[pl.BlockSpec((tq, HEAD), lambda h, i: (i, qb + h)), kv,
                  pl.BlockSpec((M, HEAD), lambda h, i: (0, N_MEM + h)),
                  pl.BlockSpec((tq, HEAD), lambda h, i: (i, gb + h)),
                  pl.BlockSpec((tq, HEAD), lambda h, i: (i, yb + h))],
        out_specs=[tile, tile, kv, kv],
        out_shape=[SDS((S, MEM_W), _BF), SDS((S, MEM_W), _BF), SDS((M, MEM_W), F32), SDS((M, MEM_W), F32)],
        compiler_params=_cp("parallel", "arbitrary"), name="mem_attn_bwd")(p_b, mkv, mkv, p_f, dy)


def _fox_cumsum(p_f, b_f_pad):
    S = p_f.shape[0]
    tr = min(256, S)
    fb = F_FL // 128

    def body(z_ref, b_ref, col_ref, row_ref, carry_ref):
        i = pl.program_id(0)

        @pl.when(i == 0)
        def _():
            carry_ref[...] = jnp.zeros_like(carry_ref)

        z = z_ref[...] + b_ref[...]
        lf = jnp.minimum(z, 0.0) - jnp.log1p(jnp.exp(-jnp.abs(z)))
        r = lax.broadcasted_iota(jnp.int32, (tr, tr), 0)
        c = lax.broadcasted_iota(jnp.int32, (tr, tr), 1)
        tri = (c <= r).astype(F32)
        cs = jnp.dot(tri, lf, precision=lax.Precision.HIGHEST, preferred_element_type=F32) + carry_ref[...]
        carry_ref[...] = cs[tr - 1:tr, :]
        cst = cs.T
        for hd in range(N_FOX):
            col_ref[hd] = cs[:, hd:hd + 1]
            row_ref[hd] = cst[hd:hd + 1, :]

    return pl.pallas_call(
        body, grid=(S // tr,),
        in_specs=[pl.BlockSpec((tr, 128), lambda i: (i, fb)), pl.BlockSpec((1, 128), lambda i: (0, 0))],
        out_specs=[pl.BlockSpec((N_FOX, tr, 1), lambda i: (0, i, 0)), pl.BlockSpec((N_FOX, 1, tr), lambda i: (0, 0, i))],
        out_shape=[SDS((N_FOX, S, 1), F32), SDS((N_FOX, 1, S), F32)], scratch_shapes=[pltpu.VMEM((1, 128), F32)],
        compiler_params=_cp("arbitrary"), name="fox_cumsum")(p_f, b_f_pad)


def _fox_dlogf(dr, p_f, b_f_pad):
    S = p_f.shape[0]
    tr = min(256, S)
    nb = S // tr
    fb = F_FL // 128
    wide = PART_W - F_FL

    def body(dr_ref, z_ref, b_ref, dz_ref, gb_ref, carry_ref):
        i = pl.program_id(0)

        @pl.when(i == 0)
        def _():
            carry_ref[...] = jnp.zeros_like(carry_ref)
            gb_ref[...] = jnp.zeros_like(gb_ref)

        heads = [dr_ref[hd, 0:1, :] + dr_ref[hd, 1:2, :] for hd in range(N_FOX)]
        dc = jnp.concatenate(heads + [jnp.zeros((128 - N_FOX, tr), F32)], axis=0).T
        r = lax.broadcasted_iota(jnp.int32, (tr, tr), 0)
        c = lax.broadcasted_iota(jnp.int32, (tr, tr), 1)
        tri = (c >= r).astype(F32)
        rc = jnp.dot(tri, dc, precision=lax.Precision.HIGHEST, preferred_element_type=F32) + carry_ref[...]
        carry_ref[...] = rc[0:1, :]
        z = z_ref[...] + b_ref[...]
        dz = rc * _sigmoid(-z)
        gb_ref[...] += jnp.sum(dz, axis=0, keepdims=True)
        dz_ref[...] = jnp.concatenate([dz.astype(_BF), jnp.zeros((tr, wide - 128), _BF)], axis=1)

    return pl.pallas_call(
        body, grid=(nb,),
        in_specs=[pl.BlockSpec((N_FOX, 8, tr), lambda i: (0, 0, nb - 1 - i)),
                  pl.BlockSpec((tr, 128), lambda i: (nb - 1 - i, fb)), pl.BlockSpec((1, 128), lambda i: (0, 0))],
        out_specs=[pl.BlockSpec((tr, wide), lambda i: (nb - 1 - i, 0)), pl.BlockSpec((1, 128), lambda i: (0, 0))],
        out_shape=[SDS((S, wide), _BF), SDS((1, 128), F32)], scratch_shapes=[pltpu.VMEM((1, 128), F32)],
        compiler_params=_cp("arbitrary"), name="fox_dlogf")(dr, p_f, b_f_pad)


def _fox_fwd(p_b, p_f, c_col, c_row):
    S = p_b.shape[0]
    t = min(T_ATT, S)
    nq = S // t
    wide = 2 * t
    hq = t // 2
    qb, kb, vb, gb = B_Q // HEAD, B_K // HEAD, B_V // HEAD, F_FG // HEAD
    kq = SCALE * 1.4426950408889634

    def body(q_ref, k_ref, v_ref, cc_ref, cr_ref, g_ref, o_ref, y_ref, lse_ref, va_ref, m_ref, acc_ref):
        i = pl.program_id(1)

        @pl.when(i == 0)
        def _():
            va_ref[:, 0:HEAD] = v_ref[...]
            lane = lax.broadcasted_iota(jnp.int32, (S, HEAD), 1)
            va_ref[:, HEAD:2 * HEAD] = jnp.where(lane == 0, 1.0, 0.0).astype(_BF)

        m_ref[...] = jnp.full_like(m_ref, NEG)
        acc_ref[...] = jnp.zeros_like(acc_ref)

        def block(off, width, masked):
            k = k_ref[pl.ds(off, width), :]
            va = va_ref[pl.ds(off, width), :]
            csr = cr_ref[:, pl.ds(off, width)] * (1.0 / SCALE)
            for half in range(2):
                rows = slice(half * hq, (half + 1) * hq)
                u = _dot(q_ref[rows, :], k, NT) - csr
                if masked:
                    row = lax.broadcasted_iota(jnp.int32, (hq, width), 0) + half * hq
                    col = lax.broadcasted_iota(jnp.int32, (hq, width), 1)
                    u = jnp.where(col <= row, u, NEG)
                m_old = m_ref[rows, :]
                m_new = jnp.maximum(m_old, jnp.max(u, axis=-1, keepdims=True))
                alpha = jnp.exp2((m_old - m_new) * kq)
                p = jnp.exp2((u - m_new) * kq)
                acc_ref[rows, :] = alpha * acc_ref[rows, :] + _dot(p.astype(_BF), va, NN)
                m_ref[rows, :] = m_new

        def step(j, carry):
            block(pl.multiple_of(j * wide, wide), wide, False)
            return carry

        lax.fori_loop(0, i // 2, step, 0)

        @pl.when(i % 2 == 1)
        def _():
            block(pl.multiple_of((i - 1) * t, t), t, False)

        block(pl.multiple_of(i * t, t), t, True)
        l = acc_ref[:, HEAD:HEAD + 1]
        o = acc_ref[:, 0:HEAD] / l
        gv = g_ref[...]
        o_ref[...] = o.astype(_BF)
        y_ref[...] = (o * (gv * _sigmoid(gv))).astype(_BF)
        lse_ref[...] = cc_ref[...] + SCALE * m_ref[...] + jnp.log(l)

    tile = pl.BlockSpec((t, HEAD), lambda h, i: (i, h))
    return pl.pallas_call(
        body, grid=(N_FOX, nq),
        in_specs=[pl.BlockSpec((t, HEAD), lambda h, i: (i, qb + h)),
                  pl.BlockSpec((S, HEAD), lambda h, i: (0, kb + h)),
                  pl.BlockSpec((S, HEAD), lambda h, i: (0, vb + h)),
                  pl.BlockSpec((None, t, 1), lambda h, i: (h, i, 0)),
                  pl.BlockSpec((None, 1, S), lambda h, i: (h, 0, 0)),
                  pl.BlockSpec((t, HEAD), lambda h, i: (i, gb + h))],
        out_specs=[tile, tile, pl.BlockSpec((None, t, 1), lambda h, i: (h, i, 0))],
        out_shape=[SDS((S, FOX_W), _BF), SDS((S, FOX_W), _BF), SDS((N_FOX, S, 1), F32)],
        scratch_shapes=[pltpu.VMEM((S, 2 * HEAD), _BF), pltpu.VMEM((t, 1), F32), pltpu.VMEM((t, 2 * HEAD), F32)],
        compiler_params=_cp("parallel", "arbitrary"), name="fox_fwd")(p_b, p_b, p_b, c_col, c_row, p_f)


def _fox_bwd_prep(dy, p_f, o, lse, c_col):
    S = dy.shape[0]
    t = min(T_ATT, S)
    yb, gb = CONV_W // HEAD, F_FG // HEAD

    def body(dy_ref, g_ref, o_ref, lse_ref, cc_ref, do_ref, dg_ref, rows_ref):
        gv = g_ref[...]
        sg = _sigmoid(gv)
        dyv = dy_ref[...]
        ov = o_ref[...].astype(F32)
        do = dyv * (gv * sg)
        do_ref[...] = do.astype(_BF)
        dg_ref[...] = (dyv * ov * _dsilu(gv, sg)).astype(_BF)
        delta = jnp.sum(do * ov, axis=-1, keepdims=True)
        a = cc_ref[...] - lse_ref[...]
        lane = lax.broadcasted_iota(jnp.int32, (t, 128), 1)
        mat = jnp.where(lane == 0, a, jnp.where(lane == 1, delta, 0.0))
        rows_ref[...] = mat.T[0:8, :]

    tile = pl.BlockSpec((t, HEAD), lambda h, i: (i, h))
    col = pl.BlockSpec((None, t, 1), lambda h, i: (h, i, 0))
    return pl.pallas_call(
        body, grid=(N_FOX, S // t),
        in_specs=[pl.BlockSpec((t, HEAD), lambda h, i: (i, yb + h)), pl.BlockSpec((t, HEAD), lambda h, i: (i, gb + h)),
                  tile, col, col],
        out_specs=[tile, tile, pl.BlockSpec((None, 8, t), lambda h, i: (h, 0, i))],
        out_shape=[SDS((S, FOX_W), _BF), SDS((S, FOX_W), _BF), SDS((N_FOX, 8, S), F32)],
        compiler_params=_cp("parallel", "parallel"), name="fox_bwd_prep")(dy, p_f, o, lse, c_col)


def _fox_bwd(p_b, do, rows, c_col):
    S = p_b.shape[0]
    t = min(T_ATT, S)
    nk = S // t
    qb, kb, vb = B_Q // HEAD, B_K // HEAD, B_V // HEAD

    def body(k_ref, v_ref, q_ref, do_ref, rows_ref, cc_ref, dq_ref, dk_ref, dv_ref, dr_ref,
             dqt_ref, dka_ref, dva_ref, dca_ref, dra_ref):
        j = pl.program_id(1)

        @pl.when(j == 0)
        def _():
            dqt_ref[...] = jnp.zeros_like(dqt_ref)
            dra_ref[...] = jnp.zeros_like(dra_ref)

        k = k_ref[...]
        v = v_ref[...]
        kt = k.astype(F32).T.astype(_BF)
        cc = cc_ref[...]
        dka_ref[...] = jnp.zeros_like(dka_ref)
        dva_ref[...] = jnp.zeros_like(dva_ref)
        dca_ref[...] = jnp.zeros_like(dca_ref)

        def block(i, masked):
            off = pl.multiple_of(i * t, t)
            q = q_ref[pl.ds(off, t), :]
            dov = do_ref[pl.ds(off, t), :]
            a_row = rows_ref[0:1, pl.ds(off, t)]
            delta_row = rows_ref[1:2, pl.ds(off, t)]
            st = _dot(k, q, NT) * SCALE + (a_row - cc)
            if masked:
                srow = lax.broadcasted_iota(jnp.int32, (t, t), 0)
                tcol = lax.broadcasted_iota(jnp.int32, (t, t), 1)
                st = jnp.where(srow <= tcol, st, NEG)
            pt = jnp.exp(st)
            dva_ref[...] += _dot(pt.astype(_BF), dov, NN)
            dpt = _dot(v, dov, NT)
            dst = pt * (dpt - delta_row)
            part = dst[:, 0:128]
            for gidx in range(1, t // 128):
                part = part + dst[:, gidx * 128:(gidx + 1) * 128]
            dca_ref[...] += part
            dra_ref[0:1, pl.ds(off, t)] += jnp.sum(dst, axis=0, keepdims=True)
            dsb = dst.astype(_BF)
            dka_ref[...] += _dot(dsb, q, NN)
            dqt_ref[:, pl.ds(off, t)] += _dot(kt, dsb, NN)

        block(j, True)

        def step(i, carry):
            block(i, False)
            return carry

        lax.fori_loop(j + 1, nk, step, 0)
        dk_ref[...] = (dka_ref[...] * SCALE).astype(_BF)
        dv_ref[...] = dva_ref[...].astype(_BF)
        dra_ref[1:2, pl.ds(pl.multiple_of(j * t, t), t)] = -jnp.sum(dca_ref[...].T, axis=0, keepdims=True)

        @pl.when(j == nk - 1)
        def _():
            dr_ref[...] = dra_ref[...]
            for ci in range(nk):
                dq_ref[ci * t:(ci + 1) * t, :] = (dqt_ref[:, ci * t:(ci + 1) * t].T * SCALE).astype(_BF)

    tile = pl.BlockSpec((t, HEAD), lambda h, j: (j, h))
    return pl.pallas_call(
        body, grid=(N_FOX, nk),
        in_specs=[pl.BlockSpec((t, HEAD), lambda h, j: (j, kb + h)),
                  pl.BlockSpec((t, HEAD), lambda h, j: (j, vb + h)),
                  pl.BlockSpec((S, HEAD), lambda h, j: (0, qb + h)),
                  pl.BlockSpec((S, HEAD), lambda h, j: (0, h)),
                  pl.BlockSpec((None, 8, S), lambda h, j: (h, 0, 0)),
                  pl.BlockSpec((None, t, 1), lambda h, j: (h, j, 0))],
        out_specs=[pl.BlockSpec((S, HEAD), lambda h, j: (0, h)), tile, tile,
                   pl.BlockSpec((None, 8, S), lambda h, j: (h, 0, 0))],
        out_shape=[SDS((S, FOX_W), _BF), SDS((S, FOX_W), _BF), SDS((S, FOX_W), _BF), SDS((N_FOX, 8, S), F32)],
        scratch_shapes=[pltpu.VMEM((HEAD, S), F32), pltpu.VMEM((t, HEAD), F32), pltpu.VMEM((t, HEAD), F32),
                        pltpu.VMEM((t, 128), F32), pltpu.VMEM((8, S), F32)],
        compiler_params=_cp("parallel", "arbitrary"), name="fox_bwd")(p_b, p_b, p_b, do, rows, c_col)


CHUNK = 32


def _conv_taps(ext_ref, w_ref, first, out_fn, n_rows):
    def chunk(c, carry):
        r0 = pl.multiple_of(c * CHUNK, CHUNK)
        win = ext_ref[pl.ds(r0, 2 * CHUNK), :]
        acc = jnp.zeros((CHUNK, CONV_W), F32)
        for k in range(CONV_K):
            f = first(k)
            acc = acc + w_ref[k:k + 1, :] * win[f:f + CHUNK, :]
        out_fn(r0, acc)
        return carry

    lax.fori_loop(0, n_rows // CHUNK, chunk, 0)


def _conv_fwd(p_f, conv_w, conv_b, ln_g, ln_b, w_pw):
    S = p_f.shape[0]
    tc = min(T_CONV, S)
    hb = tc // HALO

    def body(a_ref, b_ref, gc_ref, ap_ref, bp_ref, w_ref, cb_ref, lg_ref, lb_ref, pw_ref, y_ref, u1_ref, ext_ref):
        i = pl.program_id(0)
        prev = ap_ref[...] * _sigmoid(bp_ref[...])
        ext_ref[0:HALO, :] = jnp.where(i > 0, prev, 0.0)
        ext_ref[HALO:HALO + tc, :] = a_ref[...] * _sigmoid(b_ref[...])
        cb = cb_ref[...]

        def put(r0, acc):
            u1_ref[pl.ds(r0, CHUNK), :] = acc + cb

        _conv_taps(ext_ref, w_ref, lambda k: HALO - (CONV_K - 1) + k, put, tc)
        u1 = u1_ref[...]
        mu = jnp.mean(u1, axis=-1, keepdims=True)
        d = u1 - mu
        rstd = lax.rsqrt(jnp.mean(d * d, axis=-1, keepdims=True) + EPS)
        u2 = d * rstd * lg_ref[...] + lb_ref[...]
        u3 = u2 * _sigmoid(u2)
        pw = _dot(u3.astype(_BF), pw_ref[...], NN)
        gc = gc_ref[...]
        y_ref[...] = (pw * (gc * _sigmoid(gc))).astype(_BF)

    blk = lambda cb_: pl.BlockSpec((tc, CONV_W), lambda i: (i, cb_))
    halo = lambda cb_: pl.BlockSpec((HALO, CONV_W), lambda i: (jnp.maximum(i * hb - 1, 0), cb_))
    vec = pl.BlockSpec((1, CONV_W), lambda i: (0, 0))
    return pl.pallas_call(
        body, grid=(S // tc,),
        in_specs=[blk(0), blk(1), blk(2), halo(0), halo(1), pl.BlockSpec((CONV_K, CONV_W), lambda i: (0, 0)),
                  vec, vec, vec, pl.BlockSpec((CONV_W, CONV_W), lambda i: (0, 0))],
        out_specs=[pl.BlockSpec((tc, CONV_W), lambda i: (i, 0)), pl.BlockSpec((tc, CONV_W), lambda i: (i, 0))],
        out_shape=[SDS((S, CONV_W), _BF), SDS((S, CONV_W), F32)],
        scratch_shapes=[pltpu.VMEM((tc + 2 * HALO, CONV_W), F32)],
        compiler_params=_cp("parallel"), name="conv_fwd")(p_f, p_f, p_f, p_f, p_f, conv_w, conv_b, ln_g, ln_b, w_pw)


def _conv_bwd1(u1, p_f, dy, ln_g, ln_b, w_pw):
    S = u1.shape[0]
    tc = min(T_CONV, S)

    def body(u1_ref, gc_ref, dy_ref, lg_ref, lb_ref, pw_ref, du1_ref, dgc_ref, gpw_ref, glg_ref, glb_ref, gcb_ref):
        i = pl.program_id(0)
        u1v = u1_ref[...]
        mu = jnp.mean(u1v, axis=-1, keepdims=True)
        d = u1v - mu
        rstd = lax.rsqrt(jnp.mean(d * d, axis=-1, keepdims=True) + EPS)
        xh = d * rstd
        lg = lg_ref[...]
        u2 = xh * lg + lb_ref[...]
        sg2 = _sigmoid(u2)
        u3b = (u2 * sg2).astype(_BF)
        w = pw_ref[...]
        pw = _dot(u3b, w, NN)
        gc = gc_ref[...]
        sgc = _sigmoid(gc)
        dyv = dy_ref[...]
        dpw = (dyv * (gc * sgc)).astype(_BF)
        dgc_ref[...] = (dyv * pw * _dsilu(gc, sgc)).astype(_BF)
        gpw = _dot(u3b, dpw, TN)
        du2 = _dot(dpw, w, NT) * _dsilu(u2, sg2)
        glg = jnp.sum(du2 * xh, axis=0, keepdims=True)
        glb = jnp.sum(du2, axis=0, keepdims=True)
        dxh = du2 * lg
        du1 = rstd * (dxh - jnp.mean(dxh, axis=-1, keepdims=True) - xh * jnp.mean(dxh * xh, axis=-1, keepdims=True))
        du1_ref[...] = du1
        gcb = jnp.sum(du1, axis=0, keepdims=True)

        @pl.when(i == 0)
        def _():
            gpw_ref[...] = gpw
            glg_ref[...] = glg
            glb_ref[...] = glb
            gcb_ref[...] = gcb

        @pl.when(i > 0)
        def _():
            gpw_ref[...] += gpw
            glg_ref[...] += glg
            glb_ref[...] += glb
            gcb_ref[...] += gcb

    row = pl.BlockSpec((tc, CONV_W), lambda i: (i, 0))
    vec = pl.BlockSpec((1, CONV_W), lambda i: (0, 0))
    sq = pl.BlockSpec((CONV_W, CONV_W), lambda i: (0, 0))
    return pl.pallas_call(
        body, grid=(S // tc,),
        in_specs=[row, pl.BlockSpec((tc, CONV_W), lambda i: (i, F_GC // CONV_W)), row, vec, vec, sq],
        out_specs=[row, row, sq, vec, vec, vec],
        out_shape=[SDS((S, CONV_W), F32), SDS((S, CONV_W), _BF), SDS((CONV_W, CONV_W), F32),
                   SDS((1, CONV_W), F32), SDS((1, CONV_W), F32), SDS((1, CONV_W), F32)],
        compiler_params=_cp("arbitrary"), name="conv_bwd1")(u1, p_f, dy, ln_g, ln_b, w_pw)


def _conv_bwd2(du1, p_f, conv_w):
    S = du1.shape[0]
    tc = min(T_CONV, S)
    hb = tc // HALO
    nblk = S // tc
    last_halo = S // HALO - 1

    def body(d_ref, dn_ref, a_ref, b_ref, ap_ref, bp_ref, w_ref, da_ref, db_ref, gw_ref, ext_ref, dext_ref, du0_ref):
        i = pl.program_id(0)
        av = a_ref[...]
        sb = _sigmoid(b_ref[...])
        prev = ap_ref[...] * _sigmoid(bp_ref[...])
        ext_ref[0:HALO, :] = jnp.where(i > 0, prev, 0.0)
        ext_ref[HALO:HALO + tc, :] = av * sb
        dext_ref[0:tc, :] = d_ref[...]
        dext_ref[tc:tc + HALO, :] = jnp.where(i < nblk - 1, dn_ref[...], 0.0)

        def put(r0, acc):
            du0_ref[pl.ds(r0, CHUNK), :] = acc

        _conv_taps(dext_ref, w_ref, lambda k: CONV_K - 1 - k, put, tc)
        du0 = du0_ref[...]
        da_ref[...] = (du0 * sb).astype(_BF)
        db_ref[...] = (du0 * av * sb * (1.0 - sb)).astype(_BF)

        def chunk(c, carry):
            r0 = pl.multiple_of(c * CHUNK, CHUNK)
            win = ext_ref[pl.ds(r0, 2 * CHUNK), :]
            dv = dext_ref[pl.ds(r0, CHUNK), :]
            rows = [jnp.sum(dv * win[HALO - (CONV_K - 1) + k:HALO - (CONV_K - 1) + k + CHUNK, :], axis=0, keepdims=True)
                    for k in range(CONV_K)]
            rows.append(jnp.zeros((1, CONV_W), F32))
            return carry + jnp.concatenate(rows, axis=0)

        gw = lax.fori_loop(0, tc // CHUNK, chunk, jnp.zeros((CONV_K + 1, CONV_W), F32))

        @pl.when(i == 0)
        def _():
            gw_ref[...] = gw

        @pl.when(i > 0)
        def _():
            gw_ref[...] += gw

    row = pl.BlockSpec((tc, CONV_W), lambda i: (i, 0))
    blk = lambda cb_: pl.BlockSpec((tc, CONV_W), lambda i: (i, cb_))
    halo = lambda cb_: pl.BlockSpec((HALO, CONV_W), lambda i: (jnp.maximum(i * hb - 1, 0), cb_))
    nxt = pl.BlockSpec((HALO, CONV_W), lambda i: (jnp.minimum((i + 1) * hb, last_halo), 0))
    return pl.pallas_call(
        body, grid=(nblk,),
        in_specs=[row, nxt, blk(0), blk(1), halo(0), halo(1), pl.BlockSpec((CONV_K, CONV_W), lambda i: (0, 0))],
        out_specs=[row, row, pl.BlockSpec((CONV_K + 1, CONV_W), lambda i: (0, 0))],
        out_shape=[SDS((S, CONV_W), _BF), SDS((S, CONV_W), _BF), SDS((CONV_K + 1, CONV_W), F32)],
        scratch_shapes=[pltpu.VMEM((tc + 2 * HALO, CONV_W), F32), pltpu.VMEM((tc + 2 * HALO, CONV_W), F32),
                        pltpu.VMEM((tc, CONV_W), F32)],
        compiler_params=_cp("arbitrary"), name="conv_bwd2")(du1, du1, p_f, p_f, p_f, p_f, conv_w)


def _exchange(srcs, scatter, name):
    n = len(srcs)
    out_shape = [SDS((N_DEV,) + (s.shape[1:] if sc else s.shape), s.dtype) for s, sc in zip(srcs, scatter)]

    def body(*refs):
        src_refs, dst_refs = refs[:n], refs[n:2 * n]
        send_sems, recv_sems, local_sems = refs[2 * n:]
        x, y, c = lax.axis_index("x"), lax.axis_index("y"), lax.axis_index("c")
        me = 4 * x + 2 * y + c
        copies = []
        for a in range(n):
            for f in range(1, N_DEV):
                px = 1 - x if f & 4 else x
                py = 1 - y if f & 2 else y
                pc = 1 - c if f & 1 else c
                peer = 4 * px + 2 * py + pc
                src = src_refs[a].at[peer] if scatter[a] else src_refs[a]
                cp = pltpu.make_async_remote_copy(
                    src_ref=src, dst_ref=dst_refs[a].at[me], send_sem=send_sems.at[a, f - 1],
                    recv_sem=recv_sems.at[a, f - 1], device_id=(px, py, pc), device_id_type=MESH)
                cp.start()
                copies.append(cp)
            own = src_refs[a].at[me] if scatter[a] else src_refs[a]
            lc = pltpu.make_async_copy(own, dst_refs[a].at[me], local_sems.at[a])
            lc.start()
            copies.append(lc)
        for cp in copies:
            cp.wait()

    anyspec = pl.BlockSpec(memory_space=pl.ANY)
    return pl.pallas_call(
        body, in_specs=[anyspec] * n, out_specs=[anyspec] * n, out_shape=out_shape,
        scratch_shapes=[pltpu.SemaphoreType.DMA((n, N_DEV - 1)), pltpu.SemaphoreType.DMA((n, N_DEV - 1)),
                        pltpu.SemaphoreType.DMA((n,))],
        name=name)(*srcs)


def _flip_peer(f, x, y, c):
    return (1 - x if f & 4 else x, 1 - y if f & 2 else y, 1 - c if f & 1 else c)


def _scatter_start(srcs):
    n = len(srcs)
    lands = [lax.empty((N_DEV - 1,) + s.shape[1:], s.dtype) for s in srcs]

    def body(*refs):
        src_refs, land_refs = refs[:n], refs[n:2 * n]
        send_sems, recv_sems = refs[2 * n:3 * n], refs[3 * n:4 * n]
        token = refs[-1]
        x, y, c = lax.axis_index("x"), lax.axis_index("y"), lax.axis_index("c")
        for a in range(n):
            for f in range(1, N_DEV):
                px, py, pc = _flip_peer(f, x, y, c)
                pltpu.make_async_remote_copy(
                    src_ref=src_refs[a].at[4 * px + 2 * py + pc], dst_ref=land_refs[a].at[f - 1],
                    send_sem=send_sems[a], recv_sem=recv_sems[a], device_id=(px, py, pc), device_id_type=MESH).start()
        token[...] = jnp.zeros_like(token)

    hbm = pl.BlockSpec(memory_space=pltpu.HBM)
    sem = pl.BlockSpec(memory_space=pltpu.SEMAPHORE)
    bufs = [pltpu.with_memory_space_constraint(b, pltpu.HBM) for b in list(srcs) + lands]
    out = pl.pallas_call(
        body, name="scatter_start",
        out_shape=(*[pltpu.SemaphoreType.DMA(())] * (2 * n), *[pltpu.HBM(b.shape, b.dtype) for b in bufs],
                   SDS((8, 128), F32)),
        in_specs=[hbm] * (2 * n),
        out_specs=(*[sem] * (2 * n), *[hbm] * (2 * n), pl.BlockSpec(memory_space=pltpu.VMEM)),
        input_output_aliases={i: 2 * n + i for i in range(2 * n)},
        compiler_params=pltpu.CompilerParams(has_side_effects=pltpu.SideEffectType.DATAFLOW_SIDE_EFFECTING))(*bufs)
    return list(out[:2 * n]), list(out[2 * n:3 * n]), list(out[3 * n:4 * n]), out[-1]


def _scatter_wait(sems, srcs, lands, after):
    n = len(srcs)

    def body(*refs):
        src_refs, land_refs = refs[:n], refs[n:2 * n]
        send_sems, recv_sems = refs[2 * n:3 * n], refs[3 * n:4 * n]
        x, y, c = lax.axis_index("x"), lax.axis_index("y"), lax.axis_index("c")
        for a in range(n):
            seven = pltpu.make_async_remote_copy(
                src_ref=src_refs[a].at[pl.ds(0, N_DEV - 1)], dst_ref=land_refs[a], send_sem=send_sems[a],
                recv_sem=recv_sems[a], device_id=(x, y, c), device_id_type=MESH)
            seven.wait_send()
            seven.wait_recv()

    hbm = pl.BlockSpec(memory_space=pltpu.HBM)
    sem = pl.BlockSpec(memory_space=pltpu.SEMAPHORE)
    bufs = list(srcs) + list(lands)
    out = pl.pallas_call(
        body, name="scatter_wait", out_shape=tuple(pltpu.HBM(b.shape, b.dtype) for b in bufs),
        in_specs=[hbm] * (2 * n) + [sem] * (2 * n) + [pl.BlockSpec(memory_space=pl.ANY)],
        out_specs=tuple([hbm] * (2 * n)), input_output_aliases={i: i for i in range(2 * n)},
        compiler_params=pltpu.CompilerParams(has_side_effects=pltpu.SideEffectType.DATAFLOW_SIDE_EFFECTING))(
            *bufs, *sems, after)
    return list(out[:n]), list(out[n:])


def _gather_two_level(srcs, name):
    n = len(srcs)
    out_shape = [SDS((N_DEV,) + s.shape, s.dtype) for s in srcs]

    def body(*refs):
        src_refs, dst_refs = refs[:n], refs[n:2 * n]
        send_sems, recv_sems, local_sems = refs[2 * n:]
        x, y, c = lax.axis_index("x"), lax.axis_index("y"), lax.axis_index("c")
        sibling = (x, y, 1 - c)
        chips = [(1 - x, y), (x, 1 - y), (1 - x, 1 - y)]

        def slot(a, px, py, pc):
            return dst_refs[a].at[4 * px + 2 * py + pc]

        def copy(a, k, block, to, src=None):
            return pltpu.make_async_remote_copy(
                src_ref=slot(a, *block) if src is None else src, dst_ref=slot(a, *block),
                send_sem=send_sems.at[a, k], recv_sem=recv_sems.at[a, k], device_id=to, device_id_type=MESH)

        own, sends = [], []
        for a in range(n):
            mine = pltpu.make_async_copy(src_refs[a], slot(a, x, y, c), local_sems.at[a])
            mine.start()
            own.append(mine)
            first = [copy(a, 1 + j, (x, y, c), (*chip, c), src=src_refs[a]) for j, chip in enumerate(chips)]
            first.append(copy(a, 0, (x, y, c), sibling, src=src_refs[a]))
            for cp in first:
                cp.start()
            sends += first
        for a in range(n):
            for j, chip in enumerate(chips):
                copy(a, 1 + j, (*chip, c), (x, y, c)).wait_recv()
                fwd = copy(a, 4 + j, (*chip, c), sibling)
                fwd.start()
                sends.append(fwd)
        for a in range(n):
            copy(a, 0, (x, y, 1 - c), (x, y, c)).wait_recv()
            for j, chip in enumerate(chips):
                copy(a, 4 + j, (*chip, 1 - c), (x, y, c)).wait_recv()
        for cp in sends:
            cp.wait_send()
        for cp in own:
            cp.wait()

    anyspec = pl.BlockSpec(memory_space=pl.ANY)
    return pl.pallas_call(
        body, in_specs=[anyspec] * n, out_specs=[anyspec] * n, out_shape=out_shape,
        scratch_shapes=[pltpu.SemaphoreType.DMA((n, N_DEV - 1)), pltpu.SemaphoreType.DMA((n, N_DEV - 1)),
                        pltpu.SemaphoreType.DMA((n,))],
        name=name)(*srcs)


def _adamw(parts, w, m, v, name, tr=256, own=None):
    lead = w.ndim == 3
    R, C = w.shape[-2:]
    tr = tr if R % tr == 0 else R
    n_parts = parts.shape[0]
    first = [] if own is None else [own]

    def body(*refs):
        p_ref, w_ref, m_ref, v_ref, g_ref, d_ref, nm_ref, nv_ref = refs[len(first):]
        terms = [r[0] for r in refs[:len(first)]] + [p_ref[dev] for dev in range(n_parts)]
        g = terms[0].astype(F32)
        for term in terms[1:]:
            g = g + term.astype(F32)
        mn = ADAM_B1 * m_ref[...] + (1.0 - ADAM_B1) * g
        vn = ADAM_B2 * v_ref[...] + (1.0 - ADAM_B2) * (g * g)
        m_hat = mn / (1.0 - ADAM_B1 ** ADAM_STEP)
        v_hat = vn / (1.0 - ADAM_B2 ** ADAM_STEP)
        g_ref[...] = g
        d_ref[...] = -ADAM_LR * (m_hat / (jnp.sqrt(v_hat) + ADAM_EPS) + ADAM_WD * w_ref[...])
        nm_ref[...] = mn
        nv_ref[...] = vn

    blk = pl.BlockSpec((None, tr, C), lambda i: (0, i, 0)) if lead else pl.BlockSpec((tr, C), lambda i: (i, 0))
    return pl.pallas_call(
        body, grid=(R // tr,),
        in_specs=[pl.BlockSpec((1, tr, C), lambda i: (0, i, 0))] * len(first)
        + [pl.BlockSpec((n_parts, tr, C), lambda i: (0, i, 0)), blk, blk, blk],
        out_specs=[blk] * 4, out_shape=[SDS(w.shape, F32)] * 4,
        compiler_params=_cp("parallel"), name=name)(*first, parts, w, m, v)


def _split_w_in(w_cat, D):
    z = jnp.zeros((D, PART_W - F_FL - N_FOX), w_cat.dtype)
    w_f = jnp.concatenate([w_cat[:, O_A:O_Q], w_cat[:, O_FG:O_MQ], w_cat[:, O_MG:D_IN], w_cat[:, O_FL:O_FG], z], axis=1)
    w_b = jnp.concatenate([w_cat[:, O_Q:O_FL], w_cat[:, O_MQ:O_MG]], axis=1)
    return w_f, w_b


def _merge_w_in(g_f, g_b):
    return jnp.concatenate([g_f[:, F_A:F_FG], g_b[:, B_Q:B_MQ], g_f[:, F_FL:F_FL + N_FOX], g_f[:, F_FG:F_MG],
                            g_b[:, B_MQ:PART_W], g_f[:, F_MG:F_FL]], axis=1)


def _pack_small(norm, mem_norm, final, conv_b, ln_g, ln_b, b_f, extra, D):
    width = max(D, PACK_W)
    row3 = jnp.concatenate([conv_b.reshape(-1), ln_g.reshape(-1), ln_b.reshape(-1), b_f.reshape(-1)])
    rows = [norm.reshape(-1), mem_norm.reshape(-1), final.reshape(-1), row3, extra.reshape(-1)]
    rows = [jnp.pad(r, (0, width - r.shape[0])) for r in rows]
    return jnp.concatenate([jnp.stack(rows), jnp.zeros((3, width), F32)], axis=0)


def _unpack_small(p, D):
    c = CONV_W
    return dict(norm_g=p[0:1, :D], mem_norm_g=p[1:2, :D], final_g=p[2, :D], conv_b=p[3:4, 0:c],
                conv_ln_g=p[3:4, c:2 * c], conv_ln_b=p[3:4, 2 * c:3 * c], b_f=p[3:4, 3 * c:3 * c + N_FOX])


def kernel(x, mem, norm_g, mem_norm_g, w_in, b_f, conv_w, conv_b, conv_ln_g, conv_ln_b, w_conv_pw, w_mem_kv, w_out, final_g, loss_target, m_norm_g, m_mem_norm_g, m_w_in, m_b_f, m_conv_w, m_conv_b, m_conv_ln_g, m_conv_ln_b, m_w_conv_pw, m_w_mem_kv, m_w_out, m_final_g, v_norm_g, v_mem_norm_g, v_w_in, v_b_f, v_conv_w, v_conv_b, v_conv_ln_g, v_conv_ln_b, v_w_conv_pw, v_w_mem_kv, v_w_out, v_final_g):
    S, D = x.shape[1], x.shape[2]
    M = mem.shape[1]
    xs, ms, tgt = x[0], mem[0], loss_target[0]
    cols = w_in.shape[2]

    g_in, g_cw, g_pw, g_kv, g_out = _gather_two_level(
        [w_in[0].astype(_BF), conv_w[0], w_conv_pw[0].astype(_BF), w_mem_kv[0].astype(_BF), w_out[0].astype(_BF)],
        "gather_weights")
    w_f, w_b = _split_w_in(jnp.transpose(g_in, (1, 0, 2)).reshape(D, N_DEV * cols), D)
    conv_w_full = jnp.transpose(g_cw, (1, 0, 2)).reshape(CONV_K, CONV_W)
    w_pw_full = g_pw.reshape(CONV_W, CONV_W)
    w_kv_full = g_kv.reshape(D, 2 * MEM_W)
    w_out_full = g_out.reshape(CONV_W + FOX_W + MEM_W, D)
    b_f_pad = jnp.pad(b_f, ((0, 0), (0, 128 - N_FOX)))

    h, r1 = _rms_fwd(xs, norm_g)
    p_f = _matmul(h, w_f, "nn", F32, "proj_f")
    p_b = _matmul(h, w_b, "nn", _BF, "proj_b")
    mkv, mhat = _mem_kv_fwd(ms, mem_norm_g, w_kv_full)
    y_conv, u1 = _conv_fwd(p_f, conv_w_full, conv_b, conv_ln_g, conv_ln_b, w_pw_full)
    c_col, c_row = _fox_cumsum(p_f, b_f_pad)
    o_fox, y_fox, lse = _fox_fwd(p_b, p_f, c_col, c_row)
    y_mem = _mem_attn_fwd(p_b, p_f, mkv)
    y = jnp.concatenate([y_conv, y_fox, y_mem], axis=1)
    z = _matmul(y, w_out_full, "nn", F32, "out_proj")
    dx2, g_final, loss_part = _head_loss(xs, z, tgt, final_g.reshape(1, D))

    dy = _matmul(dx2, w_out_full, "nt", F32, "d_y")
    gw_out = _matmul(y, dx2, "tn", _BF, "gw_out")
    d_mq, d_mg, d_mk, d_mv = _mem_attn_bwd(p_b, p_f, mkv, dy)
    gw_kv, g_mem_norm = _mem_kv_bwd(jnp.concatenate([d_mk, d_mv], axis=1), mhat, mem_norm_g, w_kv_full)
    d_o, d_fg, rows = _fox_bwd_prep(dy, p_f, o_fox, lse, c_col)
    d_q, d_k, d_v, dr = _fox_bwd(p_b, d_o, rows, c_col)
    d_fl, g_bf = _fox_dlogf(dr, p_f, b_f_pad)
    du1, d_gc, gw_pw, g_ln_g, g_ln_b, g_cb = _conv_bwd1(u1, p_f, dy, conv_ln_g, conv_ln_b, w_pw_full)
    d_a, d_b, gw_cw = _conv_bwd2(du1, p_f, conv_w_full)
    dp = jnp.concatenate([d_a, d_b, d_gc, d_fg, d_mg, d_fl, d_q, d_k, d_v, d_mq], axis=1)
    w_all = jnp.concatenate([w_f, w_b], axis=1)
    gw_all = _matmul(h, dp, "tn", _BF, "gw_in")

    gw_in_cat = _merge_w_in(gw_all[:, :PART_W], gw_all[:, PART_W:])
    send_in = jnp.transpose(gw_in_cat.reshape(D, N_DEV, cols), (1, 0, 2))
    send_cw = jnp.transpose(gw_cw[:CONV_K].reshape(CONV_K, N_DEV, CONV_W // N_DEV), (1, 0, 2))
    send_pw = gw_pw.astype(_BF).reshape(N_DEV, CONV_W // N_DEV, CONV_W)
    send_kv = gw_kv.reshape(N_DEV, D // N_DEV, 2 * MEM_W)
    send_out = gw_out.reshape(N_DEV, (CONV_W + FOX_W + MEM_W) // N_DEV, D)
    sems, sent, lands, token = _scatter_start([send_in, send_cw, send_pw, send_kv, send_out])
    dh = _matmul(dp, w_all, "nt", F32, "d_h", after=token)
    grad_x, g_norm = _rms_bwd(xs, r1, dh, dx2, norm_g)
    sent, lands = _scatter_wait(sems, sent, lands, grad_x)
    me = 4 * lax.axis_index("x") + 2 * lax.axis_index("y") + lax.axis_index("c")
    own = [lax.dynamic_index_in_dim(s_, me, 0, keepdims=True) for s_ in sent]

    small = _pack_small(g_norm, g_mem_norm, g_final, g_cb, g_ln_g, g_ln_b, g_bf[:, :N_FOX], loss_part[0, 0:1], D)
    r_small, = _exchange([small], [False], "exchange_small")

    res = {}
    res["w_in"] = _adamw(lands[0], w_in, m_w_in, v_w_in, "adamw_w_in", own=own[0])
    res["conv_w"] = _adamw(lands[1], conv_w, m_conv_w, v_conv_w, "adamw_conv_w", own=own[1])
    res["w_conv_pw"] = _adamw(lands[2], w_conv_pw, m_w_conv_pw, v_w_conv_pw, "adamw_w_pw", own=own[2])
    res["w_mem_kv"] = _adamw(lands[3], w_mem_kv, m_w_mem_kv, v_w_mem_kv, "adamw_w_kv", own=own[3])
    res["w_out"] = _adamw(lands[4], w_out, m_w_out, v_w_out, "adamw_w_out", own=own[4])
    zero = jnp.zeros((1,), F32)
    pk = lambda a, b_, c_, d_, e, f_, g_: _pack_small(a, b_, c_, d_, e, f_, g_, zero, D)
    sm = _adamw(r_small,
                pk(norm_g, mem_norm_g, final_g, conv_b, conv_ln_g, conv_ln_b, b_f),
                pk(m_norm_g, m_mem_norm_g, m_final_g, m_conv_b, m_conv_ln_g, m_conv_ln_b, m_b_f),
                pk(v_norm_g, v_mem_norm_g, v_final_g, v_conv_b, v_conv_ln_g, v_conv_ln_b, v_b_f), "adamw_small")
    loss = sm[0][4, 0]
    small_out = [_unpack_small(a, D) for a in sm]
    names = ["norm_g", "mem_norm_g", "w_in", "b_f", "conv_w", "conv_b", "conv_ln_g", "conv_ln_b", "w_conv_pw",
             "w_mem_kv", "w_out", "final_g"]
    outs = [loss, grad_x[None]]
    for kind in range(4):
        for nme in names:
            outs.append(res[nme][kind] if nme in res else small_out[kind][nme])
    return tuple(outs)
```

```python
import functools

import jax
import jax.numpy as jnp
from jax import lax
from jax.experimental import pallas as pl
from jax.experimental.pallas import tpu as pltpu

F32 = jnp.float32
_BF = jnp.bfloat16
SDS = jax.ShapeDtypeStruct
MESH = pl.DeviceIdType.MESH

N_DEV = 8
HEAD = 128
N_FOX = 8
N_MEM = 4
CONV_W = 512
CONV_K = 31
FOX_W = N_FOX * HEAD
MEM_W = N_MEM * HEAD
D_IN = 3 * CONV_W + 4 * FOX_W + N_FOX + 2 * MEM_W
EPS = 1e-6
SCALE = HEAD ** -0.5
NEG = -1e30

ADAM_LR = 0.001
ADAM_B1 = 0.9
ADAM_B2 = 0.999
ADAM_EPS = 1e-08
ADAM_WD = 0.01
ADAM_STEP = 10

PART_W = 3584
F_A, F_B, F_GC, F_FG, F_MG, F_FL = 0, 512, 1024, 1536, 2560, 3072
B_Q, B_K, B_V, B_MQ = 0, 1024, 2048, 3072
O_A, O_B, O_GC = 0, 512, 1024
O_Q, O_K, O_V = 1536, 2560, 3584
O_FL, O_FG, O_MQ, O_MG = 4608, 4616, 5640, 6152

HALO = 32
T_ATT = 512
T_ROW = 256
T_CONV = 512
VMEM_LIMIT = 56 * 1024 * 1024
PACK_W = 2048


def _cp(*sem):
    return pltpu.CompilerParams(dimension_semantics=sem, vmem_limit_bytes=VMEM_LIMIT)


def _sigmoid(x):
    return jax.nn.sigmoid(x)


def _dsilu(x, sg):
    return sg * (1.0 + x * (1.0 - sg))


def _dot(a, b, dims):
    return lax.dot_general(a, b, (dims, ((), ())), preferred_element_type=F32)


NN = ((1,), (0,))
NT = ((1,), (1,))
TN = ((0,), (0,))


def _pick(n, pref):
    if n <= pref:
        return n
    t = pref - pref % 128
    while n % t:
        t -= 128
    return t


def _matmul(a, b, mode, out_dtype, name, tm=512, tn=1024, tk=2048, after=None, b_cols=None):
    col0 = 0
    if mode == "nn":
        (M, K), (K2, N) = a.shape, b.shape
        if b_cols is not None:
            col0, N = b_cols
    elif mode == "nt":
        (M, K), (N, K2) = a.shape, b.shape
    else:
        (K, M), (K2, N) = a.shape, b.shape
    assert K == K2
    tm, tn, tk = _pick(M, tm), _pick(N, tn), _pick(K, tk)
    assert M % tm == 0 and N % tn == 0 and K % tk == 0, (name, M, N, K)
    nk = K // tk
    dims = {"nn": NN, "nt": NT, "tn": TN}[mode]
    n_in = 2 if after is None else 3

    def body(*refs):
        a_ref, b_ref = refs[0], refs[1]
        o_ref = refs[n_in]
        p = _dot(a_ref[...].astype(_BF), b_ref[...].astype(_BF), dims)
        if nk == 1:
            o_ref[...] = p.astype(out_dtype)
            return
        acc_ref = refs[n_in + 1]
        k = pl.program_id(2)

        @pl.when(k == 0)
        def _():
            acc_ref[...] = p

        @pl.when(jnp.logical_and(k > 0, k < nk - 1))
        def _():
            acc_ref[...] += p

        @pl.when(k == nk - 1)
        def _():
            o_ref[...] = (acc_ref[...] + p).astype(out_dtype)

    if mode == "nn":
        assert col0 % tn == 0
        jb = col0 // tn
        a_spec = pl.BlockSpec((tm, tk), lambda i, j, k: (i, k))
        b_spec = pl.BlockSpec((tk, tn), lambda i, j, k: (k, j + jb))
    elif mode == "nt":
        a_spec = pl.BlockSpec((tm, tk), lambda i, j, k: (i, k))
        b_spec = pl.BlockSpec((tn, tk), lambda i, j, k: (j, k))
    else:
        a_spec = pl.BlockSpec((tk, tm), lambda i, j, k: (k, i))
        b_spec = pl.BlockSpec((tk, tn), lambda i, j, k: (k, j))
    in_specs, args = [a_spec, b_spec], [a, b]
    if after is not None:
        in_specs.append(pl.BlockSpec(memory_space=pl.ANY))
        args.append(after)
    return pl.pallas_call(
        body, grid=(M // tm, N // tn, nk), in_specs=in_specs,
        out_specs=pl.BlockSpec((tm, tn), lambda i, j, k: (i, j)),
        out_shape=SDS((M, N), out_dtype), scratch_shapes=[] if nk == 1 else [pltpu.VMEM((tm, tn), F32)],
        compiler_params=_cp("parallel", "parallel", "arbitrary"), name=name)(*args)


def _rms_fwd(x, g):
    S, D = x.shape
    tr = min(T_ROW, S)

    def body(x_ref, g_ref, h_ref, r_ref):
        xv = x_ref[...]
        r = lax.rsqrt(jnp.mean(xv * xv, axis=-1, keepdims=True) + EPS)
        h_ref[...] = (xv * r * g_ref[...]).astype(_BF)
        r_ref[...] = r

    return pl.pallas_call(
        body, grid=(S // tr,),
        in_specs=[pl.BlockSpec((tr, D), lambda i: (i, 0)), pl.BlockSpec((1, D), lambda i: (0, 0))],
        out_specs=[pl.BlockSpec((tr, D), lambda i: (i, 0)), pl.BlockSpec((tr, 1), lambda i: (i, 0))],
        out_shape=[SDS((S, D), _BF), SDS((S, 1), F32)],
        compiler_params=_cp("parallel"), name="rms_fwd")(x, g)


def _head_loss(x, z, target, g):
    S, D = x.shape
    tr = min(T_ROW, S)

    def body(x_ref, z_ref, t_ref, g_ref, dx2_ref, gg_ref, loss_ref):
        i = pl.program_id(0)
        x2 = x_ref[...] + z_ref[...]
        r = lax.rsqrt(jnp.mean(x2 * x2, axis=-1, keepdims=True) + EPS)
        xh = x2 * r
        gv = g_ref[...]
        diff = xh * gv - t_ref[...]
        lsum = 0.5 * jnp.sum(jnp.mean(diff * diff, axis=-1, keepdims=True), axis=0, keepdims=True)
        dout = diff * (1.0 / D)
        gd = dout * gv
        dx2_ref[...] = r * (gd - xh * jnp.mean(gd * xh, axis=-1, keepdims=True))
        gg = jnp.sum(dout * xh, axis=0, keepdims=True)

        @pl.when(i == 0)
        def _():
            gg_ref[...] = gg
            loss_ref[...] = jnp.broadcast_to(lsum, (1, 128))

        @pl.when(i > 0)
        def _():
            gg_ref[...] += gg
            loss_ref[...] += jnp.broadcast_to(lsum, (1, 128))

    row = pl.BlockSpec((tr, D), lambda i: (i, 0))
    return pl.pallas_call(
        body, grid=(S // tr,), in_specs=[row, row, row, pl.BlockSpec((1, D), lambda i: (0, 0))],
        out_specs=[row, pl.BlockSpec((1, D), lambda i: (0, 0)), pl.BlockSpec((1, 128), lambda i: (0, 0))],
        out_shape=[SDS((S, D), F32), SDS((1, D), F32), SDS((1, 128), F32)],
        compiler_params=_cp("arbitrary"), name="head_loss")(x, z, target, g)


def _rms_bwd(x, r, dh, dx2, g):
    S, D = x.shape
    tr = min(T_ROW, S)

    def body(x_ref, r_ref, dh_ref, dx2_ref, g_ref, gx_ref, gg_ref):
        i = pl.program_id(0)
        rv = r_ref[...]
        xh = x_ref[...] * rv
        dh_ = dh_ref[...]
        gd = dh_ * g_ref[...]
        gx_ref[...] = dx2_ref[...] + rv * (gd - xh * jnp.mean(gd * xh, axis=-1, keepdims=True))
        gg = jnp.sum(dh_ * xh, axis=0, keepdims=True)

        @pl.when(i == 0)
        def _():
            gg_ref[...] = gg

        @pl.when(i > 0)
        def _():
            gg_ref[...] += gg

    row = pl.BlockSpec((tr, D), lambda i: (i, 0))
    return pl.pallas_call(
        body, grid=(S // tr,),
        in_specs=[row, pl.BlockSpec((tr, 1), lambda i: (i, 0)), row, row, pl.BlockSpec((1, D), lambda i: (0, 0))],
        out_specs=[row, pl.BlockSpec((1, D), lambda i: (0, 0))],
        out_shape=[SDS((S, D), F32), SDS((1, D), F32)],
        compiler_params=_cp("arbitrary"), name="rms_bwd")(x, r, dh, dx2, g)


def _mem_kv_fwd(mem, g, w_kv):
    M, D = mem.shape

    def body(mem_ref, g_ref, w_ref, mkv_ref, mhat_ref):
        mv = mem_ref[...]
        mhat = mv * lax.rsqrt(jnp.mean(mv * mv, axis=-1, keepdims=True) + EPS)
        mhat_ref[...] = mhat
        mkv_ref[...] = _dot((mhat * g_ref[...]).astype(_BF), w_ref[...], NN).astype(_BF)

    return pl.pallas_call(
        body, out_shape=[SDS((M, 2 * MEM_W), _BF), SDS((M, D), F32)],
        compiler_params=pltpu.CompilerParams(vmem_limit_bytes=VMEM_LIMIT), name="mem_kv_fwd")(mem, g, w_kv)


def _mem_kv_bwd(dmkv, mhat, g, w_kv):
    M, D = mhat.shape

    def body(d_ref, mhat_ref, g_ref, w_ref, gw_ref, gg_ref):
        d = d_ref[...].astype(_BF)
        mhat = mhat_ref[...]
        gw_ref[...] = _dot((mhat * g_ref[...]).astype(_BF), d, TN).astype(_BF)
        dmh = _dot(d, w_ref[...], NT)
        gg_ref[...] = jnp.sum(dmh * mhat, axis=0, keepdims=True)

    return pl.pallas_call(
        body, out_shape=[SDS((D, 2 * MEM_W), _BF), SDS((1, D), F32)],
        compiler_params=pltpu.CompilerParams(vmem_limit_bytes=VMEM_LIMIT), name="mem_kv_bwd")(dmkv, mhat, g, w_kv)


def _mem_attn_fwd(p_b, p_f, mkv):
    S = p_b.shape[0]
    M = mkv.shape[0]
    tq = min(T_ATT, S)
    qb, gb = B_MQ // HEAD, F_MG // HEAD

    def body(q_ref, k_ref, v_ref, g_ref, y_ref):
        s = _dot(q_ref[...], k_ref[...], NT) * SCALE
        m = jnp.max(s, axis=-1, keepdims=True)
        e = jnp.exp(s - m)
        p = e / jnp.sum(e, axis=-1, keepdims=True)
        o = _dot(p.astype(_BF), v_ref[...], NN)
        gv = g_ref[...]
        y_ref[...] = (o * (gv * _sigmoid(gv))).astype(_BF)

    return pl.pallas_call(
        body, grid=(S // tq, N_MEM),
        in_specs=[pl.BlockSpec((tq, HEAD), lambda i, h: (i, qb + h)),
                  pl.BlockSpec((M, HEAD), lambda i, h: (0, h)),
                  pl.BlockSpec((M, HEAD), lambda i, h: (0, N_MEM + h)),
                  pl.BlockSpec((tq, HEAD), lambda i, h: (i, gb + h))],
        out_specs=pl.BlockSpec((tq, HEAD), lambda i, h: (i, h)),
        out_shape=SDS((S, MEM_W), _BF), compiler_params=_cp("parallel", "parallel"), name="mem_attn_fwd")(p_b, mkv, mkv, p_f)


def _mem_attn_bwd(p_b, p_f, mkv, dy):
    S = p_b.shape[0]
    M = mkv.shape[0]
    tq = min(T_ATT, S)
    qb, gb, yb = B_MQ // HEAD, F_MG // HEAD, (CONV_W + FOX_W) // HEAD

    def body(q_ref, k_ref, v_ref, g_ref, dy_ref, dq_ref, dg_ref, dk_ref, dv_ref):
        i = pl.program_id(1)
        q, k, v = q_ref[...], k_ref[...], v_ref[...]
        s = _dot(q, k, NT) * SCALE
        m = jnp.max(s, axis=-1, keepdims=True)
        e = jnp.exp(s - m)
        p = e / jnp.sum(e, axis=-1, keepdims=True)
        pb = p.astype(_BF)
        o = _dot(pb, v, NN)
        gv = g_ref[...]
        sg = _sigmoid(gv)
        dyv = dy_ref[...]
        do = dyv * (gv * sg)
        dg_ref[...] = (dyv * o * _dsilu(gv, sg)).astype(_BF)
        dob = do.astype(_BF)
        dp = _dot(dob, v, NT)
        ds = p * (dp - jnp.sum(dp * p, axis=-1, keepdims=True)) * SCALE
        dsb = ds.astype(_BF)
        dq_ref[...] = _dot(dsb, k, NN).astype(_BF)
        dk = _dot(dsb, q, TN)
        dv = _dot(pb, dob, TN)

        @pl.when(i == 0)
        def _():
            dk_ref[...] = dk
            dv_ref[...] = dv

        @pl.when(i > 0)
        def _():
            dk_ref[...] += dk
            dv_ref[...] += dv

    tile = pl.BlockSpec((tq, HEAD), lambda h, i: (i, h))
    kv = pl.BlockSpec((M, HEAD), lambda h, i: (0, h))
    return pl.pallas_call(
        body, grid=(N_MEM, S // tq),
        in_specs=[pl.BlockSpec((tq, HEAD), lambda h, i: (i, qb + h)), kv,
                  pl.BlockSpec((M, HEAD), lambda h, i: (0, N_MEM + h)),
                  pl.BlockSpec((tq, HEAD), lambda h, i: (i, gb + h)),
                  pl.BlockSpec((tq, HEAD), lambda h, i: (i, yb + h))],
        out_specs=[tile, tile, kv, kv],
        out_shape=[SDS((S, MEM_W), _BF), SDS((S, MEM_W), _BF), SDS((M, MEM_W), F32), SDS((M, MEM_W), F32)],
        compiler_params=_cp("parallel", "arbitrary"), name="mem_attn_bwd")(p_b, mkv, mkv, p_f, dy)


def _fox_cumsum(p_f, b_f_pad):
    S = p_f.shape[0]
    tr = min(256, S)
    fb = F_FL // 128

    def body(z_ref, b_ref, col_ref, row_ref, carry_ref):
        i = pl.program_id(0)

        @pl.when(i == 0)
        def _():
            carry_ref[...] = jnp.zeros_like(carry_ref)

        z = z_ref[...] + b_ref[...]
        lf = jnp.minimum(z, 0.0) - jnp.log1p(jnp.exp(-jnp.abs(z)))
        r = lax.broadcasted_iota(jnp.int32, (tr, tr), 0)
        c = lax.broadcasted_iota(jnp.int32, (tr, tr), 1)
        tri = (c <= r).astype(F32)
        cs = jnp.dot(tri, lf, precision=lax.Precision.HIGHEST, preferred_element_type=F32) + carry_ref[...]
        carry_ref[...] = cs[tr - 1:tr, :]
        cst = cs.T
        for hd in range(N_FOX):
            col_ref[hd] = cs[:, hd:hd + 1]
            row_ref[hd] = cst[hd:hd + 1, :]

    return pl.pallas_call(
        body, grid=(S // tr,),
        in_specs=[pl.BlockSpec((tr, 128), lambda i: (i, fb)), pl.BlockSpec((1, 128), lambda i: (0, 0))],
        out_specs=[pl.BlockSpec((N_FOX, tr, 1), lambda i: (0, i, 0)), pl.BlockSpec((N_FOX, 1, tr), lambda i: (0, 0, i))],
        out_shape=[SDS((N_FOX, S, 1), F32), SDS((N_FOX, 1, S), F32)], scratch_shapes=[pltpu.VMEM((1, 128), F32)],
        compiler_params=_cp("arbitrary"), name="fox_cumsum")(p_f, b_f_pad)


def _fox_dlogf(dr, p_f, b_f_pad):
    S = p_f.shape[0]
    tr = min(256, S)
    nb = S // tr
    fb = F_FL // 128
    wide = PART_W - F_FL

    def body(dr_ref, z_ref, b_ref, dz_ref, gb_ref, carry_ref):
        i = pl.program_id(0)

        @pl.when(i == 0)
        def _():
            carry_ref[...] = jnp.zeros_like(carry_ref)
            gb_ref[...] = jnp.zeros_like(gb_ref)

        heads = [dr_ref[hd, 0:1, :] + dr_ref[hd, 1:2, :] for hd in range(N_FOX)]
        dc = jnp.concatenate(heads + [jnp.zeros((128 - N_FOX, tr), F32)], axis=0).T
        r = lax.broadcasted_iota(jnp.int32, (tr, tr), 0)
        c = lax.broadcasted_iota(jnp.int32, (tr, tr), 1)
        tri = (c >= r).astype(F32)
        rc = jnp.dot(tri, dc, precision=lax.Precision.HIGHEST, preferred_element_type=F32) + carry_ref[...]
        carry_ref[...] = rc[0:1, :]
        z = z_ref[...] + b_ref[...]
        dz = rc * _sigmoid(-z)
        gb_ref[...] += jnp.sum(dz, axis=0, keepdims=True)
        dz_ref[...] = jnp.concatenate([dz.astype(_BF), jnp.zeros((tr, wide - 128), _BF)], axis=1)

    return pl.pallas_call(
        body, grid=(nb,),
        in_specs=[pl.BlockSpec((N_FOX, 8, tr), lambda i: (0, 0, nb - 1 - i)),
                  pl.BlockSpec((tr, 128), lambda i: (nb - 1 - i, fb)), pl.BlockSpec((1, 128), lambda i: (0, 0))],
        out_specs=[pl.BlockSpec((tr, wide), lambda i: (nb - 1 - i, 0)), pl.BlockSpec((1, 128), lambda i: (0, 0))],
        out_shape=[SDS((S, wide), _BF), SDS((1, 128), F32)], scratch_shapes=[pltpu.VMEM((1, 128), F32)],
        compiler_params=_cp("arbitrary"), name="fox_dlogf")(dr, p_f, b_f_pad)


def _fox_fwd(p_b, p_f, c_col, c_row):
    S = p_b.shape[0]
    t = min(2 * T_ATT, S)
    nq = S // t
    rc = min(256, t)
    qb, kb, vb, gb = B_Q // HEAD, B_K // HEAD, B_V // HEAD, F_FG // HEAD
    kq = SCALE * 1.4426950408889634

    def body(q_ref, k_ref, v_ref, cc_ref, cr_ref, g_ref, o_ref, y_ref, lse_ref, va_ref, ua_ref, ub_ref, m_ref, acc_ref):
        i = pl.program_id(1)

        @pl.when(i == 0)
        def _():
            va_ref[:, 0:HEAD] = v_ref[...]
            lane = lax.broadcasted_iota(jnp.int32, (S, HEAD), 1)
            va_ref[:, HEAD:2 * HEAD] = jnp.where(lane == 0, 1.0, 0.0).astype(_BF)

        m_ref[...] = jnp.full_like(m_ref, NEG)
        acc_ref[...] = jnp.zeros_like(acc_ref)

        def scores(b, u_ref):
            off = pl.multiple_of(b * t, t)
            k = k_ref[pl.ds(off, t), :]
            csr = cr_ref[:, pl.ds(off, t)] * (1.0 / SCALE)
            for r in range(0, t, rc):
                u_ref[r:r + rc, :] = _dot(q_ref[r:r + rc, :], k, NT) - csr

        def absorb(b, u_ref, masked):
            va = va_ref[pl.ds(pl.multiple_of(b * t, t), t), :]
            for r in range(0, t, rc):
                u = u_ref[r:r + rc, :]
                if masked:
                    row = lax.broadcasted_iota(jnp.int32, (rc, t), 0) + r
                    col = lax.broadcasted_iota(jnp.int32, (rc, t), 1)
                    u = jnp.where(col <= row, u, NEG)
                m_old = m_ref[r:r + rc, :]
                m_new = jnp.maximum(m_old, jnp.max(u, axis=-1, keepdims=True))
                alpha = jnp.exp2((m_old - m_new) * kq)
                p = jnp.exp2((u - m_new) * kq)
                acc_ref[r:r + rc, :] = alpha * acc_ref[r:r + rc, :] + _dot(p.astype(_BF), va, NN)
                m_ref[r:r + rc, :] = m_new

        scores(0, ua_ref)

        def pair(pi, carry):
            b = 2 * pi
            scores(b + 1, ub_ref)
            absorb(b, ua_ref, False)
            scores(b + 2, ua_ref)
            absorb(b + 1, ub_ref, False)
            return carry

        lax.fori_loop(0, i // 2, pair, 0)

        @pl.when(i % 2 == 1)
        def _():
            scores(i, ub_ref)
            absorb(i - 1, ua_ref, False)
            absorb(i, ub_ref, True)

        @pl.when(i % 2 == 0)
        def _():
            absorb(i, ua_ref, True)

        l = acc_ref[:, HEAD:HEAD + 1]
        o = acc_ref[:, 0:HEAD] / l
        gv = g_ref[...]
        o_ref[...] = o.astype(_BF)
        y_ref[...] = (o * (gv * _sigmoid(gv))).astype(_BF)
        lse_ref[...] = cc_ref[...] + SCALE * m_ref[...] + jnp.log(l)

    tile = pl.BlockSpec((t, HEAD), lambda h, i: (i, h))
    return pl.pallas_call(
        body, grid=(N_FOX, nq),
        in_specs=[pl.BlockSpec((t, HEAD), lambda h, i: (i, qb + h)),
                  pl.BlockSpec((S, HEAD), lambda h, i: (0, kb + h)),
                  pl.BlockSpec((S, HEAD), lambda h, i: (0, vb + h)),
                  pl.BlockSpec((None, t, 1), lambda h, i: (h, i, 0)),
                  pl.BlockSpec((None, 1, S), lambda h, i: (h, 0, 0)),
                  pl.BlockSpec((t, HEAD), lambda h, i: (i, gb + h))],
        out_specs=[tile, tile, pl.BlockSpec((None, t, 1), lambda h, i: (h, i, 0))],
        out_shape=[SDS((S, FOX_W), _BF), SDS((S, FOX_W), _BF), SDS((N_FOX, S, 1), F32)],
        scratch_shapes=[pltpu.VMEM((S, 2 * HEAD), _BF), pltpu.VMEM((t, t), F32), pltpu.VMEM((t, t), F32),
                        pltpu.VMEM((t, 1), F32), pltpu.VMEM((t, 2 * HEAD), F32)],
        compiler_params=_cp("parallel", "arbitrary"), name="fox_fwd")(p_b, p_b, p_b, c_col, c_row, p_f)


def _fox_bwd_prep(dy, p_f, o, lse, c_col):
    S = dy.shape[0]
    t = min(T_ATT, S)
    hg = 4
    wd = hg * HEAD

    def body(dy_ref, g_ref, o_ref, lse_ref, cc_ref, do_ref, dg_ref, rows_ref):
        gv = g_ref[...]
        sg = _sigmoid(gv)
        dyv = dy_ref[...]
        ov = o_ref[...].astype(F32)
        do = dyv * (gv * sg)
        do_ref[...] = do.astype(_BF)
        dg_ref[...] = (dyv * ov * _dsilu(gv, sg)).astype(_BF)
        prod = do * ov
        lane = lax.broadcasted_iota(jnp.int32, (t, 128), 1)
        for hd in range(hg):
            delta = jnp.sum(prod[:, hd * HEAD:(hd + 1) * HEAD], axis=-1, keepdims=True)
            a = cc_ref[hd] - lse_ref[hd]
            mat = jnp.where(lane == 0, a, jnp.where(lane == 1, delta, 0.0))
            rows_ref[hd] = mat.T[0:8, :]

    tile = pl.BlockSpec((t, wd), lambda g, i: (i, g))
    col = pl.BlockSpec((hg, t, 1), lambda g, i: (g, i, 0))
    return pl.pallas_call(
        body, grid=(N_FOX // hg, S // t),
        in_specs=[pl.BlockSpec((t, wd), lambda g, i: (i, CONV_W // wd + g)),
                  pl.BlockSpec((t, wd), lambda g, i: (i, F_FG // wd + g)), tile, col, col],
        out_specs=[tile, tile, pl.BlockSpec((hg, 8, t), lambda g, i: (g, 0, i))],
        out_shape=[SDS((S, FOX_W), _BF), SDS((S, FOX_W), _BF), SDS((N_FOX, 8, S), F32)],
        compiler_params=_cp("parallel", "parallel"), name="fox_bwd_prep")(dy, p_f, o, lse, c_col)


def _fox_bwd(p_b, do, rows, c_col):
    S = p_b.shape[0]
    t = min(T_ATT, S)
    nk = S // t
    qb, kb, vb = B_Q // HEAD, B_K // HEAD, B_V // HEAD

    def body(k_ref, v_ref, q_ref, do_ref, rows_ref, cc_ref, dq_ref, dk_ref, dv_ref, dr_ref,
             dqt_ref, dka_ref, dva_ref, dca_ref, dra_ref):
        j = pl.program_id(1)

        @pl.when(j == 0)
        def _():
            dqt_ref[...] = jnp.zeros_like(dqt_ref)
            dra_ref[...] = jnp.zeros_like(dra_ref)

        k = k_ref[...]
        v = v_ref[...]
        kt = k.astype(F32).T.astype(_BF)
        cc = cc_ref[...]
        dka_ref[...] = jnp.zeros_like(dka_ref)
        dva_ref[...] = jnp.zeros_like(dva_ref)
        dca_ref[...] = jnp.zeros_like(dca_ref)

        def block(i, masked):
            off = pl.multiple_of(i * t, t)
            q = q_ref[pl.ds(off, t), :]
            dov = do_ref[pl.ds(off, t), :]
            a_row = rows_ref[0:1, pl.ds(off, t)]
            delta_row = rows_ref[1:2, pl.ds(off, t)]
            st = _dot(k, q, NT) * SCALE + (a_row - cc)
            if masked:
                srow = lax.broadcasted_iota(jnp.int32, (t, t), 0)
                tcol = lax.broadcasted_iota(jnp.int32, (t, t), 1)
                st = jnp.where(srow <= tcol, st, NEG)
            pt = jnp.exp(st)
            dva_ref[...] += _dot(pt.astype(_BF), dov, NN)
            dpt = _dot(v, dov, NT)
            dst = pt * (dpt - delta_row)
            part = dst[:, 0:128]
            for gidx in range(1, t // 128):
                part = part + dst[:, gidx * 128:(gidx + 1) * 128]
            dca_ref[...] += part
            dra_ref[0:1, pl.ds(off, t)] += jnp.sum(dst, axis=0, keepdims=True)
            dsb = dst.astype(_BF)
            dka_ref[...] += _dot(dsb, q, NN)
            dqt_ref[:, pl.ds(off, t)] += _dot(kt, dsb, NN)

        block(j, True)

        def step(i, carry):
            block(i, False)
            return carry

        lax.fori_loop(j + 1, nk, step, 0)
        dk_ref[...] = (dka_ref[...] * SCALE).astype(_BF)
        dv_ref[...] = dva_ref[...].astype(_BF)
        dra_ref[1:2, pl.ds(pl.multiple_of(j * t, t), t)] = -jnp.sum(dca_ref[...].T, axis=0, keepdims=True)

        @pl.when(j == nk - 1)
        def _():
            dr_ref[...] = dra_ref[...]
            for ci in range(nk):
                dq_ref[ci * t:(ci + 1) * t, :] = (dqt_ref[:, ci * t:(ci + 1) * t].T * SCALE).astype(_BF)

    tile = pl.BlockSpec((t, HEAD), lambda h, j: (j, h))
    return pl.pallas_call(
        body, grid=(N_FOX, nk),
        in_specs=[pl.BlockSpec((t, HEAD), lambda h, j: (j, kb + h)),
                  pl.BlockSpec((t, HEAD), lambda h, j: (j, vb + h)),
                  pl.BlockSpec((S, HEAD), lambda h, j: (0, qb + h)),
                  pl.BlockSpec((S, HEAD), lambda h, j: (0, h)),
                  pl.BlockSpec((None, 8, S), lambda h, j: (h, 0, 0)),
                  pl.BlockSpec((None, t, 1), lambda h, j: (h, j, 0))],
        out_specs=[pl.BlockSpec((S, HEAD), lambda h, j: (0, h)), tile, tile,
                   pl.BlockSpec((None, 8, S), lambda h, j: (h, 0, 0))],
        out_shape=[SDS((S, FOX_W), _BF), SDS((S, FOX_W), _BF), SDS((S, FOX_W), _BF), SDS((N_FOX, 8, S), F32)],
        scratch_shapes=[pltpu.VMEM((HEAD, S), F32), pltpu.VMEM((t, HEAD), F32), pltpu.VMEM((t, HEAD), F32),
                        pltpu.VMEM((t, 128), F32), pltpu.VMEM((8, S), F32)],
        compiler_params=_cp("parallel", "arbitrary"), name="fox_bwd")(p_b, p_b, p_b, do, rows, c_col)


CHUNK = 32


def _conv_taps(ext_ref, w_ref, first, out_fn, n_rows):
    def chunk(c, carry):
        r0 = pl.multiple_of(c * CHUNK, CHUNK)
        win = ext_ref[pl.ds(r0, 2 * CHUNK), :]
        acc = jnp.zeros((CHUNK, CONV_W), F32)
        for k in range(CONV_K):
            f = first(k)
            acc = acc + w_ref[k:k + 1, :] * win[f:f + CHUNK, :]
        out_fn(r0, acc)
        return carry

    lax.fori_loop(0, n_rows // CHUNK, chunk, 0)


def _conv_fwd(p_f, conv_w, conv_b, ln_g, ln_b, w_pw):
    S = p_f.shape[0]
    tc = min(T_CONV, S)
    hb = tc // HALO

    def body(a_ref, b_ref, gc_ref, ap_ref, bp_ref, w_ref, cb_ref, lg_ref, lb_ref, pw_ref, y_ref, u1_ref, ext_ref):
        i = pl.program_id(0)
        prev = ap_ref[...] * _sigmoid(bp_ref[...])
        ext_ref[0:HALO, :] = jnp.where(i > 0, prev, 0.0)
        ext_ref[HALO:HALO + tc, :] = a_ref[...] * _sigmoid(b_ref[...])
        cb = cb_ref[...]

        def put(r0, acc):
            u1_ref[pl.ds(r0, CHUNK), :] = acc + cb

        _conv_taps(ext_ref, w_ref, lambda k: HALO - (CONV_K - 1) + k, put, tc)
        u1 = u1_ref[...]
        mu = jnp.mean(u1, axis=-1, keepdims=True)
        d = u1 - mu
        rstd = lax.rsqrt(jnp.mean(d * d, axis=-1, keepdims=True) + EPS)
        u2 = d * rstd * lg_ref[...] + lb_ref[...]
        u3 = u2 * _sigmoid(u2)
        pw = _dot(u3.astype(_BF), pw_ref[...], NN)
        gc = gc_ref[...]
        y_ref[...] = (pw * (gc * _sigmoid(gc))).astype(_BF)

    blk = lambda cb_: pl.BlockSpec((tc, CONV_W), lambda i: (i, cb_))
    halo = lambda cb_: pl.BlockSpec((HALO, CONV_W), lambda i: (jnp.maximum(i * hb - 1, 0), cb_))
    vec = pl.BlockSpec((1, CONV_W), lambda i: (0, 0))
    return pl.pallas_call(
        body, grid=(S // tc,),
        in_specs=[blk(0), blk(1), blk(2), halo(0), halo(1), pl.BlockSpec((CONV_K, CONV_W), lambda i: (0, 0)),
                  vec, vec, vec, pl.BlockSpec((CONV_W, CONV_W), lambda i: (0, 0))],
        out_specs=[pl.BlockSpec((tc, CONV_W), lambda i: (i, 0)), pl.BlockSpec((tc, CONV_W), lambda i: (i, 0))],
        out_shape=[SDS((S, CONV_W), _BF), SDS((S, CONV_W), F32)],
        scratch_shapes=[pltpu.VMEM((tc + 2 * HALO, CONV_W), F32)],
        compiler_params=_cp("parallel"), name="conv_fwd")(p_f, p_f, p_f, p_f, p_f, conv_w, conv_b, ln_g, ln_b, w_pw)


def _conv_bwd1(u1, p_f, dy, ln_g, ln_b, w_pw):
    S = u1.shape[0]
    tc = min(T_CONV, S)

    def body(u1_ref, gc_ref, dy_ref, lg_ref, lb_ref, pw_ref, du1_ref, dgc_ref, gpw_ref, glg_ref, glb_ref, gcb_ref):
        i = pl.program_id(0)
        u1v = u1_ref[...]
        mu = jnp.mean(u1v, axis=-1, keepdims=True)
        d = u1v - mu
        rstd = lax.rsqrt(jnp.mean(d * d, axis=-1, keepdims=True) + EPS)
        xh = d * rstd
        lg = lg_ref[...]
        u2 = xh * lg + lb_ref[...]
        sg2 = _sigmoid(u2)
        u3b = (u2 * sg2).astype(_BF)
        w = pw_ref[...]
        pw = _dot(u3b, w, NN)
        gc = gc_ref[...]
        sgc = _sigmoid(gc)
        dyv = dy_ref[...]
        dpw = (dyv * (gc * sgc)).astype(_BF)
        dgc_ref[...] = (dyv * pw * _dsilu(gc, sgc)).astype(_BF)
        gpw = _dot(u3b, dpw, TN)
        du2 = _dot(dpw, w, NT) * _dsilu(u2, sg2)
        glg = jnp.sum(du2 * xh, axis=0, keepdims=True)
        glb = jnp.sum(du2, axis=0, keepdims=True)
        dxh = du2 * lg
        du1 = rstd * (dxh - jnp.mean(dxh, axis=-1, keepdims=True) - xh * jnp.mean(dxh * xh, axis=-1, keepdims=True))
        du1_ref[...] = du1
        gcb = jnp.sum(du1, axis=0, keepdims=True)

        @pl.when(i == 0)
        def _():
            gpw_ref[...] = gpw
            glg_ref[...] = glg
            glb_ref[...] = glb
            gcb_ref[...] = gcb

        @pl.when(i > 0)
        def _():
            gpw_ref[...] += gpw
            glg_ref[...] += glg
            glb_ref[...] += glb
            gcb_ref[...] += gcb

    row = pl.BlockSpec((tc, CONV_W), lambda i: (i, 0))
    vec = pl.BlockSpec((1, CONV_W), lambda i: (0, 0))
    sq = pl.BlockSpec((CONV_W, CONV_W), lambda i: (0, 0))
    return pl.pallas_call(
        body, grid=(S // tc,),
        in_specs=[row, pl.BlockSpec((tc, CONV_W), lambda i: (i, F_GC // CONV_W)), row, vec, vec, sq],
        out_specs=[row, row, sq, vec, vec, vec],
        out_shape=[SDS((S, CONV_W), F32), SDS((S, CONV_W), _BF), SDS((CONV_W, CONV_W), F32),
                   SDS((1, CONV_W), F32), SDS((1, CONV_W), F32), SDS((1, CONV_W), F32)],
        compiler_params=_cp("arbitrary"), name="conv_bwd1")(u1, p_f, dy, ln_g, ln_b, w_pw)


def _conv_bwd2(du1, p_f, conv_w):
    S = du1.shape[0]
    tc = min(T_CONV, S)
    hb = tc // HALO
    nblk = S // tc
    last_halo = S // HALO - 1

    def body(d_ref, dn_ref, a_ref, b_ref, ap_ref, bp_ref, w_ref, da_ref, db_ref, gw_ref, ext_ref, dext_ref, du0_ref):
        i = pl.program_id(0)
        av = a_ref[...]
        sb = _sigmoid(b_ref[...])
        prev = ap_ref[...] * _sigmoid(bp_ref[...])
        ext_ref[0:HALO, :] = jnp.where(i > 0, prev, 0.0)
        ext_ref[HALO:HALO + tc, :] = av * sb
        dext_ref[0:tc, :] = d_ref[...]
        dext_ref[tc:tc + HALO, :] = jnp.where(i < nblk - 1, dn_ref[...], 0.0)

        def put(r0, acc):
            du0_ref[pl.ds(r0, CHUNK), :] = acc

        _conv_taps(dext_ref, w_ref, lambda k: CONV_K - 1 - k, put, tc)
        du0 = du0_ref[...]
        da_ref[...] = (du0 * sb).astype(_BF)
        db_ref[...] = (du0 * av * sb * (1.0 - sb)).astype(_BF)

        def chunk(c, carry):
            r0 = pl.multiple_of(c * CHUNK, CHUNK)
            win = ext_ref[pl.ds(r0, 2 * CHUNK), :]
            dv = dext_ref[pl.ds(r0, CHUNK), :]
            rows = [jnp.sum(dv * win[HALO - (CONV_K - 1) + k:HALO - (CONV_K - 1) + k + CHUNK, :], axis=0, keepdims=True)
                    for k in range(CONV_K)]
            rows.append(jnp.zeros((1, CONV_W), F32))
            return carry + jnp.concatenate(rows, axis=0)

        gw = lax.fori_loop(0, tc // CHUNK, chunk, jnp.zeros((CONV_K + 1, CONV_W), F32))

        @pl.when(i == 0)
        def _():
            gw_ref[...] = gw

        @pl.when(i > 0)
        def _():
            gw_ref[...] += gw

    row = pl.BlockSpec((tc, CONV_W), lambda i: (i, 0))
    blk = lambda cb_: pl.BlockSpec((tc, CONV_W), lambda i: (i, cb_))
    halo = lambda cb_: pl.BlockSpec((HALO, CONV_W), lambda i: (jnp.maximum(i * hb - 1, 0), cb_))
    nxt = pl.BlockSpec((HALO, CONV_W), lambda i: (jnp.minimum((i + 1) * hb, last_halo), 0))
    return pl.pallas_call(
        body, grid=(nblk,),
        in_specs=[row, nxt, blk(0), blk(1), halo(0), halo(1), pl.BlockSpec((CONV_K, CONV_W), lambda i: (0, 0))],
        out_specs=[row, row, pl.BlockSpec((CONV_K + 1, CONV_W), lambda i: (0, 0))],
        out_shape=[SDS((S, CONV_W), _BF), SDS((S, CONV_W), _BF), SDS((CONV_K + 1, CONV_W), F32)],
        scratch_shapes=[pltpu.VMEM((tc + 2 * HALO, CONV_W), F32), pltpu.VMEM((tc + 2 * HALO, CONV_W), F32),
                        pltpu.VMEM((tc, CONV_W), F32)],
        compiler_params=_cp("arbitrary"), name="conv_bwd2")(du1, du1, p_f, p_f, p_f, p_f, conv_w)


def _exchange(srcs, scatter, name):
    n = len(srcs)
    out_shape = [SDS((N_DEV,) + (s.shape[1:] if sc else s.shape), s.dtype) for s, sc in zip(srcs, scatter)]

    def body(*refs):
        src_refs, dst_refs = refs[:n], refs[n:2 * n]
        send_sems, recv_sems, local_sems = refs[2 * n:]
        x, y, c = lax.axis_index("x"), lax.axis_index("y"), lax.axis_index("c")
        me = 4 * x + 2 * y + c
        copies = []
        for a in range(n):
            for f in range(1, N_DEV):
                px = 1 - x if f & 4 else x
                py = 1 - y if f & 2 else y
                pc = 1 - c if f & 1 else c
                peer = 4 * px + 2 * py + pc
                src = src_refs[a].at[peer] if scatter[a] else src_refs[a]
                cp = pltpu.make_async_remote_copy(
                    src_ref=src, dst_ref=dst_refs[a].at[me], send_sem=send_sems.at[a, f - 1],
                    recv_sem=recv_sems.at[a, f - 1], device_id=(px, py, pc), device_id_type=MESH)
                cp.start()
                copies.append(cp)
            own = src_refs[a].at[me] if scatter[a] else src_refs[a]
            lc = pltpu.make_async_copy(own, dst_refs[a].at[me], local_sems.at[a])
            lc.start()
            copies.append(lc)
        for cp in copies:
            cp.wait()

    anyspec = pl.BlockSpec(memory_space=pl.ANY)
    return pl.pallas_call(
        body, in_specs=[anyspec] * n, out_specs=[anyspec] * n, out_shape=out_shape,
        scratch_shapes=[pltpu.SemaphoreType.DMA((n, N_DEV - 1)), pltpu.SemaphoreType.DMA((n, N_DEV - 1)),
                        pltpu.SemaphoreType.DMA((n,))],
        name=name)(*srcs)


def _flip_peer(f, x, y, c):
    return (1 - x if f & 4 else x, 1 - y if f & 2 else y, 1 - c if f & 1 else c)


def _scatter_start(srcs):
    n = len(srcs)
    lands = [lax.empty((N_DEV - 1,) + s.shape[1:], s.dtype) for s in srcs]

    def body(*refs):
        src_refs, land_refs = refs[:n], refs[n:2 * n]
        send_sems, recv_sems = refs[2 * n:3 * n], refs[3 * n:4 * n]
        token = refs[-1]
        x, y, c = lax.axis_index("x"), lax.axis_index("y"), lax.axis_index("c")
        for a in range(n):
            for f in range(1, N_DEV):
                px, py, pc = _flip_peer(f, x, y, c)
                pltpu.make_async_remote_copy(
                    src_ref=src_refs[a].at[4 * px + 2 * py + pc], dst_ref=land_refs[a].at[f - 1],
                    send_sem=send_sems[a], recv_sem=recv_sems[a], device_id=(px, py, pc), device_id_type=MESH).start()
        token[...] = jnp.zeros_like(token)

    hbm = pl.BlockSpec(memory_space=pltpu.HBM)
    sem = pl.BlockSpec(memory_space=pltpu.SEMAPHORE)
    bufs = [pltpu.with_memory_space_constraint(b, pltpu.HBM) for b in list(srcs) + lands]
    out = pl.pallas_call(
        body, name="scatter_start",
        out_shape=(*[pltpu.SemaphoreType.DMA(())] * (2 * n), *[pltpu.HBM(b.shape, b.dtype) for b in bufs],
                   SDS((8, 128), F32)),
        in_specs=[hbm] * (2 * n),
        out_specs=(*[sem] * (2 * n), *[hbm] * (2 * n), pl.BlockSpec(memory_space=pltpu.VMEM)),
        input_output_aliases={i: 2 * n + i for i in range(2 * n)},
        compiler_params=pltpu.CompilerParams(has_side_effects=pltpu.SideEffectType.DATAFLOW_SIDE_EFFECTING))(*bufs)
    return list(out[:2 * n]), list(out[2 * n:3 * n]), list(out[3 * n:4 * n]), out[-1]


def _scatter_wait(sems, srcs, lands, after):
    n = len(srcs)

    def body(*refs):
        src_refs, land_refs = refs[:n], refs[n:2 * n]
        send_sems, recv_sems = refs[2 * n:3 * n], refs[3 * n:4 * n]
        x, y, c = lax.axis_index("x"), lax.axis_index("y"), lax.axis_index("c")
        for a in range(n):
            seven = pltpu.make_async_remote_copy(
                src_ref=src_refs[a].at[pl.ds(0, N_DEV - 1)], dst_ref=land_refs[a], send_sem=send_sems[a],
                recv_sem=recv_sems[a], device_id=(x, y, c), device_id_type=MESH)
            seven.wait_send()
            seven.wait_recv()

    hbm = pl.BlockSpec(memory_space=pltpu.HBM)
    sem = pl.BlockSpec(memory_space=pltpu.SEMAPHORE)
    bufs = list(srcs) + list(lands)
    out = pl.pallas_call(
        body, name="scatter_wait", out_shape=tuple(pltpu.HBM(b.shape, b.dtype) for b in bufs),
        in_specs=[hbm] * (2 * n) + [sem] * (2 * n) + [pl.BlockSpec(memory_space=pl.ANY)],
        out_specs=tuple([hbm] * (2 * n)), input_output_aliases={i: i for i in range(2 * n)},
        compiler_params=pltpu.CompilerParams(has_side_effects=pltpu.SideEffectType.DATAFLOW_SIDE_EFFECTING))(
            *bufs, *sems, after)
    return list(out[:n]), list(out[n:])


def _gather_two_level(srcs, name):
    n = len(srcs)
    out_shape = [SDS((N_DEV,) + s.shape, s.dtype) for s in srcs]

    def body(*refs):
        src_refs, dst_refs = refs[:n], refs[n:2 * n]
        send_sems, recv_sems, local_sems = refs[2 * n:]
        x, y, c = lax.axis_index("x"), lax.axis_index("y"), lax.axis_index("c")
        sibling = (x, y, 1 - c)
        chips = [(1 - x, y), (x, 1 - y), (1 - x, 1 - y)]

        def slot(a, px, py, pc):
            return dst_refs[a].at[4 * px + 2 * py + pc]

        def copy(a, k, block, to, src=None):
            return pltpu.make_async_remote_copy(
                src_ref=slot(a, *block) if src is None else src, dst_ref=slot(a, *block),
                send_sem=send_sems.at[a, k], recv_sem=recv_sems.at[a, k], device_id=to, device_id_type=MESH)

        own, sends = [], []
        for a in range(n):
            mine = pltpu.make_async_copy(src_refs[a], slot(a, x, y, c), local_sems.at[a])
            mine.start()
            own.append(mine)
            first = [copy(a, 1 + j, (x, y, c), (*chip, c), src=src_refs[a]) for j, chip in enumerate(chips)]
            first.append(copy(a, 0, (x, y, c), sibling, src=src_refs[a]))
            for cp in first:
                cp.start()
            sends += first
        for a in range(n):
            for j, chip in enumerate(chips):
                copy(a, 1 + j, (*chip, c), (x, y, c)).wait_recv()
                fwd = copy(a, 4 + j, (*chip, c), sibling)
                fwd.start()
                sends.append(fwd)
        for a in range(n):
            copy(a, 0, (x, y, 1 - c), (x, y, c)).wait_recv()
            for j, chip in enumerate(chips):
                copy(a, 4 + j, (*chip, 1 - c), (x, y, c)).wait_recv()
        for cp in sends:
            cp.wait_send()
        for cp in own:
            cp.wait()

    anyspec = pl.BlockSpec(memory_space=pl.ANY)
    return pl.pallas_call(
        body, in_specs=[anyspec] * n, out_specs=[anyspec] * n, out_shape=out_shape,
        scratch_shapes=[pltpu.SemaphoreType.DMA((n, N_DEV - 1)), pltpu.SemaphoreType.DMA((n, N_DEV - 1)),
                        pltpu.SemaphoreType.DMA((n,))],
        name=name)(*srcs)


def _adamw(parts, w, m, v, name, tr=256, own=None):
    lead = w.ndim == 3
    R, C = w.shape[-2:]
    tr = tr if R % tr == 0 else R
    n_parts = parts.shape[0]
    first = [] if own is None else [own]

    def body(*refs):
        p_ref, w_ref, m_ref, v_ref, g_ref, d_ref, nm_ref, nv_ref = refs[len(first):]
        terms = [r[0] for r in refs[:len(first)]] + [p_ref[dev] for dev in range(n_parts)]
        g = terms[0].astype(F32)
        for term in terms[1:]:
            g = g + term.astype(F32)
        mn = ADAM_B1 * m_ref[...] + (1.0 - ADAM_B1) * g
        vn = ADAM_B2 * v_ref[...] + (1.0 - ADAM_B2) * (g * g)
        m_hat = mn / (1.0 - ADAM_B1 ** ADAM_STEP)
        v_hat = vn / (1.0 - ADAM_B2 ** ADAM_STEP)
        g_ref[...] = g
        d_ref[...] = -ADAM_LR * (m_hat / (jnp.sqrt(v_hat) + ADAM_EPS) + ADAM_WD * w_ref[...])
        nm_ref[...] = mn
        nv_ref[...] = vn

    blk = pl.BlockSpec((None, tr, C), lambda i: (0, i, 0)) if lead else pl.BlockSpec((tr, C), lambda i: (i, 0))
    return pl.pallas_call(
        body, grid=(R // tr,),
        in_specs=[pl.BlockSpec((1, tr, C), lambda i: (0, i, 0))] * len(first)
        + [pl.BlockSpec((n_parts, tr, C), lambda i: (0, i, 0)), blk, blk, blk],
        out_specs=[blk] * 4, out_shape=[SDS(w.shape, F32)] * 4,
        compiler_params=_cp("parallel"), name=name)(*first, parts, w, m, v)


def _split_w_in(w_cat, D):
    z = jnp.zeros((D, PART_W - F_FL - N_FOX), w_cat.dtype)
    return jnp.concatenate([w_cat[:, O_A:O_Q], w_cat[:, O_FG:O_MQ], w_cat[:, O_MG:D_IN], w_cat[:, O_FL:O_FG], z,
                            w_cat[:, O_Q:O_FL], w_cat[:, O_MQ:O_MG]], axis=1)


def _merge_w_in(g_f, g_b):
    return jnp.concatenate([g_f[:, F_A:F_FG], g_b[:, B_Q:B_MQ], g_f[:, F_FL:F_FL + N_FOX], g_f[:, F_FG:F_MG],
                            g_b[:, B_MQ:PART_W], g_f[:, F_MG:F_FL]], axis=1)


def _pack_small(norm, mem_norm, final, conv_b, ln_g, ln_b, b_f, extra, D):
    width = max(D, PACK_W)
    row3 = jnp.concatenate([conv_b.reshape(-1), ln_g.reshape(-1), ln_b.reshape(-1), b_f.reshape(-1)])
    rows = [norm.reshape(-1), mem_norm.reshape(-1), final.reshape(-1), row3, extra.reshape(-1)]
    rows = [jnp.pad(r, (0, width - r.shape[0])) for r in rows]
    return jnp.concatenate([jnp.stack(rows), jnp.zeros((3, width), F32)], axis=0)


def _unpack_small(p, D):
    c = CONV_W
    return dict(norm_g=p[0:1, :D], mem_norm_g=p[1:2, :D], final_g=p[2, :D], conv_b=p[3:4, 0:c],
                conv_ln_g=p[3:4, c:2 * c], conv_ln_b=p[3:4, 2 * c:3 * c], b_f=p[3:4, 3 * c:3 * c + N_FOX])


def kernel(x, mem, norm_g, mem_norm_g, w_in, b_f, conv_w, conv_b, conv_ln_g, conv_ln_b, w_conv_pw, w_mem_kv, w_out, final_g, loss_target, m_norm_g, m_mem_norm_g, m_w_in, m_b_f, m_conv_w, m_conv_b, m_conv_ln_g, m_conv_ln_b, m_w_conv_pw, m_w_mem_kv, m_w_out, m_final_g, v_norm_g, v_mem_norm_g, v_w_in, v_b_f, v_conv_w, v_conv_b, v_conv_ln_g, v_conv_ln_b, v_w_conv_pw, v_w_mem_kv, v_w_out, v_final_g):
    S, D = x.shape[1], x.shape[2]
    M = mem.shape[1]
    xs, ms, tgt = x[0], mem[0], loss_target[0]
    cols = w_in.shape[2]

    g_in, g_cw, g_pw, g_kv, g_out = _gather_two_level(
        [w_in[0].astype(_BF), conv_w[0], w_conv_pw[0].astype(_BF), w_mem_kv[0].astype(_BF), w_out[0].astype(_BF)],
        "gather_weights")
    w_all = _split_w_in(jnp.transpose(g_in, (1, 0, 2)).reshape(D, N_DEV * cols), D)
    conv_w_full = jnp.transpose(g_cw, (1, 0, 2)).reshape(CONV_K, CONV_W)
    w_pw_full = g_pw.reshape(CONV_W, CONV_W)
    w_kv_full = g_kv.reshape(D, 2 * MEM_W)
    w_out_full = g_out.reshape(CONV_W + FOX_W + MEM_W, D)
    b_f_pad = jnp.pad(b_f, ((0, 0), (0, 128 - N_FOX)))

    h, r1 = _rms_fwd(xs, norm_g)
    p_f = _matmul(h, w_all, "nn", F32, "proj_f", b_cols=(0, PART_W))
    p_b = _matmul(h, w_all, "nn", _BF, "proj_b", b_cols=(PART_W, PART_W))
    mkv, mhat = _mem_kv_fwd(ms, mem_norm_g, w_kv_full)
    y_conv, u1 = _conv_fwd(p_f, conv_w_full, conv_b, conv_ln_g, conv_ln_b, w_pw_full)
    c_col, c_row = _fox_cumsum(p_f, b_f_pad)
    o_fox, y_fox, lse = _fox_fwd(p_b, p_f, c_col, c_row)
    y_mem = _mem_attn_fwd(p_b, p_f, mkv)
    y = jnp.concatenate([y_conv, y_fox, y_mem], axis=1)
    z = _matmul(y, w_out_full, "nn", F32, "out_proj")
    dx2, g_final, loss_part = _head_loss(xs, z, tgt, final_g.reshape(1, D))

    dy = _matmul(dx2, w_out_full, "nt", F32, "d_y")
    gw_out = _matmul(y, dx2, "tn", _BF, "gw_out")
    d_mq, d_mg, d_mk, d_mv = _mem_attn_bwd(p_b, p_f, mkv, dy)
    gw_kv, g_mem_norm = _mem_kv_bwd(jnp.concatenate([d_mk, d_mv], axis=1), mhat, mem_norm_g, w_kv_full)
    d_o, d_fg, rows = _fox_bwd_prep(dy, p_f, o_fox, lse, c_col)
    d_q, d_k, d_v, dr = _fox_bwd(p_b, d_o, rows, c_col)
    d_fl, g_bf = _fox_dlogf(dr, p_f, b_f_pad)
    du1, d_gc, gw_pw, g_ln_g, g_ln_b, g_cb = _conv_bwd1(u1, p_f, dy, conv_ln_g, conv_ln_b, w_pw_full)
    d_a, d_b, gw_cw = _conv_bwd2(du1, p_f, conv_w_full)
    dp = jnp.concatenate([d_a, d_b, d_gc, d_fg, d_mg, d_fl, d_q, d_k, d_v, d_mq], axis=1)
    gw_all = _matmul(h, dp, "tn", _BF, "gw_in")

    gw_in_cat = _merge_w_in(gw_all[:, :PART_W], gw_all[:, PART_W:])
    send_in = jnp.transpose(gw_in_cat.reshape(D, N_DEV, cols), (1, 0, 2))
    send_cw = jnp.transpose(gw_cw[:CONV_K].reshape(CONV_K, N_DEV, CONV_W // N_DEV), (1, 0, 2))
    send_pw = gw_pw.astype(_BF).reshape(N_DEV, CONV_W // N_DEV, CONV_W)
    send_kv = gw_kv.reshape(N_DEV, D // N_DEV, 2 * MEM_W)
    send_out = gw_out.reshape(N_DEV, (CONV_W + FOX_W + MEM_W) // N_DEV, D)
    sems, sent, lands, token = _scatter_start([send_in, send_cw, send_pw, send_kv, send_out])
    dh = _matmul(dp, w_all, "nt", F32, "d_h", after=token)
    grad_x, g_norm = _rms_bwd(xs, r1, dh, dx2, norm_g)
    sent, lands = _scatter_wait(sems, sent, lands, grad_x)
    me = 4 * lax.axis_index("x") + 2 * lax.axis_index("y") + lax.axis_index("c")
    own = [lax.dynamic_index_in_dim(s_, me, 0, keepdims=True) for s_ in sent]

    small = _pack_small(g_norm, g_mem_norm, g_final, g_cb, g_ln_g, g_ln_b, g_bf[:, :N_FOX], loss_part[0, 0:1], D)
    r_small, = _exchange([small], [False], "exchange_small")

    res = {}
    res["w_in"] = _adamw(lands[0], w_in, m_w_in, v_w_in, "adamw_w_in", own=own[0])
    res["conv_w"] = _adamw(lands[1], conv_w, m_conv_w, v_conv_w, "adamw_conv_w", own=own[1])
    res["w_conv_pw"] = _adamw(lands[2], w_conv_pw, m_w_conv_pw, v_w_conv_pw, "adamw_w_pw", own=own[2])
    res["w_mem_kv"] = _adamw(lands[3], w_mem_kv, m_w_mem_kv, v_w_mem_kv, "adamw_w_kv", own=own[3])
    res["w_out"] = _adamw(lands[4], w_out, m_w_out, v_w_out, "adamw_w_out", own=own[4])
    zero = jnp.zeros((1,), F32)
    pk = lambda a, b_, c_, d_, e, f_, g_: _pack_small(a, b_, c_, d_, e, f_, g_, zero, D)
    sm = _adamw(r_small,
                pk(norm_g, mem_norm_g, final_g, conv_b, conv_ln_g, conv_ln_b, b_f),
                pk(m_norm_g, m_mem_norm_g, m_final_g, m_conv_b, m_conv_ln_g, m_conv_ln_b, m_b_f),
                pk(v_norm_g, v_mem_norm_g, v_final_g, v_conv_b, v_conv_ln_g, v_conv_ln_b, v_b_f), "adamw_small")
    loss = sm[0][4, 0]
    small_out = [_unpack_small(a, D) for a in sm]
    names = ["norm_g", "mem_norm_g", "w_in", "b_f", "conv_w", "conv_b", "conv_ln_g", "conv_ln_b", "w_conv_pw",
             "w_mem_kv", "w_out", "final_g"]
    outs = [loss, grad_x[None]]
    for kind in range(4):
        for nme in names:
            outs.append(res[nme][kind] if nme in res else small_out[kind][nme])
    return tuple(outs)
```

```python
import functools

import jax
import jax.numpy as jnp
from jax import lax
from jax.experimental import pallas as pl
from jax.experimental.pallas import tpu as pltpu

F32 = jnp.float32
_BF = jnp.bfloat16
SDS = jax.ShapeDtypeStruct
MESH = pl.DeviceIdType.MESH

N_DEV = 8
HEAD = 128
N_FOX = 8
N_MEM = 4
CONV_W = 512
CONV_K = 31
FOX_W = N_FOX * HEAD
MEM_W = N_MEM * HEAD
D_IN = 3 * CONV_W + 4 * FOX_W + N_FOX + 2 * MEM_W
EPS = 1e-6
SCALE = HEAD ** -0.5
NEG = -1e30

ADAM_LR = 0.001
ADAM_B1 = 0.9
ADAM_B2 = 0.999
ADAM_EPS = 1e-08
ADAM_WD = 0.01
ADAM_STEP = 10

PART_W = 3584
F_A, F_B, F_GC, F_FG, F_MG, F_FL = 0, 512, 1024, 1536, 2560, 3072
B_Q, B_K, B_V, B_MQ = 0, 1024, 2048, 3072
O_A, O_B, O_GC = 0, 512, 1024
O_Q, O_K, O_V = 1536, 2560, 3584
O_FL, O_FG, O_MQ, O_MG = 4608, 4616, 5640, 6152

HALO = 32
T_ATT = 512
T_ROW = 256
T_CONV = 512
VMEM_LIMIT = 56 * 1024 * 1024
PACK_W = 2048


def _cp(*sem):
    return pltpu.CompilerParams(dimension_semantics=sem, vmem_limit_bytes=VMEM_LIMIT)


def _sigmoid(x):
    return jax.nn.sigmoid(x)


def _dsilu(x, sg):
    return sg * (1.0 + x * (1.0 - sg))


def _dot(a, b, dims):
    return lax.dot_general(a, b, (dims, ((), ())), preferred_element_type=F32)


NN = ((1,), (0,))
NT = ((1,), (1,))
TN = ((0,), (0,))


def _pick(n, pref):
    if n <= pref:
        return n
    t = pref - pref % 128
    while n % t:
        t -= 128
    return t


def _matmul(a, b, mode, out_dtype, name, tm=512, tn=1024, tk=2048, after=None, b_cols=None):
    col0 = 0
    if mode == "nn":
        (M, K), (K2, N) = a.shape, b.shape
        if b_cols is not None:
            col0, N = b_cols
    elif mode == "nt":
        (M, K), (N, K2) = a.shape, b.shape
    else:
        (K, M), (K2, N) = a.shape, b.shape
    assert K == K2
    tm, tn, tk = _pick(M, tm), _pick(N, tn), _pick(K, tk)
    assert M % tm == 0 and N % tn == 0 and K % tk == 0, (name, M, N, K)
    nk = K // tk
    dims = {"nn": NN, "nt": NT, "tn": TN}[mode]
    n_in = 2 if after is None else 3

    def body(*refs):
        a_ref, b_ref = refs[0], refs[1]
        o_ref = refs[n_in]
        p = _dot(a_ref[...].astype(_BF), b_ref[...].astype(_BF), dims)
        if nk == 1:
            o_ref[...] = p.astype(out_dtype)
            return
        acc_ref = refs[n_in + 1]
        k = pl.program_id(2)

        @pl.when(k == 0)
        def _():
            acc_ref[...] = p

        @pl.when(jnp.logical_and(k > 0, k < nk - 1))
        def _():
            acc_ref[...] += p

        @pl.when(k == nk - 1)
        def _():
            o_ref[...] = (acc_ref[...] + p).astype(out_dtype)

    if mode == "nn":
        assert col0 % tn == 0
        jb = col0 // tn
        a_spec = pl.BlockSpec((tm, tk), lambda i, j, k: (i, k))
        b_spec = pl.BlockSpec((tk, tn), lambda i, j, k: (k, j + jb))
    elif mode == "nt":
        a_spec = pl.BlockSpec((tm, tk), lambda i, j, k: (i, k))
        b_spec = pl.BlockSpec((tn, tk), lambda i, j, k: (j, k))
    else:
        a_spec = pl.BlockSpec((tk, tm), lambda i, j, k: (k, i))
        b_spec = pl.BlockSpec((tk, tn), lambda i, j, k: (k, j))
    in_specs, args = [a_spec, b_spec], [a, b]
    if after is not None:
        in_specs.append(pl.BlockSpec(memory_space=pl.ANY))
        args.append(after)
    return pl.pallas_call(
        body, grid=(M // tm, N // tn, nk), in_specs=in_specs,
        out_specs=pl.BlockSpec((tm, tn), lambda i, j, k: (i, j)),
        out_shape=SDS((M, N), out_dtype), scratch_shapes=[] if nk == 1 else [pltpu.VMEM((tm, tn), F32)],
        compiler_params=_cp("parallel", "parallel", "arbitrary"), name=name)(*args)


def _rms_fwd(x, g):
    S, D = x.shape
    tr = min(T_ROW, S)

    def body(x_ref, g_ref, h_ref, r_ref):
        xv = x_ref[...]
        r = lax.rsqrt(jnp.mean(xv * xv, axis=-1, keepdims=True) + EPS)
        h_ref[...] = (xv * r * g_ref[...]).astype(_BF)
        r_ref[...] = r

    return pl.pallas_call(
        body, grid=(S // tr,),
        in_specs=[pl.BlockSpec((tr, D), lambda i: (i, 0)), pl.BlockSpec((1, D), lambda i: (0, 0))],
        out_specs=[pl.BlockSpec((tr, D), lambda i: (i, 0)), pl.BlockSpec((tr, 1), lambda i: (i, 0))],
        out_shape=[SDS((S, D), _BF), SDS((S, 1), F32)],
        compiler_params=_cp("parallel"), name="rms_fwd")(x, g)


def _head_loss(x, z, target, g):
    S, D = x.shape
    tr = min(T_ROW, S)

    def body(x_ref, z_ref, t_ref, g_ref, dx2_ref, gg_ref, loss_ref):
        i = pl.program_id(0)
        x2 = x_ref[...] + z_ref[...]
        r = lax.rsqrt(jnp.mean(x2 * x2, axis=-1, keepdims=True) + EPS)
        xh = x2 * r
        gv = g_ref[...]
        diff = xh * gv - t_ref[...]
        lsum = 0.5 * jnp.sum(jnp.mean(diff * diff, axis=-1, keepdims=True), axis=0, keepdims=True)
        dout = diff * (1.0 / D)
        gd = dout * gv
        dx2_ref[...] = r * (gd - xh * jnp.mean(gd * xh, axis=-1, keepdims=True))
        gg = jnp.sum(dout * xh, axis=0, keepdims=True)

        @pl.when(i == 0)
        def _():
            gg_ref[...] = gg
            loss_ref[...] = jnp.broadcast_to(lsum, (1, 128))

        @pl.when(i > 0)
        def _():
            gg_ref[...] += gg
            loss_ref[...] += jnp.broadcast_to(lsum, (1, 128))

    row = pl.BlockSpec((tr, D), lambda i: (i, 0))
    return pl.pallas_call(
        body, grid=(S // tr,), in_specs=[row, row, row, pl.BlockSpec((1, D), lambda i: (0, 0))],
        out_specs=[row, pl.BlockSpec((1, D), lambda i: (0, 0)), pl.BlockSpec((1, 128), lambda i: (0, 0))],
        out_shape=[SDS((S, D), F32), SDS((1, D), F32), SDS((1, 128), F32)],
        compiler_params=_cp("arbitrary"), name="head_loss")(x, z, target, g)


def _rms_bwd(x, r, dh, dx2, g):
    S, D = x.shape
    tr = min(T_ROW, S)

    def body(x_ref, r_ref, dh_ref, dx2_ref, g_ref, gx_ref, gg_ref):
        i = pl.program_id(0)
        rv = r_ref[...]
        xh = x_ref[...] * rv
        dh_ = dh_ref[...]
        gd = dh_ * g_ref[...]
        gx_ref[...] = dx2_ref[...] + rv * (gd - xh * jnp.mean(gd * xh, axis=-1, keepdims=True))
        gg = jnp.sum(dh_ * xh, axis=0, keepdims=True)

        @pl.when(i == 0)
        def _():
            gg_ref[...] = gg

        @pl.when(i > 0)
        def _():
            gg_ref[...] += gg

    row = pl.BlockSpec((tr, D), lambda i: (i, 0))
    return pl.pallas_call(
        body, grid=(S // tr,),
        in_specs=[row, pl.BlockSpec((tr, 1), lambda i: (i, 0)), row, row, pl.BlockSpec((1, D), lambda i: (0, 0))],
        out_specs=[row, pl.BlockSpec((1, D), lambda i: (0, 0))],
        out_shape=[SDS((S, D), F32), SDS((1, D), F32)],
        compiler_params=_cp("arbitrary"), name="rms_bwd")(x, r, dh, dx2, g)


def _mem_kv_fwd(mem, g, w_kv):
    M, D = mem.shape

    def body(mem_ref, g_ref, w_ref, mkv_ref, mhat_ref):
        mv = mem_ref[...]
        mhat = mv * lax.rsqrt(jnp.mean(mv * mv, axis=-1, keepdims=True) + EPS)
        mhat_ref[...] = mhat
        mkv_ref[...] = _dot((mhat * g_ref[...]).astype(_BF), w_ref[...], NN).astype(_BF)

    return pl.pallas_call(
        body, out_shape=[SDS((M, 2 * MEM_W), _BF), SDS((M, D), F32)],
        compiler_params=pltpu.CompilerParams(vmem_limit_bytes=VMEM_LIMIT), name="mem_kv_fwd")(mem, g, w_kv)


def _mem_kv_bwd(dmkv, mhat, g, w_kv):
    M, D = mhat.shape

    def body(d_ref, mhat_ref, g_ref, w_ref, gw_ref, gg_ref):
        d = d_ref[...].astype(_BF)
        mhat = mhat_ref[...]
        gw_ref[...] = _dot((mhat * g_ref[...]).astype(_BF), d, TN).astype(_BF)
        dmh = _dot(d, w_ref[...], NT)
        gg_ref[...] = jnp.sum(dmh * mhat, axis=0, keepdims=True)

    return pl.pallas_call(
        body, out_shape=[SDS((D, 2 * MEM_W), _BF), SDS((1, D), F32)],
        compiler_params=pltpu.CompilerParams(vmem_limit_bytes=VMEM_LIMIT), name="mem_kv_bwd")(dmkv, mhat, g, w_kv)


def _mem_attn_fwd(p_b, p_f, mkv):
    S = p_b.shape[0]
    M = mkv.shape[0]
    tq = min(T_ATT, S)

    def body(q_ref, kv_ref, g_ref, y_ref):
        for hd in range(N_MEM):
            cols = slice(hd * HEAD, (hd + 1) * HEAD)
            s = _dot(q_ref[:, cols], kv_ref[:, cols], NT) * SCALE
            m = jnp.max(s, axis=-1, keepdims=True)
            e = jnp.exp(s - m)
            p = e / jnp.sum(e, axis=-1, keepdims=True)
            o = _dot(p.astype(_BF), kv_ref[:, MEM_W + hd * HEAD:MEM_W + (hd + 1) * HEAD], NN)
            gv = g_ref[:, cols]
            y_ref[:, cols] = (o * (gv * _sigmoid(gv))).astype(_BF)

    return pl.pallas_call(
        body, grid=(S // tq,),
        in_specs=[pl.BlockSpec((tq, MEM_W), lambda i: (i, B_MQ // MEM_W)),
                  pl.BlockSpec((M, 2 * MEM_W), lambda i: (0, 0)),
                  pl.BlockSpec((tq, MEM_W), lambda i: (i, F_MG // MEM_W))],
        out_specs=pl.BlockSpec((tq, MEM_W), lambda i: (i, 0)),
        out_shape=SDS((S, MEM_W), _BF), compiler_params=_cp("parallel"), name="mem_attn_fwd")(p_b, mkv, p_f)


def _mem_attn_bwd(p_b, p_f, mkv, dy):
    S = p_b.shape[0]
    M = mkv.shape[0]
    tq = min(T_ATT, S)

    def body(q_ref, kv_ref, g_ref, dy_ref, dq_ref, dg_ref, dk_ref, dv_ref):
        i = pl.program_id(0)

        @pl.when(i == 0)
        def _():
            dk_ref[...] = jnp.zeros_like(dk_ref)
            dv_ref[...] = jnp.zeros_like(dv_ref)

        for hd in range(N_MEM):
            cols = slice(hd * HEAD, (hd + 1) * HEAD)
            q, k = q_ref[:, cols], kv_ref[:, cols]
            v = kv_ref[:, MEM_W + hd * HEAD:MEM_W + (hd + 1) * HEAD]
            s = _dot(q, k, NT) * SCALE
            m = jnp.max(s, axis=-1, keepdims=True)
            e = jnp.exp(s - m)
            p = e / jnp.sum(e, axis=-1, keepdims=True)
            pb = p.astype(_BF)
            o = _dot(pb, v, NN)
            gv = g_ref[:, cols]
            sg = _sigmoid(gv)
            dyv = dy_ref[:, cols]
            do = dyv * (gv * sg)
            dg_ref[:, cols] = (dyv * o * _dsilu(gv, sg)).astype(_BF)
            dob = do.astype(_BF)
            dp = _dot(dob, v, NT)
            ds = p * (dp - jnp.sum(dp * p, axis=-1, keepdims=True)) * SCALE
            dsb = ds.astype(_BF)
            dq_ref[:, cols] = _dot(dsb, k, NN).astype(_BF)
            dk_ref[:, cols] += _dot(dsb, q, TN)
            dv_ref[:, cols] += _dot(pb, dob, TN)

    tile = pl.BlockSpec((tq, MEM_W), lambda i: (i, 0))
    kv = pl.BlockSpec((M, MEM_W), lambda i: (0, 0))
    return pl.pallas_call(
        body, grid=(S // tq,),
        in_specs=[pl.BlockSpec((tq, MEM_W), lambda i: (i, B_MQ // MEM_W)),
                  pl.BlockSpec((M, 2 * MEM_W), lambda i: (0, 0)),
                  pl.BlockSpec((tq, MEM_W), lambda i: (i, F_MG // MEM_W)),
                  pl.BlockSpec((tq, MEM_W), lambda i: (i, (CONV_W + FOX_W) // MEM_W))],
        out_specs=[tile, tile, kv, kv],
        out_shape=[SDS((S, MEM_W), _BF), SDS((S, MEM_W), _BF), SDS((M, MEM_W), F32), SDS((M, MEM_W), F32)],
        compiler_params=_cp("arbitrary"), name="mem_attn_bwd")(p_b, mkv, p_f, dy)


def _fox_cumsum(p_f, b_f_pad):
    S = p_f.shape[0]
    tr = min(256, S)
    fb = F_FL // 128

    def body(z_ref, b_ref, col_ref, row_ref, carry_ref):
        i = pl.program_id(0)

        @pl.when(i == 0)
        def _():
            carry_ref[...] = jnp.zeros_like(carry_ref)

        z = z_ref[...] + b_ref[...]
        lf = jnp.minimum(z, 0.0) - jnp.log1p(jnp.exp(-jnp.abs(z)))
        r = lax.broadcasted_iota(jnp.int32, (tr, tr), 0)
        c = lax.broadcasted_iota(jnp.int32, (tr, tr), 1)
        tri = (c <= r).astype(F32)
        cs = jnp.dot(tri, lf, precision=lax.Precision.HIGHEST, preferred_element_type=F32) + carry_ref[...]
        carry_ref[...] = cs[tr - 1:tr, :]
        cst = cs.T
        for hd in range(N_FOX):
            col_ref[hd] = cs[:, hd:hd + 1]
            row_ref[hd] = cst[hd:hd + 1, :]

    return pl.pallas_call(
        body, grid=(S // tr,),
        in_specs=[pl.BlockSpec((tr, 128), lambda i: (i, fb)), pl.BlockSpec((1, 128), lambda i: (0, 0))],
        out_specs=[pl.BlockSpec((N_FOX, tr, 1), lambda i: (0, i, 0)), pl.BlockSpec((N_FOX, 1, tr), lambda i: (0, 0, i))],
        out_shape=[SDS((N_FOX, S, 1), F32), SDS((N_FOX, 1, S), F32)], scratch_shapes=[pltpu.VMEM((1, 128), F32)],
        compiler_params=_cp("arbitrary"), name="fox_cumsum")(p_f, b_f_pad)


def _fox_dlogf(dr, p_f, b_f_pad):
    S = p_f.shape[0]
    tr = min(256, S)
    nb = S // tr
    fb = F_FL // 128
    wide = PART_W - F_FL

    def body(dr_ref, z_ref, b_ref, dz_ref, gb_ref, carry_ref):
        i = pl.program_id(0)

        @pl.when(i == 0)
        def _():
            carry_ref[...] = jnp.zeros_like(carry_ref)
            gb_ref[...] = jnp.zeros_like(gb_ref)

        heads = [dr_ref[hd, 0:1, :] + dr_ref[hd, 1:2, :] for hd in range(N_FOX)]
        dc = jnp.concatenate(heads + [jnp.zeros((128 - N_FOX, tr), F32)], axis=0).T
        r = lax.broadcasted_iota(jnp.int32, (tr, tr), 0)
        c = lax.broadcasted_iota(jnp.int32, (tr, tr), 1)
        tri = (c >= r).astype(F32)
        rc = jnp.dot(tri, dc, precision=lax.Precision.HIGHEST, preferred_element_type=F32) + carry_ref[...]
        carry_ref[...] = rc[0:1, :]
        z = z_ref[...] + b_ref[...]
        dz = rc * _sigmoid(-z)
        gb_ref[...] += jnp.sum(dz, axis=0, keepdims=True)
        dz_ref[...] = jnp.concatenate([dz.astype(_BF), jnp.zeros((tr, wide - 128), _BF)], axis=1)

    return pl.pallas_call(
        body, grid=(nb,),
        in_specs=[pl.BlockSpec((N_FOX, 8, tr), lambda i: (0, 0, nb - 1 - i)),
                  pl.BlockSpec((tr, 128), lambda i: (nb - 1 - i, fb)), pl.BlockSpec((1, 128), lambda i: (0, 0))],
        out_specs=[pl.BlockSpec((tr, wide), lambda i: (nb - 1 - i, 0)), pl.BlockSpec((1, 128), lambda i: (0, 0))],
        out_shape=[SDS((S, wide), _BF), SDS((1, 128), F32)], scratch_shapes=[pltpu.VMEM((1, 128), F32)],
        compiler_params=_cp("arbitrary"), name="fox_dlogf")(dr, p_f, b_f_pad)


def _fox_fwd(p_b, p_f, c_col, c_row):
    S = p_b.shape[0]
    t = min(2 * T_ATT, S)
    nq = S // t
    rc = min(256, t)
    qb, kb, vb, gb = B_Q // HEAD, B_K // HEAD, B_V // HEAD, F_FG // HEAD
    kq = SCALE * 1.4426950408889634

    def body(q_ref, k_ref, v_ref, cc_ref, cr_ref, g_ref, o_ref, y_ref, lse_ref, va_ref, ua_ref, ub_ref, m_ref, acc_ref):
        i = pl.program_id(1)

        @pl.when(i == 0)
        def _():
            va_ref[:, 0:HEAD] = v_ref[...]
            lane = lax.broadcasted_iota(jnp.int32, (S, HEAD), 1)
            va_ref[:, HEAD:2 * HEAD] = jnp.where(lane == 0, 1.0, 0.0).astype(_BF)

        m_ref[...] = jnp.full_like(m_ref, NEG)
        acc_ref[...] = jnp.zeros_like(acc_ref)

        def scores(b, u_ref):
            off = pl.multiple_of(b * t, t)
            k = k_ref[pl.ds(off, t), :]
            csr = cr_ref[:, pl.ds(off, t)] * (1.0 / SCALE)
            for r in range(0, t, rc):
                u_ref[r:r + rc, :] = _dot(q_ref[r:r + rc, :], k, NT) - csr

        def absorb(b, u_ref, masked):
            va = va_ref[pl.ds(pl.multiple_of(b * t, t), t), :]
            for r in range(0, t, rc):
                u = u_ref[r:r + rc, :]
                if masked:
                    row = lax.broadcasted_iota(jnp.int32, (rc, t), 0) + r
                    col = lax.broadcasted_iota(jnp.int32, (rc, t), 1)
                    u = jnp.where(col <= row, u, NEG)
                m_old = m_ref[r:r + rc, :]
                m_new = jnp.maximum(m_old, jnp.max(u, axis=-1, keepdims=True))
                alpha = jnp.exp2((m_old - m_new) * kq)
                p = jnp.exp2((u - m_new) * kq)
                acc_ref[r:r + rc, :] = alpha * acc_ref[r:r + rc, :] + _dot(p.astype(_BF), va, NN)
                m_ref[r:r + rc, :] = m_new

        scores(0, ua_ref)

        def pair(pi, carry):
            b = 2 * pi
            scores(b + 1, ub_ref)
            absorb(b, ua_ref, False)
            scores(b + 2, ua_ref)
            absorb(b + 1, ub_ref, False)
            return carry

        lax.fori_loop(0, i // 2, pair, 0)

        @pl.when(i % 2 == 1)
        def _():
            scores(i, ub_ref)
            absorb(i - 1, ua_ref, False)
            absorb(i, ub_ref, True)

        @pl.when(i % 2 == 0)
        def _():
            absorb(i, ua_ref, True)

        l = acc_ref[:, HEAD:HEAD + 1]
        o = acc_ref[:, 0:HEAD] / l
        gv = g_ref[...]
        o_ref[...] = o.astype(_BF)
        y_ref[...] = (o * (gv * _sigmoid(gv))).astype(_BF)
        lse_ref[...] = cc_ref[...] + SCALE * m_ref[...] + jnp.log(l)

    tile = pl.BlockSpec((t, HEAD), lambda h, i: (i, h))
    return pl.pallas_call(
        body, grid=(N_FOX, nq),
        in_specs=[pl.BlockSpec((t, HEAD), lambda h, i: (i, qb + h)),
                  pl.BlockSpec((S, HEAD), lambda h, i: (0, kb + h)),
                  pl.BlockSpec((S, HEAD), lambda h, i: (0, vb + h)),
                  pl.BlockSpec((None, t, 1), lambda h, i: (h, i, 0)),
                  pl.BlockSpec((None, 1, S), lambda h, i: (h, 0, 0)),
                  pl.BlockSpec((t, HEAD), lambda h, i: (i, gb + h))],
        out_specs=[tile, tile, pl.BlockSpec((None, t, 1), lambda h, i: (h, i, 0))],
        out_shape=[SDS((S, FOX_W), _BF), SDS((S, FOX_W), _BF), SDS((N_FOX, S, 1), F32)],
        scratch_shapes=[pltpu.VMEM((S, 2 * HEAD), _BF), pltpu.VMEM((t, t), F32), pltpu.VMEM((t, t), F32),
                        pltpu.VMEM((t, 1), F32), pltpu.VMEM((t, 2 * HEAD), F32)],
        compiler_params=_cp("parallel", "arbitrary"), name="fox_fwd")(p_b, p_b, p_b, c_col, c_row, p_f)


def _fox_bwd_prep(dy, p_f, o, lse, c_col):
    S = dy.shape[0]
    t = min(T_ATT, S)
    hg = 4
    wd = hg * HEAD

    def body(dy_ref, g_ref, o_ref, lse_ref, cc_ref, do_ref, dg_ref, rows_ref):
        gv = g_ref[...]
        sg = _sigmoid(gv)
        dyv = dy_ref[...]
        ov = o_ref[...].astype(F32)
        do = dyv * (gv * sg)
        do_ref[...] = do.astype(_BF)
        dg_ref[...] = (dyv * ov * _dsilu(gv, sg)).astype(_BF)
        prod = do * ov
        lane = lax.broadcasted_iota(jnp.int32, (t, 128), 1)
        for hd in range(hg):
            delta = jnp.sum(prod[:, hd * HEAD:(hd + 1) * HEAD], axis=-1, keepdims=True)
            a = cc_ref[hd] - lse_ref[hd]
            mat = jnp.where(lane == 0, a, jnp.where(lane == 1, delta, 0.0))
            rows_ref[hd] = mat.T[0:8, :]

    tile = pl.BlockSpec((t, wd), lambda g, i: (i, g))
    col = pl.BlockSpec((hg, t, 1), lambda g, i: (g, i, 0))
    return pl.pallas_call(
        body, grid=(N_FOX // hg, S // t),
        in_specs=[pl.BlockSpec((t, wd), lambda g, i: (i, CONV_W // wd + g)),
                  pl.BlockSpec((t, wd), lambda g, i: (i, F_FG // wd + g)), tile, col, col],
        out_specs=[tile, tile, pl.BlockSpec((hg, 8, t), lambda g, i: (g, 0, i))],
        out_shape=[SDS((S, FOX_W), _BF), SDS((S, FOX_W), _BF), SDS((N_FOX, 8, S), F32)],
        compiler_params=_cp("parallel", "parallel"), name="fox_bwd_prep")(dy, p_f, o, lse, c_col)


def _fox_bwd(p_b, do, rows, c_col):
    S = p_b.shape[0]
    t = min(T_ATT, S)
    nk = S // t
    qb, kb, vb = B_Q // HEAD, B_K // HEAD, B_V // HEAD

    def body(k_ref, v_ref, q_ref, do_ref, rows_ref, cc_ref, dq_ref, dk_ref, dv_ref, dr_ref,
             dqt_ref, dka_ref, dva_ref, dca_ref, dra_ref, sa_ref, pa_ref, sb_ref, pb_ref):
        j = pl.program_id(1)

        @pl.when(j == 0)
        def _():
            dqt_ref[...] = jnp.zeros_like(dqt_ref)
            dra_ref[...] = jnp.zeros_like(dra_ref)

        k = k_ref[...]
        v = v_ref[...]
        kt = k.astype(F32).T.astype(_BF)
        cc = cc_ref[...]
        dka_ref[...] = jnp.zeros_like(dka_ref)
        dva_ref[...] = jnp.zeros_like(dva_ref)
        dca_ref[...] = jnp.zeros_like(dca_ref)

        def scores(i, s_ref, p_ref):
            off = pl.multiple_of(i * t, t)
            s_ref[...] = _dot(k, q_ref[pl.ds(off, t), :], NT) * SCALE + (rows_ref[0:1, pl.ds(off, t)] - cc)
            p_ref[...] = _dot(v, do_ref[pl.ds(off, t), :], NT) - rows_ref[1:2, pl.ds(off, t)]

        def absorb(i, s_ref, p_ref, masked):
            off = pl.multiple_of(i * t, t)
            st = s_ref[...]
            if masked:
                srow = lax.broadcasted_iota(jnp.int32, (t, t), 0)
                tcol = lax.broadcasted_iota(jnp.int32, (t, t), 1)
                st = jnp.where(srow <= tcol, st, NEG)
            pt = jnp.exp(st)
            dva_ref[...] += _dot(pt.astype(_BF), do_ref[pl.ds(off, t), :], NN)
            dst = pt * p_ref[...]
            part = dst[:, 0:128]
            for gidx in range(1, t // 128):
                part = part + dst[:, gidx * 128:(gidx + 1) * 128]
            dca_ref[...] += part
            dra_ref[0:1, pl.ds(off, t)] += jnp.sum(dst, axis=0, keepdims=True)
            dsb = dst.astype(_BF)
            dka_ref[...] += _dot(dsb, q_ref[pl.ds(off, t), :], NN)
            dqt_ref[:, pl.ds(off, t)] += _dot(kt, dsb, NN)

        rest = nk - 1 - j
        scores(j, sa_ref, pa_ref)

        @pl.when(rest == 0)
        def _():
            absorb(j, sa_ref, pa_ref, True)

        @pl.when(rest > 0)
        def _():
            scores(j + 1, sb_ref, pb_ref)
            absorb(j, sa_ref, pa_ref, True)

            def pair(pi, carry):
                c = j + 1 + 2 * pi
                scores(c + 1, sa_ref, pa_ref)
                absorb(c, sb_ref, pb_ref, False)
                scores(c + 2, sb_ref, pb_ref)
                absorb(c + 1, sa_ref, pa_ref, False)
                return carry

            lax.fori_loop(0, (rest - 1) // 2, pair, 0)

            @pl.when(rest % 2 == 1)
            def _():
                absorb(nk - 1, sb_ref, pb_ref, False)

            @pl.when(rest % 2 == 0)
            def _():
                scores(nk - 1, sa_ref, pa_ref)
                absorb(nk - 2, sb_ref, pb_ref, False)
                absorb(nk - 1, sa_ref, pa_ref, False)

        dk_ref[...] = (dka_ref[...] * SCALE).astype(_BF)
        dv_ref[...] = dva_ref[...].astype(_BF)
        dra_ref[1:2, pl.ds(pl.multiple_of(j * t, t), t)] = -jnp.sum(dca_ref[...].T, axis=0, keepdims=True)

        @pl.when(j == nk - 1)
        def _():
            dr_ref[...] = dra_ref[...]
            for ci in range(nk):
                dq_ref[ci * t:(ci + 1) * t, :] = (dqt_ref[:, ci * t:(ci + 1) * t].T * SCALE).astype(_BF)

    tile = pl.BlockSpec((t, HEAD), lambda h, j: (j, h))
    return pl.pallas_call(
        body, grid=(N_FOX, nk),
        in_specs=[pl.BlockSpec((t, HEAD), lambda h, j: (j, kb + h)),
                  pl.BlockSpec((t, HEAD), lambda h, j: (j, vb + h)),
                  pl.BlockSpec((S, HEAD), lambda h, j: (0, qb + h)),
                  pl.BlockSpec((S, HEAD), lambda h, j: (0, h)),
                  pl.BlockSpec((None, 8, S), lambda h, j: (h, 0, 0)),
                  pl.BlockSpec((None, t, 1), lambda h, j: (h, j, 0))],
        out_specs=[pl.BlockSpec((S, HEAD), lambda h, j: (0, h)), tile, tile,
                   pl.BlockSpec((None, 8, S), lambda h, j: (h, 0, 0))],
        out_shape=[SDS((S, FOX_W), _BF), SDS((S, FOX_W), _BF), SDS((S, FOX_W), _BF), SDS((N_FOX, 8, S), F32)],
        scratch_shapes=[pltpu.VMEM((HEAD, S), F32), pltpu.VMEM((t, HEAD), F32), pltpu.VMEM((t, HEAD), F32),
                        pltpu.VMEM((t, 128), F32), pltpu.VMEM((8, S), F32)] + [pltpu.VMEM((t, t), F32)] * 4,
        compiler_params=_cp("parallel", "arbitrary"), name="fox_bwd")(p_b, p_b, p_b, do, rows, c_col)


CHUNK = 32


SUBLANES = 8


def _shifted_windows(win):
    n = win.shape[0]
    return [win] + [pltpu.roll(win, n - s, axis=0) for s in range(1, SUBLANES)]


def _tap(rot, f):
    return rot[f % SUBLANES][f - f % SUBLANES:f - f % SUBLANES + CHUNK, :]


def _conv_taps(ext_ref, w_ref, first, out_fn, n_rows):
    def chunk(c, carry):
        r0 = pl.multiple_of(c * CHUNK, CHUNK)
        rot = _shifted_windows(ext_ref[pl.ds(r0, 2 * CHUNK), :])
        acc = jnp.zeros((CHUNK, CONV_W), F32)
        for k in range(CONV_K):
            acc = acc + w_ref[k:k + 1, :] * _tap(rot, first(k))
        out_fn(r0, acc)
        return carry

    lax.fori_loop(0, n_rows // CHUNK, chunk, 0)


def _conv_fwd(p_f, conv_w, conv_b, ln_g, ln_b, w_pw):
    S = p_f.shape[0]
    tc = min(T_CONV, S)
    hb = tc // HALO

    def body(a_ref, b_ref, gc_ref, ap_ref, bp_ref, w_ref, cb_ref, lg_ref, lb_ref, pw_ref, y_ref, u1_ref, ext_ref):
        i = pl.program_id(0)
        prev = ap_ref[...] * _sigmoid(bp_ref[...])
        ext_ref[0:HALO, :] = jnp.where(i > 0, prev, 0.0)
        ext_ref[HALO:HALO + tc, :] = a_ref[...] * _sigmoid(b_ref[...])
        cb = cb_ref[...]

        def put(r0, acc):
            u1_ref[pl.ds(r0, CHUNK), :] = acc + cb

        _conv_taps(ext_ref, w_ref, lambda k: HALO - (CONV_K - 1) + k, put, tc)
        u1 = u1_ref[...]
        mu = jnp.mean(u1, axis=-1, keepdims=True)
        d = u1 - mu
        rstd = lax.rsqrt(jnp.mean(d * d, axis=-1, keepdims=True) + EPS)
        u2 = d * rstd * lg_ref[...] + lb_ref[...]
        u3 = u2 * _sigmoid(u2)
        pw = _dot(u3.astype(_BF), pw_ref[...], NN)
        gc = gc_ref[...]
        y_ref[...] = (pw * (gc * _sigmoid(gc))).astype(_BF)

    blk = lambda cb_: pl.BlockSpec((tc, CONV_W), lambda i: (i, cb_))
    halo = lambda cb_: pl.BlockSpec((HALO, CONV_W), lambda i: (jnp.maximum(i * hb - 1, 0), cb_))
    vec = pl.BlockSpec((1, CONV_W), lambda i: (0, 0))
    return pl.pallas_call(
        body, grid=(S // tc,),
        in_specs=[blk(0), blk(1), blk(2), halo(0), halo(1), pl.BlockSpec((CONV_K, CONV_W), lambda i: (0, 0)),
                  vec, vec, vec, pl.BlockSpec((CONV_W, CONV_W), lambda i: (0, 0))],
        out_specs=[pl.BlockSpec((tc, CONV_W), lambda i: (i, 0)), pl.BlockSpec((tc, CONV_W), lambda i: (i, 0))],
        out_shape=[SDS((S, CONV_W), _BF), SDS((S, CONV_W), F32)],
        scratch_shapes=[pltpu.VMEM((tc + 2 * HALO, CONV_W), F32)],
        compiler_params=_cp("parallel"), name="conv_fwd")(p_f, p_f, p_f, p_f, p_f, conv_w, conv_b, ln_g, ln_b, w_pw)


def _conv_bwd1(u1, p_f, dy, ln_g, ln_b, w_pw):
    S = u1.shape[0]
    tc = min(T_CONV, S)

    def body(u1_ref, gc_ref, dy_ref, lg_ref, lb_ref, pw_ref, du1_ref, dgc_ref, gpw_ref, glg_ref, glb_ref, gcb_ref):
        i = pl.program_id(0)
        u1v = u1_ref[...]
        mu = jnp.mean(u1v, axis=-1, keepdims=True)
        d = u1v - mu
        rstd = lax.rsqrt(jnp.mean(d * d, axis=-1, keepdims=True) + EPS)
        xh = d * rstd
        lg = lg_ref[...]
        u2 = xh * lg + lb_ref[...]
        sg2 = _sigmoid(u2)
        u3b = (u2 * sg2).astype(_BF)
        w = pw_ref[...]
        pw = _dot(u3b, w, NN)
        gc = gc_ref[...]
        sgc = _sigmoid(gc)
        dyv = dy_ref[...]
        dpw = (dyv * (gc * sgc)).astype(_BF)
        dgc_ref[...] = (dyv * pw * _dsilu(gc, sgc)).astype(_BF)
        gpw = _dot(u3b, dpw, TN)
        du2 = _dot(dpw, w, NT) * _dsilu(u2, sg2)
        glg = jnp.sum(du2 * xh, axis=0, keepdims=True)
        glb = jnp.sum(du2, axis=0, keepdims=True)
        dxh = du2 * lg
        du1 = rstd * (dxh - jnp.mean(dxh, axis=-1, keepdims=True) - xh * jnp.mean(dxh * xh, axis=-1, keepdims=True))
        du1_ref[...] = du1
        gcb = jnp.sum(du1, axis=0, keepdims=True)

        @pl.when(i == 0)
        def _():
            gpw_ref[...] = gpw
            glg_ref[...] = glg
            glb_ref[...] = glb
            gcb_ref[...] = gcb

        @pl.when(i > 0)
        def _():
            gpw_ref[...] += gpw
            glg_ref[...] += glg
            glb_ref[...] += glb
            gcb_ref[...] += gcb

    row = pl.BlockSpec((tc, CONV_W), lambda i: (i, 0))
    vec = pl.BlockSpec((1, CONV_W), lambda i: (0, 0))
    sq = pl.BlockSpec((CONV_W, CONV_W), lambda i: (0, 0))
    return pl.pallas_call(
        body, grid=(S // tc,),
        in_specs=[row, pl.BlockSpec((tc, CONV_W), lambda i: (i, F_GC // CONV_W)), row, vec, vec, sq],
        out_specs=[row, row, sq, vec, vec, vec],
        out_shape=[SDS((S, CONV_W), F32), SDS((S, CONV_W), _BF), SDS((CONV_W, CONV_W), F32),
                   SDS((1, CONV_W), F32), SDS((1, CONV_W), F32), SDS((1, CONV_W), F32)],
        compiler_params=_cp("arbitrary"), name="conv_bwd1")(u1, p_f, dy, ln_g, ln_b, w_pw)


def _conv_bwd2(du1, p_f, conv_w):
    S = du1.shape[0]
    tc = min(T_CONV, S)
    hb = tc // HALO
    nblk = S // tc
    last_halo = S // HALO - 1

    def body(d_ref, dn_ref, a_ref, b_ref, ap_ref, bp_ref, w_ref, da_ref, db_ref, gw_ref, ext_ref, dext_ref, du0_ref):
        i = pl.program_id(0)
        av = a_ref[...]
        sb = _sigmoid(b_ref[...])
        prev = ap_ref[...] * _sigmoid(bp_ref[...])
        ext_ref[0:HALO, :] = jnp.where(i > 0, prev, 0.0)
        ext_ref[HALO:HALO + tc, :] = av * sb
        dext_ref[0:tc, :] = d_ref[...]
        dext_ref[tc:tc + HALO, :] = jnp.where(i < nblk - 1, dn_ref[...], 0.0)

        def put(r0, acc):
            du0_ref[pl.ds(r0, CHUNK), :] = acc

        _conv_taps(dext_ref, w_ref, lambda k: CONV_K - 1 - k, put, tc)
        du0 = du0_ref[...]
        da_ref[...] = (du0 * sb).astype(_BF)
        db_ref[...] = (du0 * av * sb * (1.0 - sb)).astype(_BF)

        def chunk(c, carry):
            r0 = pl.multiple_of(c * CHUNK, CHUNK)
            rot = _shifted_windows(ext_ref[pl.ds(r0, 2 * CHUNK), :])
            dv = dext_ref[pl.ds(r0, CHUNK), :]
            rows = [jnp.sum(dv * _tap(rot, HALO - (CONV_K - 1) + k), axis=0, keepdims=True) for k in range(CONV_K)]
            rows.append(jnp.zeros((1, CONV_W), F32))
            return carry + jnp.concatenate(rows, axis=0)

        gw = lax.fori_loop(0, tc // CHUNK, chunk, jnp.zeros((CONV_K + 1, CONV_W), F32))

        @pl.when(i == 0)
        def _():
            gw_ref[...] = gw

        @pl.when(i > 0)
        def _():
            gw_ref[...] += gw

    row = pl.BlockSpec((tc, CONV_W), lambda i: (i, 0))
    blk = lambda cb_: pl.BlockSpec((tc, CONV_W), lambda i: (i, cb_))
    halo = lambda cb_: pl.BlockSpec((HALO, CONV_W), lambda i: (jnp.maximum(i * hb - 1, 0), cb_))
    nxt = pl.BlockSpec((HALO, CONV_W), lambda i: (jnp.minimum((i + 1) * hb, last_halo), 0))
    return pl.pallas_call(
        body, grid=(nblk,),
        in_specs=[row, nxt, blk(0), blk(1), halo(0), halo(1), pl.BlockSpec((CONV_K, CONV_W), lambda i: (0, 0))],
        out_specs=[row, row, pl.BlockSpec((CONV_K + 1, CONV_W), lambda i: (0, 0))],
        out_shape=[SDS((S, CONV_W), _BF), SDS((S, CONV_W), _BF), SDS((CONV_K + 1, CONV_W), F32)],
        scratch_shapes=[pltpu.VMEM((tc + 2 * HALO, CONV_W), F32), pltpu.VMEM((tc + 2 * HALO, CONV_W), F32),
                        pltpu.VMEM((tc, CONV_W), F32)],
        compiler_params=_cp("arbitrary"), name="conv_bwd2")(du1, du1, p_f, p_f, p_f, p_f, conv_w)


def _exchange(srcs, scatter, name):
    n = len(srcs)
    out_shape = [SDS((N_DEV,) + (s.shape[1:] if sc else s.shape), s.dtype) for s, sc in zip(srcs, scatter)]

    def body(*refs):
        src_refs, dst_refs = refs[:n], refs[n:2 * n]
        send_sems, recv_sems, local_sems = refs[2 * n:]
        x, y, c = lax.axis_index("x"), lax.axis_index("y"), lax.axis_index("c")
        me = 4 * x + 2 * y + c
        copies = []
        for a in range(n):
            for f in range(1, N_DEV):
                px = 1 - x if f & 4 else x
                py = 1 - y if f & 2 else y
                pc = 1 - c if f & 1 else c
                peer = 4 * px + 2 * py + pc
                src = src_refs[a].at[peer] if scatter[a] else src_refs[a]
                cp = pltpu.make_async_remote_copy(
                    src_ref=src, dst_ref=dst_refs[a].at[me], send_sem=send_sems.at[a, f - 1],
                    recv_sem=recv_sems.at[a, f - 1], device_id=(px, py, pc), device_id_type=MESH)
                cp.start()
                copies.append(cp)
            own = src_refs[a].at[me] if scatter[a] else src_refs[a]
            lc = pltpu.make_async_copy(own, dst_refs[a].at[me], local_sems.at[a])
            lc.start()
            copies.append(lc)
        for cp in copies:
            cp.wait()

    anyspec = pl.BlockSpec(memory_space=pl.ANY)
    return pl.pallas_call(
        body, in_specs=[anyspec] * n, out_specs=[anyspec] * n, out_shape=out_shape,
        scratch_shapes=[pltpu.SemaphoreType.DMA((n, N_DEV - 1)), pltpu.SemaphoreType.DMA((n, N_DEV - 1)),
                        pltpu.SemaphoreType.DMA((n,))],
        name=name)(*srcs)


def _flip_peer(f, x, y, c):
    return (1 - x if f & 4 else x, 1 - y if f & 2 else y, 1 - c if f & 1 else c)


def _scatter_start(srcs):
    n = len(srcs)
    lands = [lax.empty((N_DEV - 1,) + s.shape[1:], s.dtype) for s in srcs]

    def body(*refs):
        src_refs, land_refs = refs[:n], refs[n:2 * n]
        send_sems, recv_sems = refs[2 * n:3 * n], refs[3 * n:4 * n]
        token = refs[-1]
        x, y, c = lax.axis_index("x"), lax.axis_index("y"), lax.axis_index("c")
        for a in range(n):
            for f in range(1, N_DEV):
                px, py, pc = _flip_peer(f, x, y, c)
                pltpu.make_async_remote_copy(
                    src_ref=src_refs[a].at[4 * px + 2 * py + pc], dst_ref=land_refs[a].at[f - 1],
                    send_sem=send_sems[a], recv_sem=recv_sems[a], device_id=(px, py, pc), device_id_type=MESH).start()
        token[...] = jnp.zeros_like(token)

    hbm = pl.BlockSpec(memory_space=pltpu.HBM)
    sem = pl.BlockSpec(memory_space=pltpu.SEMAPHORE)
    bufs = [pltpu.with_memory_space_constraint(b, pltpu.HBM) for b in list(srcs) + lands]
    out = pl.pallas_call(
        body, name="scatter_start",
        out_shape=(*[pltpu.SemaphoreType.DMA(())] * (2 * n), *[pltpu.HBM(b.shape, b.dtype) for b in bufs],
                   SDS((8, 128), F32)),
        in_specs=[hbm] * (2 * n),
        out_specs=(*[sem] * (2 * n), *[hbm] * (2 * n), pl.BlockSpec(memory_space=pltpu.VMEM)),
        input_output_aliases={i: 2 * n + i for i in range(2 * n)},
        compiler_params=pltpu.CompilerParams(has_side_effects=pltpu.SideEffectType.DATAFLOW_SIDE_EFFECTING))(*bufs)
    return list(out[:2 * n]), list(out[2 * n:3 * n]), list(out[3 * n:4 * n]), out[-1]


def _scatter_wait(sems, srcs, lands, after):
    n = len(srcs)

    def body(*refs):
        src_refs, land_refs = refs[:n], refs[n:2 * n]
        send_sems, recv_sems = refs[2 * n:3 * n], refs[3 * n:4 * n]
        x, y, c = lax.axis_index("x"), lax.axis_index("y"), lax.axis_index("c")
        for a in range(n):
            seven = pltpu.make_async_remote_copy(
                src_ref=src_refs[a].at[pl.ds(0, N_DEV - 1)], dst_ref=land_refs[a], send_sem=send_sems[a],
                recv_sem=recv_sems[a], device_id=(x, y, c), device_id_type=MESH)
            seven.wait_send()
            seven.wait_recv()

    hbm = pl.BlockSpec(memory_space=pltpu.HBM)
    sem = pl.BlockSpec(memory_space=pltpu.SEMAPHORE)
    bufs = list(srcs) + list(lands)
    out = pl.pallas_call(
        body, name="scatter_wait", out_shape=tuple(pltpu.HBM(b.shape, b.dtype) for b in bufs),
        in_specs=[hbm] * (2 * n) + [sem] * (2 * n) + [pl.BlockSpec(memory_space=pl.ANY)],
        out_specs=tuple([hbm] * (2 * n)), input_output_aliases={i: i for i in range(2 * n)},
        compiler_params=pltpu.CompilerParams(has_side_effects=pltpu.SideEffectType.DATAFLOW_SIDE_EFFECTING))(
            *bufs, *sems, after)
    return list(out[:n]), list(out[n:])


def _gather_two_level(srcs, name):
    n = len(srcs)
    out_shape = [SDS((N_DEV,) + s.shape, s.dtype) for s in srcs]

    def body(*refs):
        src_refs, dst_refs = refs[:n], refs[n:2 * n]
        send_sems, recv_sems, local_sems = refs[2 * n:]
        x, y, c = lax.axis_index("x"), lax.axis_index("y"), lax.axis_index("c")
        sibling = (x, y, 1 - c)
        chips = [(1 - x, y), (x, 1 - y), (1 - x, 1 - y)]

        def slot(a, px, py, pc):
            return dst_refs[a].at[4 * px + 2 * py + pc]

        def copy(a, k, block, to, src=None):
            return pltpu.make_async_remote_copy(
                src_ref=slot(a, *block) if src is None else src, dst_ref=slot(a, *block),
                send_sem=send_sems.at[a, k], recv_sem=recv_sems.at[a, k], device_id=to, device_id_type=MESH)

        own, sends = [], []
        for a in range(n):
            mine = pltpu.make_async_copy(src_refs[a], slot(a, x, y, c), local_sems.at[a])
            mine.start()
            own.append(mine)
            first = [copy(a, 1 + j, (x, y, c), (*chip, c), src=src_refs[a]) for j, chip in enumerate(chips)]
            first.append(copy(a, 0, (x, y, c), sibling, src=src_refs[a]))
            for cp in first:
                cp.start()
            sends += first
        for a in range(n):
            for j, chip in enumerate(chips):
                copy(a, 1 + j, (*chip, c), (x, y, c)).wait_recv()
                fwd = copy(a, 4 + j, (*chip, c), sibling)
                fwd.start()
                sends.append(fwd)
        for a in range(n):
            copy(a, 0, (x, y, 1 - c), (x, y, c)).wait_recv()
            for j, chip in enumerate(chips):
                copy(a, 4 + j, (*chip, 1 - c), (x, y, c)).wait_recv()
        for cp in sends:
            cp.wait_send()
        for cp in own:
            cp.wait()

    anyspec = pl.BlockSpec(memory_space=pl.ANY)
    return pl.pallas_call(
        body, in_specs=[anyspec] * n, out_specs=[anyspec] * n, out_shape=out_shape,
        scratch_shapes=[pltpu.SemaphoreType.DMA((n, N_DEV - 1)), pltpu.SemaphoreType.DMA((n, N_DEV - 1)),
                        pltpu.SemaphoreType.DMA((n,))],
        name=name)(*srcs)


def _adamw(parts, w, m, v, name, tr=256, own=None):
    lead = w.ndim == 3
    R, C = w.shape[-2:]
    tr = tr if R % tr == 0 else R
    n_parts = parts.shape[0]
    first = [] if own is None else [own]

    def body(*refs):
        p_ref, w_ref, m_ref, v_ref, g_ref, d_ref, nm_ref, nv_ref = refs[len(first):]
        terms = [r[0] for r in refs[:len(first)]] + [p_ref[dev] for dev in range(n_parts)]
        g = terms[0].astype(F32)
        for term in terms[1:]:
            g = g + term.astype(F32)
        mn = ADAM_B1 * m_ref[...] + (1.0 - ADAM_B1) * g
        vn = ADAM_B2 * v_ref[...] + (1.0 - ADAM_B2) * (g * g)
        m_hat = mn / (1.0 - ADAM_B1 ** ADAM_STEP)
        v_hat = vn / (1.0 - ADAM_B2 ** ADAM_STEP)
        g_ref[...] = g
        d_ref[...] = -ADAM_LR * (m_hat / (jnp.sqrt(v_hat) + ADAM_EPS) + ADAM_WD * w_ref[...])
        nm_ref[...] = mn
        nv_ref[...] = vn

    blk = pl.BlockSpec((None, tr, C), lambda i: (0, i, 0)) if lead else pl.BlockSpec((tr, C), lambda i: (i, 0))
    return pl.pallas_call(
        body, grid=(R // tr,),
        in_specs=[pl.BlockSpec((1, tr, C), lambda i: (0, i, 0))] * len(first)
        + [pl.BlockSpec((n_parts, tr, C), lambda i: (0, i, 0)), blk, blk, blk],
        out_specs=[blk] * 4, out_shape=[SDS(w.shape, F32)] * 4,
        compiler_params=_cp("parallel"), name=name)(*first, parts, w, m, v)


def _split_w_in(w_cat, D):
    z = jnp.zeros((D, PART_W - F_FL - N_FOX), w_cat.dtype)
    return jnp.concatenate([w_cat[:, O_A:O_Q], w_cat[:, O_FG:O_MQ], w_cat[:, O_MG:D_IN], w_cat[:, O_FL:O_FG], z,
                            w_cat[:, O_Q:O_FL], w_cat[:, O_MQ:O_MG]], axis=1)


def _merge_w_in(g_f, g_b):
    return jnp.concatenate([g_f[:, F_A:F_FG], g_b[:, B_Q:B_MQ], g_f[:, F_FL:F_FL + N_FOX], g_f[:, F_FG:F_MG],
                            g_b[:, B_MQ:PART_W], g_f[:, F_MG:F_FL]], axis=1)


def _pack_small(norm, mem_norm, final, conv_b, ln_g, ln_b, b_f, extra, D):
    width = max(D, PACK_W)
    row3 = jnp.concatenate([conv_b.reshape(-1), ln_g.reshape(-1), ln_b.reshape(-1), b_f.reshape(-1)])
    rows = [norm.reshape(-1), mem_norm.reshape(-1), final.reshape(-1), row3, extra.reshape(-1)]
    rows = [jnp.pad(r, (0, width - r.shape[0])) for r in rows]
    return jnp.concatenate([jnp.stack(rows), jnp.zeros((3, width), F32)], axis=0)


def _unpack_small(p, D):
    c = CONV_W
    return dict(norm_g=p[0:1, :D], mem_norm_g=p[1:2, :D], final_g=p[2, :D], conv_b=p[3:4, 0:c],
                conv_ln_g=p[3:4, c:2 * c], conv_ln_b=p[3:4, 2 * c:3 * c], b_f=p[3:4, 3 * c:3 * c + N_FOX])


def kernel(x, mem, norm_g, mem_norm_g, w_in, b_f, conv_w, conv_b, conv_ln_g, conv_ln_b, w_conv_pw, w_mem_kv, w_out, final_g, loss_target, m_norm_g, m_mem_norm_g, m_w_in, m_b_f, m_conv_w, m_conv_b, m_conv_ln_g, m_conv_ln_b, m_w_conv_pw, m_w_mem_kv, m_w_out, m_final_g, v_norm_g, v_mem_norm_g, v_w_in, v_b_f, v_conv_w, v_conv_b, v_conv_ln_g, v_conv_ln_b, v_w_conv_pw, v_w_mem_kv, v_w_out, v_final_g):
    S, D = x.shape[1], x.shape[2]
    M = mem.shape[1]
    xs, ms, tgt = x[0], mem[0], loss_target[0]
    cols = w_in.shape[2]

    g_in, g_cw, g_pw, g_kv, g_out = _gather_two_level(
        [w_in[0].astype(_BF), conv_w[0], w_conv_pw[0].astype(_BF), w_mem_kv[0].astype(_BF), w_out[0].astype(_BF)],
        "gather_weights")
    w_all = _split_w_in(jnp.transpose(g_in, (1, 0, 2)).reshape(D, N_DEV * cols), D)
    conv_w_full = jnp.transpose(g_cw, (1, 0, 2)).reshape(CONV_K, CONV_W)
    w_pw_full = g_pw.reshape(CONV_W, CONV_W)
    w_kv_full = g_kv.reshape(D, 2 * MEM_W)
    w_out_full = g_out.reshape(CONV_W + FOX_W + MEM_W, D)
    b_f_pad = jnp.pad(b_f, ((0, 0), (0, 128 - N_FOX)))

    h, r1 = _rms_fwd(xs, norm_g)
    p_f = _matmul(h, w_all, "nn", F32, "proj_f", b_cols=(0, PART_W))
    p_b = _matmul(h, w_all, "nn", _BF, "proj_b", b_cols=(PART_W, PART_W))
    mkv, mhat = _mem_kv_fwd(ms, mem_norm_g, w_kv_full)
    y_conv, u1 = _conv_fwd(p_f, conv_w_full, conv_b, conv_ln_g, conv_ln_b, w_pw_full)
    c_col, c_row = _fox_cumsum(p_f, b_f_pad)
    o_fox, y_fox, lse = _fox_fwd(p_b, p_f, c_col, c_row)
    y_mem = _mem_attn_fwd(p_b, p_f, mkv)
    y = jnp.concatenate([y_conv, y_fox, y_mem], axis=1)
    z = _matmul(y, w_out_full, "nn", F32, "out_proj")
    dx2, g_final, loss_part = _head_loss(xs, z, tgt, final_g.reshape(1, D))

    dy = _matmul(dx2, w_out_full, "nt", F32, "d_y")
    gw_out = _matmul(y, dx2, "tn", _BF, "gw_out")
    d_mq, d_mg, d_mk, d_mv = _mem_attn_bwd(p_b, p_f, mkv, dy)
    gw_kv, g_mem_norm = _mem_kv_bwd(jnp.concatenate([d_mk, d_mv], axis=1), mhat, mem_norm_g, w_kv_full)
    d_o, d_fg, rows = _fox_bwd_prep(dy, p_f, o_fox, lse, c_col)
    d_q, d_k, d_v, dr = _fox_bwd(p_b, d_o, rows, c_col)
    d_fl, g_bf = _fox_dlogf(dr, p_f, b_f_pad)
    du1, d_gc, gw_pw, g_ln_g, g_ln_b, g_cb = _conv_bwd1(u1, p_f, dy, conv_ln_g, conv_ln_b, w_pw_full)
    d_a, d_b, gw_cw = _conv_bwd2(du1, p_f, conv_w_full)
    dp = jnp.concatenate([d_a, d_b, d_gc, d_fg, d_mg, d_fl, d_q, d_k, d_v, d_mq], axis=1)
    gw_all = _matmul(h, dp, "tn", _BF, "gw_in")

    gw_in_cat = _merge_w_in(gw_all[:, :PART_W], gw_all[:, PART_W:])
    send_in = jnp.transpose(gw_in_cat.reshape(D, N_DEV, cols), (1, 0, 2))
    send_cw = jnp.transpose(gw_cw[:CONV_K].reshape(CONV_K, N_DEV, CONV_W // N_DEV), (1, 0, 2))
    send_pw = gw_pw.astype(_BF).reshape(N_DEV, CONV_W // N_DEV, CONV_W)
    send_kv = gw_kv.reshape(N_DEV, D // N_DEV, 2 * MEM_W)
    send_out = gw_out.reshape(N_DEV, (CONV_W + FOX_W + MEM_W) // N_DEV, D)
    sems, sent, lands, token = _scatter_start([send_in, send_cw, send_pw, send_kv, send_out])
    dh = _matmul(dp, w_all, "nt", F32, "d_h", after=token)
    grad_x, g_norm = _rms_bwd(xs, r1, dh, dx2, norm_g)
    sent, lands = _scatter_wait(sems, sent, lands, grad_x)
    me = 4 * lax.axis_index("x") + 2 * lax.axis_index("y") + lax.axis_index("c")
    own = [lax.dynamic_index_in_dim(s_, me, 0, keepdims=True) for s_ in sent]

    small = _pack_small(g_norm, g_mem_norm, g_final, g_cb, g_ln_g, g_ln_b, g_bf[:, :N_FOX], loss_part[0, 0:1], D)
    r_small, = _exchange([small], [False], "exchange_small")

    res = {}
    res["w_in"] = _adamw(lands[0], w_in, m_w_in, v_w_in, "adamw_w_in", own=own[0])
    res["conv_w"] = _adamw(lands[1], conv_w, m_conv_w, v_conv_w, "adamw_conv_w", own=own[1])
    res["w_conv_pw"] = _adamw(lands[2], w_conv_pw, m_w_conv_pw, v_w_conv_pw, "adamw_w_pw", own=own[2])
    res["w_mem_kv"] = _adamw(lands[3], w_mem_kv, m_w_mem_kv, v_w_mem_kv, "adamw_w_kv", own=own[3])
    res["w_out"] = _adamw(lands[4], w_out, m_w_out, v_w_out, "adamw_w_out", own=own[4])
    zero = jnp.zeros((1,), F32)
    pk = lambda a, b_, c_, d_, e, f_, g_: _pack_small(a, b_, c_, d_, e, f_, g_, zero, D)
    sm = _adamw(r_small,
                pk(norm_g, mem_norm_g, final_g, conv_b, conv_ln_g, conv_ln_b, b_f),
                pk(m_norm_g, m_mem_norm_g, m_final_g, m_conv_b, m_conv_ln_g, m_conv_ln_b, m_b_f),
                pk(v_norm_g, v_mem_norm_g, v_final_g, v_conv_b, v_conv_ln_g, v_conv_ln_b, v_b_f), "adamw_small")
    loss = sm[0][4, 0]
    small_out = [_unpack_small(a, D) for a in sm]
    names = ["norm_g", "mem_norm_g", "w_in", "b_f", "conv_w", "conv_b", "conv_ln_g", "conv_ln_b", "w_conv_pw",
             "w_mem_kv", "w_out", "final_g"]
    outs = [loss, grad_x[None]]
    for kind in range(4):
        for nme in names:
            outs.append(res[nme][kind] if nme in res else small_out[kind][nme])
    return tuple(outs)
```

```python
import functools

import jax
import jax.numpy as jnp
from jax import lax
from jax.experimental import pallas as pl
from jax.experimental.pallas import tpu as pltpu

F32 = jnp.float32
_BF = jnp.bfloat16
SDS = jax.ShapeDtypeStruct
MESH = pl.DeviceIdType.MESH

N_DEV = 8
HEAD = 128
N_FOX = 8
N_MEM = 4
CONV_W = 512
CONV_K = 31
FOX_W = N_FOX * HEAD
MEM_W = N_MEM * HEAD
D_IN = 3 * CONV_W + 4 * FOX_W + N_FOX + 2 * MEM_W
EPS = 1e-6
SCALE = HEAD ** -0.5
NEG = -1e30

ADAM_LR = 0.001
ADAM_B1 = 0.9
ADAM_B2 = 0.999
ADAM_EPS = 1e-08
ADAM_WD = 0.01
ADAM_STEP = 10

PART_W = 3584
F_A, F_B, F_GC, F_FG, F_MG, F_FL = 0, 512, 1024, 1536, 2560, 3072
B_Q, B_K, B_V, B_MQ = 0, 1024, 2048, 3072
O_A, O_B, O_GC = 0, 512, 1024
O_Q, O_K, O_V = 1536, 2560, 3584
O_FL, O_FG, O_MQ, O_MG = 4608, 4616, 5640, 6152

HALO = 32
T_ATT = 512
T_ROW = 256
T_CONV = 512
VMEM_LIMIT = 56 * 1024 * 1024
PACK_W = 2048


def _cp(*sem):
    return pltpu.CompilerParams(dimension_semantics=sem, vmem_limit_bytes=VMEM_LIMIT)


def _sigmoid(x):
    return jax.nn.sigmoid(x)


def _dsilu(x, sg):
    return sg * (1.0 + x * (1.0 - sg))


def _dot(a, b, dims):
    return lax.dot_general(a, b, (dims, ((), ())), preferred_element_type=F32)


NN = ((1,), (0,))
NT = ((1,), (1,))
TN = ((0,), (0,))


def _pick(n, pref):
    if n <= pref:
        return n
    t = pref - pref % 128
    while n % t:
        t -= 128
    return t


def _matmul(a, b, mode, out_dtype, name, tm=512, tn=1024, tk=2048, after=None, b_cols=None):
    col0 = 0
    if mode == "nn":
        (M, K), (K2, N) = a.shape, b.shape
        if b_cols is not None:
            col0, N = b_cols
    elif mode == "nt":
        (M, K), (N, K2) = a.shape, b.shape
    else:
        (K, M), (K2, N) = a.shape, b.shape
    assert K == K2
    tn, tk = _pick(N, tn), _pick(K, tk)
    nk = K // tk
    tm = _pick(M, tm if nk == 1 else 2 * tm)
    assert M % tm == 0 and N % tn == 0 and K % tk == 0, (name, M, N, K)
    dims = {"nn": NN, "nt": NT, "tn": TN}[mode]
    n_in = 2 if after is None else 3

    def body(*refs):
        a_ref, b_ref = refs[0], refs[1]
        o_ref = refs[n_in]
        p = _dot(a_ref[...].astype(_BF), b_ref[...].astype(_BF), dims)
        if nk == 1:
            o_ref[...] = p.astype(out_dtype)
            return
        acc_ref = refs[n_in + 1]
        k = pl.program_id(2)

        @pl.when(k == 0)
        def _():
            acc_ref[...] = p

        @pl.when(jnp.logical_and(k > 0, k < nk - 1))
        def _():
            acc_ref[...] += p

        @pl.when(k == nk - 1)
        def _():
            o_ref[...] = (acc_ref[...] + p).astype(out_dtype)

    if mode == "nn":
        assert col0 % tn == 0
        jb = col0 // tn
        a_spec = pl.BlockSpec((tm, tk), lambda j, i, k: (i, k))
        b_spec = pl.BlockSpec((tk, tn), lambda j, i, k: (k, j + jb))
    elif mode == "nt":
        a_spec = pl.BlockSpec((tm, tk), lambda j, i, k: (i, k))
        b_spec = pl.BlockSpec((tn, tk), lambda j, i, k: (j, k))
    else:
        a_spec = pl.BlockSpec((tk, tm), lambda j, i, k: (k, i))
        b_spec = pl.BlockSpec((tk, tn), lambda j, i, k: (k, j))
    in_specs, args = [a_spec, b_spec], [a, b]
    if after is not None:
        in_specs.append(pl.BlockSpec(memory_space=pl.ANY))
        args.append(after)
    return pl.pallas_call(
        body, grid=(N // tn, M // tm, nk), in_specs=in_specs,
        out_specs=pl.BlockSpec((tm, tn), lambda j, i, k: (i, j)),
        out_shape=SDS((M, N), out_dtype), scratch_shapes=[] if nk == 1 else [pltpu.VMEM((tm, tn), F32)],
        compiler_params=_cp("parallel", "parallel", "arbitrary"), name=name)(*args)


def _rms_fwd(x, g):
    S, D = x.shape
    tr = min(T_ROW, S)

    def body(x_ref, g_ref, h_ref, r_ref):
        xv = x_ref[...]
        r = lax.rsqrt(jnp.mean(xv * xv, axis=-1, keepdims=True) + EPS)
        h_ref[...] = (xv * r * g_ref[...]).astype(_BF)
        r_ref[...] = r

    return pl.pallas_call(
        body, grid=(S // tr,),
        in_specs=[pl.BlockSpec((tr, D), lambda i: (i, 0)), pl.BlockSpec((1, D), lambda i: (0, 0))],
        out_specs=[pl.BlockSpec((tr, D), lambda i: (i, 0)), pl.BlockSpec((tr, 1), lambda i: (i, 0))],
        out_shape=[SDS((S, D), _BF), SDS((S, 1), F32)],
        compiler_params=_cp("parallel"), name="rms_fwd")(x, g)


def _head_loss(x, z, target, g):
    S, D = x.shape
    tr = min(T_ROW, S)

    def body(x_ref, z_ref, t_ref, g_ref, dx2_ref, gg_ref, loss_ref):
        i = pl.program_id(0)
        x2 = x_ref[...] + z_ref[...]
        r = lax.rsqrt(jnp.mean(x2 * x2, axis=-1, keepdims=True) + EPS)
        xh = x2 * r
        gv = g_ref[...]
        diff = xh * gv - t_ref[...]
        lsum = 0.5 * jnp.sum(jnp.mean(diff * diff, axis=-1, keepdims=True), axis=0, keepdims=True)
        dout = diff * (1.0 / D)
        gd = dout * gv
        dx2_ref[...] = r * (gd - xh * jnp.mean(gd * xh, axis=-1, keepdims=True))
        gg = jnp.sum(dout * xh, axis=0, keepdims=True)

        @pl.when(i == 0)
        def _():
            gg_ref[...] = gg
            loss_ref[...] = jnp.broadcast_to(lsum, (1, 128))

        @pl.when(i > 0)
        def _():
            gg_ref[...] += gg
            loss_ref[...] += jnp.broadcast_to(lsum, (1, 128))

    row = pl.BlockSpec((tr, D), lambda i: (i, 0))
    return pl.pallas_call(
        body, grid=(S // tr,), in_specs=[row, row, row, pl.BlockSpec((1, D), lambda i: (0, 0))],
        out_specs=[row, pl.BlockSpec((1, D), lambda i: (0, 0)), pl.BlockSpec((1, 128), lambda i: (0, 0))],
        out_shape=[SDS((S, D), F32), SDS((1, D), F32), SDS((1, 128), F32)],
        compiler_params=_cp("arbitrary"), name="head_loss")(x, z, target, g)


def _rms_bwd(x, r, dh, dx2, g):
    S, D = x.shape
    tr = min(T_ROW, S)

    def body(x_ref, r_ref, dh_ref, dx2_ref, g_ref, gx_ref, gg_ref):
        i = pl.program_id(0)
        rv = r_ref[...]
        xh = x_ref[...] * rv
        dh_ = dh_ref[...]
        gd = dh_ * g_ref[...]
        gx_ref[...] = dx2_ref[...] + rv * (gd - xh * jnp.mean(gd * xh, axis=-1, keepdims=True))
        gg = jnp.sum(dh_ * xh, axis=0, keepdims=True)

        @pl.when(i == 0)
        def _():
            gg_ref[...] = gg

        @pl.when(i > 0)
        def _():
            gg_ref[...] += gg

    row = pl.BlockSpec((tr, D), lambda i: (i, 0))
    return pl.pallas_call(
        body, grid=(S // tr,),
        in_specs=[row, pl.BlockSpec((tr, 1), lambda i: (i, 0)), row, row, pl.BlockSpec((1, D), lambda i: (0, 0))],
        out_specs=[row, pl.BlockSpec((1, D), lambda i: (0, 0))],
        out_shape=[SDS((S, D), F32), SDS((1, D), F32)],
        compiler_params=_cp("arbitrary"), name="rms_bwd")(x, r, dh, dx2, g)


def _mem_kv_fwd(mem, g, w_kv):
    M, D = mem.shape

    def body(mem_ref, g_ref, w_ref, mkv_ref, mhat_ref):
        mv = mem_ref[...]
        mhat = mv * lax.rsqrt(jnp.mean(mv * mv, axis=-1, keepdims=True) + EPS)
        mhat_ref[...] = mhat
        mkv_ref[...] = _dot((mhat * g_ref[...]).astype(_BF), w_ref[...], NN).astype(_BF)

    return pl.pallas_call(
        body, out_shape=[SDS((M, 2 * MEM_W), _BF), SDS((M, D), F32)],
        compiler_params=pltpu.CompilerParams(vmem_limit_bytes=VMEM_LIMIT), name="mem_kv_fwd")(mem, g, w_kv)


def _mem_kv_bwd(dmkv, mhat, g, w_kv):
    M, D = mhat.shape

    def body(d_ref, mhat_ref, g_ref, w_ref, gw_ref, gg_ref):
        d = d_ref[...].astype(_BF)
        mhat = mhat_ref[...]
        gw_ref[...] = _dot((mhat * g_ref[...]).astype(_BF), d, TN).astype(_BF)
        dmh = _dot(d, w_ref[...], NT)
        gg_ref[...] = jnp.sum(dmh * mhat, axis=0, keepdims=True)

    return pl.pallas_call(
        body, out_shape=[SDS((D, 2 * MEM_W), _BF), SDS((1, D), F32)],
        compiler_params=pltpu.CompilerParams(vmem_limit_bytes=VMEM_LIMIT), name="mem_kv_bwd")(dmkv, mhat, g, w_kv)


def _mem_attn_fwd(p_b, p_f, mkv):
    S = p_b.shape[0]
    M = mkv.shape[0]
    tq = min(T_ATT, S)

    def body(q_ref, kv_ref, g_ref, y_ref):
        for hd in range(N_MEM):
            cols = slice(hd * HEAD, (hd + 1) * HEAD)
            s = _dot(q_ref[:, cols], kv_ref[:, cols], NT) * SCALE
            m = jnp.max(s, axis=-1, keepdims=True)
            e = jnp.exp(s - m)
            p = e / jnp.sum(e, axis=-1, keepdims=True)
            o = _dot(p.astype(_BF), kv_ref[:, MEM_W + hd * HEAD:MEM_W + (hd + 1) * HEAD], NN)
            gv = g_ref[:, cols]
            y_ref[:, cols] = (o * (gv * _sigmoid(gv))).astype(_BF)

    return pl.pallas_call(
        body, grid=(S // tq,),
        in_specs=[pl.BlockSpec((tq, MEM_W), lambda i: (i, B_MQ // MEM_W)),
                  pl.BlockSpec((M, 2 * MEM_W), lambda i: (0, 0)),
                  pl.BlockSpec((tq, MEM_W), lambda i: (i, F_MG // MEM_W))],
        out_specs=pl.BlockSpec((tq, MEM_W), lambda i: (i, 0)),
        out_shape=SDS((S, MEM_W), _BF), compiler_params=_cp("parallel"), name="mem_attn_fwd")(p_b, mkv, p_f)


def _mem_attn_bwd(p_b, p_f, mkv, dy):
    S = p_b.shape[0]
    M = mkv.shape[0]
    tq = min(T_ATT, S)

    def body(q_ref, kv_ref, g_ref, dy_ref, dq_ref, dg_ref, dk_ref, dv_ref):
        i = pl.program_id(0)

        @pl.when(i == 0)
        def _():
            dk_ref[...] = jnp.zeros_like(dk_ref)
            dv_ref[...] = jnp.zeros_like(dv_ref)

        for hd in range(N_MEM):
            cols = slice(hd * HEAD, (hd + 1) * HEAD)
            q, k = q_ref[:, cols], kv_ref[:, cols]
            v = kv_ref[:, MEM_W + hd * HEAD:MEM_W + (hd + 1) * HEAD]
            s = _dot(q, k, NT) * SCALE
            m = jnp.max(s, axis=-1, keepdims=True)
            e = jnp.exp(s - m)
            p = e / jnp.sum(e, axis=-1, keepdims=True)
            pb = p.astype(_BF)
            o = _dot(pb, v, NN)
            gv = g_ref[:, cols]
            sg = _sigmoid(gv)
            dyv = dy_ref[:, cols]
            do = dyv * (gv * sg)
            dg_ref[:, cols] = (dyv * o * _dsilu(gv, sg)).astype(_BF)
            dob = do.astype(_BF)
            dp = _dot(dob, v, NT)
            ds = p * (dp - jnp.sum(dp * p, axis=-1, keepdims=True)) * SCALE
            dsb = ds.astype(_BF)
            dq_ref[:, cols] = _dot(dsb, k, NN).astype(_BF)
            dk_ref[:, cols] += _dot(dsb, q, TN)
            dv_ref[:, cols] += _dot(pb, dob, TN)

    tile = pl.BlockSpec((tq, MEM_W), lambda i: (i, 0))
    kv = pl.BlockSpec((M, MEM_W), lambda i: (0, 0))
    return pl.pallas_call(
        body, grid=(S // tq,),
        in_specs=[pl.BlockSpec((tq, MEM_W), lambda i: (i, B_MQ // MEM_W)),
                  pl.BlockSpec((M, 2 * MEM_W), lambda i: (0, 0)),
                  pl.BlockSpec((tq, MEM_W), lambda i: (i, F_MG // MEM_W)),
                  pl.BlockSpec((tq, MEM_W), lambda i: (i, (CONV_W + FOX_W) // MEM_W))],
        out_specs=[tile, tile, kv, kv],
        out_shape=[SDS((S, MEM_W), _BF), SDS((S, MEM_W), _BF), SDS((M, MEM_W), F32), SDS((M, MEM_W), F32)],
        compiler_params=_cp("arbitrary"), name="mem_attn_bwd")(p_b, mkv, p_f, dy)


def _fox_cumsum(p_f, b_f_pad):
    S = p_f.shape[0]
    tr = min(256, S)
    fb = F_FL // 128

    def body(z_ref, b_ref, col_ref, row_ref, carry_ref):
        i = pl.program_id(0)

        @pl.when(i == 0)
        def _():
            carry_ref[...] = jnp.zeros_like(carry_ref)

        z = z_ref[...] + b_ref[...]
        lf = jnp.minimum(z, 0.0) - jnp.log1p(jnp.exp(-jnp.abs(z)))
        r = lax.broadcasted_iota(jnp.int32, (tr, tr), 0)
        c = lax.broadcasted_iota(jnp.int32, (tr, tr), 1)
        tri = (c <= r).astype(F32)
        cs = jnp.dot(tri, lf, precision=lax.Precision.HIGHEST, preferred_element_type=F32) + carry_ref[...]
        carry_ref[...] = cs[tr - 1:tr, :]
        cst = cs.T
        for hd in range(N_FOX):
            col_ref[hd] = cs[:, hd:hd + 1]
            row_ref[hd] = cst[hd:hd + 1, :]

    return pl.pallas_call(
        body, grid=(S // tr,),
        in_specs=[pl.BlockSpec((tr, 128), lambda i: (i, fb)), pl.BlockSpec((1, 128), lambda i: (0, 0))],
        out_specs=[pl.BlockSpec((N_FOX, tr, 1), lambda i: (0, i, 0)), pl.BlockSpec((N_FOX, 1, tr), lambda i: (0, 0, i))],
        out_shape=[SDS((N_FOX, S, 1), F32), SDS((N_FOX, 1, S), F32)], scratch_shapes=[pltpu.VMEM((1, 128), F32)],
        compiler_params=_cp("arbitrary"), name="fox_cumsum")(p_f, b_f_pad)


def _fox_dlogf(dr, p_f, b_f_pad):
    S = p_f.shape[0]
    tr = min(256, S)
    nb = S // tr
    fb = F_FL // 128
    wide = PART_W - F_FL

    def body(dr_ref, z_ref, b_ref, dz_ref, gb_ref, carry_ref):
        i = pl.program_id(0)

        @pl.when(i == 0)
        def _():
            carry_ref[...] = jnp.zeros_like(carry_ref)
            gb_ref[...] = jnp.zeros_like(gb_ref)

        heads = [dr_ref[hd, 0:1, :] + dr_ref[hd, 1:2, :] for hd in range(N_FOX)]
        dc = jnp.concatenate(heads + [jnp.zeros((128 - N_FOX, tr), F32)], axis=0).T
        r = lax.broadcasted_iota(jnp.int32, (tr, tr), 0)
        c = lax.broadcasted_iota(jnp.int32, (tr, tr), 1)
        tri = (c >= r).astype(F32)
        rc = jnp.dot(tri, dc, precision=lax.Precision.HIGHEST, preferred_element_type=F32) + carry_ref[...]
        carry_ref[...] = rc[0:1, :]
        z = z_ref[...] + b_ref[...]
        dz = rc * _sigmoid(-z)
        gb_ref[...] += jnp.sum(dz, axis=0, keepdims=True)
        dz_ref[...] = jnp.concatenate([dz.astype(_BF), jnp.zeros((tr, wide - 128), _BF)], axis=1)

    return pl.pallas_call(
        body, grid=(nb,),
        in_specs=[pl.BlockSpec((N_FOX, 8, tr), lambda i: (0, 0, nb - 1 - i)),
                  pl.BlockSpec((tr, 128), lambda i: (nb - 1 - i, fb)), pl.BlockSpec((1, 128), lambda i: (0, 0))],
        out_specs=[pl.BlockSpec((tr, wide), lambda i: (nb - 1 - i, 0)), pl.BlockSpec((1, 128), lambda i: (0, 0))],
        out_shape=[SDS((S, wide), _BF), SDS((1, 128), F32)], scratch_shapes=[pltpu.VMEM((1, 128), F32)],
        compiler_params=_cp("arbitrary"), name="fox_dlogf")(dr, p_f, b_f_pad)


def _fox_fwd(p_b, p_f, c_col, c_row):
    S = p_b.shape[0]
    t = min(2 * T_ATT, S)
    nq = S // t
    rc = min(256, t)
    qb, kb, vb, gb = B_Q // HEAD, B_K // HEAD, B_V // HEAD, F_FG // HEAD
    kq = SCALE * 1.4426950408889634

    def body(q_ref, k_ref, v_ref, cc_ref, cr_ref, g_ref, o_ref, y_ref, lse_ref, va_ref, ua_ref, ub_ref, m_ref, acc_ref):
        i = pl.program_id(1)

        @pl.when(i == 0)
        def _():
            va_ref[:, 0:HEAD] = v_ref[...]
            lane = lax.broadcasted_iota(jnp.int32, (S, HEAD), 1)
            va_ref[:, HEAD:2 * HEAD] = jnp.where(lane == 0, 1.0, 0.0).astype(_BF)

        m_ref[...] = jnp.full_like(m_ref, NEG)
        acc_ref[...] = jnp.zeros_like(acc_ref)

        def scores(b, u_ref):
            off = pl.multiple_of(b * t, t)
            k = k_ref[pl.ds(off, t), :]
            csr = cr_ref[:, pl.ds(off, t)] * (1.0 / SCALE)
            for r in range(0, t, rc):
                u_ref[r:r + rc, :] = _dot(q_ref[r:r + rc, :], k, NT) - csr

        def absorb(b, u_ref, masked):
            va = va_ref[pl.ds(pl.multiple_of(b * t, t), t), :]
            for r in range(0, t, rc):
                u = u_ref[r:r + rc, :]
                if masked:
                    row = lax.broadcasted_iota(jnp.int32, (rc, t), 0) + r
                    col = lax.broadcasted_iota(jnp.int32, (rc, t), 1)
                    u = jnp.where(col <= row, u, NEG)
                m_old = m_ref[r:r + rc, :]
                m_new = jnp.maximum(m_old, jnp.max(u, axis=-1, keepdims=True))
                alpha = jnp.exp2((m_old - m_new) * kq)
                p = jnp.exp2((u - m_new) * kq)
                acc_ref[r:r + rc, :] = alpha * acc_ref[r:r + rc, :] + _dot(p.astype(_BF), va, NN)
                m_ref[r:r + rc, :] = m_new

        scores(0, ua_ref)

        def pair(pi, carry):
            b = 2 * pi
            scores(b + 1, ub_ref)
            absorb(b, ua_ref, False)
            scores(b + 2, ua_ref)
            absorb(b + 1, ub_ref, False)
            return carry

        lax.fori_loop(0, i // 2, pair, 0)

        @pl.when(i % 2 == 1)
        def _():
            scores(i, ub_ref)
            absorb(i - 1, ua_ref, False)
            absorb(i, ub_ref, True)

        @pl.when(i % 2 == 0)
        def _():
            absorb(i, ua_ref, True)

        l = acc_ref[:, HEAD:HEAD + 1]
        o = acc_ref[:, 0:HEAD] / l
        gv = g_ref[...]
        o_ref[...] = o.astype(_BF)
        y_ref[...] = (o * (gv * _sigmoid(gv))).astype(_BF)
        lse_ref[...] = cc_ref[...] + SCALE * m_ref[...] + jnp.log(l)

    tile = pl.BlockSpec((t, HEAD), lambda h, i: (i, h))
    return pl.pallas_call(
        body, grid=(N_FOX, nq),
        in_specs=[pl.BlockSpec((t, HEAD), lambda h, i: (i, qb + h)),
                  pl.BlockSpec((S, HEAD), lambda h, i: (0, kb + h)),
                  pl.BlockSpec((S, HEAD), lambda h, i: (0, vb + h)),
                  pl.BlockSpec((None, t, 1), lambda h, i: (h, i, 0)),
                  pl.BlockSpec((None, 1, S), lambda h, i: (h, 0, 0)),
                  pl.BlockSpec((t, HEAD), lambda h, i: (i, gb + h))],
        out_specs=[tile, tile, pl.BlockSpec((None, t, 1), lambda h, i: (h, i, 0))],
        out_shape=[SDS((S, FOX_W), _BF), SDS((S, FOX_W), _BF), SDS((N_FOX, S, 1), F32)],
        scratch_shapes=[pltpu.VMEM((S, 2 * HEAD), _BF), pltpu.VMEM((t, t), F32), pltpu.VMEM((t, t), F32),
                        pltpu.VMEM((t, 1), F32), pltpu.VMEM((t, 2 * HEAD), F32)],
        compiler_params=_cp("parallel", "arbitrary"), name="fox_fwd")(p_b, p_b, p_b, c_col, c_row, p_f)


def _fox_bwd_prep(dy, p_f, o, lse, c_col):
    S = dy.shape[0]
    t = min(T_ATT, S)
    hg = 4
    wd = hg * HEAD

    def body(dy_ref, g_ref, o_ref, lse_ref, cc_ref, do_ref, dg_ref, rows_ref):
        gv = g_ref[...]
        sg = _sigmoid(gv)
        dyv = dy_ref[...]
        ov = o_ref[...].astype(F32)
        do = dyv * (gv * sg)
        do_ref[...] = do.astype(_BF)
        dg_ref[...] = (dyv * ov * _dsilu(gv, sg)).astype(_BF)
        prod = do * ov
        lane = lax.broadcasted_iota(jnp.int32, (t, 128), 1)
        for hd in range(hg):
            delta = jnp.sum(prod[:, hd * HEAD:(hd + 1) * HEAD], axis=-1, keepdims=True)
            a = cc_ref[hd] - lse_ref[hd]
            mat = jnp.where(lane == 0, a, jnp.where(lane == 1, delta, 0.0))
            rows_ref[hd] = mat.T[0:8, :]

    tile = pl.BlockSpec((t, wd), lambda g, i: (i, g))
    col = pl.BlockSpec((hg, t, 1), lambda g, i: (g, i, 0))
    return pl.pallas_call(
        body, grid=(N_FOX // hg, S // t),
        in_specs=[pl.BlockSpec((t, wd), lambda g, i: (i, CONV_W // wd + g)),
                  pl.BlockSpec((t, wd), lambda g, i: (i, F_FG // wd + g)), tile, col, col],
        out_specs=[tile, tile, pl.BlockSpec((hg, 8, t), lambda g, i: (g, 0, i))],
        out_shape=[SDS((S, FOX_W), _BF), SDS((S, FOX_W), _BF), SDS((N_FOX, 8, S), F32)],
        compiler_params=_cp("parallel", "parallel"), name="fox_bwd_prep")(dy, p_f, o, lse, c_col)


def _fox_bwd(p_b, do, rows, c_col):
    S = p_b.shape[0]
    t = min(T_ATT, S)
    nk = S // t
    qb, kb, vb = B_Q // HEAD, B_K // HEAD, B_V // HEAD

    def body(k_ref, v_ref, q_ref, do_ref, rows_ref, cc_ref, dq_ref, dk_ref, dv_ref, dr_ref,
             dqt_ref, dka_ref, dva_ref, dca_ref, dra_ref, sa_ref, pa_ref, sb_ref, pb_ref):
        j = pl.program_id(1)

        @pl.when(j == 0)
        def _():
            dqt_ref[...] = jnp.zeros_like(dqt_ref)
            dra_ref[...] = jnp.zeros_like(dra_ref)

        k = k_ref[...]
        v = v_ref[...]
        kt = k.astype(F32).T.astype(_BF)
        cc = cc_ref[...]
        dka_ref[...] = jnp.zeros_like(dka_ref)
        dva_ref[...] = jnp.zeros_like(dva_ref)
        dca_ref[...] = jnp.zeros_like(dca_ref)

        def scores(i, s_ref, p_ref):
            off = pl.multiple_of(i * t, t)
            s_ref[...] = _dot(k, q_ref[pl.ds(off, t), :], NT) * SCALE + (rows_ref[0:1, pl.ds(off, t)] - cc)
            p_ref[...] = _dot(v, do_ref[pl.ds(off, t), :], NT) - rows_ref[1:2, pl.ds(off, t)]

        def absorb(i, s_ref, p_ref, masked):
            off = pl.multiple_of(i * t, t)
            st = s_ref[...]
            if masked:
                srow = lax.broadcasted_iota(jnp.int32, (t, t), 0)
                tcol = lax.broadcasted_iota(jnp.int32, (t, t), 1)
                st = jnp.where(srow <= tcol, st, NEG)
            pt = jnp.exp(st)
            dva_ref[...] += _dot(pt.astype(_BF), do_ref[pl.ds(off, t), :], NN)
            dst = pt * p_ref[...]
            part = dst[:, 0:128]
            for gidx in range(1, t // 128):
                part = part + dst[:, gidx * 128:(gidx + 1) * 128]
            dca_ref[...] += part
            dra_ref[0:1, pl.ds(off, t)] += jnp.sum(dst, axis=0, keepdims=True)
            dsb = dst.astype(_BF)
            dka_ref[...] += _dot(dsb, q_ref[pl.ds(off, t), :], NN)
            dqt_ref[:, pl.ds(off, t)] += _dot(kt, dsb, NN)

        rest = nk - 1 - j
        scores(j, sa_ref, pa_ref)

        @pl.when(rest == 0)
        def _():
            absorb(j, sa_ref, pa_ref, True)

        @pl.when(rest > 0)
        def _():
            scores(j + 1, sb_ref, pb_ref)
            absorb(j, sa_ref, pa_ref, True)

            def pair(pi, carry):
                c = j + 1 + 2 * pi
                scores(c + 1, sa_ref, pa_ref)
                absorb(c, sb_ref, pb_ref, False)
                scores(c + 2, sb_ref, pb_ref)
                absorb(c + 1, sa_ref, pa_ref, False)
                return carry

            lax.fori_loop(0, (rest - 1) // 2, pair, 0)

            @pl.when(rest % 2 == 1)
            def _():
                absorb(nk - 1, sb_ref, pb_ref, False)

            @pl.when(rest % 2 == 0)
            def _():
                scores(nk - 1, sa_ref, pa_ref)
                absorb(nk - 2, sb_ref, pb_ref, False)
                absorb(nk - 1, sa_ref, pa_ref, False)

        dk_ref[...] = (dka_ref[...] * SCALE).astype(_BF)
        dv_ref[...] = dva_ref[...].astype(_BF)
        dra_ref[1:2, pl.ds(pl.multiple_of(j * t, t), t)] = -jnp.sum(dca_ref[...].T, axis=0, keepdims=True)

        @pl.when(j == nk - 1)
        def _():
            dr_ref[...] = dra_ref[...]
            for ci in range(nk):
                dq_ref[ci * t:(ci + 1) * t, :] = (dqt_ref[:, ci * t:(ci + 1) * t].T * SCALE).astype(_BF)

    tile = pl.BlockSpec((t, HEAD), lambda h, j: (j, h))
    return pl.pallas_call(
        body, grid=(N_FOX, nk),
        in_specs=[pl.BlockSpec((t, HEAD), lambda h, j: (j, kb + h)),
                  pl.BlockSpec((t, HEAD), lambda h, j: (j, vb + h)),
                  pl.BlockSpec((S, HEAD), lambda h, j: (0, qb + h)),
                  pl.BlockSpec((S, HEAD), lambda h, j: (0, h)),
                  pl.BlockSpec((None, 8, S), lambda h, j: (h, 0, 0)),
                  pl.BlockSpec((None, t, 1), lambda h, j: (h, j, 0))],
        out_specs=[pl.BlockSpec((S, HEAD), lambda h, j: (0, h)), tile, tile,
                   pl.BlockSpec((None, 8, S), lambda h, j: (h, 0, 0))],
        out_shape=[SDS((S, FOX_W), _BF), SDS((S, FOX_W), _BF), SDS((S, FOX_W), _BF), SDS((N_FOX, 8, S), F32)],
        scratch_shapes=[pltpu.VMEM((HEAD, S), F32), pltpu.VMEM((t, HEAD), F32), pltpu.VMEM((t, HEAD), F32),
                        pltpu.VMEM((t, 128), F32), pltpu.VMEM((8, S), F32)] + [pltpu.VMEM((t, t), F32)] * 4,
        compiler_params=_cp("parallel", "arbitrary"), name="fox_bwd")(p_b, p_b, p_b, do, rows, c_col)


CHUNK = 32


SUBLANES = 8


def _shifted_windows(win):
    n = win.shape[0]
    return [win] + [pltpu.roll(win, n - s, axis=0) for s in range(1, SUBLANES)]


def _tap(rot, f):
    return rot[f % SUBLANES][f - f % SUBLANES:f - f % SUBLANES + CHUNK, :]


def _conv_taps(ext_ref, w_ref, first, out_fn, n_rows):
    def chunk(c, carry):
        r0 = pl.multiple_of(c * CHUNK, CHUNK)
        rot = _shifted_windows(ext_ref[pl.ds(r0, 2 * CHUNK), :])
        acc = jnp.zeros((CHUNK, CONV_W), F32)
        for k in range(CONV_K):
            acc = acc + w_ref[k:k + 1, :] * _tap(rot, first(k))
        out_fn(r0, acc)
        return carry

    lax.fori_loop(0, n_rows // CHUNK, chunk, 0)


def _conv_fwd(p_f, conv_w, conv_b, ln_g, ln_b, w_pw):
    S = p_f.shape[0]
    tc = min(T_CONV, S)
    hb = tc // HALO

    def body(a_ref, b_ref, gc_ref, ap_ref, bp_ref, w_ref, cb_ref, lg_ref, lb_ref, pw_ref, y_ref, u1_ref, ext_ref):
        i = pl.program_id(0)
        prev = ap_ref[...] * _sigmoid(bp_ref[...])
        ext_ref[0:HALO, :] = jnp.where(i > 0, prev, 0.0)
        ext_ref[HALO:HALO + tc, :] = a_ref[...] * _sigmoid(b_ref[...])
        cb = cb_ref[...]

        def put(r0, acc):
            u1_ref[pl.ds(r0, CHUNK), :] = acc + cb

        _conv_taps(ext_ref, w_ref, lambda k: HALO - (CONV_K - 1) + k, put, tc)
        u1 = u1_ref[...]
        mu = jnp.mean(u1, axis=-1, keepdims=True)
        d = u1 - mu
        rstd = lax.rsqrt(jnp.mean(d * d, axis=-1, keepdims=True) + EPS)
        u2 = d * rstd * lg_ref[...] + lb_ref[...]
        u3 = u2 * _sigmoid(u2)
        pw = _dot(u3.astype(_BF), pw_ref[...], NN)
        gc = gc_ref[...]
        y_ref[...] = (pw * (gc * _sigmoid(gc))).astype(_BF)

    blk = lambda cb_: pl.BlockSpec((tc, CONV_W), lambda i: (i, cb_))
    halo = lambda cb_: pl.BlockSpec((HALO, CONV_W), lambda i: (jnp.maximum(i * hb - 1, 0), cb_))
    vec = pl.BlockSpec((1, CONV_W), lambda i: (0, 0))
    return pl.pallas_call(
        body, grid=(S // tc,),
        in_specs=[blk(0), blk(1), blk(2), halo(0), halo(1), pl.BlockSpec((CONV_K, CONV_W), lambda i: (0, 0)),
                  vec, vec, vec, pl.BlockSpec((CONV_W, CONV_W), lambda i: (0, 0))],
        out_specs=[pl.BlockSpec((tc, CONV_W), lambda i: (i, 0)), pl.BlockSpec((tc, CONV_W), lambda i: (i, 0))],
        out_shape=[SDS((S, CONV_W), _BF), SDS((S, CONV_W), F32)],
        scratch_shapes=[pltpu.VMEM((tc + 2 * HALO, CONV_W), F32)],
        compiler_params=_cp("parallel"), name="conv_fwd")(p_f, p_f, p_f, p_f, p_f, conv_w, conv_b, ln_g, ln_b, w_pw)


def _conv_bwd1(u1, p_f, dy, ln_g, ln_b, w_pw):
    S = u1.shape[0]
    tc = min(T_CONV, S)

    def body(u1_ref, gc_ref, dy_ref, lg_ref, lb_ref, pw_ref, du1_ref, dgc_ref, gpw_ref, glg_ref, glb_ref, gcb_ref):
        i = pl.program_id(0)
        u1v = u1_ref[...]
        mu = jnp.mean(u1v, axis=-1, keepdims=True)
        d = u1v - mu
        rstd = lax.rsqrt(jnp.mean(d * d, axis=-1, keepdims=True) + EPS)
        xh = d * rstd
        lg = lg_ref[...]
        u2 = xh * lg + lb_ref[...]
        sg2 = _sigmoid(u2)
        u3b = (u2 * sg2).astype(_BF)
        w = pw_ref[...]
        pw = _dot(u3b, w, NN)
        gc = gc_ref[...]
        sgc = _sigmoid(gc)
        dyv = dy_ref[...]
        dpw = (dyv * (gc * sgc)).astype(_BF)
        dgc_ref[...] = (dyv * pw * _dsilu(gc, sgc)).astype(_BF)
        gpw = _dot(u3b, dpw, TN)
        du2 = _dot(dpw, w, NT) * _dsilu(u2, sg2)
        glg = jnp.sum(du2 * xh, axis=0, keepdims=True)
        glb = jnp.sum(du2, axis=0, keepdims=True)
        dxh = du2 * lg
        du1 = rstd * (dxh - jnp.mean(dxh, axis=-1, keepdims=True) - xh * jnp.mean(dxh * xh, axis=-1, keepdims=True))
        du1_ref[...] = du1
        gcb = jnp.sum(du1, axis=0, keepdims=True)

        @pl.when(i == 0)
        def _():
            gpw_ref[...] = gpw
            glg_ref[...] = glg
            glb_ref[...] = glb
            gcb_ref[...] = gcb

        @pl.when(i > 0)
        def _():
            gpw_ref[...] += gpw
            glg_ref[...] += glg
            glb_ref[...] += glb
            gcb_ref[...] += gcb

    row = pl.BlockSpec((tc, CONV_W), lambda i: (i, 0))
    vec = pl.BlockSpec((1, CONV_W), lambda i: (0, 0))
    sq = pl.BlockSpec((CONV_W, CONV_W), lambda i: (0, 0))
    return pl.pallas_call(
        body, grid=(S // tc,),
        in_specs=[row, pl.BlockSpec((tc, CONV_W), lambda i: (i, F_GC // CONV_W)), row, vec, vec, sq],
        out_specs=[row, row, sq, vec, vec, vec],
        out_shape=[SDS((S, CONV_W), F32), SDS((S, CONV_W), _BF), SDS((CONV_W, CONV_W), F32),
                   SDS((1, CONV_W), F32), SDS((1, CONV_W), F32), SDS((1, CONV_W), F32)],
        compiler_params=_cp("arbitrary"), name="conv_bwd1")(u1, p_f, dy, ln_g, ln_b, w_pw)


def _conv_bwd2(du1, p_f, conv_w):
    S = du1.shape[0]
    tc = min(T_CONV, S)
    hb = tc // HALO
    nblk = S // tc
    last_halo = S // HALO - 1

    def body(d_ref, dn_ref, a_ref, b_ref, ap_ref, bp_ref, w_ref, da_ref, db_ref, gw_ref, ext_ref, dext_ref, du0_ref):
        i = pl.program_id(0)
        av = a_ref[...]
        sb = _sigmoid(b_ref[...])
        prev = ap_ref[...] * _sigmoid(bp_ref[...])
        ext_ref[0:HALO, :] = jnp.where(i > 0, prev, 0.0)
        ext_ref[HALO:HALO + tc, :] = av * sb
        dext_ref[0:tc, :] = d_ref[...]
        dext_ref[tc:tc + HALO, :] = jnp.where(i < nblk - 1, dn_ref[...], 0.0)

        def put(r0, acc):
            du0_ref[pl.ds(r0, CHUNK), :] = acc

        _conv_taps(dext_ref, w_ref, lambda k: CONV_K - 1 - k, put, tc)
        du0 = du0_ref[...]
        da_ref[...] = (du0 * sb).astype(_BF)
        db_ref[...] = (du0 * av * sb * (1.0 - sb)).astype(_BF)

        def chunk(c, carry):
            r0 = pl.multiple_of(c * CHUNK, CHUNK)
            rot = _shifted_windows(ext_ref[pl.ds(r0, 2 * CHUNK), :])
            dv = dext_ref[pl.ds(r0, CHUNK), :]
            rows = [jnp.sum(dv * _tap(rot, HALO - (CONV_K - 1) + k), axis=0, keepdims=True) for k in range(CONV_K)]
            rows.append(jnp.zeros((1, CONV_W), F32))
            return carry + jnp.concatenate(rows, axis=0)

        gw = lax.fori_loop(0, tc // CHUNK, chunk, jnp.zeros((CONV_K + 1, CONV_W), F32))

        @pl.when(i == 0)
        def _():
            gw_ref[...] = gw

        @pl.when(i > 0)
        def _():
            gw_ref[...] += gw

    row = pl.BlockSpec((tc, CONV_W), lambda i: (i, 0))
    blk = lambda cb_: pl.BlockSpec((tc, CONV_W), lambda i: (i, cb_))
    halo = lambda cb_: pl.BlockSpec((HALO, CONV_W), lambda i: (jnp.maximum(i * hb - 1, 0), cb_))
    nxt = pl.BlockSpec((HALO, CONV_W), lambda i: (jnp.minimum((i + 1) * hb, last_halo), 0))
    return pl.pallas_call(
        body, grid=(nblk,),
        in_specs=[row, nxt, blk(0), blk(1), halo(0), halo(1), pl.BlockSpec((CONV_K, CONV_W), lambda i: (0, 0))],
        out_specs=[row, row, pl.BlockSpec((CONV_K + 1, CONV_W), lambda i: (0, 0))],
        out_shape=[SDS((S, CONV_W), _BF), SDS((S, CONV_W), _BF), SDS((CONV_K + 1, CONV_W), F32)],
        scratch_shapes=[pltpu.VMEM((tc + 2 * HALO, CONV_W), F32), pltpu.VMEM((tc + 2 * HALO, CONV_W), F32),
                        pltpu.VMEM((tc, CONV_W), F32)],
        compiler_params=_cp("arbitrary"), name="conv_bwd2")(du1, du1, p_f, p_f, p_f, p_f, conv_w)


def _exchange(srcs, scatter, name):
    n = len(srcs)
    out_shape = [SDS((N_DEV,) + (s.shape[1:] if sc else s.shape), s.dtype) for s, sc in zip(srcs, scatter)]

    def body(*refs):
        src_refs, dst_refs = refs[:n], refs[n:2 * n]
        send_sems, recv_sems, local_sems = refs[2 * n:]
        x, y, c = lax.axis_index("x"), lax.axis_index("y"), lax.axis_index("c")
        me = 4 * x + 2 * y + c
        copies = []
        for a in range(n):
            for f in range(1, N_DEV):
                px = 1 - x if f & 4 else x
                py = 1 - y if f & 2 else y
                pc = 1 - c if f & 1 else c
                peer = 4 * px + 2 * py + pc
                src = src_refs[a].at[peer] if scatter[a] else src_refs[a]
                cp = pltpu.make_async_remote_copy(
                    src_ref=src, dst_ref=dst_refs[a].at[me], send_sem=send_sems.at[a, f - 1],
                    recv_sem=recv_sems.at[a, f - 1], device_id=(px, py, pc), device_id_type=MESH)
                cp.start()
                copies.append(cp)
            own = src_refs[a].at[me] if scatter[a] else src_refs[a]
            lc = pltpu.make_async_copy(own, dst_refs[a].at[me], local_sems.at[a])
            lc.start()
            copies.append(lc)
        for cp in copies:
            cp.wait()

    anyspec = pl.BlockSpec(memory_space=pl.ANY)
    return pl.pallas_call(
        body, in_specs=[anyspec] * n, out_specs=[anyspec] * n, out_shape=out_shape,
        scratch_shapes=[pltpu.SemaphoreType.DMA((n, N_DEV - 1)), pltpu.SemaphoreType.DMA((n, N_DEV - 1)),
                        pltpu.SemaphoreType.DMA((n,))],
        name=name)(*srcs)


def _flip_peer(f, x, y, c):
    return (1 - x if f & 4 else x, 1 - y if f & 2 else y, 1 - c if f & 1 else c)


def _scatter_start(srcs):
    n = len(srcs)
    lands = [lax.empty((N_DEV - 1,) + s.shape[1:], s.dtype) for s in srcs]

    def body(*refs):
        src_refs, land_refs = refs[:n], refs[n:2 * n]
        send_sems, recv_sems = refs[2 * n:3 * n], refs[3 * n:4 * n]
        token = refs[-1]
        x, y, c = lax.axis_index("x"), lax.axis_index("y"), lax.axis_index("c")
        for a in range(n):
            for f in range(1, N_DEV):
                px, py, pc = _flip_peer(f, x, y, c)
                pltpu.make_async_remote_copy(
                    src_ref=src_refs[a].at[4 * px + 2 * py + pc], dst_ref=land_refs[a].at[f - 1],
                    send_sem=send_sems[a], recv_sem=recv_sems[a], device_id=(px, py, pc), device_id_type=MESH).start()
        token[...] = jnp.zeros_like(token)

    hbm = pl.BlockSpec(memory_space=pltpu.HBM)
    sem = pl.BlockSpec(memory_space=pltpu.SEMAPHORE)
    bufs = [pltpu.with_memory_space_constraint(b, pltpu.HBM) for b in list(srcs) + lands]
    out = pl.pallas_call(
        body, name="scatter_start",
        out_shape=(*[pltpu.SemaphoreType.DMA(())] * (2 * n), *[pltpu.HBM(b.shape, b.dtype) for b in bufs],
                   SDS((8, 128), F32)),
        in_specs=[hbm] * (2 * n),
        out_specs=(*[sem] * (2 * n), *[hbm] * (2 * n), pl.BlockSpec(memory_space=pltpu.VMEM)),
        input_output_aliases={i: 2 * n + i for i in range(2 * n)},
        compiler_params=pltpu.CompilerParams(has_side_effects=pltpu.SideEffectType.DATAFLOW_SIDE_EFFECTING))(*bufs)
    return list(out[:2 * n]), list(out[2 * n:3 * n]), list(out[3 * n:4 * n]), out[-1]


def _scatter_wait(sems, srcs, lands, after):
    n = len(srcs)

    def body(*refs):
        src_refs, land_refs = refs[:n], refs[n:2 * n]
        send_sems, recv_sems = refs[2 * n:3 * n], refs[3 * n:4 * n]
        x, y, c = lax.axis_index("x"), lax.axis_index("y"), lax.axis_index("c")
        for a in range(n):
            seven = pltpu.make_async_remote_copy(
                src_ref=src_refs[a].at[pl.ds(0, N_DEV - 1)], dst_ref=land_refs[a], send_sem=send_sems[a],
                recv_sem=recv_sems[a], device_id=(x, y, c), device_id_type=MESH)
            seven.wait_send()
            seven.wait_recv()

    hbm = pl.BlockSpec(memory_space=pltpu.HBM)
    sem = pl.BlockSpec(memory_space=pltpu.SEMAPHORE)
    bufs = list(srcs) + list(lands)
    out = pl.pallas_call(
        body, name="scatter_wait", out_shape=tuple(pltpu.HBM(b.shape, b.dtype) for b in bufs),
        in_specs=[hbm] * (2 * n) + [sem] * (2 * n) + [pl.BlockSpec(memory_space=pl.ANY)],
        out_specs=tuple([hbm] * (2 * n)), input_output_aliases={i: i for i in range(2 * n)},
        compiler_params=pltpu.CompilerParams(has_side_effects=pltpu.SideEffectType.DATAFLOW_SIDE_EFFECTING))(
            *bufs, *sems, after)
    return list(out[:n]), list(out[n:])


def _gather_two_level(srcs, name):
    n = len(srcs)
    out_shape = [SDS((N_DEV,) + s.shape, s.dtype) for s in srcs]

    def body(*refs):
        src_refs, dst_refs = refs[:n], refs[n:2 * n]
        send_sems, recv_sems, local_sems = refs[2 * n:]
        x, y, c = lax.axis_index("x"), lax.axis_index("y"), lax.axis_index("c")
        sibling = (x, y, 1 - c)
        chips = [(1 - x, y), (x, 1 - y), (1 - x, 1 - y)]

        def slot(a, px, py, pc):
            return dst_refs[a].at[4 * px + 2 * py + pc]

        def copy(a, k, block, to, src=None):
            return pltpu.make_async_remote_copy(
                src_ref=slot(a, *block) if src is None else src, dst_ref=slot(a, *block),
                send_sem=send_sems.at[a, k], recv_sem=recv_sems.at[a, k], device_id=to, device_id_type=MESH)

        own, sends = [], []
        for a in range(n):
            mine = pltpu.make_async_copy(src_refs[a], slot(a, x, y, c), local_sems.at[a])
            mine.start()
            own.append(mine)
            first = [copy(a, 1 + j, (x, y, c), (*chip, c), src=src_refs[a]) for j, chip in enumerate(chips)]
            first.append(copy(a, 0, (x, y, c), sibling, src=src_refs[a]))
            for cp in first:
                cp.start()
            sends += first
        for a in range(n):
            for j, chip in enumerate(chips):
                copy(a, 1 + j, (*chip, c), (x, y, c)).wait_recv()
                fwd = copy(a, 4 + j, (*chip, c), sibling)
                fwd.start()
                sends.append(fwd)
        for a in range(n):
            copy(a, 0, (x, y, 1 - c), (x, y, c)).wait_recv()
            for j, chip in enumerate(chips):
                copy(a, 4 + j, (*chip, 1 - c), (x, y, c)).wait_recv()
        for cp in sends:
            cp.wait_send()
        for cp in own:
            cp.wait()

    anyspec = pl.BlockSpec(memory_space=pl.ANY)
    return pl.pallas_call(
        body, in_specs=[anyspec] * n, out_specs=[anyspec] * n, out_shape=out_shape,
        scratch_shapes=[pltpu.SemaphoreType.DMA((n, N_DEV - 1)), pltpu.SemaphoreType.DMA((n, N_DEV - 1)),
                        pltpu.SemaphoreType.DMA((n,))],
        name=name)(*srcs)


def _adamw(parts, w, m, v, name, tr=256, own=None):
    lead = w.ndim == 3
    R, C = w.shape[-2:]
    tr = tr if R % tr == 0 else R
    n_parts = parts.shape[0]
    first = [] if own is None else [own]

    def body(*refs):
        p_ref, w_ref, m_ref, v_ref, g_ref, d_ref, nm_ref, nv_ref = refs[len(first):]
        terms = [r[0] for r in refs[:len(first)]] + [p_ref[dev] for dev in range(n_parts)]
        g = terms[0].astype(F32)
        for term in terms[1:]:
            g = g + term.astype(F32)
        mn = ADAM_B1 * m_ref[...] + (1.0 - ADAM_B1) * g
        vn = ADAM_B2 * v_ref[...] + (1.0 - ADAM_B2) * (g * g)
        m_hat = mn / (1.0 - ADAM_B1 ** ADAM_STEP)
        v_hat = vn / (1.0 - ADAM_B2 ** ADAM_STEP)
        g_ref[...] = g
        d_ref[...] = -ADAM_LR * (m_hat / (jnp.sqrt(v_hat) + ADAM_EPS) + ADAM_WD * w_ref[...])
        nm_ref[...] = mn
        nv_ref[...] = vn

    blk = pl.BlockSpec((None, tr, C), lambda i: (0, i, 0)) if lead else pl.BlockSpec((tr, C), lambda i: (i, 0))
    return pl.pallas_call(
        body, grid=(R // tr,),
        in_specs=[pl.BlockSpec((1, tr, C), lambda i: (0, i, 0))] * len(first)
        + [pl.BlockSpec((n_parts, tr, C), lambda i: (0, i, 0)), blk, blk, blk],
        out_specs=[blk] * 4, out_shape=[SDS(w.shape, F32)] * 4,
        compiler_params=_cp("parallel"), name=name)(*first, parts, w, m, v)


def _adamw_transposed(parts, own, w_t, m_t, v_t, name, tr=256):
    n_parts, R, C = parts.shape
    tr = tr if R % tr == 0 else R

    def body(o_ref, p_ref, w_ref, m_ref, v_ref, g_ref, d_ref, nm_ref, nv_ref):
        r = lax.broadcasted_iota(jnp.int32, (tr, tr), 0)
        c = lax.broadcasted_iota(jnp.int32, (tr, tr), 1)
        eye = (r == c).astype(parts.dtype)
        g = _dot(o_ref[0], eye, TN)
        for dev in range(n_parts):
            g = g + _dot(p_ref[dev], eye, TN)
        mn = ADAM_B1 * m_ref[...] + (1.0 - ADAM_B1) * g
        vn = ADAM_B2 * v_ref[...] + (1.0 - ADAM_B2) * (g * g)
        m_hat = mn / (1.0 - ADAM_B1 ** ADAM_STEP)
        v_hat = vn / (1.0 - ADAM_B2 ** ADAM_STEP)
        g_ref[...] = g
        d_ref[...] = -ADAM_LR * (m_hat / (jnp.sqrt(v_hat) + ADAM_EPS) + ADAM_WD * w_ref[...])
        nm_ref[...] = mn
        nv_ref[...] = vn

    blk = pl.BlockSpec((C, tr), lambda i: (0, i))
    return pl.pallas_call(
        body, grid=(R // tr,),
        in_specs=[pl.BlockSpec((1, tr, C), lambda i: (0, i, 0)), pl.BlockSpec((n_parts, tr, C), lambda i: (0, i, 0)),
                  blk, blk, blk],
        out_specs=[blk] * 4, out_shape=[SDS((C, R), F32)] * 4,
        compiler_params=_cp("parallel"), name=name)(own, parts, w_t, m_t, v_t)


def _split_w_in(w_cat, D):
    z = jnp.zeros((D, PART_W - F_FL - N_FOX), w_cat.dtype)
    return jnp.concatenate([w_cat[:, O_A:O_Q], w_cat[:, O_FG:O_MQ], w_cat[:, O_MG:D_IN], w_cat[:, O_FL:O_FG], z,
                            w_cat[:, O_Q:O_FL], w_cat[:, O_MQ:O_MG]], axis=1)


def _merge_w_in(g_f, g_b):
    return jnp.concatenate([g_f[:, F_A:F_FG], g_b[:, B_Q:B_MQ], g_f[:, F_FL:F_FL + N_FOX], g_f[:, F_FG:F_MG],
                            g_b[:, B_MQ:PART_W], g_f[:, F_MG:F_FL]], axis=1)


def _pack_small(norm, mem_norm, final, conv_b, ln_g, ln_b, b_f, extra, D):
    width = max(D, PACK_W)
    row3 = jnp.concatenate([conv_b.reshape(-1), ln_g.reshape(-1), ln_b.reshape(-1), b_f.reshape(-1)])
    rows = [norm.reshape(-1), mem_norm.reshape(-1), final.reshape(-1), row3, extra.reshape(-1)]
    rows = [jnp.pad(r, (0, width - r.shape[0])) for r in rows]
    return jnp.concatenate([jnp.stack(rows), jnp.zeros((3, width), F32)], axis=0)


def _unpack_small(p, D):
    c = CONV_W
    return dict(norm_g=p[0:1, :D], mem_norm_g=p[1:2, :D], final_g=p[2, :D], conv_b=p[3:4, 0:c],
                conv_ln_g=p[3:4, c:2 * c], conv_ln_b=p[3:4, 2 * c:3 * c], b_f=p[3:4, 3 * c:3 * c + N_FOX])


def kernel(x, mem, norm_g, mem_norm_g, w_in, b_f, conv_w, conv_b, conv_ln_g, conv_ln_b, w_conv_pw, w_mem_kv, w_out, final_g, loss_target, m_norm_g, m_mem_norm_g, m_w_in, m_b_f, m_conv_w, m_conv_b, m_conv_ln_g, m_conv_ln_b, m_w_conv_pw, m_w_mem_kv, m_w_out, m_final_g, v_norm_g, v_mem_norm_g, v_w_in, v_b_f, v_conv_w, v_conv_b, v_conv_ln_g, v_conv_ln_b, v_w_conv_pw, v_w_mem_kv, v_w_out, v_final_g):
    S, D = x.shape[1], x.shape[2]
    M = mem.shape[1]
    xs, ms, tgt = x[0], mem[0], loss_target[0]
    cols = w_in.shape[2]

    g_in, g_cw, g_pw, g_kv, g_out = _gather_two_level(
        [w_in[0].astype(_BF), conv_w[0], w_conv_pw[0].astype(_BF), w_mem_kv[0].astype(_BF), w_out[0].astype(_BF)],
        "gather_weights")
    w_all = _split_w_in(jnp.transpose(g_in, (1, 0, 2)).reshape(D, N_DEV * cols), D)
    conv_w_full = jnp.transpose(g_cw, (1, 0, 2)).reshape(CONV_K, CONV_W)
    w_pw_full = g_pw.reshape(CONV_W, CONV_W)
    w_kv_full = g_kv.reshape(D, 2 * MEM_W)
    w_out_full = g_out.reshape(CONV_W + FOX_W + MEM_W, D)
    b_f_pad = jnp.pad(b_f, ((0, 0), (0, 128 - N_FOX)))

    h, r1 = _rms_fwd(xs, norm_g)
    p_f = _matmul(h, w_all, "nn", F32, "proj_f", b_cols=(0, PART_W))
    p_b = _matmul(h, w_all, "nn", _BF, "proj_b", b_cols=(PART_W, PART_W))
    mkv, mhat = _mem_kv_fwd(ms, mem_norm_g, w_kv_full)
    y_conv, u1 = _conv_fwd(p_f, conv_w_full, conv_b, conv_ln_g, conv_ln_b, w_pw_full)
    c_col, c_row = _fox_cumsum(p_f, b_f_pad)
    o_fox, y_fox, lse = _fox_fwd(p_b, p_f, c_col, c_row)
    y_mem = _mem_attn_fwd(p_b, p_f, mkv)
    y = jnp.concatenate([y_conv, y_fox, y_mem], axis=1)
    z = _matmul(y, w_out_full, "nn", F32, "out_proj")
    dx2, g_final, loss_part = _head_loss(xs, z, tgt, final_g.reshape(1, D))

    dy = _matmul(dx2, w_out_full, "nt", F32, "d_y")
    gw_out = _matmul(y, dx2, "tn", _BF, "gw_out")
    d_mq, d_mg, d_mk, d_mv = _mem_attn_bwd(p_b, p_f, mkv, dy)
    gw_kv, g_mem_norm = _mem_kv_bwd(jnp.concatenate([d_mk, d_mv], axis=1), mhat, mem_norm_g, w_kv_full)
    d_o, d_fg, rows = _fox_bwd_prep(dy, p_f, o_fox, lse, c_col)
    d_q, d_k, d_v, dr = _fox_bwd(p_b, d_o, rows, c_col)
    d_fl, g_bf = _fox_dlogf(dr, p_f, b_f_pad)
    du1, d_gc, gw_pw, g_ln_g, g_ln_b, g_cb = _conv_bwd1(u1, p_f, dy, conv_ln_g, conv_ln_b, w_pw_full)
    d_a, d_b, gw_cw = _conv_bwd2(du1, p_f, conv_w_full)
    dp = jnp.concatenate([d_a, d_b, d_gc, d_fg, d_mg, d_fl, d_q, d_k, d_v, d_mq], axis=1)
    gw_all = _matmul(h, dp, "tn", _BF, "gw_in")

    gw_in_cat = _merge_w_in(gw_all[:, :PART_W], gw_all[:, PART_W:])
    send_in = jnp.transpose(gw_in_cat.reshape(D, N_DEV, cols), (1, 0, 2))
    send_cw = jnp.transpose(gw_cw[:CONV_K].reshape(CONV_K, N_DEV, CONV_W // N_DEV), (1, 0, 2))
    send_pw = gw_pw.astype(_BF).reshape(N_DEV, CONV_W // N_DEV, CONV_W)
    send_kv = gw_kv.reshape(N_DEV, D // N_DEV, 2 * MEM_W)
    send_out = gw_out.reshape(N_DEV, (CONV_W + FOX_W + MEM_W) // N_DEV, D)
    sems, sent, lands, token = _scatter_start([send_in, send_cw, send_pw, send_kv, send_out])
    dh = _matmul(dp, w_all, "nt", F32, "d_h", after=token)
    grad_x, g_norm = _rms_bwd(xs, r1, dh, dx2, norm_g)
    sent, lands = _scatter_wait(sems, sent, lands, grad_x)
    me = 4 * lax.axis_index("x") + 2 * lax.axis_index("y") + lax.axis_index("c")
    own = [lax.dynamic_index_in_dim(s_, me, 0, keepdims=True) for s_ in sent]

    small = _pack_small(g_norm, g_mem_norm, g_final, g_cb, g_ln_g, g_ln_b, g_bf[:, :N_FOX], loss_part[0, 0:1], D)
    r_small, = _exchange([small], [False], "exchange_small")

    res = {}
    w_in_t = _adamw_transposed(lands[0], own[0], jnp.transpose(w_in[0]), jnp.transpose(m_w_in[0]),
                               jnp.transpose(v_w_in[0]), "adamw_w_in")
    res["w_in"] = [jnp.transpose(a_)[None] for a_ in w_in_t]
    res["conv_w"] = _adamw(lands[1], conv_w, m_conv_w, v_conv_w, "adamw_conv_w", own=own[1])
    res["w_conv_pw"] = _adamw(lands[2], w_conv_pw, m_w_conv_pw, v_w_conv_pw, "adamw_w_pw", own=own[2])
    res["w_mem_kv"] = _adamw(lands[3], w_mem_kv, m_w_mem_kv, v_w_mem_kv, "adamw_w_kv", own=own[3])
    res["w_out"] = _adamw(lands[4], w_out, m_w_out, v_w_out, "adamw_w_out", own=own[4])
    zero = jnp.zeros((1,), F32)
    pk = lambda a, b_, c_, d_, e, f_, g_: _pack_small(a, b_, c_, d_, e, f_, g_, zero, D)
    sm = _adamw(r_small,
                pk(norm_g, mem_norm_g, final_g, conv_b, conv_ln_g, conv_ln_b, b_f),
                pk(m_norm_g, m_mem_norm_g, m_final_g, m_conv_b, m_conv_ln_g, m_conv_ln_b, m_b_f),
                pk(v_norm_g, v_mem_norm_g, v_final_g, v_conv_b, v_conv_ln_g, v_conv_ln_b, v_b_f), "adamw_small")
    loss = sm[0][4, 0]
    small_out = [_unpack_small(a, D) for a in sm]
    names = ["norm_g", "mem_norm_g", "w_in", "b_f", "conv_w", "conv_b", "conv_ln_g", "conv_ln_b", "w_conv_pw",
             "w_mem_kv", "w_out", "final_g"]
    outs = [loss, grad_x[None]]
    for kind in range(4):
        for nme in names:
            outs.append(res[nme][kind] if nme in res else small_out[kind][nme])
    return tuple(outs)
```

```python
import functools

import jax
import jax.numpy as jnp
from jax import lax
from jax.experimental import pallas as pl
from jax.experimental.pallas import tpu as pltpu

F32 = jnp.float32
_BF = jnp.bfloat16
SDS = jax.ShapeDtypeStruct
MESH = pl.DeviceIdType.MESH

N_DEV = 8
HEAD = 128
N_FOX = 8
N_MEM = 4
CONV_W = 512
CONV_K = 31
FOX_W = N_FOX * HEAD
MEM_W = N_MEM * HEAD
D_IN = 3 * CONV_W + 4 * FOX_W + N_FOX + 2 * MEM_W
EPS = 1e-6
SCALE = HEAD ** -0.5
NEG = -1e30

ADAM_LR = 0.001
ADAM_B1 = 0.9
ADAM_B2 = 0.999
ADAM_EPS = 1e-08
ADAM_WD = 0.01
ADAM_STEP = 10

PART_W = 3584
F_A, F_B, F_GC, F_FG, F_MG, F_FL = 0, 512, 1024, 1536, 2560, 3072
B_Q, B_K, B_V, B_MQ = 0, 1024, 2048, 3072
O_A, O_B, O_GC = 0, 512, 1024
O_Q, O_K, O_V = 1536, 2560, 3584
O_FL, O_FG, O_MQ, O_MG = 4608, 4616, 5640, 6152

HALO = 32
T_ATT = 512
T_ROW = 256
T_CONV = 512
VMEM_LIMIT = 56 * 1024 * 1024
PACK_W = 2048


def _cp(*sem):
    return pltpu.CompilerParams(dimension_semantics=sem, vmem_limit_bytes=VMEM_LIMIT)


def _sigmoid(x):
    return jax.nn.sigmoid(x)


def _dsilu(x, sg):
    return sg * (1.0 + x * (1.0 - sg))


def _dot(a, b, dims):
    return lax.dot_general(a, b, (dims, ((), ())), preferred_element_type=F32)


NN = ((1,), (0,))
NT = ((1,), (1,))
TN = ((0,), (0,))


def _pick(n, pref):
    if n <= pref:
        return n
    t = pref - pref % 128
    while n % t:
        t -= 128
    return t


def _matmul(a, b, mode, out_dtype, name, tm=512, tn=1024, tk=2048, after=None, b_cols=None):
    col0 = 0
    if mode == "nn":
        (M, K), (K2, N) = a.shape, b.shape
        if b_cols is not None:
            col0, N = b_cols
    elif mode == "nt":
        (M, K), (N, K2) = a.shape, b.shape
    else:
        (K, M), (K2, N) = a.shape, b.shape
    assert K == K2
    tn, tk = _pick(N, tn), _pick(K, tk)
    nk = K // tk
    tm = _pick(M, tm if nk == 1 else 2 * tm)
    assert M % tm == 0 and N % tn == 0 and K % tk == 0, (name, M, N, K)
    dims = {"nn": NN, "nt": NT, "tn": TN}[mode]
    n_in = 2 if after is None else 3

    def body(*refs):
        a_ref, b_ref = refs[0], refs[1]
        o_ref = refs[n_in]
        p = _dot(a_ref[...].astype(_BF), b_ref[...].astype(_BF), dims)
        if nk == 1:
            o_ref[...] = p.astype(out_dtype)
            return
        acc_ref = refs[n_in + 1]
        k = pl.program_id(2)

        @pl.when(k == 0)
        def _():
            acc_ref[...] = p

        @pl.when(jnp.logical_and(k > 0, k < nk - 1))
        def _():
            acc_ref[...] += p

        @pl.when(k == nk - 1)
        def _():
            o_ref[...] = (acc_ref[...] + p).astype(out_dtype)

    if mode == "nn":
        assert col0 % tn == 0
        jb = col0 // tn
        a_spec = pl.BlockSpec((tm, tk), lambda j, i, k: (i, k))
        b_spec = pl.BlockSpec((tk, tn), lambda j, i, k: (k, j + jb))
    elif mode == "nt":
        a_spec = pl.BlockSpec((tm, tk), lambda j, i, k: (i, k))
        b_spec = pl.BlockSpec((tn, tk), lambda j, i, k: (j, k))
    else:
        a_spec = pl.BlockSpec((tk, tm), lambda j, i, k: (k, i))
        b_spec = pl.BlockSpec((tk, tn), lambda j, i, k: (k, j))
    in_specs, args = [a_spec, b_spec], [a, b]
    if after is not None:
        in_specs.append(pl.BlockSpec(memory_space=pl.ANY))
        args.append(after)
    return pl.pallas_call(
        body, grid=(N // tn, M // tm, nk), in_specs=in_specs,
        out_specs=pl.BlockSpec((tm, tn), lambda j, i, k: (i, j)),
        out_shape=SDS((M, N), out_dtype), scratch_shapes=[] if nk == 1 else [pltpu.VMEM((tm, tn), F32)],
        compiler_params=_cp("parallel", "parallel", "arbitrary"), name=name)(*args)


def _rms_fwd(x, g):
    S, D = x.shape
    tr = min(T_ROW, S)

    def body(x_ref, g_ref, h_ref, r_ref):
        xv = x_ref[...]
        r = lax.rsqrt(jnp.mean(xv * xv, axis=-1, keepdims=True) + EPS)
        h_ref[...] = (xv * r * g_ref[...]).astype(_BF)
        r_ref[...] = r

    return pl.pallas_call(
        body, grid=(S // tr,),
        in_specs=[pl.BlockSpec((tr, D), lambda i: (i, 0)), pl.BlockSpec((1, D), lambda i: (0, 0))],
        out_specs=[pl.BlockSpec((tr, D), lambda i: (i, 0)), pl.BlockSpec((tr, 1), lambda i: (i, 0))],
        out_shape=[SDS((S, D), _BF), SDS((S, 1), F32)],
        compiler_params=_cp("parallel"), name="rms_fwd")(x, g)


def _head_loss(x, z, target, g):
    S, D = x.shape
    tr = min(T_ROW, S)

    def body(x_ref, z_ref, t_ref, g_ref, dx2_ref, gg_ref, loss_ref):
        i = pl.program_id(0)
        x2 = x_ref[...] + z_ref[...]
        r = lax.rsqrt(jnp.mean(x2 * x2, axis=-1, keepdims=True) + EPS)
        xh = x2 * r
        gv = g_ref[...]
        diff = xh * gv - t_ref[...]
        lsum = 0.5 * jnp.sum(jnp.mean(diff * diff, axis=-1, keepdims=True), axis=0, keepdims=True)
        dout = diff * (1.0 / D)
        gd = dout * gv
        dx2_ref[...] = r * (gd - xh * jnp.mean(gd * xh, axis=-1, keepdims=True))
        gg = jnp.sum(dout * xh, axis=0, keepdims=True)

        @pl.when(i == 0)
        def _():
            gg_ref[...] = gg
            loss_ref[...] = jnp.broadcast_to(lsum, (1, 128))

        @pl.when(i > 0)
        def _():
            gg_ref[...] += gg
            loss_ref[...] += jnp.broadcast_to(lsum, (1, 128))

    row = pl.BlockSpec((tr, D), lambda i: (i, 0))
    return pl.pallas_call(
        body, grid=(S // tr,), in_specs=[row, row, row, pl.BlockSpec((1, D), lambda i: (0, 0))],
        out_specs=[row, pl.BlockSpec((1, D), lambda i: (0, 0)), pl.BlockSpec((1, 128), lambda i: (0, 0))],
        out_shape=[SDS((S, D), F32), SDS((1, D), F32), SDS((1, 128), F32)],
        compiler_params=_cp("arbitrary"), name="head_loss")(x, z, target, g)


def _rms_bwd(x, r, dh, dx2, g):
    S, D = x.shape
    tr = min(T_ROW, S)

    def body(x_ref, r_ref, dh_ref, dx2_ref, g_ref, gx_ref, gg_ref):
        i = pl.program_id(0)
        rv = r_ref[...]
        xh = x_ref[...] * rv
        dh_ = dh_ref[...]
        gd = dh_ * g_ref[...]
        gx_ref[...] = dx2_ref[...] + rv * (gd - xh * jnp.mean(gd * xh, axis=-1, keepdims=True))
        gg = jnp.sum(dh_ * xh, axis=0, keepdims=True)

        @pl.when(i == 0)
        def _():
            gg_ref[...] = gg

        @pl.when(i > 0)
        def _():
            gg_ref[...] += gg

    row = pl.BlockSpec((tr, D), lambda i: (i, 0))
    return pl.pallas_call(
        body, grid=(S // tr,),
        in_specs=[row, pl.BlockSpec((tr, 1), lambda i: (i, 0)), row, row, pl.BlockSpec((1, D), lambda i: (0, 0))],
        out_specs=[row, pl.BlockSpec((1, D), lambda i: (0, 0))],
        out_shape=[SDS((S, D), F32), SDS((1, D), F32)],
        compiler_params=_cp("arbitrary"), name="rms_bwd")(x, r, dh, dx2, g)


def _mem_kv_fwd(mem, g, w_kv):
    M, D = mem.shape

    def body(mem_ref, g_ref, w_ref, mkv_ref, mhat_ref):
        mv = mem_ref[...]
        mhat = mv * lax.rsqrt(jnp.mean(mv * mv, axis=-1, keepdims=True) + EPS)
        mhat_ref[...] = mhat
        mkv_ref[...] = _dot((mhat * g_ref[...]).astype(_BF), w_ref[...], NN).astype(_BF)

    return pl.pallas_call(
        body, out_shape=[SDS((M, 2 * MEM_W), _BF), SDS((M, D), F32)],
        compiler_params=pltpu.CompilerParams(vmem_limit_bytes=VMEM_LIMIT), name="mem_kv_fwd")(mem, g, w_kv)


def _mem_kv_bwd(dmkv, mhat, g, w_kv):
    M, D = mhat.shape

    def body(d_ref, mhat_ref, g_ref, w_ref, gw_ref, gg_ref):
        d = d_ref[...].astype(_BF)
        mhat = mhat_ref[...]
        gw_ref[...] = _dot((mhat * g_ref[...]).astype(_BF), d, TN).astype(_BF)
        dmh = _dot(d, w_ref[...], NT)
        gg_ref[...] = jnp.sum(dmh * mhat, axis=0, keepdims=True)

    return pl.pallas_call(
        body, out_shape=[SDS((D, 2 * MEM_W), _BF), SDS((1, D), F32)],
        compiler_params=pltpu.CompilerParams(vmem_limit_bytes=VMEM_LIMIT), name="mem_kv_bwd")(dmkv, mhat, g, w_kv)


def _mem_attn_fwd(p_b, p_f, mkv, y):
    S = p_b.shape[0]
    M = mkv.shape[0]
    tq = min(T_ATT, S)

    def body(q_ref, kv_ref, g_ref, yin_ref, y_ref):
        for hd in range(N_MEM):
            cols = slice(hd * HEAD, (hd + 1) * HEAD)
            s = _dot(q_ref[:, cols], kv_ref[:, cols], NT) * SCALE
            m = jnp.max(s, axis=-1, keepdims=True)
            e = jnp.exp(s - m)
            p = e / jnp.sum(e, axis=-1, keepdims=True)
            o = _dot(p.astype(_BF), kv_ref[:, MEM_W + hd * HEAD:MEM_W + (hd + 1) * HEAD], NN)
            gv = g_ref[:, cols]
            y_ref[:, cols] = (o * (gv * _sigmoid(gv))).astype(_BF)

    return pl.pallas_call(
        body, grid=(S // tq,),
        in_specs=[pl.BlockSpec((tq, MEM_W), lambda i: (i, B_MQ // MEM_W)),
                  pl.BlockSpec((M, 2 * MEM_W), lambda i: (0, 0)),
                  pl.BlockSpec((tq, MEM_W), lambda i: (i, F_MG // MEM_W)), pl.BlockSpec(memory_space=pl.ANY)],
        out_specs=pl.BlockSpec((tq, MEM_W), lambda i: (i, (CONV_W + FOX_W) // MEM_W)),
        out_shape=SDS(y.shape, y.dtype), input_output_aliases={3: 0},
        compiler_params=_cp("parallel"), name="mem_attn_fwd")(p_b, mkv, p_f, y)


def _mem_attn_bwd(p_b, p_f, mkv, dy):
    S = p_b.shape[0]
    M = mkv.shape[0]
    tq = min(T_ATT, S)

    def body(q_ref, kv_ref, g_ref, dy_ref, dq_ref, dg_ref, dk_ref, dv_ref):
        i = pl.program_id(0)

        @pl.when(i == 0)
        def _():
            dk_ref[...] = jnp.zeros_like(dk_ref)
            dv_ref[...] = jnp.zeros_like(dv_ref)

        for hd in range(N_MEM):
            cols = slice(hd * HEAD, (hd + 1) * HEAD)
            q, k = q_ref[:, cols], kv_ref[:, cols]
            v = kv_ref[:, MEM_W + hd * HEAD:MEM_W + (hd + 1) * HEAD]
            s = _dot(q, k, NT) * SCALE
            m = jnp.max(s, axis=-1, keepdims=True)
            e = jnp.exp(s - m)
            p = e / jnp.sum(e, axis=-1, keepdims=True)
            pb = p.astype(_BF)
            o = _dot(pb, v, NN)
            gv = g_ref[:, cols]
            sg = _sigmoid(gv)
            dyv = dy_ref[:, cols]
            do = dyv * (gv * sg)
            dg_ref[:, cols] = (dyv * o * _dsilu(gv, sg)).astype(_BF)
            dob = do.astype(_BF)
            dp = _dot(dob, v, NT)
            ds = p * (dp - jnp.sum(dp * p, axis=-1, keepdims=True)) * SCALE
            dsb = ds.astype(_BF)
            dq_ref[:, cols] = _dot(dsb, k, NN).astype(_BF)
            dk_ref[:, cols] += _dot(dsb, q, TN)
            dv_ref[:, cols] += _dot(pb, dob, TN)

    tile = pl.BlockSpec((tq, MEM_W), lambda i: (i, 0))
    kv = pl.BlockSpec((M, MEM_W), lambda i: (0, 0))
    return pl.pallas_call(
        body, grid=(S // tq,),
        in_specs=[pl.BlockSpec((tq, MEM_W), lambda i: (i, B_MQ // MEM_W)),
                  pl.BlockSpec((M, 2 * MEM_W), lambda i: (0, 0)),
                  pl.BlockSpec((tq, MEM_W), lambda i: (i, F_MG // MEM_W)),
                  pl.BlockSpec((tq, MEM_W), lambda i: (i, (CONV_W + FOX_W) // MEM_W))],
        out_specs=[tile, tile, kv, kv],
        out_shape=[SDS((S, MEM_W), _BF), SDS((S, MEM_W), _BF), SDS((M, MEM_W), F32), SDS((M, MEM_W), F32)],
        compiler_params=_cp("arbitrary"), name="mem_attn_bwd")(p_b, mkv, p_f, dy)


def _fox_cumsum(p_f, b_f_pad):
    S = p_f.shape[0]
    tr = min(256, S)
    fb = F_FL // 128

    def body(z_ref, b_ref, col_ref, row_ref, carry_ref):
        i = pl.program_id(0)

        @pl.when(i == 0)
        def _():
            carry_ref[...] = jnp.zeros_like(carry_ref)

        z = z_ref[...] + b_ref[...]
        lf = jnp.minimum(z, 0.0) - jnp.log1p(jnp.exp(-jnp.abs(z)))
        r = lax.broadcasted_iota(jnp.int32, (tr, tr), 0)
        c = lax.broadcasted_iota(jnp.int32, (tr, tr), 1)
        tri = (c <= r).astype(F32)
        cs = jnp.dot(tri, lf, precision=lax.Precision.HIGHEST, preferred_element_type=F32) + carry_ref[...]
        carry_ref[...] = cs[tr - 1:tr, :]
        cst = cs.T
        for hd in range(N_FOX):
            col_ref[hd] = cs[:, hd:hd + 1]
            row_ref[hd] = cst[hd:hd + 1, :]

    return pl.pallas_call(
        body, grid=(S // tr,),
        in_specs=[pl.BlockSpec((tr, 128), lambda i: (i, fb)), pl.BlockSpec((1, 128), lambda i: (0, 0))],
        out_specs=[pl.BlockSpec((N_FOX, tr, 1), lambda i: (0, i, 0)), pl.BlockSpec((N_FOX, 1, tr), lambda i: (0, 0, i))],
        out_shape=[SDS((N_FOX, S, 1), F32), SDS((N_FOX, 1, S), F32)], scratch_shapes=[pltpu.VMEM((1, 128), F32)],
        compiler_params=_cp("arbitrary"), name="fox_cumsum")(p_f, b_f_pad)


def _fox_dlogf(dr, p_f, b_f_pad, dp):
    S = p_f.shape[0]
    tr = min(256, S)
    nb = S // tr
    fb = F_FL // 128
    wide = PART_W - F_FL

    def body(dr_ref, z_ref, b_ref, dp_ref, dz_ref, gb_ref, carry_ref):
        i = pl.program_id(0)

        @pl.when(i == 0)
        def _():
            carry_ref[...] = jnp.zeros_like(carry_ref)
            gb_ref[...] = jnp.zeros_like(gb_ref)

        heads = [dr_ref[hd, 0:1, :] + dr_ref[hd, 1:2, :] for hd in range(N_FOX)]
        dc = jnp.concatenate(heads + [jnp.zeros((128 - N_FOX, tr), F32)], axis=0).T
        r = lax.broadcasted_iota(jnp.int32, (tr, tr), 0)
        c = lax.broadcasted_iota(jnp.int32, (tr, tr), 1)
        tri = (c >= r).astype(F32)
        rc = jnp.dot(tri, dc, precision=lax.Precision.HIGHEST, preferred_element_type=F32) + carry_ref[...]
        carry_ref[...] = rc[0:1, :]
        z = z_ref[...] + b_ref[...]
        dz = rc * _sigmoid(-z)
        gb_ref[...] += jnp.sum(dz, axis=0, keepdims=True)
        dz_ref[...] = jnp.concatenate([dz.astype(_BF), jnp.zeros((tr, wide - 128), _BF)], axis=1)

    return pl.pallas_call(
        body, grid=(nb,),
        in_specs=[pl.BlockSpec((N_FOX, 8, tr), lambda i: (0, 0, nb - 1 - i)),
                  pl.BlockSpec((tr, 128), lambda i: (nb - 1 - i, fb)), pl.BlockSpec((1, 128), lambda i: (0, 0)),
                  pl.BlockSpec(memory_space=pl.ANY)],
        out_specs=[pl.BlockSpec((tr, wide), lambda i: (nb - 1 - i, F_FL // wide)),
                   pl.BlockSpec((1, 128), lambda i: (0, 0))],
        out_shape=[SDS(dp.shape, dp.dtype), SDS((1, 128), F32)], scratch_shapes=[pltpu.VMEM((1, 128), F32)],
        input_output_aliases={3: 0},
        compiler_params=_cp("arbitrary"), name="fox_dlogf")(dr, p_f, b_f_pad, dp)


def _fox_fwd(p_b, p_f, c_col, c_row, y):
    S = p_b.shape[0]
    t = min(2 * T_ATT, S)
    nq = S // t
    rc = min(256, t)
    qb, kb, vb, gb = B_Q // HEAD, B_K // HEAD, B_V // HEAD, F_FG // HEAD
    kq = SCALE * 1.4426950408889634

    def body(q_ref, k_ref, v_ref, cc_ref, cr_ref, g_ref, yin_ref, o_ref, y_ref, lse_ref, va_ref, ua_ref, ub_ref, m_ref,
             acc_ref):
        i = pl.program_id(1)

        @pl.when(i == 0)
        def _():
            va_ref[:, 0:HEAD] = v_ref[...]
            lane = lax.broadcasted_iota(jnp.int32, (S, HEAD), 1)
            va_ref[:, HEAD:2 * HEAD] = jnp.where(lane == 0, 1.0, 0.0).astype(_BF)

        m_ref[...] = jnp.full_like(m_ref, NEG)
        acc_ref[...] = jnp.zeros_like(acc_ref)

        def scores(b, u_ref):
            off = pl.multiple_of(b * t, t)
            k = k_ref[pl.ds(off, t), :]
            csr = cr_ref[:, pl.ds(off, t)] * (1.0 / SCALE)
            for r in range(0, t, rc):
                u_ref[r:r + rc, :] = _dot(q_ref[r:r + rc, :], k, NT) - csr

        def absorb(b, u_ref, masked):
            va = va_ref[pl.ds(pl.multiple_of(b * t, t), t), :]
            for r in range(0, t, rc):
                u = u_ref[r:r + rc, :]
                if masked:
                    row = lax.broadcasted_iota(jnp.int32, (rc, t), 0) + r
                    col = lax.broadcasted_iota(jnp.int32, (rc, t), 1)
                    u = jnp.where(col <= row, u, NEG)
                m_old = m_ref[r:r + rc, :]
                m_new = jnp.maximum(m_old, jnp.max(u, axis=-1, keepdims=True))
                alpha = jnp.exp2((m_old - m_new) * kq)
                p = jnp.exp2((u - m_new) * kq)
                acc_ref[r:r + rc, :] = alpha * acc_ref[r:r + rc, :] + _dot(p.astype(_BF), va, NN)
                m_ref[r:r + rc, :] = m_new

        scores(0, ua_ref)

        def pair(pi, carry):
            b = 2 * pi
            scores(b + 1, ub_ref)
            absorb(b, ua_ref, False)
            scores(b + 2, ua_ref)
            absorb(b + 1, ub_ref, False)
            return carry

        lax.fori_loop(0, i // 2, pair, 0)

        @pl.when(i % 2 == 1)
        def _():
            scores(i, ub_ref)
            absorb(i - 1, ua_ref, False)
            absorb(i, ub_ref, True)

        @pl.when(i % 2 == 0)
        def _():
            absorb(i, ua_ref, True)

        l = acc_ref[:, HEAD:HEAD + 1]
        o = acc_ref[:, 0:HEAD] / l
        gv = g_ref[...]
        o_ref[...] = o.astype(_BF)
        y_ref[...] = (o * (gv * _sigmoid(gv))).astype(_BF)
        lse_ref[...] = cc_ref[...] + SCALE * m_ref[...] + jnp.log(l)

    tile = pl.BlockSpec((t, HEAD), lambda h, i: (i, h))
    return pl.pallas_call(
        body, grid=(N_FOX, nq),
        in_specs=[pl.BlockSpec((t, HEAD), lambda h, i: (i, qb + h)),
                  pl.BlockSpec((S, HEAD), lambda h, i: (0, kb + h)),
                  pl.BlockSpec((S, HEAD), lambda h, i: (0, vb + h)),
                  pl.BlockSpec((None, t, 1), lambda h, i: (h, i, 0)),
                  pl.BlockSpec((None, 1, S), lambda h, i: (h, 0, 0)),
                  pl.BlockSpec((t, HEAD), lambda h, i: (i, gb + h)), pl.BlockSpec(memory_space=pl.ANY)],
        out_specs=[tile, pl.BlockSpec((t, HEAD), lambda h, i: (i, CONV_W // HEAD + h)),
                   pl.BlockSpec((None, t, 1), lambda h, i: (h, i, 0))],
        out_shape=[SDS((S, FOX_W), _BF), SDS(y.shape, y.dtype), SDS((N_FOX, S, 1), F32)],
        input_output_aliases={6: 1},
        scratch_shapes=[pltpu.VMEM((S, 2 * HEAD), _BF), pltpu.VMEM((t, t), F32), pltpu.VMEM((t, t), F32),
                        pltpu.VMEM((t, 1), F32), pltpu.VMEM((t, 2 * HEAD), F32)],
        compiler_params=_cp("parallel", "arbitrary"), name="fox_fwd")(p_b, p_b, p_b, c_col, c_row, p_f, y)


def _fox_bwd_prep(dy, p_f, o, lse, c_col, dp):
    S = dy.shape[0]
    t = min(T_ATT, S)
    hg = 4
    wd = hg * HEAD

    def body(dy_ref, g_ref, o_ref, lse_ref, cc_ref, dp_ref, do_ref, dg_ref, rows_ref):
        gv = g_ref[...]
        sg = _sigmoid(gv)
        dyv = dy_ref[...]
        ov = o_ref[...].astype(F32)
        do = dyv * (gv * sg)
        do_ref[...] = do.astype(_BF)
        dg_ref[...] = (dyv * ov * _dsilu(gv, sg)).astype(_BF)
        prod = do * ov
        lane = lax.broadcasted_iota(jnp.int32, (t, 128), 1)
        for hd in range(hg):
            delta = jnp.sum(prod[:, hd * HEAD:(hd + 1) * HEAD], axis=-1, keepdims=True)
            a = cc_ref[hd] - lse_ref[hd]
            mat = jnp.where(lane == 0, a, jnp.where(lane == 1, delta, 0.0))
            rows_ref[hd] = mat.T[0:8, :]

    tile = pl.BlockSpec((t, wd), lambda g, i: (i, g))
    col = pl.BlockSpec((hg, t, 1), lambda g, i: (g, i, 0))
    return pl.pallas_call(
        body, grid=(N_FOX // hg, S // t),
        in_specs=[pl.BlockSpec((t, wd), lambda g, i: (i, CONV_W // wd + g)),
                  pl.BlockSpec((t, wd), lambda g, i: (i, F_FG // wd + g)), tile, col, col,
                  pl.BlockSpec(memory_space=pl.ANY)],
        out_specs=[tile, pl.BlockSpec((t, wd), lambda g, i: (i, F_FG // wd + g)),
                   pl.BlockSpec((hg, 8, t), lambda g, i: (g, 0, i))],
        out_shape=[SDS((S, FOX_W), _BF), SDS(dp.shape, dp.dtype), SDS((N_FOX, 8, S), F32)],
        input_output_aliases={5: 1},
        compiler_params=_cp("parallel", "parallel"), name="fox_bwd_prep")(dy, p_f, o, lse, c_col, dp)


def _fox_bwd(p_b, do, rows, c_col):
    S = p_b.shape[0]
    t = min(T_ATT, S)
    nk = S // t
    qb, kb, vb = B_Q // HEAD, B_K // HEAD, B_V // HEAD

    def body(k_ref, v_ref, q_ref, do_ref, rows_ref, cc_ref, dq_ref, dk_ref, dv_ref, dr_ref,
             dqt_ref, dka_ref, dva_ref, dca_ref, dra_ref, sa_ref, pa_ref, sb_ref, pb_ref):
        j = pl.program_id(1)

        @pl.when(j == 0)
        def _():
            dqt_ref[...] = jnp.zeros_like(dqt_ref)
            dra_ref[...] = jnp.zeros_like(dra_ref)

        k = k_ref[...]
        v = v_ref[...]
        kt = k.astype(F32).T.astype(_BF)
        cc = cc_ref[...]
        dka_ref[...] = jnp.zeros_like(dka_ref)
        dva_ref[...] = jnp.zeros_like(dva_ref)
        dca_ref[...] = jnp.zeros_like(dca_ref)

        def scores(i, s_ref, p_ref):
            off = pl.multiple_of(i * t, t)
            s_ref[...] = _dot(k, q_ref[pl.ds(off, t), :], NT) * SCALE + (rows_ref[0:1, pl.ds(off, t)] - cc)
            p_ref[...] = _dot(v, do_ref[pl.ds(off, t), :], NT) - rows_ref[1:2, pl.ds(off, t)]

        def absorb(i, s_ref, p_ref, masked):
            off = pl.multiple_of(i * t, t)
            st = s_ref[...]
            if masked:
                srow = lax.broadcasted_iota(jnp.int32, (t, t), 0)
                tcol = lax.broadcasted_iota(jnp.int32, (t, t), 1)
                st = jnp.where(srow <= tcol, st, NEG)
            pt = jnp.exp(st)
            dva_ref[...] += _dot(pt.astype(_BF), do_ref[pl.ds(off, t), :], NN)
            dst = pt * p_ref[...]
            part = dst[:, 0:128]
            for gidx in range(1, t // 128):
                part = part + dst[:, gidx * 128:(gidx + 1) * 128]
            dca_ref[...] += part
            dra_ref[0:1, pl.ds(off, t)] += jnp.sum(dst, axis=0, keepdims=True)
            dsb = dst.astype(_BF)
            dka_ref[...] += _dot(dsb, q_ref[pl.ds(off, t), :], NN)
            dqt_ref[:, pl.ds(off, t)] += _dot(kt, dsb, NN)

        rest = nk - 1 - j
        scores(j, sa_ref, pa_ref)

        @pl.when(rest == 0)
        def _():
            absorb(j, sa_ref, pa_ref, True)

        @pl.when(rest > 0)
        def _():
            scores(j + 1, sb_ref, pb_ref)
            absorb(j, sa_ref, pa_ref, True)

            def pair(pi, carry):
                c = j + 1 + 2 * pi
                scores(c + 1, sa_ref, pa_ref)
                absorb(c, sb_ref, pb_ref, False)
                scores(c + 2, sb_ref, pb_ref)
                absorb(c + 1, sa_ref, pa_ref, False)
                return carry

            lax.fori_loop(0, (rest - 1) // 2, pair, 0)

            @pl.when(rest % 2 == 1)
            def _():
                absorb(nk - 1, sb_ref, pb_ref, False)

            @pl.when(rest % 2 == 0)
            def _():
                scores(nk - 1, sa_ref, pa_ref)
                absorb(nk - 2, sb_ref, pb_ref, False)
                absorb(nk - 1, sa_ref, pa_ref, False)

        dk_ref[...] = (dka_ref[...] * SCALE).astype(_BF)
        dv_ref[...] = dva_ref[...].astype(_BF)
        dra_ref[1:2, pl.ds(pl.multiple_of(j * t, t), t)] = -jnp.sum(dca_ref[...].T, axis=0, keepdims=True)

        @pl.when(j == nk - 1)
        def _():
            dr_ref[...] = dra_ref[...]
            for ci in range(nk):
                dq_ref[ci * t:(ci + 1) * t, :] = (dqt_ref[:, ci * t:(ci + 1) * t].T * SCALE).astype(_BF)

    tile = pl.BlockSpec((t, HEAD), lambda h, j: (j, h))
    return pl.pallas_call(
        body, grid=(N_FOX, nk),
        in_specs=[pl.BlockSpec((t, HEAD), lambda h, j: (j, kb + h)),
                  pl.BlockSpec((t, HEAD), lambda h, j: (j, vb + h)),
                  pl.BlockSpec((S, HEAD), lambda h, j: (0, qb + h)),
                  pl.BlockSpec((S, HEAD), lambda h, j: (0, h)),
                  pl.BlockSpec((None, 8, S), lambda h, j: (h, 0, 0)),
                  pl.BlockSpec((None, t, 1), lambda h, j: (h, j, 0))],
        out_specs=[pl.BlockSpec((S, HEAD), lambda h, j: (0, h)), tile, tile,
                   pl.BlockSpec((None, 8, S), lambda h, j: (h, 0, 0))],
        out_shape=[SDS((S, FOX_W), _BF), SDS((S, FOX_W), _BF), SDS((S, FOX_W), _BF), SDS((N_FOX, 8, S), F32)],
        scratch_shapes=[pltpu.VMEM((HEAD, S), F32), pltpu.VMEM((t, HEAD), F32), pltpu.VMEM((t, HEAD), F32),
                        pltpu.VMEM((t, 128), F32), pltpu.VMEM((8, S), F32)] + [pltpu.VMEM((t, t), F32)] * 4,
        compiler_params=_cp("parallel", "arbitrary"), name="fox_bwd")(p_b, p_b, p_b, do, rows, c_col)


CHUNK = 32


SUBLANES = 8


def _shifted_windows(win):
    n = win.shape[0]
    return [win] + [pltpu.roll(win, n - s, axis=0) for s in range(1, SUBLANES)]


def _tap(rot, f):
    return rot[f % SUBLANES][f - f % SUBLANES:f - f % SUBLANES + CHUNK, :]


def _conv_taps(ext_ref, w_ref, first, out_fn, n_rows):
    def chunk(c, carry):
        r0 = pl.multiple_of(c * CHUNK, CHUNK)
        rot = _shifted_windows(ext_ref[pl.ds(r0, 2 * CHUNK), :])
        acc = jnp.zeros((CHUNK, CONV_W), F32)
        for k in range(CONV_K):
            acc = acc + w_ref[k:k + 1, :] * _tap(rot, first(k))
        out_fn(r0, acc)
        return carry

    lax.fori_loop(0, n_rows // CHUNK, chunk, 0)


def _conv_fwd(p_f, conv_w, conv_b, ln_g, ln_b, w_pw):
    S = p_f.shape[0]
    tc = min(T_CONV, S)
    hb = tc // HALO

    def body(a_ref, b_ref, gc_ref, ap_ref, bp_ref, w_ref, cb_ref, lg_ref, lb_ref, pw_ref, y_ref, u1_ref, ext_ref):
        i = pl.program_id(0)
        prev = ap_ref[...] * _sigmoid(bp_ref[...])
        ext_ref[0:HALO, :] = jnp.where(i > 0, prev, 0.0)
        ext_ref[HALO:HALO + tc, :] = a_ref[...] * _sigmoid(b_ref[...])
        cb = cb_ref[...]

        def put(r0, acc):
            u1_ref[pl.ds(r0, CHUNK), :] = acc + cb

        _conv_taps(ext_ref, w_ref, lambda k: HALO - (CONV_K - 1) + k, put, tc)
        u1 = u1_ref[...]
        mu = jnp.mean(u1, axis=-1, keepdims=True)
        d = u1 - mu
        rstd = lax.rsqrt(jnp.mean(d * d, axis=-1, keepdims=True) + EPS)
        u2 = d * rstd * lg_ref[...] + lb_ref[...]
        u3 = u2 * _sigmoid(u2)
        pw = _dot(u3.astype(_BF), pw_ref[...], NN)
        gc = gc_ref[...]
        y_ref[...] = (pw * (gc * _sigmoid(gc))).astype(_BF)

    blk = lambda cb_: pl.BlockSpec((tc, CONV_W), lambda i: (i, cb_))
    halo = lambda cb_: pl.BlockSpec((HALO, CONV_W), lambda i: (jnp.maximum(i * hb - 1, 0), cb_))
    vec = pl.BlockSpec((1, CONV_W), lambda i: (0, 0))
    return pl.pallas_call(
        body, grid=(S // tc,),
        in_specs=[blk(0), blk(1), blk(2), halo(0), halo(1), pl.BlockSpec((CONV_K, CONV_W), lambda i: (0, 0)),
                  vec, vec, vec, pl.BlockSpec((CONV_W, CONV_W), lambda i: (0, 0))],
        out_specs=[pl.BlockSpec((tc, CONV_W), lambda i: (i, 0)), pl.BlockSpec((tc, CONV_W), lambda i: (i, 0))],
        out_shape=[SDS((S, CONV_W + FOX_W + MEM_W), _BF), SDS((S, CONV_W), F32)],
        scratch_shapes=[pltpu.VMEM((tc + 2 * HALO, CONV_W), F32)],
        compiler_params=_cp("parallel"), name="conv_fwd")(p_f, p_f, p_f, p_f, p_f, conv_w, conv_b, ln_g, ln_b, w_pw)


def _conv_bwd1(u1, p_f, dy, ln_g, ln_b, w_pw):
    S = u1.shape[0]
    tc = min(T_CONV, S)

    def body(u1_ref, gc_ref, dy_ref, lg_ref, lb_ref, pw_ref, du1_ref, dgc_ref, gpw_ref, glg_ref, glb_ref, gcb_ref):
        i = pl.program_id(0)
        u1v = u1_ref[...]
        mu = jnp.mean(u1v, axis=-1, keepdims=True)
        d = u1v - mu
        rstd = lax.rsqrt(jnp.mean(d * d, axis=-1, keepdims=True) + EPS)
        xh = d * rstd
        lg = lg_ref[...]
        u2 = xh * lg + lb_ref[...]
        sg2 = _sigmoid(u2)
        u3b = (u2 * sg2).astype(_BF)
        w = pw_ref[...]
        pw = _dot(u3b, w, NN)
        gc = gc_ref[...]
        sgc = _sigmoid(gc)
        dyv = dy_ref[...]
        dpw = (dyv * (gc * sgc)).astype(_BF)
        dgc_ref[...] = (dyv * pw * _dsilu(gc, sgc)).astype(_BF)
        gpw = _dot(u3b, dpw, TN)
        du2 = _dot(dpw, w, NT) * _dsilu(u2, sg2)
        glg = jnp.sum(du2 * xh, axis=0, keepdims=True)
        glb = jnp.sum(du2, axis=0, keepdims=True)
        dxh = du2 * lg
        du1 = rstd * (dxh - jnp.mean(dxh, axis=-1, keepdims=True) - xh * jnp.mean(dxh * xh, axis=-1, keepdims=True))
        du1_ref[...] = du1
        gcb = jnp.sum(du1, axis=0, keepdims=True)

        @pl.when(i == 0)
        def _():
            gpw_ref[...] = gpw
            glg_ref[...] = glg
            glb_ref[...] = glb
            gcb_ref[...] = gcb

        @pl.when(i > 0)
        def _():
            gpw_ref[...] += gpw
            glg_ref[...] += glg
            glb_ref[...] += glb
            gcb_ref[...] += gcb

    row = pl.BlockSpec((tc, CONV_W), lambda i: (i, 0))
    vec = pl.BlockSpec((1, CONV_W), lambda i: (0, 0))
    sq = pl.BlockSpec((CONV_W, CONV_W), lambda i: (0, 0))
    return pl.pallas_call(
        body, grid=(S // tc,),
        in_specs=[row, pl.BlockSpec((tc, CONV_W), lambda i: (i, F_GC // CONV_W)), row, vec, vec, sq],
        out_specs=[row, pl.BlockSpec((tc, CONV_W), lambda i: (i, F_GC // CONV_W)), sq, vec, vec, vec],
        out_shape=[SDS((S, CONV_W), F32), SDS((S, 2 * PART_W), _BF), SDS((CONV_W, CONV_W), F32),
                   SDS((1, CONV_W), F32), SDS((1, CONV_W), F32), SDS((1, CONV_W), F32)],
        compiler_params=_cp("arbitrary"), name="conv_bwd1")(u1, p_f, dy, ln_g, ln_b, w_pw)


def _conv_bwd2(du1, p_f, conv_w, dp):
    S = du1.shape[0]
    tc = min(T_CONV, S)
    hb = tc // HALO
    nblk = S // tc
    last_halo = S // HALO - 1

    def body(d_ref, dn_ref, a_ref, b_ref, ap_ref, bp_ref, w_ref, dp_ref, dab_ref, gw_ref, ext_ref, dext_ref, du0_ref):
        i = pl.program_id(0)
        av = a_ref[...]
        sb = _sigmoid(b_ref[...])
        prev = ap_ref[...] * _sigmoid(bp_ref[...])
        ext_ref[0:HALO, :] = jnp.where(i > 0, prev, 0.0)
        ext_ref[HALO:HALO + tc, :] = av * sb
        dext_ref[0:tc, :] = d_ref[...]
        dext_ref[tc:tc + HALO, :] = jnp.where(i < nblk - 1, dn_ref[...], 0.0)

        def put(r0, acc):
            du0_ref[pl.ds(r0, CHUNK), :] = acc

        _conv_taps(dext_ref, w_ref, lambda k: CONV_K - 1 - k, put, tc)
        du0 = du0_ref[...]
        dab_ref[:, 0:CONV_W] = (du0 * sb).astype(_BF)
        dab_ref[:, CONV_W:2 * CONV_W] = (du0 * av * sb * (1.0 - sb)).astype(_BF)

        def chunk(c, carry):
            r0 = pl.multiple_of(c * CHUNK, CHUNK)
            rot = _shifted_windows(ext_ref[pl.ds(r0, 2 * CHUNK), :])
            dv = dext_ref[pl.ds(r0, CHUNK), :]
            rows = [jnp.sum(dv * _tap(rot, HALO - (CONV_K - 1) + k), axis=0, keepdims=True) for k in range(CONV_K)]
            rows.append(jnp.zeros((1, CONV_W), F32))
            return carry + jnp.concatenate(rows, axis=0)

        gw = lax.fori_loop(0, tc // CHUNK, chunk, jnp.zeros((CONV_K + 1, CONV_W), F32))

        @pl.when(i == 0)
        def _():
            gw_ref[...] = gw

        @pl.when(i > 0)
        def _():
            gw_ref[...] += gw

    row = pl.BlockSpec((tc, CONV_W), lambda i: (i, 0))
    blk = lambda cb_: pl.BlockSpec((tc, CONV_W), lambda i: (i, cb_))
    halo = lambda cb_: pl.BlockSpec((HALO, CONV_W), lambda i: (jnp.maximum(i * hb - 1, 0), cb_))
    nxt = pl.BlockSpec((HALO, CONV_W), lambda i: (jnp.minimum((i + 1) * hb, last_halo), 0))
    return pl.pallas_call(
        body, grid=(nblk,),
        in_specs=[row, nxt, blk(0), blk(1), halo(0), halo(1), pl.BlockSpec((CONV_K, CONV_W), lambda i: (0, 0)),
                  pl.BlockSpec(memory_space=pl.ANY)],
        out_specs=[pl.BlockSpec((tc, 2 * CONV_W), lambda i: (i, 0)), pl.BlockSpec((CONV_K + 1, CONV_W), lambda i: (0, 0))],
        out_shape=[SDS(dp.shape, dp.dtype), SDS((CONV_K + 1, CONV_W), F32)], input_output_aliases={7: 0},
        scratch_shapes=[pltpu.VMEM((tc + 2 * HALO, CONV_W), F32), pltpu.VMEM((tc + 2 * HALO, CONV_W), F32),
                        pltpu.VMEM((tc, CONV_W), F32)],
        compiler_params=_cp("arbitrary"), name="conv_bwd2")(du1, du1, p_f, p_f, p_f, p_f, conv_w, dp)


def _exchange(srcs, scatter, name):
    n = len(srcs)
    out_shape = [SDS((N_DEV,) + (s.shape[1:] if sc else s.shape), s.dtype) for s, sc in zip(srcs, scatter)]

    def body(*refs):
        src_refs, dst_refs = refs[:n], refs[n:2 * n]
        send_sems, recv_sems, local_sems = refs[2 * n:]
        x, y, c = lax.axis_index("x"), lax.axis_index("y"), lax.axis_index("c")
        me = 4 * x + 2 * y + c
        copies = []
        for a in range(n):
            for f in range(1, N_DEV):
                px = 1 - x if f & 4 else x
                py = 1 - y if f & 2 else y
                pc = 1 - c if f & 1 else c
                peer = 4 * px + 2 * py + pc
                src = src_refs[a].at[peer] if scatter[a] else src_refs[a]
                cp = pltpu.make_async_remote_copy(
                    src_ref=src, dst_ref=dst_refs[a].at[me], send_sem=send_sems.at[a, f - 1],
                    recv_sem=recv_sems.at[a, f - 1], device_id=(px, py, pc), device_id_type=MESH)
                cp.start()
                copies.append(cp)
            own = src_refs[a].at[me] if scatter[a] else src_refs[a]
            lc = pltpu.make_async_copy(own, dst_refs[a].at[me], local_sems.at[a])
            lc.start()
            copies.append(lc)
        for cp in copies:
            cp.wait()

    anyspec = pl.BlockSpec(memory_space=pl.ANY)
    return pl.pallas_call(
        body, in_specs=[anyspec] * n, out_specs=[anyspec] * n, out_shape=out_shape,
        scratch_shapes=[pltpu.SemaphoreType.DMA((n, N_DEV - 1)), pltpu.SemaphoreType.DMA((n, N_DEV - 1)),
                        pltpu.SemaphoreType.DMA((n,))],
        name=name)(*srcs)


def _flip_peer(f, x, y, c):
    return (1 - x if f & 4 else x, 1 - y if f & 2 else y, 1 - c if f & 1 else c)


def _scatter_start(srcs):
    n = len(srcs)
    lands = [lax.empty((N_DEV - 1,) + s.shape[1:], s.dtype) for s in srcs]

    def body(*refs):
        src_refs, land_refs = refs[:n], refs[n:2 * n]
        send_sems, recv_sems = refs[2 * n:3 * n], refs[3 * n:4 * n]
        token = refs[-1]
        x, y, c = lax.axis_index("x"), lax.axis_index("y"), lax.axis_index("c")
        for a in range(n):
            for f in range(1, N_DEV):
                px, py, pc = _flip_peer(f, x, y, c)
                pltpu.make_async_remote_copy(
                    src_ref=src_refs[a].at[4 * px + 2 * py + pc], dst_ref=land_refs[a].at[f - 1],
                    send_sem=send_sems[a], recv_sem=recv_sems[a], device_id=(px, py, pc), device_id_type=MESH).start()
        token[...] = jnp.zeros_like(token)

    hbm = pl.BlockSpec(memory_space=pltpu.HBM)
    sem = pl.BlockSpec(memory_space=pltpu.SEMAPHORE)
    bufs = [pltpu.with_memory_space_constraint(b, pltpu.HBM) for b in list(srcs) + lands]
    out = pl.pallas_call(
        body, name="scatter_start",
        out_shape=(*[pltpu.SemaphoreType.DMA(())] * (2 * n), *[pltpu.HBM(b.shape, b.dtype) for b in bufs],
                   SDS((8, 128), F32)),
        in_specs=[hbm] * (2 * n),
        out_specs=(*[sem] * (2 * n), *[hbm] * (2 * n), pl.BlockSpec(memory_space=pltpu.VMEM)),
        input_output_aliases={i: 2 * n + i for i in range(2 * n)},
        compiler_params=pltpu.CompilerParams(has_side_effects=pltpu.SideEffectType.DATAFLOW_SIDE_EFFECTING))(*bufs)
    return list(out[:2 * n]), list(out[2 * n:3 * n]), list(out[3 * n:4 * n]), out[-1]


def _scatter_wait(sems, srcs, lands, after):
    n = len(srcs)

    def body(*refs):
        src_refs, land_refs = refs[:n], refs[n:2 * n]
        send_sems, recv_sems = refs[2 * n:3 * n], refs[3 * n:4 * n]
        x, y, c = lax.axis_index("x"), lax.axis_index("y"), lax.axis_index("c")
        for a in range(n):
            seven = pltpu.make_async_remote_copy(
                src_ref=src_refs[a].at[pl.ds(0, N_DEV - 1)], dst_ref=land_refs[a], send_sem=send_sems[a],
                recv_sem=recv_sems[a], device_id=(x, y, c), device_id_type=MESH)
            seven.wait_send()
            seven.wait_recv()

    hbm = pl.BlockSpec(memory_space=pltpu.HBM)
    sem = pl.BlockSpec(memory_space=pltpu.SEMAPHORE)
    bufs = list(srcs) + list(lands)
    out = pl.pallas_call(
        body, name="scatter_wait", out_shape=tuple(pltpu.HBM(b.shape, b.dtype) for b in bufs),
        in_specs=[hbm] * (2 * n) + [sem] * (2 * n) + [pl.BlockSpec(memory_space=pl.ANY)],
        out_specs=tuple([hbm] * (2 * n)), input_output_aliases={i: i for i in range(2 * n)},
        compiler_params=pltpu.CompilerParams(has_side_effects=pltpu.SideEffectType.DATAFLOW_SIDE_EFFECTING))(
            *bufs, *sems, after)
    return list(out[:n]), list(out[n:])


def _gather_two_level(srcs, name):
    n = len(srcs)
    out_shape = [SDS((N_DEV,) + s.shape, s.dtype) for s in srcs]

    def body(*refs):
        src_refs, dst_refs = refs[:n], refs[n:2 * n]
        send_sems, recv_sems, local_sems = refs[2 * n:]
        x, y, c = lax.axis_index("x"), lax.axis_index("y"), lax.axis_index("c")
        sibling = (x, y, 1 - c)
        chips = [(1 - x, y), (x, 1 - y), (1 - x, 1 - y)]

        def slot(a, px, py, pc):
            return dst_refs[a].at[4 * px + 2 * py + pc]

        def copy(a, k, block, to, src=None):
            return pltpu.make_async_remote_copy(
                src_ref=slot(a, *block) if src is None else src, dst_ref=slot(a, *block),
                send_sem=send_sems.at[a, k], recv_sem=recv_sems.at[a, k], device_id=to, device_id_type=MESH)

        own, sends = [], []
        for a in range(n):
            mine = pltpu.make_async_copy(src_refs[a], slot(a, x, y, c), local_sems.at[a])
            mine.start()
            own.append(mine)
            first = [copy(a, 1 + j, (x, y, c), (*chip, c), src=src_refs[a]) for j, chip in enumerate(chips)]
            first.append(copy(a, 0, (x, y, c), sibling, src=src_refs[a]))
            for cp in first:
                cp.start()
            sends += first
        for a in range(n):
            for j, chip in enumerate(chips):
                copy(a, 1 + j, (*chip, c), (x, y, c)).wait_recv()
                fwd = copy(a, 4 + j, (*chip, c), sibling)
                fwd.start()
                sends.append(fwd)
        for a in range(n):
            copy(a, 0, (x, y, 1 - c), (x, y, c)).wait_recv()
            for j, chip in enumerate(chips):
                copy(a, 4 + j, (*chip, 1 - c), (x, y, c)).wait_recv()
        for cp in sends:
            cp.wait_send()
        for cp in own:
            cp.wait()

    anyspec = pl.BlockSpec(memory_space=pl.ANY)
    return pl.pallas_call(
        body, in_specs=[anyspec] * n, out_specs=[anyspec] * n, out_shape=out_shape,
        scratch_shapes=[pltpu.SemaphoreType.DMA((n, N_DEV - 1)), pltpu.SemaphoreType.DMA((n, N_DEV - 1)),
                        pltpu.SemaphoreType.DMA((n,))],
        name=name)(*srcs)


def _adamw(parts, w, m, v, name, tr=256, own=None):
    lead = w.ndim == 3
    R, C = w.shape[-2:]
    tr = tr if R % tr == 0 else R
    n_parts = parts.shape[0]
    first = [] if own is None else [own]

    def body(*refs):
        p_ref, w_ref, m_ref, v_ref, g_ref, d_ref, nm_ref, nv_ref = refs[len(first):]
        terms = [r[0] for r in refs[:len(first)]] + [p_ref[dev] for dev in range(n_parts)]
        g = terms[0].astype(F32)
        for term in terms[1:]:
            g = g + term.astype(F32)
        mn = ADAM_B1 * m_ref[...] + (1.0 - ADAM_B1) * g
        vn = ADAM_B2 * v_ref[...] + (1.0 - ADAM_B2) * (g * g)
        m_hat = mn / (1.0 - ADAM_B1 ** ADAM_STEP)
        v_hat = vn / (1.0 - ADAM_B2 ** ADAM_STEP)
        g_ref[...] = g
        d_ref[...] = -ADAM_LR * (m_hat / (jnp.sqrt(v_hat) + ADAM_EPS) + ADAM_WD * w_ref[...])
        nm_ref[...] = mn
        nv_ref[...] = vn

    blk = pl.BlockSpec((None, tr, C), lambda i: (0, i, 0)) if lead else pl.BlockSpec((tr, C), lambda i: (i, 0))
    return pl.pallas_call(
        body, grid=(R // tr,),
        in_specs=[pl.BlockSpec((1, tr, C), lambda i: (0, i, 0))] * len(first)
        + [pl.BlockSpec((n_parts, tr, C), lambda i: (0, i, 0)), blk, blk, blk],
        out_specs=[blk] * 4, out_shape=[SDS(w.shape, F32)] * 4,
        compiler_params=_cp("parallel"), name=name)(*first, parts, w, m, v)


def _adamw_transposed(parts, own, w_t, m_t, v_t, name, tr=256):
    n_parts, R, C = parts.shape
    tr = tr if R % tr == 0 else R

    def body(o_ref, p_ref, w_ref, m_ref, v_ref, g_ref, d_ref, nm_ref, nv_ref):
        r = lax.broadcasted_iota(jnp.int32, (tr, tr), 0)
        c = lax.broadcasted_iota(jnp.int32, (tr, tr), 1)
        eye = (r == c).astype(parts.dtype)
        g = _dot(o_ref[0], eye, TN)
        for dev in range(n_parts):
            g = g + _dot(p_ref[dev], eye, TN)
        mn = ADAM_B1 * m_ref[...] + (1.0 - ADAM_B1) * g
        vn = ADAM_B2 * v_ref[...] + (1.0 - ADAM_B2) * (g * g)
        m_hat = mn / (1.0 - ADAM_B1 ** ADAM_STEP)
        v_hat = vn / (1.0 - ADAM_B2 ** ADAM_STEP)
        g_ref[...] = g
        d_ref[...] = -ADAM_LR * (m_hat / (jnp.sqrt(v_hat) + ADAM_EPS) + ADAM_WD * w_ref[...])
        nm_ref[...] = mn
        nv_ref[...] = vn

    blk = pl.BlockSpec((C, tr), lambda i: (0, i))
    return pl.pallas_call(
        body, grid=(R // tr,),
        in_specs=[pl.BlockSpec((1, tr, C), lambda i: (0, i, 0)), pl.BlockSpec((n_parts, tr, C), lambda i: (0, i, 0)),
                  blk, blk, blk],
        out_specs=[blk] * 4, out_shape=[SDS((C, R), F32)] * 4,
        compiler_params=_cp("parallel"), name=name)(own, parts, w_t, m_t, v_t)


def _split_w_in(w_cat, D):
    z = jnp.zeros((D, PART_W - F_FL - N_FOX), w_cat.dtype)
    return jnp.concatenate([w_cat[:, O_A:O_Q], w_cat[:, O_FG:O_MQ], w_cat[:, O_MG:D_IN], w_cat[:, O_FL:O_FG], z,
                            w_cat[:, O_Q:O_FL], w_cat[:, O_MQ:O_MG]], axis=1)


def _merge_w_in(g_f, g_b):
    return jnp.concatenate([g_f[:, F_A:F_FG], g_b[:, B_Q:B_MQ], g_f[:, F_FL:F_FL + N_FOX], g_f[:, F_FG:F_MG],
                            g_b[:, B_MQ:PART_W], g_f[:, F_MG:F_FL]], axis=1)


def _pack_small(norm, mem_norm, final, conv_b, ln_g, ln_b, b_f, extra, D):
    width = max(D, PACK_W)
    row3 = jnp.concatenate([conv_b.reshape(-1), ln_g.reshape(-1), ln_b.reshape(-1), b_f.reshape(-1)])
    rows = [norm.reshape(-1), mem_norm.reshape(-1), final.reshape(-1), row3, extra.reshape(-1)]
    rows = [jnp.pad(r, (0, width - r.shape[0])) for r in rows]
    return jnp.concatenate([jnp.stack(rows), jnp.zeros((3, width), F32)], axis=0)


def _unpack_small(p, D):
    c = CONV_W
    return dict(norm_g=p[0:1, :D], mem_norm_g=p[1:2, :D], final_g=p[2, :D], conv_b=p[3:4, 0:c],
                conv_ln_g=p[3:4, c:2 * c], conv_ln_b=p[3:4, 2 * c:3 * c], b_f=p[3:4, 3 * c:3 * c + N_FOX])


def kernel(x, mem, norm_g, mem_norm_g, w_in, b_f, conv_w, conv_b, conv_ln_g, conv_ln_b, w_conv_pw, w_mem_kv, w_out, final_g, loss_target, m_norm_g, m_mem_norm_g, m_w_in, m_b_f, m_conv_w, m_conv_b, m_conv_ln_g, m_conv_ln_b, m_w_conv_pw, m_w_mem_kv, m_w_out, m_final_g, v_norm_g, v_mem_norm_g, v_w_in, v_b_f, v_conv_w, v_conv_b, v_conv_ln_g, v_conv_ln_b, v_w_conv_pw, v_w_mem_kv, v_w_out, v_final_g):
    S, D = x.shape[1], x.shape[2]
    M = mem.shape[1]
    xs, ms, tgt = x[0], mem[0], loss_target[0]
    cols = w_in.shape[2]

    g_in, g_cw, g_pw, g_kv, g_out = _gather_two_level(
        [w_in[0].astype(_BF), conv_w[0], w_conv_pw[0].astype(_BF), w_mem_kv[0].astype(_BF), w_out[0].astype(_BF)],
        "gather_weights")
    w_all = _split_w_in(jnp.transpose(g_in, (1, 0, 2)).reshape(D, N_DEV * cols), D)
    conv_w_full = jnp.transpose(g_cw, (1, 0, 2)).reshape(CONV_K, CONV_W)
    w_pw_full = g_pw.reshape(CONV_W, CONV_W)
    w_kv_full = g_kv.reshape(D, 2 * MEM_W)
    w_out_full = g_out.reshape(CONV_W + FOX_W + MEM_W, D)
    b_f_pad = jnp.pad(b_f, ((0, 0), (0, 128 - N_FOX)))

    h, r1 = _rms_fwd(xs, norm_g)
    p_f = _matmul(h, w_all, "nn", F32, "proj_f", b_cols=(0, PART_W))
    p_b = _matmul(h, w_all, "nn", _BF, "proj_b", b_cols=(PART_W, PART_W))
    mkv, mhat = _mem_kv_fwd(ms, mem_norm_g, w_kv_full)
    y, u1 = _conv_fwd(p_f, conv_w_full, conv_b, conv_ln_g, conv_ln_b, w_pw_full)
    c_col, c_row = _fox_cumsum(p_f, b_f_pad)
    o_fox, y, lse = _fox_fwd(p_b, p_f, c_col, c_row, y)
    y = _mem_attn_fwd(p_b, p_f, mkv, y)
    z = _matmul(y, w_out_full, "nn", F32, "out_proj")
    dx2, g_final, loss_part = _head_loss(xs, z, tgt, final_g.reshape(1, D))

    dy = _matmul(dx2, w_out_full, "nt", F32, "d_y")
    gw_out = _matmul(y, dx2, "tn", _BF, "gw_out")
    d_mq, d_mg, d_mk, d_mv = _mem_attn_bwd(p_b, p_f, mkv, dy)
    gw_kv, g_mem_norm = _mem_kv_bwd(jnp.concatenate([d_mk, d_mv], axis=1), mhat, mem_norm_g, w_kv_full)
    du1, dp, gw_pw, g_ln_g, g_ln_b, g_cb = _conv_bwd1(u1, p_f, dy, conv_ln_g, conv_ln_b, w_pw_full)
    dp, gw_cw = _conv_bwd2(du1, p_f, conv_w_full, dp)
    d_o, dp, rows = _fox_bwd_prep(dy, p_f, o_fox, lse, c_col, dp)
    d_q, d_k, d_v, dr = _fox_bwd(p_b, d_o, rows, c_col)
    dp, g_bf = _fox_dlogf(dr, p_f, b_f_pad, dp)
    for piece, col in ((d_mg, F_MG), (d_q, PART_W + B_Q), (d_k, PART_W + B_K), (d_v, PART_W + B_V),
                       (d_mq, PART_W + B_MQ)):
        dp = lax.dynamic_update_slice(dp, piece, (0, col))
    gw_all = _matmul(h, dp, "tn", _BF, "gw_in")

    gw_in_cat = _merge_w_in(gw_all[:, :PART_W], gw_all[:, PART_W:])
    send_in = jnp.transpose(gw_in_cat.reshape(D, N_DEV, cols), (1, 0, 2))
    send_cw = jnp.transpose(gw_cw[:CONV_K].reshape(CONV_K, N_DEV, CONV_W // N_DEV), (1, 0, 2))
    send_pw = gw_pw.astype(_BF).reshape(N_DEV, CONV_W // N_DEV, CONV_W)
    send_kv = gw_kv.reshape(N_DEV, D // N_DEV, 2 * MEM_W)
    send_out = gw_out.reshape(N_DEV, (CONV_W + FOX_W + MEM_W) // N_DEV, D)
    sems, sent, lands, token = _scatter_start([send_in, send_cw, send_pw, send_kv, send_out])
    dh = _matmul(dp, w_all, "nt", F32, "d_h", after=token)
    grad_x, g_norm = _rms_bwd(xs, r1, dh, dx2, norm_g)
    sent, lands = _scatter_wait(sems, sent, lands, grad_x)
    me = 4 * lax.axis_index("x") + 2 * lax.axis_index("y") + lax.axis_index("c")
    own = [lax.dynamic_index_in_dim(s_, me, 0, keepdims=True) for s_ in sent]

    small = _pack_small(g_norm, g_mem_norm, g_final, g_cb, g_ln_g, g_ln_b, g_bf[:, :N_FOX], loss_part[0, 0:1], D)
    r_small, = _exchange([small], [False], "exchange_small")

    res = {}
    w_in_t = _adamw_transposed(lands[0], own[0], jnp.transpose(w_in[0]), jnp.transpose(m_w_in[0]),
                               jnp.transpose(v_w_in[0]), "adamw_w_in")
    res["w_in"] = [jnp.transpose(a_)[None] for a_ in w_in_t]
    res["conv_w"] = _adamw(lands[1], conv_w, m_conv_w, v_conv_w, "adamw_conv_w", own=own[1])
    res["w_conv_pw"] = _adamw(lands[2], w_conv_pw, m_w_conv_pw, v_w_conv_pw, "adamw_w_pw", own=own[2])
    res["w_mem_kv"] = _adamw(lands[3], w_mem_kv, m_w_mem_kv, v_w_mem_kv, "adamw_w_kv", own=own[3])
    res["w_out"] = _adamw(lands[4], w_out, m_w_out, v_w_out, "adamw_w_out", own=own[4])
    zero = jnp.zeros((1,), F32)
    pk = lambda a, b_, c_, d_, e, f_, g_: _pack_small(a, b_, c_, d_, e, f_, g_, zero, D)
    sm = _adamw(r_small,
                pk(norm_g, mem_norm_g, final_g, conv_b, conv_ln_g, conv_ln_b, b_f),
                pk(m_norm_g, m_mem_norm_g, m_final_g, m_conv_b, m_conv_ln_g, m_conv_ln_b, m_b_f),
                pk(v_norm_g, v_mem_norm_g, v_final_g, v_conv_b, v_conv_ln_g, v_conv_ln_b, v_b_f), "adamw_small")
    loss = sm[0][4, 0]
    small_out = [_unpack_small(a, D) for a in sm]
    names = ["norm_g", "mem_norm_g", "w_in", "b_f", "conv_w", "conv_b", "conv_ln_g", "conv_ln_b", "w_conv_pw",
             "w_mem_kv", "w_out", "final_g"]
    outs = [loss, grad_x[None]]
    for kind in range(4):
        for nme in names:
            outs.append(res[nme][kind] if nme in res else small_out[kind][nme])
    return tuple(outs)
```

```python
import functools

import jax
import jax.numpy as jnp
from jax import lax
from jax.experimental import pallas as pl
from jax.experimental.pallas import tpu as pltpu

F32 = jnp.float32
_BF = jnp.bfloat16
SDS = jax.ShapeDtypeStruct
MESH = pl.DeviceIdType.MESH

N_DEV = 8
HEAD = 128
N_FOX = 8
N_MEM = 4
CONV_W = 512
CONV_K = 31
FOX_W = N_FOX * HEAD
MEM_W = N_MEM * HEAD
D_IN = 3 * CONV_W + 4 * FOX_W + N_FOX + 2 * MEM_W
EPS = 1e-6
SCALE = HEAD ** -0.5
NEG = -1e30

ADAM_LR = 0.001
ADAM_B1 = 0.9
ADAM_B2 = 0.999
ADAM_EPS = 1e-08
ADAM_WD = 0.01
ADAM_STEP = 10

PART_W = 3584
F_A, F_B, F_GC, F_FG, F_MG, F_FL = 0, 512, 1024, 1536, 2560, 3072
B_Q, B_K, B_V, B_MQ = 0, 1024, 2048, 3072
O_A, O_B, O_GC = 0, 512, 1024
O_Q, O_K, O_V = 1536, 2560, 3584
O_FL, O_FG, O_MQ, O_MG = 4608, 4616, 5640, 6152

HALO = 32
T_ATT = 512
T_ROW = 256
T_CONV = 512
VMEM_LIMIT = 56 * 1024 * 1024
PACK_W = 2048


def _cp(*sem):
    return pltpu.CompilerParams(dimension_semantics=sem, vmem_limit_bytes=VMEM_LIMIT)


def _sigmoid(x):
    return jax.nn.sigmoid(x)


def _dsilu(x, sg):
    return sg * (1.0 + x * (1.0 - sg))


def _dot(a, b, dims):
    return lax.dot_general(a, b, (dims, ((), ())), preferred_element_type=F32)


NN = ((1,), (0,))
NT = ((1,), (1,))
TN = ((0,), (0,))


def _pick(n, pref):
    if n <= pref:
        return n
    t = pref - pref % 128
    while n % t:
        t -= 128
    return t


def _matmul(a, b, mode, out_dtype, name, tm=512, tn=1024, tk=2048, after=None, b_cols=None):
    col0 = 0
    if mode == "nn":
        (M, K), (K2, N) = a.shape, b.shape
        if b_cols is not None:
            col0, N = b_cols
    elif mode == "nt":
        (M, K), (N, K2) = a.shape, b.shape
    else:
        (K, M), (K2, N) = a.shape, b.shape
    assert K == K2
    tn, tk = _pick(N, tn), _pick(K, tk)
    nk = K // tk
    tm = _pick(M, tm if nk == 1 else 2 * tm)
    assert M % tm == 0 and N % tn == 0 and K % tk == 0, (name, M, N, K)
    dims = {"nn": NN, "nt": NT, "tn": TN}[mode]
    n_in = 2 if after is None else 3

    def body(*refs):
        a_ref, b_ref = refs[0], refs[1]
        o_ref = refs[n_in]
        p = _dot(a_ref[...].astype(_BF), b_ref[...].astype(_BF), dims)
        if nk == 1:
            o_ref[...] = p.astype(out_dtype)
            return
        acc_ref = refs[n_in + 1]
        k = pl.program_id(2)

        @pl.when(k == 0)
        def _():
            acc_ref[...] = p

        @pl.when(jnp.logical_and(k > 0, k < nk - 1))
        def _():
            acc_ref[...] += p

        @pl.when(k == nk - 1)
        def _():
            o_ref[...] = (acc_ref[...] + p).astype(out_dtype)

    if mode == "nn":
        assert col0 % tn == 0
        jb = col0 // tn
        a_spec = pl.BlockSpec((tm, tk), lambda j, i, k: (i, k))
        b_spec = pl.BlockSpec((tk, tn), lambda j, i, k: (k, j + jb))
    elif mode == "nt":
        a_spec = pl.BlockSpec((tm, tk), lambda j, i, k: (i, k))
        b_spec = pl.BlockSpec((tn, tk), lambda j, i, k: (j, k))
    else:
        a_spec = pl.BlockSpec((tk, tm), lambda j, i, k: (k, i))
        b_spec = pl.BlockSpec((tk, tn), lambda j, i, k: (k, j))
    in_specs, args = [a_spec, b_spec], [a, b]
    if after is not None:
        in_specs.append(pl.BlockSpec(memory_space=pl.ANY))
        args.append(after)
    return pl.pallas_call(
        body, grid=(N // tn, M // tm, nk), in_specs=in_specs,
        out_specs=pl.BlockSpec((tm, tn), lambda j, i, k: (i, j)),
        out_shape=SDS((M, N), out_dtype), scratch_shapes=[] if nk == 1 else [pltpu.VMEM((tm, tn), F32)],
        compiler_params=_cp("parallel", "parallel", "arbitrary"), name=name)(*args)


def _rms_fwd(x, g):
    S, D = x.shape
    tr = min(T_ROW, S)

    def body(x_ref, g_ref, h_ref, r_ref):
        xv = x_ref[...]
        r = lax.rsqrt(jnp.mean(xv * xv, axis=-1, keepdims=True) + EPS)
        h_ref[...] = (xv * r * g_ref[...]).astype(_BF)
        r_ref[...] = r

    return pl.pallas_call(
        body, grid=(S // tr,),
        in_specs=[pl.BlockSpec((tr, D), lambda i: (i, 0)), pl.BlockSpec((1, D), lambda i: (0, 0))],
        out_specs=[pl.BlockSpec((tr, D), lambda i: (i, 0)), pl.BlockSpec((tr, 1), lambda i: (i, 0))],
        out_shape=[SDS((S, D), _BF), SDS((S, 1), F32)],
        compiler_params=_cp("parallel"), name="rms_fwd")(x, g)


def _head_loss(x, z, target, g):
    S, D = x.shape
    tr = min(T_ROW, S)

    def body(x_ref, z_ref, t_ref, g_ref, dx2_ref, gg_ref, loss_ref):
        i = pl.program_id(0)
        x2 = x_ref[...] + z_ref[...]
        r = lax.rsqrt(jnp.mean(x2 * x2, axis=-1, keepdims=True) + EPS)
        xh = x2 * r
        gv = g_ref[...]
        diff = xh * gv - t_ref[...]
        lsum = 0.5 * jnp.sum(jnp.mean(diff * diff, axis=-1, keepdims=True), axis=0, keepdims=True)
        dout = diff * (1.0 / D)
        gd = dout * gv
        dx2_ref[...] = r * (gd - xh * jnp.mean(gd * xh, axis=-1, keepdims=True))
        gg = jnp.sum(dout * xh, axis=0, keepdims=True)

        @pl.when(i == 0)
        def _():
            gg_ref[...] = gg
            loss_ref[...] = jnp.broadcast_to(lsum, (1, 128))

        @pl.when(i > 0)
        def _():
            gg_ref[...] += gg
            loss_ref[...] += jnp.broadcast_to(lsum, (1, 128))

    row = pl.BlockSpec((tr, D), lambda i: (i, 0))
    return pl.pallas_call(
        body, grid=(S // tr,), in_specs=[row, row, row, pl.BlockSpec((1, D), lambda i: (0, 0))],
        out_specs=[row, pl.BlockSpec((1, D), lambda i: (0, 0)), pl.BlockSpec((1, 128), lambda i: (0, 0))],
        out_shape=[SDS((S, D), F32), SDS((1, D), F32), SDS((1, 128), F32)],
        compiler_params=_cp("arbitrary"), name="head_loss")(x, z, target, g)


def _rms_bwd(x, r, dh, dx2, g):
    S, D = x.shape
    tr = min(T_ROW, S)

    def body(x_ref, r_ref, dh_ref, dx2_ref, g_ref, gx_ref, gg_ref):
        i = pl.program_id(0)
        rv = r_ref[...]
        xh = x_ref[...] * rv
        dh_ = dh_ref[...]
        gd = dh_ * g_ref[...]
        gx_ref[...] = dx2_ref[...] + rv * (gd - xh * jnp.mean(gd * xh, axis=-1, keepdims=True))
        gg = jnp.sum(dh_ * xh, axis=0, keepdims=True)

        @pl.when(i == 0)
        def _():
            gg_ref[...] = gg

        @pl.when(i > 0)
        def _():
            gg_ref[...] += gg

    row = pl.BlockSpec((tr, D), lambda i: (i, 0))
    return pl.pallas_call(
        body, grid=(S // tr,),
        in_specs=[row, pl.BlockSpec((tr, 1), lambda i: (i, 0)), row, row, pl.BlockSpec((1, D), lambda i: (0, 0))],
        out_specs=[row, pl.BlockSpec((1, D), lambda i: (0, 0))],
        out_shape=[SDS((S, D), F32), SDS((1, D), F32)],
        compiler_params=_cp("arbitrary"), name="rms_bwd")(x, r, dh, dx2, g)


def _mem_kv_fwd(mem, g, w_kv):
    M, D = mem.shape

    def body(mem_ref, g_ref, w_ref, mkv_ref, mhat_ref):
        mv = mem_ref[...]
        mhat = mv * lax.rsqrt(jnp.mean(mv * mv, axis=-1, keepdims=True) + EPS)
        mhat_ref[...] = mhat
        mkv_ref[...] = _dot((mhat * g_ref[...]).astype(_BF), w_ref[...], NN).astype(_BF)

    return pl.pallas_call(
        body, out_shape=[SDS((M, 2 * MEM_W), _BF), SDS((M, D), F32)],
        compiler_params=pltpu.CompilerParams(vmem_limit_bytes=VMEM_LIMIT), name="mem_kv_fwd")(mem, g, w_kv)


def _mem_kv_bwd(dmkv, mhat, g, w_kv):
    M, D = mhat.shape

    def body(d_ref, mhat_ref, g_ref, w_ref, gw_ref, gg_ref):
        d = d_ref[...].astype(_BF)
        mhat = mhat_ref[...]
        gw_ref[...] = _dot((mhat * g_ref[...]).astype(_BF), d, TN).astype(_BF)
        dmh = _dot(d, w_ref[...], NT)
        gg_ref[...] = jnp.sum(dmh * mhat, axis=0, keepdims=True)

    return pl.pallas_call(
        body, out_shape=[SDS((D, 2 * MEM_W), _BF), SDS((1, D), F32)],
        compiler_params=pltpu.CompilerParams(vmem_limit_bytes=VMEM_LIMIT), name="mem_kv_bwd")(dmkv, mhat, g, w_kv)


def _mem_attn_fwd(p_b, p_f, mkv, y):
    S = p_b.shape[0]
    M = mkv.shape[0]
    tq = min(T_ATT, S)

    def body(q_ref, kv_ref, g_ref, yin_ref, y_ref):
        for hd in range(N_MEM):
            cols = slice(hd * HEAD, (hd + 1) * HEAD)
            s = _dot(q_ref[:, cols], kv_ref[:, cols], NT) * SCALE
            m = jnp.max(s, axis=-1, keepdims=True)
            e = jnp.exp(s - m)
            p = e / jnp.sum(e, axis=-1, keepdims=True)
            o = _dot(p.astype(_BF), kv_ref[:, MEM_W + hd * HEAD:MEM_W + (hd + 1) * HEAD], NN)
            gv = g_ref[:, cols]
            y_ref[:, cols] = (o * (gv * _sigmoid(gv))).astype(_BF)

    return pl.pallas_call(
        body, grid=(S // tq,),
        in_specs=[pl.BlockSpec((tq, MEM_W), lambda i: (i, B_MQ // MEM_W)),
                  pl.BlockSpec((M, 2 * MEM_W), lambda i: (0, 0)),
                  pl.BlockSpec((tq, MEM_W), lambda i: (i, F_MG // MEM_W)), pl.BlockSpec(memory_space=pl.ANY)],
        out_specs=pl.BlockSpec((tq, MEM_W), lambda i: (i, (CONV_W + FOX_W) // MEM_W)),
        out_shape=SDS(y.shape, y.dtype), input_output_aliases={3: 0},
        compiler_params=_cp("parallel"), name="mem_attn_fwd")(p_b, mkv, p_f, y)


def _mem_attn_bwd(p_b, p_f, mkv, dy):
    S = p_b.shape[0]
    M = mkv.shape[0]
    tq = min(T_ATT, S)

    def body(q_ref, kv_ref, g_ref, dy_ref, dq_ref, dg_ref, dk_ref, dv_ref):
        i = pl.program_id(0)

        @pl.when(i == 0)
        def _():
            dk_ref[...] = jnp.zeros_like(dk_ref)
            dv_ref[...] = jnp.zeros_like(dv_ref)

        for hd in range(N_MEM):
            cols = slice(hd * HEAD, (hd + 1) * HEAD)
            q, k = q_ref[:, cols], kv_ref[:, cols]
            v = kv_ref[:, MEM_W + hd * HEAD:MEM_W + (hd + 1) * HEAD]
            s = _dot(q, k, NT) * SCALE
            m = jnp.max(s, axis=-1, keepdims=True)
            e = jnp.exp(s - m)
            p = e / jnp.sum(e, axis=-1, keepdims=True)
            pb = p.astype(_BF)
            o = _dot(pb, v, NN)
            gv = g_ref[:, cols]
            sg = _sigmoid(gv)
            dyv = dy_ref[:, cols]
            do = dyv * (gv * sg)
            dg_ref[:, cols] = (dyv * o * _dsilu(gv, sg)).astype(_BF)
            dob = do.astype(_BF)
            dp = _dot(dob, v, NT)
            ds = p * (dp - jnp.sum(dp * p, axis=-1, keepdims=True)) * SCALE
            dsb = ds.astype(_BF)
            dq_ref[:, cols] = _dot(dsb, k, NN).astype(_BF)
            dk_ref[:, cols] += _dot(dsb, q, TN)
            dv_ref[:, cols] += _dot(pb, dob, TN)

    tile = pl.BlockSpec((tq, MEM_W), lambda i: (i, 0))
    kv = pl.BlockSpec((M, MEM_W), lambda i: (0, 0))
    return pl.pallas_call(
        body, grid=(S // tq,),
        in_specs=[pl.BlockSpec((tq, MEM_W), lambda i: (i, B_MQ // MEM_W)),
                  pl.BlockSpec((M, 2 * MEM_W), lambda i: (0, 0)),
                  pl.BlockSpec((tq, MEM_W), lambda i: (i, F_MG // MEM_W)),
                  pl.BlockSpec((tq, MEM_W), lambda i: (i, (CONV_W + FOX_W) // MEM_W))],
        out_specs=[tile, tile, kv, kv],
        out_shape=[SDS((S, MEM_W), _BF), SDS((S, MEM_W), _BF), SDS((M, MEM_W), F32), SDS((M, MEM_W), F32)],
        compiler_params=_cp("arbitrary"), name="mem_attn_bwd")(p_b, mkv, p_f, dy)


def _fox_cumsum(p_f, b_f_pad):
    S = p_f.shape[0]
    tr = min(256, S)
    fb = F_FL // 128

    def body(z_ref, b_ref, col_ref, row_ref, carry_ref):
        i = pl.program_id(0)

        @pl.when(i == 0)
        def _():
            carry_ref[...] = jnp.zeros_like(carry_ref)

        z = z_ref[...] + b_ref[...]
        lf = jnp.minimum(z, 0.0) - jnp.log1p(jnp.exp(-jnp.abs(z)))
        r = lax.broadcasted_iota(jnp.int32, (tr, tr), 0)
        c = lax.broadcasted_iota(jnp.int32, (tr, tr), 1)
        tri = (c <= r).astype(F32)
        cs = jnp.dot(tri, lf, precision=lax.Precision.HIGHEST, preferred_element_type=F32) + carry_ref[...]
        carry_ref[...] = cs[tr - 1:tr, :]
        cst = cs.T
        for hd in range(N_FOX):
            col_ref[hd] = cs[:, hd:hd + 1]
            row_ref[hd] = cst[hd:hd + 1, :]

    return pl.pallas_call(
        body, grid=(S // tr,),
        in_specs=[pl.BlockSpec((tr, 128), lambda i: (i, fb)), pl.BlockSpec((1, 128), lambda i: (0, 0))],
        out_specs=[pl.BlockSpec((N_FOX, tr, 1), lambda i: (0, i, 0)), pl.BlockSpec((N_FOX, 1, tr), lambda i: (0, 0, i))],
        out_shape=[SDS((N_FOX, S, 1), F32), SDS((N_FOX, 1, S), F32)], scratch_shapes=[pltpu.VMEM((1, 128), F32)],
        compiler_params=_cp("arbitrary"), name="fox_cumsum")(p_f, b_f_pad)


def _fox_dlogf(dr, p_f, b_f_pad, dp):
    S = p_f.shape[0]
    tr = min(256, S)
    nb = S // tr
    fb = F_FL // 128
    wide = PART_W - F_FL

    def body(dr_ref, z_ref, b_ref, dp_ref, dz_ref, gb_ref, carry_ref):
        i = pl.program_id(0)

        @pl.when(i == 0)
        def _():
            carry_ref[...] = jnp.zeros_like(carry_ref)
            gb_ref[...] = jnp.zeros_like(gb_ref)

        heads = [dr_ref[hd, 0:1, :] + dr_ref[hd, 1:2, :] for hd in range(N_FOX)]
        dc = jnp.concatenate(heads + [jnp.zeros((128 - N_FOX, tr), F32)], axis=0).T
        r = lax.broadcasted_iota(jnp.int32, (tr, tr), 0)
        c = lax.broadcasted_iota(jnp.int32, (tr, tr), 1)
        tri = (c >= r).astype(F32)
        rc = jnp.dot(tri, dc, precision=lax.Precision.HIGHEST, preferred_element_type=F32) + carry_ref[...]
        carry_ref[...] = rc[0:1, :]
        z = z_ref[...] + b_ref[...]
        dz = rc * _sigmoid(-z)
        gb_ref[...] += jnp.sum(dz, axis=0, keepdims=True)
        dz_ref[...] = jnp.concatenate([dz.astype(_BF), jnp.zeros((tr, wide - 128), _BF)], axis=1)

    return pl.pallas_call(
        body, grid=(nb,),
        in_specs=[pl.BlockSpec((N_FOX, 8, tr), lambda i: (0, 0, nb - 1 - i)),
                  pl.BlockSpec((tr, 128), lambda i: (nb - 1 - i, fb)), pl.BlockSpec((1, 128), lambda i: (0, 0)),
                  pl.BlockSpec(memory_space=pl.ANY)],
        out_specs=[pl.BlockSpec((tr, wide), lambda i: (nb - 1 - i, F_FL // wide)),
                   pl.BlockSpec((1, 128), lambda i: (0, 0))],
        out_shape=[SDS(dp.shape, dp.dtype), SDS((1, 128), F32)], scratch_shapes=[pltpu.VMEM((1, 128), F32)],
        input_output_aliases={3: 0},
        compiler_params=_cp("arbitrary"), name="fox_dlogf")(dr, p_f, b_f_pad, dp)


def _fox_fwd(p_b, p_f, c_col, c_row, y):
    S = p_b.shape[0]
    t = min(2 * T_ATT, S)
    nq = S // t
    rc = min(256, t)
    qb, kb, vb, gb = B_Q // HEAD, B_K // HEAD, B_V // HEAD, F_FG // HEAD
    kq = SCALE * 1.4426950408889634

    def body(q_ref, k_ref, v_ref, cc_ref, cr_ref, g_ref, yin_ref, o_ref, y_ref, lse_ref, va_ref, ua_ref, ub_ref, m_ref,
             acc_ref):
        i = pl.program_id(1)

        @pl.when(i == 0)
        def _():
            va_ref[:, 0:HEAD] = v_ref[...]
            lane = lax.broadcasted_iota(jnp.int32, (S, HEAD), 1)
            va_ref[:, HEAD:2 * HEAD] = jnp.where(lane == 0, 1.0, 0.0).astype(_BF)

        m_ref[...] = jnp.full_like(m_ref, NEG)
        acc_ref[...] = jnp.zeros_like(acc_ref)

        def scores(b, u_ref):
            off = pl.multiple_of(b * t, t)
            k = k_ref[pl.ds(off, t), :]
            csr = cr_ref[:, pl.ds(off, t)] * (1.0 / SCALE)
            for r in range(0, t, rc):
                u_ref[r:r + rc, :] = _dot(q_ref[r:r + rc, :], k, NT) - csr

        def absorb(b, u_ref, masked):
            va = va_ref[pl.ds(pl.multiple_of(b * t, t), t), :]
            for r in range(0, t, rc):
                u = u_ref[r:r + rc, :]
                if masked:
                    row = lax.broadcasted_iota(jnp.int32, (rc, t), 0) + r
                    col = lax.broadcasted_iota(jnp.int32, (rc, t), 1)
                    u = jnp.where(col <= row, u, NEG)
                m_old = m_ref[r:r + rc, :]
                m_new = jnp.maximum(m_old, jnp.max(u, axis=-1, keepdims=True))
                alpha = jnp.exp2((m_old - m_new) * kq)
                p = jnp.exp2((u - m_new) * kq)
                acc_ref[r:r + rc, :] = alpha * acc_ref[r:r + rc, :] + _dot(p.astype(_BF), va, NN)
                m_ref[r:r + rc, :] = m_new

        scores(0, ua_ref)

        def pair(pi, carry):
            b = 2 * pi
            scores(b + 1, ub_ref)
            absorb(b, ua_ref, False)
            scores(b + 2, ua_ref)
            absorb(b + 1, ub_ref, False)
            return carry

        lax.fori_loop(0, i // 2, pair, 0)

        @pl.when(i % 2 == 1)
        def _():
            scores(i, ub_ref)
            absorb(i - 1, ua_ref, False)
            absorb(i, ub_ref, True)

        @pl.when(i % 2 == 0)
        def _():
            absorb(i, ua_ref, True)

        l = acc_ref[:, HEAD:HEAD + 1]
        o = acc_ref[:, 0:HEAD] / l
        gv = g_ref[...]
        o_ref[...] = o.astype(_BF)
        y_ref[...] = (o * (gv * _sigmoid(gv))).astype(_BF)
        lse_ref[...] = cc_ref[...] + SCALE * m_ref[...] + jnp.log(l)

    tile = pl.BlockSpec((t, HEAD), lambda h, i: (i, h))
    return pl.pallas_call(
        body, grid=(N_FOX, nq),
        in_specs=[pl.BlockSpec((t, HEAD), lambda h, i: (i, qb + h)),
                  pl.BlockSpec((S, HEAD), lambda h, i: (0, kb + h)),
                  pl.BlockSpec((S, HEAD), lambda h, i: (0, vb + h)),
                  pl.BlockSpec((None, t, 1), lambda h, i: (h, i, 0)),
                  pl.BlockSpec((None, 1, S), lambda h, i: (h, 0, 0)),
                  pl.BlockSpec((t, HEAD), lambda h, i: (i, gb + h)), pl.BlockSpec(memory_space=pl.ANY)],
        out_specs=[tile, pl.BlockSpec((t, HEAD), lambda h, i: (i, CONV_W // HEAD + h)),
                   pl.BlockSpec((None, t, 1), lambda h, i: (h, i, 0))],
        out_shape=[SDS((S, FOX_W), _BF), SDS(y.shape, y.dtype), SDS((N_FOX, S, 1), F32)],
        input_output_aliases={6: 1},
        scratch_shapes=[pltpu.VMEM((S, 2 * HEAD), _BF), pltpu.VMEM((t, t), F32), pltpu.VMEM((t, t), F32),
                        pltpu.VMEM((t, 1), F32), pltpu.VMEM((t, 2 * HEAD), F32)],
        compiler_params=_cp("parallel", "arbitrary"), name="fox_fwd")(p_b, p_b, p_b, c_col, c_row, p_f, y)


def _fox_bwd_prep(dy, p_f, o, lse, c_col, dp):
    S = dy.shape[0]
    t = min(T_ATT, S)
    hg = 4
    wd = hg * HEAD

    def body(dy_ref, g_ref, o_ref, lse_ref, cc_ref, dp_ref, do_ref, dg_ref, rows_ref):
        gv = g_ref[...]
        sg = _sigmoid(gv)
        dyv = dy_ref[...]
        ov = o_ref[...].astype(F32)
        do = dyv * (gv * sg)
        do_ref[...] = do.astype(_BF)
        dg_ref[...] = (dyv * ov * _dsilu(gv, sg)).astype(_BF)
        prod = do * ov
        lane = lax.broadcasted_iota(jnp.int32, (t, 128), 1)
        for hd in range(hg):
            delta = jnp.sum(prod[:, hd * HEAD:(hd + 1) * HEAD], axis=-1, keepdims=True)
            a = cc_ref[hd] - lse_ref[hd]
            mat = jnp.where(lane == 0, a, jnp.where(lane == 1, delta, 0.0))
            rows_ref[hd] = mat.T[0:8, :]

    tile = pl.BlockSpec((t, wd), lambda g, i: (i, g))
    col = pl.BlockSpec((hg, t, 1), lambda g, i: (g, i, 0))
    return pl.pallas_call(
        body, grid=(N_FOX // hg, S // t),
        in_specs=[pl.BlockSpec((t, wd), lambda g, i: (i, CONV_W // wd + g)),
                  pl.BlockSpec((t, wd), lambda g, i: (i, F_FG // wd + g)), tile, col, col,
                  pl.BlockSpec(memory_space=pl.ANY)],
        out_specs=[tile, pl.BlockSpec((t, wd), lambda g, i: (i, F_FG // wd + g)),
                   pl.BlockSpec((hg, 8, t), lambda g, i: (g, 0, i))],
        out_shape=[SDS((S, FOX_W), _BF), SDS(dp.shape, dp.dtype), SDS((N_FOX, 8, S), F32)],
        input_output_aliases={5: 1},
        compiler_params=_cp("parallel", "parallel"), name="fox_bwd_prep")(dy, p_f, o, lse, c_col, dp)


def _fox_bwd(p_b, do, rows, c_col):
    S = p_b.shape[0]
    t = min(T_ATT, S)
    nk = S // t
    qb, kb, vb = B_Q // HEAD, B_K // HEAD, B_V // HEAD

    def body(k_ref, v_ref, q_ref, do_ref, rows_ref, cc_ref, dq_ref, dk_ref, dv_ref, dr_ref,
             dqt_ref, dka_ref, dva_ref, dca_ref, dra_ref, sa_ref, pa_ref, sb_ref, pb_ref):
        j = pl.program_id(1)

        @pl.when(j == 0)
        def _():
            dqt_ref[...] = jnp.zeros_like(dqt_ref)
            dra_ref[...] = jnp.zeros_like(dra_ref)

        k = k_ref[...]
        v = v_ref[...]
        kt = k.astype(F32).T.astype(_BF)
        cc = cc_ref[...]
        dka_ref[...] = jnp.zeros_like(dka_ref)
        dva_ref[...] = jnp.zeros_like(dva_ref)
        dca_ref[...] = jnp.zeros_like(dca_ref)

        def scores(i, s_ref, p_ref):
            off = pl.multiple_of(i * t, t)
            s_ref[...] = _dot(k, q_ref[pl.ds(off, t), :], NT) * SCALE + (rows_ref[0:1, pl.ds(off, t)] - cc)
            p_ref[...] = _dot(v, do_ref[pl.ds(off, t), :], NT) - rows_ref[1:2, pl.ds(off, t)]

        def absorb(i, s_ref, p_ref, masked):
            off = pl.multiple_of(i * t, t)
            st = s_ref[...]
            if masked:
                srow = lax.broadcasted_iota(jnp.int32, (t, t), 0)
                tcol = lax.broadcasted_iota(jnp.int32, (t, t), 1)
                st = jnp.where(srow <= tcol, st, NEG)
            pt = jnp.exp(st)
            dva_ref[...] += _dot(pt.astype(_BF), do_ref[pl.ds(off, t), :], NN)
            dst = pt * p_ref[...]
            part = dst[:, 0:128]
            for gidx in range(1, t // 128):
                part = part + dst[:, gidx * 128:(gidx + 1) * 128]
            dca_ref[...] += part
            dra_ref[0:1, pl.ds(off, t)] += jnp.sum(dst, axis=0, keepdims=True)
            dsb = dst.astype(_BF)
            dka_ref[...] += _dot(dsb, q_ref[pl.ds(off, t), :], NN)
            dqt_ref[:, pl.ds(off, t)] += _dot(kt, dsb, NN)

        rest = nk - 1 - j
        scores(j, sa_ref, pa_ref)

        @pl.when(rest == 0)
        def _():
            absorb(j, sa_ref, pa_ref, True)

        @pl.when(rest > 0)
        def _():
            scores(j + 1, sb_ref, pb_ref)
            absorb(j, sa_ref, pa_ref, True)

            def pair(pi, carry):
                c = j + 1 + 2 * pi
                scores(c + 1, sa_ref, pa_ref)
                absorb(c, sb_ref, pb_ref, False)
                scores(c + 2, sb_ref, pb_ref)
                absorb(c + 1, sa_ref, pa_ref, False)
                return carry

            lax.fori_loop(0, (rest - 1) // 2, pair, 0)

            @pl.when(rest % 2 == 1)
            def _():
                absorb(nk - 1, sb_ref, pb_ref, False)

            @pl.when(rest % 2 == 0)
            def _():
                scores(nk - 1, sa_ref, pa_ref)
                absorb(nk - 2, sb_ref, pb_ref, False)
                absorb(nk - 1, sa_ref, pa_ref, False)

        dk_ref[...] = (dka_ref[...] * SCALE).astype(_BF)
        dv_ref[...] = dva_ref[...].astype(_BF)
        dra_ref[1:2, pl.ds(pl.multiple_of(j * t, t), t)] = -jnp.sum(dca_ref[...].T, axis=0, keepdims=True)

        @pl.when(j == nk - 1)
        def _():
            dr_ref[...] = dra_ref[...]
            for ci in range(nk):
                dq_ref[ci * t:(ci + 1) * t, :] = (dqt_ref[:, ci * t:(ci + 1) * t].T * SCALE).astype(_BF)

    tile = pl.BlockSpec((t, HEAD), lambda h, j: (j, h))
    return pl.pallas_call(
        body, grid=(N_FOX, nk),
        in_specs=[pl.BlockSpec((t, HEAD), lambda h, j: (j, kb + h)),
                  pl.BlockSpec((t, HEAD), lambda h, j: (j, vb + h)),
                  pl.BlockSpec((S, HEAD), lambda h, j: (0, qb + h)),
                  pl.BlockSpec((S, HEAD), lambda h, j: (0, h)),
                  pl.BlockSpec((None, 8, S), lambda h, j: (h, 0, 0)),
                  pl.BlockSpec((None, t, 1), lambda h, j: (h, j, 0))],
        out_specs=[pl.BlockSpec((S, HEAD), lambda h, j: (0, h)), tile, tile,
                   pl.BlockSpec((None, 8, S), lambda h, j: (h, 0, 0))],
        out_shape=[SDS((S, FOX_W), _BF), SDS((S, FOX_W), _BF), SDS((S, FOX_W), _BF), SDS((N_FOX, 8, S), F32)],
        scratch_shapes=[pltpu.VMEM((HEAD, S), F32), pltpu.VMEM((t, HEAD), F32), pltpu.VMEM((t, HEAD), F32),
                        pltpu.VMEM((t, 128), F32), pltpu.VMEM((8, S), F32)] + [pltpu.VMEM((t, t), F32)] * 4,
        compiler_params=_cp("parallel", "arbitrary"), name="fox_bwd")(p_b, p_b, p_b, do, rows, c_col)


CHUNK = 32


SUBLANES = 8


def _shifted_windows(win):
    n = win.shape[0]
    return [win] + [pltpu.roll(win, n - s, axis=0) for s in range(1, SUBLANES)]


def _tap(rot, f):
    return rot[f % SUBLANES][f - f % SUBLANES:f - f % SUBLANES + CHUNK, :]


def _conv_taps(ext_ref, w_ref, first, out_fn, n_rows):
    def chunk(c, carry):
        r0 = pl.multiple_of(c * CHUNK, CHUNK)
        rot = _shifted_windows(ext_ref[pl.ds(r0, 2 * CHUNK), :])
        acc = jnp.zeros((CHUNK, CONV_W), F32)
        for k in range(CONV_K):
            acc = acc + w_ref[k:k + 1, :] * _tap(rot, first(k))
        out_fn(r0, acc)
        return carry

    lax.fori_loop(0, n_rows // CHUNK, chunk, 0)


def _conv_fwd(p_f, conv_w, conv_b, ln_g, ln_b, w_pw):
    S = p_f.shape[0]
    tc = min(T_CONV, S)
    hb = tc // HALO

    def body(a_ref, b_ref, gc_ref, ap_ref, bp_ref, w_ref, cb_ref, lg_ref, lb_ref, pw_ref, y_ref, u1_ref, ext_ref):
        i = pl.program_id(0)
        prev = ap_ref[...] * _sigmoid(bp_ref[...])
        ext_ref[0:HALO, :] = jnp.where(i > 0, prev, 0.0)
        ext_ref[HALO:HALO + tc, :] = a_ref[...] * _sigmoid(b_ref[...])
        cb = cb_ref[...]

        def put(r0, acc):
            u1_ref[pl.ds(r0, CHUNK), :] = acc + cb

        _conv_taps(ext_ref, w_ref, lambda k: HALO - (CONV_K - 1) + k, put, tc)
        u1 = u1_ref[...]
        mu = jnp.mean(u1, axis=-1, keepdims=True)
        d = u1 - mu
        rstd = lax.rsqrt(jnp.mean(d * d, axis=-1, keepdims=True) + EPS)
        u2 = d * rstd * lg_ref[...] + lb_ref[...]
        u3 = u2 * _sigmoid(u2)
        pw = _dot(u3.astype(_BF), pw_ref[...], NN)
        gc = gc_ref[...]
        y_ref[...] = (pw * (gc * _sigmoid(gc))).astype(_BF)

    blk = lambda cb_: pl.BlockSpec((tc, CONV_W), lambda i: (i, cb_))
    halo = lambda cb_: pl.BlockSpec((HALO, CONV_W), lambda i: (jnp.maximum(i * hb - 1, 0), cb_))
    vec = pl.BlockSpec((1, CONV_W), lambda i: (0, 0))
    return pl.pallas_call(
        body, grid=(S // tc,),
        in_specs=[blk(0), blk(1), blk(2), halo(0), halo(1), pl.BlockSpec((CONV_K, CONV_W), lambda i: (0, 0)),
                  vec, vec, vec, pl.BlockSpec((CONV_W, CONV_W), lambda i: (0, 0))],
        out_specs=[pl.BlockSpec((tc, CONV_W), lambda i: (i, 0)), pl.BlockSpec((tc, CONV_W), lambda i: (i, 0))],
        out_shape=[SDS((S, CONV_W + FOX_W + MEM_W), _BF), SDS((S, CONV_W), F32)],
        scratch_shapes=[pltpu.VMEM((tc + 2 * HALO, CONV_W), F32)],
        compiler_params=_cp("parallel"), name="conv_fwd")(p_f, p_f, p_f, p_f, p_f, conv_w, conv_b, ln_g, ln_b, w_pw)


def _conv_bwd1(u1, p_f, dy, ln_g, ln_b, w_pw):
    S = u1.shape[0]
    tc = min(T_CONV, S)

    def body(u1_ref, gc_ref, dy_ref, lg_ref, lb_ref, pw_ref, du1_ref, dgc_ref, gpw_ref, glg_ref, glb_ref, gcb_ref):
        i = pl.program_id(0)
        u1v = u1_ref[...]
        mu = jnp.mean(u1v, axis=-1, keepdims=True)
        d = u1v - mu
        rstd = lax.rsqrt(jnp.mean(d * d, axis=-1, keepdims=True) + EPS)
        xh = d * rstd
        lg = lg_ref[...]
        u2 = xh * lg + lb_ref[...]
        sg2 = _sigmoid(u2)
        u3b = (u2 * sg2).astype(_BF)
        w = pw_ref[...]
        pw = _dot(u3b, w, NN)
        gc = gc_ref[...]
        sgc = _sigmoid(gc)
        dyv = dy_ref[...]
        dpw = (dyv * (gc * sgc)).astype(_BF)
        dgc_ref[...] = (dyv * pw * _dsilu(gc, sgc)).astype(_BF)
        gpw = _dot(u3b, dpw, TN)
        du2 = _dot(dpw, w, NT) * _dsilu(u2, sg2)
        glg = jnp.sum(du2 * xh, axis=0, keepdims=True)
        glb = jnp.sum(du2, axis=0, keepdims=True)
        dxh = du2 * lg
        du1 = rstd * (dxh - jnp.mean(dxh, axis=-1, keepdims=True) - xh * jnp.mean(dxh * xh, axis=-1, keepdims=True))
        du1_ref[...] = du1
        gcb = jnp.sum(du1, axis=0, keepdims=True)

        @pl.when(i == 0)
        def _():
            gpw_ref[...] = gpw
            glg_ref[...] = glg
            glb_ref[...] = glb
            gcb_ref[...] = gcb

        @pl.when(i > 0)
        def _():
            gpw_ref[...] += gpw
            glg_ref[...] += glg
            glb_ref[...] += glb
            gcb_ref[...] += gcb

    row = pl.BlockSpec((tc, CONV_W), lambda i: (i, 0))
    vec = pl.BlockSpec((1, CONV_W), lambda i: (0, 0))
    sq = pl.BlockSpec((CONV_W, CONV_W), lambda i: (0, 0))
    return pl.pallas_call(
        body, grid=(S // tc,),
        in_specs=[row, pl.BlockSpec((tc, CONV_W), lambda i: (i, F_GC // CONV_W)), row, vec, vec, sq],
        out_specs=[row, pl.BlockSpec((tc, CONV_W), lambda i: (i, F_GC // CONV_W)), sq, vec, vec, vec],
        out_shape=[SDS((S, CONV_W), F32), SDS((S, 2 * PART_W), _BF), SDS((CONV_W, CONV_W), F32),
                   SDS((1, CONV_W), F32), SDS((1, CONV_W), F32), SDS((1, CONV_W), F32)],
        compiler_params=_cp("arbitrary"), name="conv_bwd1")(u1, p_f, dy, ln_g, ln_b, w_pw)


def _conv_bwd2(du1, p_f, conv_w, dp):
    S = du1.shape[0]
    tc = min(T_CONV, S)
    hb = tc // HALO
    nblk = S // tc
    last_halo = S // HALO - 1

    def body(d_ref, dn_ref, a_ref, b_ref, ap_ref, bp_ref, w_ref, dp_ref, dab_ref, gw_ref, ext_ref, dext_ref, du0_ref):
        i = pl.program_id(0)
        av = a_ref[...]
        sb = _sigmoid(b_ref[...])
        prev = ap_ref[...] * _sigmoid(bp_ref[...])
        ext_ref[0:HALO, :] = jnp.where(i > 0, prev, 0.0)
        ext_ref[HALO:HALO + tc, :] = av * sb
        dext_ref[0:tc, :] = d_ref[...]
        dext_ref[tc:tc + HALO, :] = jnp.where(i < nblk - 1, dn_ref[...], 0.0)

        def put(r0, acc):
            du0_ref[pl.ds(r0, CHUNK), :] = acc

        _conv_taps(dext_ref, w_ref, lambda k: CONV_K - 1 - k, put, tc)
        du0 = du0_ref[...]
        dab_ref[:, 0:CONV_W] = (du0 * sb).astype(_BF)
        dab_ref[:, CONV_W:2 * CONV_W] = (du0 * av * sb * (1.0 - sb)).astype(_BF)

        def chunk(c, carry):
            r0 = pl.multiple_of(c * CHUNK, CHUNK)
            rot = _shifted_windows(ext_ref[pl.ds(r0, 2 * CHUNK), :])
            dv = dext_ref[pl.ds(r0, CHUNK), :]
            rows = [jnp.sum(dv * _tap(rot, HALO - (CONV_K - 1) + k), axis=0, keepdims=True) for k in range(CONV_K)]
            rows.append(jnp.zeros((1, CONV_W), F32))
            return carry + jnp.concatenate(rows, axis=0)

        gw = lax.fori_loop(0, tc // CHUNK, chunk, jnp.zeros((CONV_K + 1, CONV_W), F32))

        @pl.when(i == 0)
        def _():
            gw_ref[...] = gw

        @pl.when(i > 0)
        def _():
            gw_ref[...] += gw

    row = pl.BlockSpec((tc, CONV_W), lambda i: (i, 0))
    blk = lambda cb_: pl.BlockSpec((tc, CONV_W), lambda i: (i, cb_))
    halo = lambda cb_: pl.BlockSpec((HALO, CONV_W), lambda i: (jnp.maximum(i * hb - 1, 0), cb_))
    nxt = pl.BlockSpec((HALO, CONV_W), lambda i: (jnp.minimum((i + 1) * hb, last_halo), 0))
    return pl.pallas_call(
        body, grid=(nblk,),
        in_specs=[row, nxt, blk(0), blk(1), halo(0), halo(1), pl.BlockSpec((CONV_K, CONV_W), lambda i: (0, 0)),
                  pl.BlockSpec(memory_space=pl.ANY)],
        out_specs=[pl.BlockSpec((tc, 2 * CONV_W), lambda i: (i, 0)), pl.BlockSpec((CONV_K + 1, CONV_W), lambda i: (0, 0))],
        out_shape=[SDS(dp.shape, dp.dtype), SDS((CONV_K + 1, CONV_W), F32)], input_output_aliases={7: 0},
        scratch_shapes=[pltpu.VMEM((tc + 2 * HALO, CONV_W), F32), pltpu.VMEM((tc + 2 * HALO, CONV_W), F32),
                        pltpu.VMEM((tc, CONV_W), F32)],
        compiler_params=_cp("arbitrary"), name="conv_bwd2")(du1, du1, p_f, p_f, p_f, p_f, conv_w, dp)


def _exchange(srcs, scatter, name):
    n = len(srcs)
    out_shape = [SDS((N_DEV,) + (s.shape[1:] if sc else s.shape), s.dtype) for s, sc in zip(srcs, scatter)]

    def body(*refs):
        src_refs, dst_refs = refs[:n], refs[n:2 * n]
        send_sems, recv_sems, local_sems = refs[2 * n:]
        x, y, c = lax.axis_index("x"), lax.axis_index("y"), lax.axis_index("c")
        me = 4 * x + 2 * y + c
        copies = []
        for a in range(n):
            for f in range(1, N_DEV):
                px = 1 - x if f & 4 else x
                py = 1 - y if f & 2 else y
                pc = 1 - c if f & 1 else c
                peer = 4 * px + 2 * py + pc
                src = src_refs[a].at[peer] if scatter[a] else src_refs[a]
                cp = pltpu.make_async_remote_copy(
                    src_ref=src, dst_ref=dst_refs[a].at[me], send_sem=send_sems.at[a, f - 1],
                    recv_sem=recv_sems.at[a, f - 1], device_id=(px, py, pc), device_id_type=MESH)
                cp.start()
                copies.append(cp)
            own = src_refs[a].at[me] if scatter[a] else src_refs[a]
            lc = pltpu.make_async_copy(own, dst_refs[a].at[me], local_sems.at[a])
            lc.start()
            copies.append(lc)
        for cp in copies:
            cp.wait()

    anyspec = pl.BlockSpec(memory_space=pl.ANY)
    return pl.pallas_call(
        body, in_specs=[anyspec] * n, out_specs=[anyspec] * n, out_shape=out_shape,
        scratch_shapes=[pltpu.SemaphoreType.DMA((n, N_DEV - 1)), pltpu.SemaphoreType.DMA((n, N_DEV - 1)),
                        pltpu.SemaphoreType.DMA((n,))],
        name=name)(*srcs)


def _flip_peer(f, x, y, c):
    return (1 - x if f & 4 else x, 1 - y if f & 2 else y, 1 - c if f & 1 else c)


def _scatter_start(srcs):
    n = len(srcs)
    lands = [lax.empty((N_DEV - 1,) + s.shape[1:], s.dtype) for s in srcs]

    def body(*refs):
        src_refs, land_refs = refs[:n], refs[n:2 * n]
        send_sems, recv_sems = refs[2 * n:3 * n], refs[3 * n:4 * n]
        token = refs[-1]
        x, y, c = lax.axis_index("x"), lax.axis_index("y"), lax.axis_index("c")
        for a in range(n):
            for f in range(1, N_DEV):
                px, py, pc = _flip_peer(f, x, y, c)
                pltpu.make_async_remote_copy(
                    src_ref=src_refs[a].at[4 * px + 2 * py + pc], dst_ref=land_refs[a].at[f - 1],
                    send_sem=send_sems[a], recv_sem=recv_sems[a], device_id=(px, py, pc), device_id_type=MESH).start()
        token[...] = jnp.zeros_like(token)

    hbm = pl.BlockSpec(memory_space=pltpu.HBM)
    sem = pl.BlockSpec(memory_space=pltpu.SEMAPHORE)
    bufs = [pltpu.with_memory_space_constraint(b, pltpu.HBM) for b in list(srcs) + lands]
    out = pl.pallas_call(
        body, name="scatter_start",
        out_shape=(*[pltpu.SemaphoreType.DMA(())] * (2 * n), *[pltpu.HBM(b.shape, b.dtype) for b in bufs],
                   SDS((8, 128), F32)),
        in_specs=[hbm] * (2 * n),
        out_specs=(*[sem] * (2 * n), *[hbm] * (2 * n), pl.BlockSpec(memory_space=pltpu.VMEM)),
        input_output_aliases={i: 2 * n + i for i in range(2 * n)},
        compiler_params=pltpu.CompilerParams(has_side_effects=pltpu.SideEffectType.DATAFLOW_SIDE_EFFECTING))(*bufs)
    return list(out[:2 * n]), list(out[2 * n:3 * n]), list(out[3 * n:4 * n]), out[-1]


def _scatter_wait(sems, srcs, lands, after):
    n = len(srcs)

    def body(*refs):
        src_refs, land_refs = refs[:n], refs[n:2 * n]
        send_sems, recv_sems = refs[2 * n:3 * n], refs[3 * n:4 * n]
        x, y, c = lax.axis_index("x"), lax.axis_index("y"), lax.axis_index("c")
        for a in range(n):
            seven = pltpu.make_async_remote_copy(
                src_ref=src_refs[a].at[pl.ds(0, N_DEV - 1)], dst_ref=land_refs[a], send_sem=send_sems[a],
                recv_sem=recv_sems[a], device_id=(x, y, c), device_id_type=MESH)
            seven.wait_send()
            seven.wait_recv()

    hbm = pl.BlockSpec(memory_space=pltpu.HBM)
    sem = pl.BlockSpec(memory_space=pltpu.SEMAPHORE)
    bufs = list(srcs) + list(lands)
    out = pl.pallas_call(
        body, name="scatter_wait", out_shape=tuple(pltpu.HBM(b.shape, b.dtype) for b in bufs),
        in_specs=[hbm] * (2 * n) + [sem] * (2 * n) + [pl.BlockSpec(memory_space=pl.ANY)],
        out_specs=tuple([hbm] * (2 * n)), input_output_aliases={i: i for i in range(2 * n)},
        compiler_params=pltpu.CompilerParams(has_side_effects=pltpu.SideEffectType.DATAFLOW_SIDE_EFFECTING))(
            *bufs, *sems, after)
    return list(out[:n]), list(out[n:])


def _gather_two_level(srcs, name):
    n = len(srcs)
    out_shape = [SDS((N_DEV,) + s.shape, s.dtype) for s in srcs]

    def body(*refs):
        src_refs, dst_refs = refs[:n], refs[n:2 * n]
        send_sems, recv_sems, local_sems = refs[2 * n:]
        x, y, c = lax.axis_index("x"), lax.axis_index("y"), lax.axis_index("c")
        sibling = (x, y, 1 - c)
        chips = [(1 - x, y), (x, 1 - y), (1 - x, 1 - y)]

        def slot(a, px, py, pc):
            return dst_refs[a].at[4 * px + 2 * py + pc]

        def copy(a, k, block, to, src=None):
            return pltpu.make_async_remote_copy(
                src_ref=slot(a, *block) if src is None else src, dst_ref=slot(a, *block),
                send_sem=send_sems.at[a, k], recv_sem=recv_sems.at[a, k], device_id=to, device_id_type=MESH)

        own, sends = [], []
        for a in range(n):
            mine = pltpu.make_async_copy(src_refs[a], slot(a, x, y, c), local_sems.at[a])
            mine.start()
            own.append(mine)
            first = [copy(a, 1 + j, (x, y, c), (*chip, c), src=src_refs[a]) for j, chip in enumerate(chips)]
            first.append(copy(a, 0, (x, y, c), sibling, src=src_refs[a]))
            for cp in first:
                cp.start()
            sends += first
        for a in range(n):
            for j, chip in enumerate(chips):
                copy(a, 1 + j, (*chip, c), (x, y, c)).wait_recv()
                fwd = copy(a, 4 + j, (*chip, c), sibling)
                fwd.start()
                sends.append(fwd)
        for a in range(n):
            copy(a, 0, (x, y, 1 - c), (x, y, c)).wait_recv()
            for j, chip in enumerate(chips):
                copy(a, 4 + j, (*chip, 1 - c), (x, y, c)).wait_recv()
        for cp in sends:
            cp.wait_send()
        for cp in own:
            cp.wait()

    anyspec = pl.BlockSpec(memory_space=pl.ANY)
    return pl.pallas_call(
        body, in_specs=[anyspec] * n, out_specs=[anyspec] * n, out_shape=out_shape,
        scratch_shapes=[pltpu.SemaphoreType.DMA((n, N_DEV - 1)), pltpu.SemaphoreType.DMA((n, N_DEV - 1)),
                        pltpu.SemaphoreType.DMA((n,))],
        name=name)(*srcs)


def _adamw(parts, w, m, v, name, tr=256, own=None):
    lead = w.ndim == 3
    R, C = w.shape[-2:]
    tr = tr if R % tr == 0 else R
    n_parts = parts.shape[0]
    first = [] if own is None else [own]

    def body(*refs):
        p_ref, w_ref, m_ref, v_ref, g_ref, d_ref, nm_ref, nv_ref = refs[len(first):]
        terms = [r[0] for r in refs[:len(first)]] + [p_ref[dev] for dev in range(n_parts)]
        g = terms[0].astype(F32)
        for term in terms[1:]:
            g = g + term.astype(F32)
        mn = ADAM_B1 * m_ref[...] + (1.0 - ADAM_B1) * g
        vn = ADAM_B2 * v_ref[...] + (1.0 - ADAM_B2) * (g * g)
        m_hat = mn / (1.0 - ADAM_B1 ** ADAM_STEP)
        v_hat = vn / (1.0 - ADAM_B2 ** ADAM_STEP)
        g_ref[...] = g
        d_ref[...] = -ADAM_LR * (m_hat / (jnp.sqrt(v_hat) + ADAM_EPS) + ADAM_WD * w_ref[...])
        nm_ref[...] = mn
        nv_ref[...] = vn

    blk = pl.BlockSpec((None, tr, C), lambda i: (0, i, 0)) if lead else pl.BlockSpec((tr, C), lambda i: (i, 0))
    return pl.pallas_call(
        body, grid=(R // tr,),
        in_specs=[pl.BlockSpec((1, tr, C), lambda i: (0, i, 0))] * len(first)
        + [pl.BlockSpec((n_parts, tr, C), lambda i: (0, i, 0)), blk, blk, blk],
        out_specs=[blk] * 4, out_shape=[SDS(w.shape, F32)] * 4,
        compiler_params=_cp("parallel"), name=name)(*first, parts, w, m, v)


def _adamw_transposed(parts, own, w_t, m_t, v_t, name, tr=256):
    n_parts, R, C = parts.shape
    tr = tr if R % tr == 0 else R

    def body(o_ref, p_ref, w_ref, m_ref, v_ref, g_ref, d_ref, nm_ref, nv_ref):
        r = lax.broadcasted_iota(jnp.int32, (tr, tr), 0)
        c = lax.broadcasted_iota(jnp.int32, (tr, tr), 1)
        eye = (r == c).astype(parts.dtype)
        g = _dot(o_ref[0], eye, TN)
        for dev in range(n_parts):
            g = g + _dot(p_ref[dev], eye, TN)
        mn = ADAM_B1 * m_ref[...] + (1.0 - ADAM_B1) * g
        vn = ADAM_B2 * v_ref[...] + (1.0 - ADAM_B2) * (g * g)
        m_hat = mn / (1.0 - ADAM_B1 ** ADAM_STEP)
        v_hat = vn / (1.0 - ADAM_B2 ** ADAM_STEP)
        g_ref[...] = g
        d_ref[...] = -ADAM_LR * (m_hat / (jnp.sqrt(v_hat) + ADAM_EPS) + ADAM_WD * w_ref[...])
        nm_ref[...] = mn
        nv_ref[...] = vn

    blk = pl.BlockSpec((C, tr), lambda i: (0, i))
    return pl.pallas_call(
        body, grid=(R // tr,),
        in_specs=[pl.BlockSpec((1, tr, C), lambda i: (0, i, 0)), pl.BlockSpec((n_parts, tr, C), lambda i: (0, i, 0)),
                  blk, blk, blk],
        out_specs=[blk] * 4, out_shape=[SDS((C, R), F32)] * 4,
        compiler_params=_cp("parallel"), name=name)(own, parts, w_t, m_t, v_t)


_SEGMENTS = ((O_A, O_Q, F_A), (O_Q, O_FL, PART_W + B_Q), (O_FL, O_FG, F_FL), (O_FG, O_MQ, F_FG),
             (O_MQ, O_MG, PART_W + B_MQ), (O_MG, D_IN, F_MG))


def _shards_to_aligned(shards):
    n, D, cols = shards.shape
    pieces = []
    for o0, o1, a0 in sorted(_SEGMENTS, key=lambda s: s[2]):
        for d in range(o0 // cols, (o1 - 1) // cols + 1):
            lo, hi = max(o0, d * cols), min(o1, (d + 1) * cols)
            pieces.append(shards[d, :, lo - d * cols:hi - d * cols])
        if a0 == F_FL:
            pieces.append(jnp.zeros((D, PART_W - F_FL - (o1 - o0)), shards.dtype))
    return jnp.concatenate(pieces, axis=1)


def _aligned_to_shards(g_all, cols):
    shards = []
    for d in range(N_DEV):
        pieces = []
        for o0, o1, a0 in _SEGMENTS:
            lo, hi = max(o0, d * cols), min(o1, (d + 1) * cols)
            if lo < hi:
                pieces.append(g_all[:, a0 + lo - o0:a0 + hi - o0])
        shards.append(jnp.concatenate(pieces, axis=1))
    return jnp.stack(shards)


def _pack_small(norm, mem_norm, final, conv_b, ln_g, ln_b, b_f, extra, D):
    width = max(D, PACK_W)
    row3 = jnp.concatenate([conv_b.reshape(-1), ln_g.reshape(-1), ln_b.reshape(-1), b_f.reshape(-1)])
    rows = [norm.reshape(-1), mem_norm.reshape(-1), final.reshape(-1), row3, extra.reshape(-1)]
    rows = [jnp.pad(r, (0, width - r.shape[0])) for r in rows]
    return jnp.concatenate([jnp.stack(rows), jnp.zeros((3, width), F32)], axis=0)


def _unpack_small(p, D):
    c = CONV_W
    return dict(norm_g=p[0:1, :D], mem_norm_g=p[1:2, :D], final_g=p[2, :D], conv_b=p[3:4, 0:c],
                conv_ln_g=p[3:4, c:2 * c], conv_ln_b=p[3:4, 2 * c:3 * c], b_f=p[3:4, 3 * c:3 * c + N_FOX])


def kernel(x, mem, norm_g, mem_norm_g, w_in, b_f, conv_w, conv_b, conv_ln_g, conv_ln_b, w_conv_pw, w_mem_kv, w_out, final_g, loss_target, m_norm_g, m_mem_norm_g, m_w_in, m_b_f, m_conv_w, m_conv_b, m_conv_ln_g, m_conv_ln_b, m_w_conv_pw, m_w_mem_kv, m_w_out, m_final_g, v_norm_g, v_mem_norm_g, v_w_in, v_b_f, v_conv_w, v_conv_b, v_conv_ln_g, v_conv_ln_b, v_w_conv_pw, v_w_mem_kv, v_w_out, v_final_g):
    S, D = x.shape[1], x.shape[2]
    M = mem.shape[1]
    xs, ms, tgt = x[0], mem[0], loss_target[0]
    cols = w_in.shape[2]

    g_in, g_cw, g_pw, g_kv, g_out = _gather_two_level(
        [w_in[0].astype(_BF), conv_w[0], w_conv_pw[0].astype(_BF), w_mem_kv[0].astype(_BF), w_out[0].astype(_BF)],
        "gather_weights")
    w_all = _shards_to_aligned(g_in)
    conv_w_full = jnp.transpose(g_cw, (1, 0, 2)).reshape(CONV_K, CONV_W)
    w_pw_full = g_pw.reshape(CONV_W, CONV_W)
    w_kv_full = g_kv.reshape(D, 2 * MEM_W)
    w_out_full = g_out.reshape(CONV_W + FOX_W + MEM_W, D)
    b_f_pad = jnp.pad(b_f, ((0, 0), (0, 128 - N_FOX)))

    h, r1 = _rms_fwd(xs, norm_g)
    p_f = _matmul(h, w_all, "nn", F32, "proj_f", b_cols=(0, PART_W))
    p_b = _matmul(h, w_all, "nn", _BF, "proj_b", b_cols=(PART_W, PART_W))
    mkv, mhat = _mem_kv_fwd(ms, mem_norm_g, w_kv_full)
    y, u1 = _conv_fwd(p_f, conv_w_full, conv_b, conv_ln_g, conv_ln_b, w_pw_full)
    c_col, c_row = _fox_cumsum(p_f, b_f_pad)
    o_fox, y, lse = _fox_fwd(p_b, p_f, c_col, c_row, y)
    y = _mem_attn_fwd(p_b, p_f, mkv, y)
    z = _matmul(y, w_out_full, "nn", F32, "out_proj")
    dx2, g_final, loss_part = _head_loss(xs, z, tgt, final_g.reshape(1, D))

    dy = _matmul(dx2, w_out_full, "nt", F32, "d_y")
    gw_out = _matmul(y, dx2, "tn", _BF, "gw_out")
    d_mq, d_mg, d_mk, d_mv = _mem_attn_bwd(p_b, p_f, mkv, dy)
    gw_kv, g_mem_norm = _mem_kv_bwd(jnp.concatenate([d_mk, d_mv], axis=1), mhat, mem_norm_g, w_kv_full)
    du1, dp, gw_pw, g_ln_g, g_ln_b, g_cb = _conv_bwd1(u1, p_f, dy, conv_ln_g, conv_ln_b, w_pw_full)
    dp, gw_cw = _conv_bwd2(du1, p_f, conv_w_full, dp)
    d_o, dp, rows = _fox_bwd_prep(dy, p_f, o_fox, lse, c_col, dp)
    d_q, d_k, d_v, dr = _fox_bwd(p_b, d_o, rows, c_col)
    dp, g_bf = _fox_dlogf(dr, p_f, b_f_pad, dp)
    for piece, col in ((d_mg, F_MG), (d_q, PART_W + B_Q), (d_k, PART_W + B_K), (d_v, PART_W + B_V),
                       (d_mq, PART_W + B_MQ)):
        dp = lax.dynamic_update_slice(dp, piece, (0, col))
    gw_all = _matmul(h, dp, "tn", _BF, "gw_in")

    send_in = _aligned_to_shards(gw_all, cols)
    send_cw = jnp.transpose(gw_cw[:CONV_K].reshape(CONV_K, N_DEV, CONV_W // N_DEV), (1, 0, 2))
    send_pw = gw_pw.astype(_BF).reshape(N_DEV, CONV_W // N_DEV, CONV_W)
    send_kv = gw_kv.reshape(N_DEV, D // N_DEV, 2 * MEM_W)
    send_out = gw_out.reshape(N_DEV, (CONV_W + FOX_W + MEM_W) // N_DEV, D)
    sems, sent, lands, token = _scatter_start([send_in, send_cw, send_pw, send_kv, send_out])
    dh = _matmul(dp, w_all, "nt", F32, "d_h", after=token)
    grad_x, g_norm = _rms_bwd(xs, r1, dh, dx2, norm_g)
    sent, lands = _scatter_wait(sems, sent, lands, grad_x)
    me = 4 * lax.axis_index("x") + 2 * lax.axis_index("y") + lax.axis_index("c")
    own = [lax.dynamic_index_in_dim(s_, me, 0, keepdims=True) for s_ in sent]

    small = _pack_small(g_norm, g_mem_norm, g_final, g_cb, g_ln_g, g_ln_b, g_bf[:, :N_FOX], loss_part[0, 0:1], D)
    r_small, = _exchange([small], [False], "exchange_small")

    res = {}
    w_in_t = _adamw_transposed(lands[0], own[0], jnp.transpose(w_in[0]), jnp.transpose(m_w_in[0]),
                               jnp.transpose(v_w_in[0]), "adamw_w_in")
    res["w_in"] = [jnp.transpose(a_)[None] for a_ in w_in_t]
    res["conv_w"] = _adamw(lands[1], conv_w, m_conv_w, v_conv_w, "adamw_conv_w", own=own[1])
    res["w_conv_pw"] = _adamw(lands[2], w_conv_pw, m_w_conv_pw, v_w_conv_pw, "adamw_w_pw", own=own[2])
    res["w_mem_kv"] = _adamw(lands[3], w_mem_kv, m_w_mem_kv, v_w_mem_kv, "adamw_w_kv", own=own[3])
    res["w_out"] = _adamw(lands[4], w_out, m_w_out, v_w_out, "adamw_w_out", own=own[4])
    zero = jnp.zeros((1,), F32)
    pk = lambda a, b_, c_, d_, e, f_, g_: _pack_small(a, b_, c_, d_, e, f_, g_, zero, D)
    sm = _adamw(r_small,
                pk(norm_g, mem_norm_g, final_g, conv_b, conv_ln_g, conv_ln_b, b_f),
                pk(m_norm_g, m_mem_norm_g, m_final_g, m_conv_b, m_conv_ln_g, m_conv_ln_b, m_b_f),
                pk(v_norm_g, v_mem_norm_g, v_final_g, v_conv_b, v_conv_ln_g, v_conv_ln_b, v_b_f), "adamw_small")
    loss = sm[0][4, 0]
    small_out = [_unpack_small(a, D) for a in sm]
    names = ["norm_g", "mem_norm_g", "w_in", "b_f", "conv_w", "conv_b", "conv_ln_g", "conv_ln_b", "w_conv_pw",
             "w_mem_kv", "w_out", "final_g"]
    outs = [loss, grad_x[None]]
    for kind in range(4):
        for nme in names:
            outs.append(res[nme][kind] if nme in res else small_out[kind][nme])
    return tuple(outs)
```

```python
import functools

import jax
import jax.numpy as jnp
from jax import lax
from jax.experimental import pallas as pl
from jax.experimental.pallas import tpu as pltpu

F32 = jnp.float32
_BF = jnp.bfloat16
SDS = jax.ShapeDtypeStruct
MESH = pl.DeviceIdType.MESH

N_DEV = 8
HEAD = 128
N_FOX = 8
N_MEM = 4
CONV_W = 512
CONV_K = 31
FOX_W = N_FOX * HEAD
MEM_W = N_MEM * HEAD
D_IN = 3 * CONV_W + 4 * FOX_W + N_FOX + 2 * MEM_W
EPS = 1e-6
SCALE = HEAD ** -0.5
NEG = -1e30

ADAM_LR = 0.001
ADAM_B1 = 0.9
ADAM_B2 = 0.999
ADAM_EPS = 1e-08
ADAM_WD = 0.01
ADAM_STEP = 10

PART_W = 3584
F_A, F_B, F_GC, F_FG, F_MG, F_FL = 0, 512, 1024, 1536, 2560, 3072
B_Q, B_K, B_V, B_MQ = 0, 1024, 2048, 3072
O_A, O_B, O_GC = 0, 512, 1024
O_Q, O_K, O_V = 1536, 2560, 3584
O_FL, O_FG, O_MQ, O_MG = 4608, 4616, 5640, 6152

HALO = 32
T_ATT = 512
T_ROW = 256
T_CONV = 512
VMEM_LIMIT = 56 * 1024 * 1024
PACK_W = 2048


def _cp(*sem):
    return pltpu.CompilerParams(dimension_semantics=sem, vmem_limit_bytes=VMEM_LIMIT)


def _sigmoid(x):
    return jax.nn.sigmoid(x)


def _dsilu(x, sg):
    return sg * (1.0 + x * (1.0 - sg))


def _dot(a, b, dims):
    return lax.dot_general(a, b, (dims, ((), ())), preferred_element_type=F32)


NN = ((1,), (0,))
NT = ((1,), (1,))
TN = ((0,), (0,))


def _pick(n, pref):
    if n <= pref:
        return n
    t = pref - pref % 128
    while n % t:
        t -= 128
    return t


def _matmul(a, b, mode, out_dtype, name, tm=512, tn=1024, tk=2048, after=None, b_cols=None):
    col0 = 0
    if mode == "nn":
        (M, K), (K2, N) = a.shape, b.shape
        if b_cols is not None:
            col0, N = b_cols
    elif mode == "nt":
        (M, K), (N, K2) = a.shape, b.shape
    else:
        (K, M), (K2, N) = a.shape, b.shape
    assert K == K2
    tn, tk = _pick(N, tn), _pick(K, tk)
    nk = K // tk
    tm = _pick(M, tm if nk == 1 else 2 * tm)
    assert M % tm == 0 and N % tn == 0 and K % tk == 0, (name, M, N, K)
    dims = {"nn": NN, "nt": NT, "tn": TN}[mode]
    n_in = 2 if after is None else 3

    def body(*refs):
        a_ref, b_ref = refs[0], refs[1]
        o_ref = refs[n_in]
        p = _dot(a_ref[...].astype(_BF), b_ref[...].astype(_BF), dims)
        if nk == 1:
            o_ref[...] = p.astype(out_dtype)
            return
        acc_ref = refs[n_in + 1]
        k = pl.program_id(2)

        @pl.when(k == 0)
        def _():
            acc_ref[...] = p

        @pl.when(jnp.logical_and(k > 0, k < nk - 1))
        def _():
            acc_ref[...] += p

        @pl.when(k == nk - 1)
        def _():
            o_ref[...] = (acc_ref[...] + p).astype(out_dtype)

    if mode == "nn":
        assert col0 % tn == 0
        jb = col0 // tn
        a_spec = pl.BlockSpec((tm, tk), lambda j, i, k: (i, k))
        b_spec = pl.BlockSpec((tk, tn), lambda j, i, k: (k, j + jb))
    elif mode == "nt":
        a_spec = pl.BlockSpec((tm, tk), lambda j, i, k: (i, k))
        b_spec = pl.BlockSpec((tn, tk), lambda j, i, k: (j, k))
    else:
        a_spec = pl.BlockSpec((tk, tm), lambda j, i, k: (k, i))
        b_spec = pl.BlockSpec((tk, tn), lambda j, i, k: (k, j))
    in_specs, args = [a_spec, b_spec], [a, b]
    if after is not None:
        in_specs.append(pl.BlockSpec(memory_space=pl.ANY))
        args.append(after)
    return pl.pallas_call(
        body, grid=(N // tn, M // tm, nk), in_specs=in_specs,
        out_specs=pl.BlockSpec((tm, tn), lambda j, i, k: (i, j)),
        out_shape=SDS((M, N), out_dtype), scratch_shapes=[] if nk == 1 else [pltpu.VMEM((tm, tn), F32)],
        compiler_params=_cp("parallel", "parallel", "arbitrary"), name=name)(*args)


def _rms_fwd(x, g):
    S, D = x.shape
    tr = min(T_ROW, S)

    def body(x_ref, g_ref, h_ref, r_ref):
        xv = x_ref[...]
        r = lax.rsqrt(jnp.mean(xv * xv, axis=-1, keepdims=True) + EPS)
        h_ref[...] = (xv * r * g_ref[...]).astype(_BF)
        r_ref[...] = r

    return pl.pallas_call(
        body, grid=(S // tr,),
        in_specs=[pl.BlockSpec((tr, D), lambda i: (i, 0)), pl.BlockSpec((1, D), lambda i: (0, 0))],
        out_specs=[pl.BlockSpec((tr, D), lambda i: (i, 0)), pl.BlockSpec((tr, 1), lambda i: (i, 0))],
        out_shape=[SDS((S, D), _BF), SDS((S, 1), F32)],
        compiler_params=_cp("parallel"), name="rms_fwd")(x, g)


def _head_loss(x, z, target, g):
    S, D = x.shape
    tr = min(T_ROW, S)

    def body(x_ref, z_ref, t_ref, g_ref, dx2_ref, gg_ref, loss_ref):
        i = pl.program_id(0)
        x2 = x_ref[...] + z_ref[...]
        r = lax.rsqrt(jnp.mean(x2 * x2, axis=-1, keepdims=True) + EPS)
        xh = x2 * r
        gv = g_ref[...]
        diff = xh * gv - t_ref[...]
        lsum = 0.5 * jnp.sum(jnp.mean(diff * diff, axis=-1, keepdims=True), axis=0, keepdims=True)
        dout = diff * (1.0 / D)
        gd = dout * gv
        dx2_ref[...] = r * (gd - xh * jnp.mean(gd * xh, axis=-1, keepdims=True))
        gg = jnp.sum(dout * xh, axis=0, keepdims=True)

        @pl.when(i == 0)
        def _():
            gg_ref[...] = gg
            loss_ref[...] = jnp.broadcast_to(lsum, (1, 128))

        @pl.when(i > 0)
        def _():
            gg_ref[...] += gg
            loss_ref[...] += jnp.broadcast_to(lsum, (1, 128))

    row = pl.BlockSpec((tr, D), lambda i: (i, 0))
    return pl.pallas_call(
        body, grid=(S // tr,), in_specs=[row, row, row, pl.BlockSpec((1, D), lambda i: (0, 0))],
        out_specs=[row, pl.BlockSpec((1, D), lambda i: (0, 0)), pl.BlockSpec((1, 128), lambda i: (0, 0))],
        out_shape=[SDS((S, D), F32), SDS((1, D), F32), SDS((1, 128), F32)],
        compiler_params=_cp("arbitrary"), name="head_loss")(x, z, target, g)


def _rms_bwd(x, r, dh, dx2, g):
    S, D = x.shape
    tr = min(T_ROW, S)

    def body(x_ref, r_ref, dh_ref, dx2_ref, g_ref, gx_ref, gg_ref):
        i = pl.program_id(0)
        rv = r_ref[...]
        xh = x_ref[...] * rv
        dh_ = dh_ref[...]
        gd = dh_ * g_ref[...]
        gx_ref[...] = dx2_ref[...] + rv * (gd - xh * jnp.mean(gd * xh, axis=-1, keepdims=True))
        gg = jnp.sum(dh_ * xh, axis=0, keepdims=True)

        @pl.when(i == 0)
        def _():
            gg_ref[...] = gg

        @pl.when(i > 0)
        def _():
            gg_ref[...] += gg

    row = pl.BlockSpec((tr, D), lambda i: (i, 0))
    return pl.pallas_call(
        body, grid=(S // tr,),
        in_specs=[row, pl.BlockSpec((tr, 1), lambda i: (i, 0)), row, row, pl.BlockSpec((1, D), lambda i: (0, 0))],
        out_specs=[row, pl.BlockSpec((1, D), lambda i: (0, 0))],
        out_shape=[SDS((S, D), F32), SDS((1, D), F32)],
        compiler_params=_cp("arbitrary"), name="rms_bwd")(x, r, dh, dx2, g)


def _mem_kv_fwd(mem, g, w_kv):
    M, D = mem.shape

    def body(mem_ref, g_ref, w_ref, mkv_ref, mhat_ref):
        mv = mem_ref[...]
        mhat = mv * lax.rsqrt(jnp.mean(mv * mv, axis=-1, keepdims=True) + EPS)
        mhat_ref[...] = mhat
        mkv_ref[...] = _dot((mhat * g_ref[...]).astype(_BF), w_ref[...], NN).astype(_BF)

    return pl.pallas_call(
        body, out_shape=[SDS((M, 2 * MEM_W), _BF), SDS((M, D), F32)],
        compiler_params=pltpu.CompilerParams(vmem_limit_bytes=VMEM_LIMIT), name="mem_kv_fwd")(mem, g, w_kv)


def _mem_kv_bwd(dmkv, mhat, g, w_kv):
    M, D = mhat.shape

    def body(d_ref, mhat_ref, g_ref, w_ref, gw_ref, gg_ref):
        d = d_ref[...].astype(_BF)
        mhat = mhat_ref[...]
        gw_ref[...] = _dot((mhat * g_ref[...]).astype(_BF), d, TN).astype(_BF)
        dmh = _dot(d, w_ref[...], NT)
        gg_ref[...] = jnp.sum(dmh * mhat, axis=0, keepdims=True)

    return pl.pallas_call(
        body, out_shape=[SDS((D, 2 * MEM_W), _BF), SDS((1, D), F32)],
        compiler_params=pltpu.CompilerParams(vmem_limit_bytes=VMEM_LIMIT), name="mem_kv_bwd")(dmkv, mhat, g, w_kv)


def _mem_attn_fwd(p_b, p_f, mkv, y):
    S = p_b.shape[0]
    M = mkv.shape[0]
    tq = min(T_ATT, S)

    def body(q_ref, kv_ref, g_ref, yin_ref, y_ref):
        for hd in range(N_MEM):
            cols = slice(hd * HEAD, (hd + 1) * HEAD)
            s = _dot(q_ref[:, cols], kv_ref[:, cols], NT) * SCALE
            m = jnp.max(s, axis=-1, keepdims=True)
            e = jnp.exp(s - m)
            p = e / jnp.sum(e, axis=-1, keepdims=True)
            o = _dot(p.astype(_BF), kv_ref[:, MEM_W + hd * HEAD:MEM_W + (hd + 1) * HEAD], NN)
            gv = g_ref[:, cols]
            y_ref[:, cols] = (o * (gv * _sigmoid(gv))).astype(_BF)

    return pl.pallas_call(
        body, grid=(S // tq,),
        in_specs=[pl.BlockSpec((tq, MEM_W), lambda i: (i, B_MQ // MEM_W)),
                  pl.BlockSpec((M, 2 * MEM_W), lambda i: (0, 0)),
                  pl.BlockSpec((tq, MEM_W), lambda i: (i, F_MG // MEM_W)), pl.BlockSpec(memory_space=pl.ANY)],
        out_specs=pl.BlockSpec((tq, MEM_W), lambda i: (i, (CONV_W + FOX_W) // MEM_W)),
        out_shape=SDS(y.shape, y.dtype), input_output_aliases={3: 0},
        compiler_params=_cp("parallel"), name="mem_attn_fwd")(p_b, mkv, p_f, y)


def _mem_attn_bwd(p_b, p_f, mkv, dy):
    S = p_b.shape[0]
    M = mkv.shape[0]
    tq = min(T_ATT, S)

    def body(q_ref, kv_ref, g_ref, dy_ref, dq_ref, dg_ref, dk_ref, dv_ref):
        i = pl.program_id(0)

        @pl.when(i == 0)
        def _():
            dk_ref[...] = jnp.zeros_like(dk_ref)
            dv_ref[...] = jnp.zeros_like(dv_ref)

        for hd in range(N_MEM):
            cols = slice(hd * HEAD, (hd + 1) * HEAD)
            q, k = q_ref[:, cols], kv_ref[:, cols]
            v = kv_ref[:, MEM_W + hd * HEAD:MEM_W + (hd + 1) * HEAD]
            s = _dot(q, k, NT) * SCALE
            m = jnp.max(s, axis=-1, keepdims=True)
            e = jnp.exp(s - m)
            p = e / jnp.sum(e, axis=-1, keepdims=True)
            pb = p.astype(_BF)
            o = _dot(pb, v, NN)
            gv = g_ref[:, cols]
            sg = _sigmoid(gv)
            dyv = dy_ref[:, cols]
            do = dyv * (gv * sg)
            dg_ref[:, cols] = (dyv * o * _dsilu(gv, sg)).astype(_BF)
            dob = do.astype(_BF)
            dp = _dot(dob, v, NT)
            ds = p * (dp - jnp.sum(dp * p, axis=-1, keepdims=True)) * SCALE
            dsb = ds.astype(_BF)
            dq_ref[:, cols] = _dot(dsb, k, NN).astype(_BF)
            dk_ref[:, cols] += _dot(dsb, q, TN)
            dv_ref[:, cols] += _dot(pb, dob, TN)

    tile = pl.BlockSpec((tq, MEM_W), lambda i: (i, 0))
    kv = pl.BlockSpec((M, MEM_W), lambda i: (0, 0))
    return pl.pallas_call(
        body, grid=(S // tq,),
        in_specs=[pl.BlockSpec((tq, MEM_W), lambda i: (i, B_MQ // MEM_W)),
                  pl.BlockSpec((M, 2 * MEM_W), lambda i: (0, 0)),
                  pl.BlockSpec((tq, MEM_W), lambda i: (i, F_MG // MEM_W)),
                  pl.BlockSpec((tq, MEM_W), lambda i: (i, (CONV_W + FOX_W) // MEM_W))],
        out_specs=[tile, tile, kv, kv],
        out_shape=[SDS((S, MEM_W), _BF), SDS((S, MEM_W), _BF), SDS((M, MEM_W), F32), SDS((M, MEM_W), F32)],
        compiler_params=_cp("arbitrary"), name="mem_attn_bwd")(p_b, mkv, p_f, dy)


def _fox_cumsum(p_f, b_f_pad):
    S = p_f.shape[0]
    tr = min(256, S)
    fb = F_FL // 128

    def body(z_ref, b_ref, col_ref, row_ref, carry_ref):
        i = pl.program_id(0)

        @pl.when(i == 0)
        def _():
            carry_ref[...] = jnp.zeros_like(carry_ref)

        z = z_ref[...] + b_ref[...]
        lf = jnp.minimum(z, 0.0) - jnp.log1p(jnp.exp(-jnp.abs(z)))
        r = lax.broadcasted_iota(jnp.int32, (tr, tr), 0)
        c = lax.broadcasted_iota(jnp.int32, (tr, tr), 1)
        tri = (c <= r).astype(F32)
        cs = jnp.dot(tri, lf, precision=lax.Precision.HIGHEST, preferred_element_type=F32) + carry_ref[...]
        carry_ref[...] = cs[tr - 1:tr, :]
        cst = cs.T
        for hd in range(N_FOX):
            col_ref[hd] = cs[:, hd:hd + 1]
            row_ref[hd] = cst[hd:hd + 1, :]

    return pl.pallas_call(
        body, grid=(S // tr,),
        in_specs=[pl.BlockSpec((tr, 128), lambda i: (i, fb)), pl.BlockSpec((1, 128), lambda i: (0, 0))],
        out_specs=[pl.BlockSpec((N_FOX, tr, 1), lambda i: (0, i, 0)), pl.BlockSpec((N_FOX, 1, tr), lambda i: (0, 0, i))],
        out_shape=[SDS((N_FOX, S, 1), F32), SDS((N_FOX, 1, S), F32)], scratch_shapes=[pltpu.VMEM((1, 128), F32)],
        compiler_params=_cp("arbitrary"), name="fox_cumsum")(p_f, b_f_pad)


def _fox_dlogf(dr, p_f, b_f_pad, dp):
    S = p_f.shape[0]
    tr = min(256, S)
    nb = S // tr
    fb = F_FL // 128
    wide = PART_W - F_FL

    def body(dr_ref, z_ref, b_ref, dp_ref, dz_ref, gb_ref, carry_ref):
        i = pl.program_id(0)

        @pl.when(i == 0)
        def _():
            carry_ref[...] = jnp.zeros_like(carry_ref)
            gb_ref[...] = jnp.zeros_like(gb_ref)

        heads = [dr_ref[hd, 0:1, :] + dr_ref[hd, 1:2, :] for hd in range(N_FOX)]
        dc = jnp.concatenate(heads + [jnp.zeros((128 - N_FOX, tr), F32)], axis=0).T
        r = lax.broadcasted_iota(jnp.int32, (tr, tr), 0)
        c = lax.broadcasted_iota(jnp.int32, (tr, tr), 1)
        tri = (c >= r).astype(F32)
        rc = jnp.dot(tri, dc, precision=lax.Precision.HIGHEST, preferred_element_type=F32) + carry_ref[...]
        carry_ref[...] = rc[0:1, :]
        z = z_ref[...] + b_ref[...]
        dz = rc * _sigmoid(-z)
        gb_ref[...] += jnp.sum(dz, axis=0, keepdims=True)
        dz_ref[...] = jnp.concatenate([dz.astype(_BF), jnp.zeros((tr, wide - 128), _BF)], axis=1)

    return pl.pallas_call(
        body, grid=(nb,),
        in_specs=[pl.BlockSpec((N_FOX, 8, tr), lambda i: (0, 0, nb - 1 - i)),
                  pl.BlockSpec((tr, 128), lambda i: (nb - 1 - i, fb)), pl.BlockSpec((1, 128), lambda i: (0, 0)),
                  pl.BlockSpec(memory_space=pl.ANY)],
        out_specs=[pl.BlockSpec((tr, wide), lambda i: (nb - 1 - i, F_FL // wide)),
                   pl.BlockSpec((1, 128), lambda i: (0, 0))],
        out_shape=[SDS(dp.shape, dp.dtype), SDS((1, 128), F32)], scratch_shapes=[pltpu.VMEM((1, 128), F32)],
        input_output_aliases={3: 0},
        compiler_params=_cp("arbitrary"), name="fox_dlogf")(dr, p_f, b_f_pad, dp)


def _fox_fwd(p_b, p_f, c_col, c_row, y):
    S = p_b.shape[0]
    t = min(2 * T_ATT, S)
    nq = S // t
    rc = min(256, t)
    qb, kb, vb, gb = B_Q // HEAD, B_K // HEAD, B_V // HEAD, F_FG // HEAD
    kq = SCALE * 1.4426950408889634

    def body(q_ref, k_ref, v_ref, cc_ref, cr_ref, g_ref, yin_ref, o_ref, y_ref, lse_ref, va_ref, ua_ref, ub_ref, m_ref,
             acc_ref):
        i = pl.program_id(1)

        @pl.when(i == 0)
        def _():
            va_ref[:, 0:HEAD] = v_ref[...]
            lane = lax.broadcasted_iota(jnp.int32, (S, HEAD), 1)
            va_ref[:, HEAD:2 * HEAD] = jnp.where(lane == 0, 1.0, 0.0).astype(_BF)

        m_ref[...] = jnp.full_like(m_ref, NEG)
        acc_ref[...] = jnp.zeros_like(acc_ref)

        def scores(b, u_ref):
            off = pl.multiple_of(b * t, t)
            k = k_ref[pl.ds(off, t), :]
            csr = cr_ref[:, pl.ds(off, t)] * (1.0 / SCALE)
            for r in range(0, t, rc):
                u_ref[r:r + rc, :] = _dot(q_ref[r:r + rc, :], k, NT) - csr

        def absorb(b, u_ref, masked):
            va = va_ref[pl.ds(pl.multiple_of(b * t, t), t), :]
            for r in range(0, t, rc):
                u = u_ref[r:r + rc, :]
                if masked:
                    row = lax.broadcasted_iota(jnp.int32, (rc, t), 0) + r
                    col = lax.broadcasted_iota(jnp.int32, (rc, t), 1)
                    u = jnp.where(col <= row, u, NEG)
                m_old = m_ref[r:r + rc, :]
                m_new = jnp.maximum(m_old, jnp.max(u, axis=-1, keepdims=True))
                alpha = jnp.exp2((m_old - m_new) * kq)
                p = jnp.exp2((u - m_new) * kq)
                acc_ref[r:r + rc, :] = alpha * acc_ref[r:r + rc, :] + _dot(p.astype(_BF), va, NN)
                m_ref[r:r + rc, :] = m_new

        scores(0, ua_ref)

        def pair(pi, carry):
            b = 2 * pi
            scores(b + 1, ub_ref)
            absorb(b, ua_ref, False)
            scores(b + 2, ua_ref)
            absorb(b + 1, ub_ref, False)
            return carry

        lax.fori_loop(0, i // 2, pair, 0)

        @pl.when(i % 2 == 1)
        def _():
            scores(i, ub_ref)
            absorb(i - 1, ua_ref, False)
            absorb(i, ub_ref, True)

        @pl.when(i % 2 == 0)
        def _():
            absorb(i, ua_ref, True)

        l = acc_ref[:, HEAD:HEAD + 1]
        o = acc_ref[:, 0:HEAD] / l
        gv = g_ref[...]
        o_ref[...] = o.astype(_BF)
        y_ref[...] = (o * (gv * _sigmoid(gv))).astype(_BF)
        lse_ref[...] = cc_ref[...] + SCALE * m_ref[...] + jnp.log(l)

    tile = pl.BlockSpec((t, HEAD), lambda h, i: (i, h))
    return pl.pallas_call(
        body, grid=(N_FOX, nq),
        in_specs=[pl.BlockSpec((t, HEAD), lambda h, i: (i, qb + h)),
                  pl.BlockSpec((S, HEAD), lambda h, i: (0, kb + h)),
                  pl.BlockSpec((S, HEAD), lambda h, i: (0, vb + h)),
                  pl.BlockSpec((None, t, 1), lambda h, i: (h, i, 0)),
                  pl.BlockSpec((None, 1, S), lambda h, i: (h, 0, 0)),
                  pl.BlockSpec((t, HEAD), lambda h, i: (i, gb + h)), pl.BlockSpec(memory_space=pl.ANY)],
        out_specs=[tile, pl.BlockSpec((t, HEAD), lambda h, i: (i, CONV_W // HEAD + h)),
                   pl.BlockSpec((None, t, 1), lambda h, i: (h, i, 0))],
        out_shape=[SDS((S, FOX_W), _BF), SDS(y.shape, y.dtype), SDS((N_FOX, S, 1), F32)],
        input_output_aliases={6: 1},
        scratch_shapes=[pltpu.VMEM((S, 2 * HEAD), _BF), pltpu.VMEM((t, t), F32), pltpu.VMEM((t, t), F32),
                        pltpu.VMEM((t, 1), F32), pltpu.VMEM((t, 2 * HEAD), F32)],
        compiler_params=_cp("parallel", "arbitrary"), name="fox_fwd")(p_b, p_b, p_b, c_col, c_row, p_f, y)


def _fox_bwd_prep(dy, p_f, o, lse, c_col, dp):
    S = dy.shape[0]
    t = min(T_ATT, S)
    hg = 4
    wd = hg * HEAD

    def body(dy_ref, g_ref, o_ref, lse_ref, cc_ref, dp_ref, do_ref, dg_ref, rows_ref):
        gv = g_ref[...]
        sg = _sigmoid(gv)
        dyv = dy_ref[...]
        ov = o_ref[...].astype(F32)
        do = dyv * (gv * sg)
        do_ref[...] = do.astype(_BF)
        dg_ref[...] = (dyv * ov * _dsilu(gv, sg)).astype(_BF)
        prod = do * ov
        lane = lax.broadcasted_iota(jnp.int32, (t, 128), 1)
        for hd in range(hg):
            delta = jnp.sum(prod[:, hd * HEAD:(hd + 1) * HEAD], axis=-1, keepdims=True)
            a = cc_ref[hd] - lse_ref[hd]
            mat = jnp.where(lane == 0, a, jnp.where(lane == 1, delta, 0.0))
            rows_ref[hd] = mat.T[0:8, :]

    tile = pl.BlockSpec((t, wd), lambda g, i: (i, g))
    col = pl.BlockSpec((hg, t, 1), lambda g, i: (g, i, 0))
    return pl.pallas_call(
        body, grid=(N_FOX // hg, S // t),
        in_specs=[pl.BlockSpec((t, wd), lambda g, i: (i, CONV_W // wd + g)),
                  pl.BlockSpec((t, wd), lambda g, i: (i, F_FG // wd + g)), tile, col, col,
                  pl.BlockSpec(memory_space=pl.ANY)],
        out_specs=[tile, pl.BlockSpec((t, wd), lambda g, i: (i, F_FG // wd + g)),
                   pl.BlockSpec((hg, 8, t), lambda g, i: (g, 0, i))],
        out_shape=[SDS((S, FOX_W), _BF), SDS(dp.shape, dp.dtype), SDS((N_FOX, 8, S), F32)],
        input_output_aliases={5: 1},
        compiler_params=_cp("parallel", "parallel"), name="fox_bwd_prep")(dy, p_f, o, lse, c_col, dp)


def _fox_bwd(p_b, do, rows, c_col):
    S = p_b.shape[0]
    t = min(T_ATT, S)
    nk = S // t
    qb, kb, vb = B_Q // HEAD, B_K // HEAD, B_V // HEAD

    def body(k_ref, v_ref, q_ref, do_ref, rows_ref, cc_ref, dq_ref, dk_ref, dv_ref, dr_ref,
             dqt_ref, dka_ref, dva_ref, dca_ref, dra_ref, sa_ref, pa_ref, sb_ref, pb_ref):
        j = pl.program_id(1)

        @pl.when(j == 0)
        def _():
            dqt_ref[...] = jnp.zeros_like(dqt_ref)
            dra_ref[...] = jnp.zeros_like(dra_ref)

        k = k_ref[...]
        v = v_ref[...]
        kt = k.astype(F32).T.astype(_BF)
        cc = cc_ref[...]
        dka_ref[...] = jnp.zeros_like(dka_ref)
        dva_ref[...] = jnp.zeros_like(dva_ref)
        dca_ref[...] = jnp.zeros_like(dca_ref)

        def scores(i, s_ref, p_ref):
            off = pl.multiple_of(i * t, t)
            s_ref[...] = _dot(k, q_ref[pl.ds(off, t), :], NT) * SCALE + (rows_ref[0:1, pl.ds(off, t)] - cc)
            p_ref[...] = _dot(v, do_ref[pl.ds(off, t), :], NT) - rows_ref[1:2, pl.ds(off, t)]

        def absorb(i, s_ref, p_ref, masked):
            off = pl.multiple_of(i * t, t)
            st = s_ref[...]
            if masked:
                srow = lax.broadcasted_iota(jnp.int32, (t, t), 0)
                tcol = lax.broadcasted_iota(jnp.int32, (t, t), 1)
                st = jnp.where(srow <= tcol, st, NEG)
            pt = jnp.exp(st)
            dva_ref[...] += _dot(pt.astype(_BF), do_ref[pl.ds(off, t), :], NN)
            dst = pt * p_ref[...]
            part = dst[:, 0:128]
            for gidx in range(1, t // 128):
                part = part + dst[:, gidx * 128:(gidx + 1) * 128]
            dca_ref[...] += part
            dra_ref[0:1, pl.ds(off, t)] += jnp.sum(dst, axis=0, keepdims=True)
            dsb = dst.astype(_BF)
            dka_ref[...] += _dot(dsb, q_ref[pl.ds(off, t), :], NN)
            dqt_ref[:, pl.ds(off, t)] += _dot(kt, dsb, NN)

        rest = nk - 1 - j
        scores(j, sa_ref, pa_ref)

        @pl.when(rest == 0)
        def _():
            absorb(j, sa_ref, pa_ref, True)

        @pl.when(rest > 0)
        def _():
            scores(j + 1, sb_ref, pb_ref)
            absorb(j, sa_ref, pa_ref, True)

            def pair(pi, carry):
                c = j + 1 + 2 * pi
                scores(c + 1, sa_ref, pa_ref)
                absorb(c, sb_ref, pb_ref, False)
                scores(c + 2, sb_ref, pb_ref)
                absorb(c + 1, sa_ref, pa_ref, False)
                return carry

            lax.fori_loop(0, (rest - 1) // 2, pair, 0)

            @pl.when(rest % 2 == 1)
            def _():
                absorb(nk - 1, sb_ref, pb_ref, False)

            @pl.when(rest % 2 == 0)
            def _():
                scores(nk - 1, sa_ref, pa_ref)
                absorb(nk - 2, sb_ref, pb_ref, False)
                absorb(nk - 1, sa_ref, pa_ref, False)

        dk_ref[...] = (dka_ref[...] * SCALE).astype(_BF)
        dv_ref[...] = dva_ref[...].astype(_BF)
        dra_ref[1:2, pl.ds(pl.multiple_of(j * t, t), t)] = -jnp.sum(dca_ref[...].T, axis=0, keepdims=True)

        @pl.when(j == nk - 1)
        def _():
            dr_ref[...] = dra_ref[...]
            for ci in range(nk):
                dq_ref[ci * t:(ci + 1) * t, :] = (dqt_ref[:, ci * t:(ci + 1) * t].T * SCALE).astype(_BF)

    tile = pl.BlockSpec((t, HEAD), lambda h, j: (j, h))
    return pl.pallas_call(
        body, grid=(N_FOX, nk),
        in_specs=[pl.BlockSpec((t, HEAD), lambda h, j: (j, kb + h)),
                  pl.BlockSpec((t, HEAD), lambda h, j: (j, vb + h)),
                  pl.BlockSpec((S, HEAD), lambda h, j: (0, qb + h)),
                  pl.BlockSpec((S, HEAD), lambda h, j: (0, h)),
                  pl.BlockSpec((None, 8, S), lambda h, j: (h, 0, 0)),
                  pl.BlockSpec((None, t, 1), lambda h, j: (h, j, 0))],
        out_specs=[pl.BlockSpec((S, HEAD), lambda h, j: (0, h)), tile, tile,
                   pl.BlockSpec((None, 8, S), lambda h, j: (h, 0, 0))],
        out_shape=[SDS((S, FOX_W), _BF), SDS((S, FOX_W), _BF), SDS((S, FOX_W), _BF), SDS((N_FOX, 8, S), F32)],
        scratch_shapes=[pltpu.VMEM((HEAD, S), F32), pltpu.VMEM((t, HEAD), F32), pltpu.VMEM((t, HEAD), F32),
                        pltpu.VMEM((t, 128), F32), pltpu.VMEM((8, S), F32)] + [pltpu.VMEM((t, t), F32)] * 4,
        compiler_params=_cp("parallel", "arbitrary"), name="fox_bwd")(p_b, p_b, p_b, do, rows, c_col)


CHUNK = 32


SUBLANES = 8


def _shifted_windows(win):
    n = win.shape[0]
    return [win] + [pltpu.roll(win, n - s, axis=0) for s in range(1, SUBLANES)]


def _tap(rot, f):
    return rot[f % SUBLANES][f - f % SUBLANES:f - f % SUBLANES + CHUNK, :]


def _conv_taps(ext_ref, w_ref, first, out_fn, n_rows):
    def chunk(c, carry):
        r0 = pl.multiple_of(c * CHUNK, CHUNK)
        rot = _shifted_windows(ext_ref[pl.ds(r0, 2 * CHUNK), :])
        acc = jnp.zeros((CHUNK, CONV_W), F32)
        for k in range(CONV_K):
            acc = acc + w_ref[k:k + 1, :] * _tap(rot, first(k))
        out_fn(r0, acc)
        return carry

    lax.fori_loop(0, n_rows // CHUNK, chunk, 0)


def _conv_fwd(p_f, conv_w, conv_b, ln_g, ln_b, w_pw):
    S = p_f.shape[0]
    tc = min(T_CONV, S)
    hb = tc // HALO

    def body(a_ref, b_ref, gc_ref, ap_ref, bp_ref, w_ref, cb_ref, lg_ref, lb_ref, pw_ref, y_ref, u1_ref, ext_ref):
        i = pl.program_id(0)
        prev = ap_ref[...] * _sigmoid(bp_ref[...])
        ext_ref[0:HALO, :] = jnp.where(i > 0, prev, 0.0)
        ext_ref[HALO:HALO + tc, :] = a_ref[...] * _sigmoid(b_ref[...])
        cb = cb_ref[...]

        def put(r0, acc):
            u1_ref[pl.ds(r0, CHUNK), :] = acc + cb

        _conv_taps(ext_ref, w_ref, lambda k: HALO - (CONV_K - 1) + k, put, tc)
        u1 = u1_ref[...]
        mu = jnp.mean(u1, axis=-1, keepdims=True)
        d = u1 - mu
        rstd = lax.rsqrt(jnp.mean(d * d, axis=-1, keepdims=True) + EPS)
        u2 = d * rstd * lg_ref[...] + lb_ref[...]
        u3 = u2 * _sigmoid(u2)
        pw = _dot(u3.astype(_BF), pw_ref[...], NN)
        gc = gc_ref[...]
        y_ref[...] = (pw * (gc * _sigmoid(gc))).astype(_BF)

    blk = lambda cb_: pl.BlockSpec((tc, CONV_W), lambda i: (i, cb_))
    halo = lambda cb_: pl.BlockSpec((HALO, CONV_W), lambda i: (jnp.maximum(i * hb - 1, 0), cb_))
    vec = pl.BlockSpec((1, CONV_W), lambda i: (0, 0))
    return pl.pallas_call(
        body, grid=(S // tc,),
        in_specs=[blk(0), blk(1), blk(2), halo(0), halo(1), pl.BlockSpec((CONV_K, CONV_W), lambda i: (0, 0)),
                  vec, vec, vec, pl.BlockSpec((CONV_W, CONV_W), lambda i: (0, 0))],
        out_specs=[pl.BlockSpec((tc, CONV_W), lambda i: (i, 0)), pl.BlockSpec((tc, CONV_W), lambda i: (i, 0))],
        out_shape=[SDS((S, CONV_W + FOX_W + MEM_W), _BF), SDS((S, CONV_W), F32)],
        scratch_shapes=[pltpu.VMEM((tc + 2 * HALO, CONV_W), F32)],
        compiler_params=_cp("parallel"), name="conv_fwd")(p_f, p_f, p_f, p_f, p_f, conv_w, conv_b, ln_g, ln_b, w_pw)


def _conv_bwd1(u1, p_f, dy, ln_g, ln_b, w_pw):
    S = u1.shape[0]
    tc = min(T_CONV, S)

    def body(u1_ref, gc_ref, dy_ref, lg_ref, lb_ref, pw_ref, du1_ref, dgc_ref, gpw_ref, glg_ref, glb_ref, gcb_ref):
        i = pl.program_id(0)
        u1v = u1_ref[...]
        mu = jnp.mean(u1v, axis=-1, keepdims=True)
        d = u1v - mu
        rstd = lax.rsqrt(jnp.mean(d * d, axis=-1, keepdims=True) + EPS)
        xh = d * rstd
        lg = lg_ref[...]
        u2 = xh * lg + lb_ref[...]
        sg2 = _sigmoid(u2)
        u3b = (u2 * sg2).astype(_BF)
        w = pw_ref[...]
        pw = _dot(u3b, w, NN)
        gc = gc_ref[...]
        sgc = _sigmoid(gc)
        dyv = dy_ref[...]
        dpw = (dyv * (gc * sgc)).astype(_BF)
        dgc_ref[...] = (dyv * pw * _dsilu(gc, sgc)).astype(_BF)
        gpw = _dot(u3b, dpw, TN)
        du2 = _dot(dpw, w, NT) * _dsilu(u2, sg2)
        glg = jnp.sum(du2 * xh, axis=0, keepdims=True)
        glb = jnp.sum(du2, axis=0, keepdims=True)
        dxh = du2 * lg
        du1 = rstd * (dxh - jnp.mean(dxh, axis=-1, keepdims=True) - xh * jnp.mean(dxh * xh, axis=-1, keepdims=True))
        du1_ref[...] = du1
        gcb = jnp.sum(du1, axis=0, keepdims=True)

        @pl.when(i == 0)
        def _():
            gpw_ref[...] = gpw
            glg_ref[...] = glg
            glb_ref[...] = glb
            gcb_ref[...] = gcb

        @pl.when(i > 0)
        def _():
            gpw_ref[...] += gpw
            glg_ref[...] += glg
            glb_ref[...] += glb
            gcb_ref[...] += gcb

    row = pl.BlockSpec((tc, CONV_W), lambda i: (i, 0))
    vec = pl.BlockSpec((1, CONV_W), lambda i: (0, 0))
    sq = pl.BlockSpec((CONV_W, CONV_W), lambda i: (0, 0))
    return pl.pallas_call(
        body, grid=(S // tc,),
        in_specs=[row, pl.BlockSpec((tc, CONV_W), lambda i: (i, F_GC // CONV_W)), row, vec, vec, sq],
        out_specs=[row, pl.BlockSpec((tc, CONV_W), lambda i: (i, F_GC // CONV_W)), sq, vec, vec, vec],
        out_shape=[SDS((S, CONV_W), F32), SDS((S, 2 * PART_W), _BF), SDS((CONV_W, CONV_W), F32),
                   SDS((1, CONV_W), F32), SDS((1, CONV_W), F32), SDS((1, CONV_W), F32)],
        compiler_params=_cp("arbitrary"), name="conv_bwd1")(u1, p_f, dy, ln_g, ln_b, w_pw)


def _conv_bwd2(du1, p_f, conv_w, dp):
    S = du1.shape[0]
    tc = min(T_CONV, S)
    hb = tc // HALO
    nblk = S // tc
    last_halo = S // HALO - 1

    def body(d_ref, dn_ref, a_ref, b_ref, ap_ref, bp_ref, w_ref, dp_ref, dab_ref, gw_ref, ext_ref, dext_ref, du0_ref,
             gacc_ref):
        i = pl.program_id(0)
        av = a_ref[...]
        sb = _sigmoid(b_ref[...])
        prev = ap_ref[...] * _sigmoid(bp_ref[...])
        ext_ref[0:HALO, :] = jnp.where(i > 0, prev, 0.0)
        ext_ref[HALO:HALO + tc, :] = av * sb
        dext_ref[0:tc, :] = d_ref[...]
        dext_ref[tc:tc + HALO, :] = jnp.where(i < nblk - 1, dn_ref[...], 0.0)

        def put(r0, acc):
            du0_ref[pl.ds(r0, CHUNK), :] = acc

        _conv_taps(dext_ref, w_ref, lambda k: CONV_K - 1 - k, put, tc)
        du0 = du0_ref[...]
        dab_ref[:, 0:CONV_W] = (du0 * sb).astype(_BF)
        dab_ref[:, CONV_W:2 * CONV_W] = (du0 * av * sb * (1.0 - sb)).astype(_BF)

        gacc_ref[...] = jnp.zeros_like(gacc_ref)

        def chunk(c, carry):
            r0 = pl.multiple_of(c * CHUNK, CHUNK)
            rot = _shifted_windows(ext_ref[pl.ds(r0, 2 * CHUNK), :])
            dv = dext_ref[pl.ds(r0, CHUNK), :]
            for k in range(CONV_K):
                prod = dv * _tap(rot, HALO - (CONV_K - 1) + k)
                part = prod[0:SUBLANES]
                for r in range(SUBLANES, CHUNK, SUBLANES):
                    part = part + prod[r:r + SUBLANES]
                gacc_ref[k * SUBLANES:(k + 1) * SUBLANES, :] += part
            return carry

        lax.fori_loop(0, tc // CHUNK, chunk, 0)
        rows = [jnp.sum(gacc_ref[k * SUBLANES:(k + 1) * SUBLANES, :], axis=0, keepdims=True) for k in range(CONV_K)]
        rows.append(jnp.zeros((1, CONV_W), F32))
        gw = jnp.concatenate(rows, axis=0)

        @pl.when(i == 0)
        def _():
            gw_ref[...] = gw

        @pl.when(i > 0)
        def _():
            gw_ref[...] += gw

    row = pl.BlockSpec((tc, CONV_W), lambda i: (i, 0))
    blk = lambda cb_: pl.BlockSpec((tc, CONV_W), lambda i: (i, cb_))
    halo = lambda cb_: pl.BlockSpec((HALO, CONV_W), lambda i: (jnp.maximum(i * hb - 1, 0), cb_))
    nxt = pl.BlockSpec((HALO, CONV_W), lambda i: (jnp.minimum((i + 1) * hb, last_halo), 0))
    return pl.pallas_call(
        body, grid=(nblk,),
        in_specs=[row, nxt, blk(0), blk(1), halo(0), halo(1), pl.BlockSpec((CONV_K, CONV_W), lambda i: (0, 0)),
                  pl.BlockSpec(memory_space=pl.ANY)],
        out_specs=[pl.BlockSpec((tc, 2 * CONV_W), lambda i: (i, 0)), pl.BlockSpec((CONV_K + 1, CONV_W), lambda i: (0, 0))],
        out_shape=[SDS(dp.shape, dp.dtype), SDS((CONV_K + 1, CONV_W), F32)], input_output_aliases={7: 0},
        scratch_shapes=[pltpu.VMEM((tc + 2 * HALO, CONV_W), F32), pltpu.VMEM((tc + 2 * HALO, CONV_W), F32),
                        pltpu.VMEM((tc, CONV_W), F32), pltpu.VMEM((CONV_K * SUBLANES, CONV_W), F32)],
        compiler_params=_cp("arbitrary"), name="conv_bwd2")(du1, du1, p_f, p_f, p_f, p_f, conv_w, dp)


def _exchange(srcs, scatter, name):
    n = len(srcs)
    out_shape = [SDS((N_DEV,) + (s.shape[1:] if sc else s.shape), s.dtype) for s, sc in zip(srcs, scatter)]

    def body(*refs):
        src_refs, dst_refs = refs[:n], refs[n:2 * n]
        send_sems, recv_sems, local_sems = refs[2 * n:]
        x, y, c = lax.axis_index("x"), lax.axis_index("y"), lax.axis_index("c")
        me = 4 * x + 2 * y + c
        copies = []
        for a in range(n):
            for f in range(1, N_DEV):
                px = 1 - x if f & 4 else x
                py = 1 - y if f & 2 else y
                pc = 1 - c if f & 1 else c
                peer = 4 * px + 2 * py + pc
                src = src_refs[a].at[peer] if scatter[a] else src_refs[a]
                cp = pltpu.make_async_remote_copy(
                    src_ref=src, dst_ref=dst_refs[a].at[me], send_sem=send_sems.at[a, f - 1],
                    recv_sem=recv_sems.at[a, f - 1], device_id=(px, py, pc), device_id_type=MESH)
                cp.start()
                copies.append(cp)
            own = src_refs[a].at[me] if scatter[a] else src_refs[a]
            lc = pltpu.make_async_copy(own, dst_refs[a].at[me], local_sems.at[a])
            lc.start()
            copies.append(lc)
        for cp in copies:
            cp.wait()

    anyspec = pl.BlockSpec(memory_space=pl.ANY)
    return pl.pallas_call(
        body, in_specs=[anyspec] * n, out_specs=[anyspec] * n, out_shape=out_shape,
        scratch_shapes=[pltpu.SemaphoreType.DMA((n, N_DEV - 1)), pltpu.SemaphoreType.DMA((n, N_DEV - 1)),
                        pltpu.SemaphoreType.DMA((n,))],
        name=name)(*srcs)


def _flip_peer(f, x, y, c):
    return (1 - x if f & 4 else x, 1 - y if f & 2 else y, 1 - c if f & 1 else c)


def _scatter_start(srcs):
    n = len(srcs)
    lands = [lax.empty((N_DEV - 1,) + s.shape[1:], s.dtype) for s in srcs]

    def body(*refs):
        src_refs, land_refs = refs[:n], refs[n:2 * n]
        send_sems, recv_sems = refs[2 * n:3 * n], refs[3 * n:4 * n]
        token = refs[-1]
        x, y, c = lax.axis_index("x"), lax.axis_index("y"), lax.axis_index("c")
        for a in range(n):
            for f in range(1, N_DEV):
                px, py, pc = _flip_peer(f, x, y, c)
                pltpu.make_async_remote_copy(
                    src_ref=src_refs[a].at[4 * px + 2 * py + pc], dst_ref=land_refs[a].at[f - 1],
                    send_sem=send_sems[a], recv_sem=recv_sems[a], device_id=(px, py, pc), device_id_type=MESH).start()
        token[...] = jnp.zeros_like(token)

    hbm = pl.BlockSpec(memory_space=pltpu.HBM)
    sem = pl.BlockSpec(memory_space=pltpu.SEMAPHORE)
    bufs = [pltpu.with_memory_space_constraint(b, pltpu.HBM) for b in list(srcs) + lands]
    out = pl.pallas_call(
        body, name="scatter_start",
        out_shape=(*[pltpu.SemaphoreType.DMA(())] * (2 * n), *[pltpu.HBM(b.shape, b.dtype) for b in bufs],
                   SDS((8, 128), F32)),
        in_specs=[hbm] * (2 * n),
        out_specs=(*[sem] * (2 * n), *[hbm] * (2 * n), pl.BlockSpec(memory_space=pltpu.VMEM)),
        input_output_aliases={i: 2 * n + i for i in range(2 * n)},
        compiler_params=pltpu.CompilerParams(has_side_effects=pltpu.SideEffectType.DATAFLOW_SIDE_EFFECTING))(*bufs)
    return list(out[:2 * n]), list(out[2 * n:3 * n]), list(out[3 * n:4 * n]), out[-1]


def _scatter_wait(sems, srcs, lands, after):
    n = len(srcs)

    def body(*refs):
        src_refs, land_refs = refs[:n], refs[n:2 * n]
        send_sems, recv_sems = refs[2 * n:3 * n], refs[3 * n:4 * n]
        x, y, c = lax.axis_index("x"), lax.axis_index("y"), lax.axis_index("c")
        for a in range(n):
            seven = pltpu.make_async_remote_copy(
                src_ref=src_refs[a].at[pl.ds(0, N_DEV - 1)], dst_ref=land_refs[a], send_sem=send_sems[a],
                recv_sem=recv_sems[a], device_id=(x, y, c), device_id_type=MESH)
            seven.wait_send()
            seven.wait_recv()

    hbm = pl.BlockSpec(memory_space=pltpu.HBM)
    sem = pl.BlockSpec(memory_space=pltpu.SEMAPHORE)
    bufs = list(srcs) + list(lands)
    out = pl.pallas_call(
        body, name="scatter_wait", out_shape=tuple(pltpu.HBM(b.shape, b.dtype) for b in bufs),
        in_specs=[hbm] * (2 * n) + [sem] * (2 * n) + [pl.BlockSpec(memory_space=pl.ANY)],
        out_specs=tuple([hbm] * (2 * n)), input_output_aliases={i: i for i in range(2 * n)},
        compiler_params=pltpu.CompilerParams(has_side_effects=pltpu.SideEffectType.DATAFLOW_SIDE_EFFECTING))(
            *bufs, *sems, after)
    return list(out[:n]), list(out[n:])


def _gather_two_level(srcs, name):
    n = len(srcs)
    out_shape = [SDS((N_DEV,) + s.shape, s.dtype) for s in srcs]

    def body(*refs):
        src_refs, dst_refs = refs[:n], refs[n:2 * n]
        send_sems, recv_sems, local_sems = refs[2 * n:]
        x, y, c = lax.axis_index("x"), lax.axis_index("y"), lax.axis_index("c")
        sibling = (x, y, 1 - c)
        chips = [(1 - x, y), (x, 1 - y), (1 - x, 1 - y)]

        def slot(a, px, py, pc):
            return dst_refs[a].at[4 * px + 2 * py + pc]

        def copy(a, k, block, to, src=None):
            return pltpu.make_async_remote_copy(
                src_ref=slot(a, *block) if src is None else src, dst_ref=slot(a, *block),
                send_sem=send_sems.at[a, k], recv_sem=recv_sems.at[a, k], device_id=to, device_id_type=MESH)

        own, sends = [], []
        for a in range(n):
            mine = pltpu.make_async_copy(src_refs[a], slot(a, x, y, c), local_sems.at[a])
            mine.start()
            own.append(mine)
            first = [copy(a, 1 + j, (x, y, c), (*chip, c), src=src_refs[a]) for j, chip in enumerate(chips)]
            first.append(copy(a, 0, (x, y, c), sibling, src=src_refs[a]))
            for cp in first:
                cp.start()
            sends += first
        for a in range(n):
            for j, chip in enumerate(chips):
                copy(a, 1 + j, (*chip, c), (x, y, c)).wait_recv()
                fwd = copy(a, 4 + j, (*chip, c), sibling)
                fwd.start()
                sends.append(fwd)
        for a in range(n):
            copy(a, 0, (x, y, 1 - c), (x, y, c)).wait_recv()
            for j, chip in enumerate(chips):
                copy(a, 4 + j, (*chip, 1 - c), (x, y, c)).wait_recv()
        for cp in sends:
            cp.wait_send()
        for cp in own:
            cp.wait()

    anyspec = pl.BlockSpec(memory_space=pl.ANY)
    return pl.pallas_call(
        body, in_specs=[anyspec] * n, out_specs=[anyspec] * n, out_shape=out_shape,
        scratch_shapes=[pltpu.SemaphoreType.DMA((n, N_DEV - 1)), pltpu.SemaphoreType.DMA((n, N_DEV - 1)),
                        pltpu.SemaphoreType.DMA((n,))],
        name=name)(*srcs)


def _adamw(parts, w, m, v, name, tr=256, own=None):
    lead = w.ndim == 3
    R, C = w.shape[-2:]
    tr = tr if R % tr == 0 else R
    n_parts = parts.shape[0]
    first = [] if own is None else [own]

    def body(*refs):
        p_ref, w_ref, m_ref, v_ref, g_ref, d_ref, nm_ref, nv_ref = refs[len(first):]
        terms = [r[0] for r in refs[:len(first)]] + [p_ref[dev] for dev in range(n_parts)]
        g = terms[0].astype(F32)
        for term in terms[1:]:
            g = g + term.astype(F32)
        mn = ADAM_B1 * m_ref[...] + (1.0 - ADAM_B1) * g
        vn = ADAM_B2 * v_ref[...] + (1.0 - ADAM_B2) * (g * g)
        m_hat = mn / (1.0 - ADAM_B1 ** ADAM_STEP)
        v_hat = vn / (1.0 - ADAM_B2 ** ADAM_STEP)
        g_ref[...] = g
        d_ref[...] = -ADAM_LR * (m_hat / (jnp.sqrt(v_hat) + ADAM_EPS) + ADAM_WD * w_ref[...])
        nm_ref[...] = mn
        nv_ref[...] = vn

    blk = pl.BlockSpec((None, tr, C), lambda i: (0, i, 0)) if lead else pl.BlockSpec((tr, C), lambda i: (i, 0))
    return pl.pallas_call(
        body, grid=(R // tr,),
        in_specs=[pl.BlockSpec((1, tr, C), lambda i: (0, i, 0))] * len(first)
        + [pl.BlockSpec((n_parts, tr, C), lambda i: (0, i, 0)), blk, blk, blk],
        out_specs=[blk] * 4, out_shape=[SDS(w.shape, F32)] * 4,
        compiler_params=_cp("parallel"), name=name)(*first, parts, w, m, v)


def _adamw_transposed(parts, own, w_t, m_t, v_t, name, tr=256):
    n_parts, R, C = parts.shape
    tr = tr if R % tr == 0 else R

    def body(o_ref, p_ref, w_ref, m_ref, v_ref, g_ref, d_ref, nm_ref, nv_ref):
        r = lax.broadcasted_iota(jnp.int32, (tr, tr), 0)
        c = lax.broadcasted_iota(jnp.int32, (tr, tr), 1)
        eye = (r == c).astype(parts.dtype)
        g = _dot(o_ref[0], eye, TN)
        for dev in range(n_parts):
            g = g + _dot(p_ref[dev], eye, TN)
        mn = ADAM_B1 * m_ref[...] + (1.0 - ADAM_B1) * g
        vn = ADAM_B2 * v_ref[...] + (1.0 - ADAM_B2) * (g * g)
        m_hat = mn / (1.0 - ADAM_B1 ** ADAM_STEP)
        v_hat = vn / (1.0 - ADAM_B2 ** ADAM_STEP)
        g_ref[...] = g
        d_ref[...] = -ADAM_LR * (m_hat / (jnp.sqrt(v_hat) + ADAM_EPS) + ADAM_WD * w_ref[...])
        nm_ref[...] = mn
        nv_ref[...] = vn

    blk = pl.BlockSpec((C, tr), lambda i: (0, i))
    return pl.pallas_call(
        body, grid=(R // tr,),
        in_specs=[pl.BlockSpec((1, tr, C), lambda i: (0, i, 0)), pl.BlockSpec((n_parts, tr, C), lambda i: (0, i, 0)),
                  blk, blk, blk],
        out_specs=[blk] * 4, out_shape=[SDS((C, R), F32)] * 4,
        compiler_params=_cp("parallel"), name=name)(own, parts, w_t, m_t, v_t)


_SEGMENTS = ((O_A, O_Q, F_A), (O_Q, O_FL, PART_W + B_Q), (O_FL, O_FG, F_FL), (O_FG, O_MQ, F_FG),
             (O_MQ, O_MG, PART_W + B_MQ), (O_MG, D_IN, F_MG))


def _shards_to_aligned(shards):
    n, D, cols = shards.shape
    pieces = []
    for o0, o1, a0 in sorted(_SEGMENTS, key=lambda s: s[2]):
        for d in range(o0 // cols, (o1 - 1) // cols + 1):
            lo, hi = max(o0, d * cols), min(o1, (d + 1) * cols)
            pieces.append(shards[d, :, lo - d * cols:hi - d * cols])
        if a0 == F_FL:
            pieces.append(jnp.zeros((D, PART_W - F_FL - (o1 - o0)), shards.dtype))
    return jnp.concatenate(pieces, axis=1)


def _aligned_to_shards(g_all, cols):
    shards = []
    for d in range(N_DEV):
        pieces = []
        for o0, o1, a0 in _SEGMENTS:
            lo, hi = max(o0, d * cols), min(o1, (d + 1) * cols)
            if lo < hi:
                pieces.append(g_all[:, a0 + lo - o0:a0 + hi - o0])
        shards.append(jnp.concatenate(pieces, axis=1))
    return jnp.stack(shards)


def _pack_small(norm, mem_norm, final, conv_b, ln_g, ln_b, b_f, extra, D):
    width = max(D, PACK_W)
    row3 = jnp.concatenate([conv_b.reshape(-1), ln_g.reshape(-1), ln_b.reshape(-1), b_f.reshape(-1)])
    rows = [norm.reshape(-1), mem_norm.reshape(-1), final.reshape(-1), row3, extra.reshape(-1)]
    rows = [jnp.pad(r, (0, width - r.shape[0])) for r in rows]
    return jnp.concatenate([jnp.stack(rows), jnp.zeros((3, width), F32)], axis=0)


def _unpack_small(p, D):
    c = CONV_W
    return dict(norm_g=p[0:1, :D], mem_norm_g=p[1:2, :D], final_g=p[2, :D], conv_b=p[3:4, 0:c],
                conv_ln_g=p[3:4, c:2 * c], conv_ln_b=p[3:4, 2 * c:3 * c], b_f=p[3:4, 3 * c:3 * c + N_FOX])


def kernel(x, mem, norm_g, mem_norm_g, w_in, b_f, conv_w, conv_b, conv_ln_g, conv_ln_b, w_conv_pw, w_mem_kv, w_out, final_g, loss_target, m_norm_g, m_mem_norm_g, m_w_in, m_b_f, m_conv_w, m_conv_b, m_conv_ln_g, m_conv_ln_b, m_w_conv_pw, m_w_mem_kv, m_w_out, m_final_g, v_norm_g, v_mem_norm_g, v_w_in, v_b_f, v_conv_w, v_conv_b, v_conv_ln_g, v_conv_ln_b, v_w_conv_pw, v_w_mem_kv, v_w_out, v_final_g):
    S, D = x.shape[1], x.shape[2]
    M = mem.shape[1]
    xs, ms, tgt = x[0], mem[0], loss_target[0]
    cols = w_in.shape[2]

    g_in, g_cw, g_pw, g_kv, g_out = _gather_two_level(
        [w_in[0].astype(_BF), conv_w[0], w_conv_pw[0].astype(_BF), w_mem_kv[0].astype(_BF), w_out[0].astype(_BF)],
        "gather_weights")
    w_all = _shards_to_aligned(g_in)
    conv_w_full = jnp.transpose(g_cw, (1, 0, 2)).reshape(CONV_K, CONV_W)
    w_pw_full = g_pw.reshape(CONV_W, CONV_W)
    w_kv_full = g_kv.reshape(D, 2 * MEM_W)
    w_out_full = g_out.reshape(CONV_W + FOX_W + MEM_W, D)
    b_f_pad = jnp.pad(b_f, ((0, 0), (0, 128 - N_FOX)))

    h, r1 = _rms_fwd(xs, norm_g)
    p_f = _matmul(h, w_all, "nn", F32, "proj_f", b_cols=(0, PART_W))
    p_b = _matmul(h, w_all, "nn", _BF, "proj_b", b_cols=(PART_W, PART_W))
    mkv, mhat = _mem_kv_fwd(ms, mem_norm_g, w_kv_full)
    y, u1 = _conv_fwd(p_f, conv_w_full, conv_b, conv_ln_g, conv_ln_b, w_pw_full)
    c_col, c_row = _fox_cumsum(p_f, b_f_pad)
    o_fox, y, lse = _fox_fwd(p_b, p_f, c_col, c_row, y)
    y = _mem_attn_fwd(p_b, p_f, mkv, y)
    z = _matmul(y, w_out_full, "nn", F32, "out_proj")
    dx2, g_final, loss_part = _head_loss(xs, z, tgt, final_g.reshape(1, D))

    dy = _matmul(dx2, w_out_full, "nt", F32, "d_y")
    gw_out = _matmul(y, dx2, "tn", _BF, "gw_out")
    d_mq, d_mg, d_mk, d_mv = _mem_attn_bwd(p_b, p_f, mkv, dy)
    gw_kv, g_mem_norm = _mem_kv_bwd(jnp.concatenate([d_mk, d_mv], axis=1), mhat, mem_norm_g, w_kv_full)
    du1, dp, gw_pw, g_ln_g, g_ln_b, g_cb = _conv_bwd1(u1, p_f, dy, conv_ln_g, conv_ln_b, w_pw_full)
    dp, gw_cw = _conv_bwd2(du1, p_f, conv_w_full, dp)
    d_o, dp, rows = _fox_bwd_prep(dy, p_f, o_fox, lse, c_col, dp)
    d_q, d_k, d_v, dr = _fox_bwd(p_b, d_o, rows, c_col)
    dp, g_bf = _fox_dlogf(dr, p_f, b_f_pad, dp)
    for piece, col in ((d_mg, F_MG), (d_q, PART_W + B_Q), (d_k, PART_W + B_K), (d_v, PART_W + B_V),
                       (d_mq, PART_W + B_MQ)):
        dp = lax.dynamic_update_slice(dp, piece, (0, col))
    gw_all = _matmul(h, dp, "tn", _BF, "gw_in")

    send_in = _aligned_to_shards(gw_all, cols)
    send_cw = jnp.transpose(gw_cw[:CONV_K].reshape(CONV_K, N_DEV, CONV_W // N_DEV), (1, 0, 2))
    send_pw = gw_pw.astype(_BF).reshape(N_DEV, CONV_W // N_DEV, CONV_W)
    send_kv = gw_kv.reshape(N_DEV, D // N_DEV, 2 * MEM_W)
    send_out = gw_out.reshape(N_DEV, (CONV_W + FOX_W + MEM_W) // N_DEV, D)
    sems, sent, lands, token = _scatter_start([send_in, send_cw, send_pw, send_kv, send_out])
    dh = _matmul(dp, w_all, "nt", F32, "d_h", after=token)
    grad_x, g_norm = _rms_bwd(xs, r1, dh, dx2, norm_g)
    sent, lands = _scatter_wait(sems, sent, lands, grad_x)
    me = 4 * lax.axis_index("x") + 2 * lax.axis_index("y") + lax.axis_index("c")
    own = [lax.dynamic_index_in_dim(s_, me, 0, keepdims=True) for s_ in sent]

    small = _pack_small(g_norm, g_mem_norm, g_final, g_cb, g_ln_g, g_ln_b, g_bf[:, :N_FOX], loss_part[0, 0:1], D)
    r_small, = _exchange([small], [False], "exchange_small")

    res = {}
    w_in_t = _adamw_transposed(lands[0], own[0], jnp.transpose(w_in[0]), jnp.transpose(m_w_in[0]),
                               jnp.transpose(v_w_in[0]), "adamw_w_in")
    res["w_in"] = [jnp.transpose(a_)[None] for a_ in w_in_t]
    res["conv_w"] = _adamw(lands[1], conv_w, m_conv_w, v_conv_w, "adamw_conv_w", own=own[1])
    res["w_conv_pw"] = _adamw(lands[2], w_conv_pw, m_w_conv_pw, v_w_conv_pw, "adamw_w_pw", own=own[2])
    res["w_mem_kv"] = _adamw(lands[3], w_mem_kv, m_w_mem_kv, v_w_mem_kv, "adamw_w_kv", own=own[3])
    res["w_out"] = _adamw(lands[4], w_out, m_w_out, v_w_out, "adamw_w_out", own=own[4])
    zero = jnp.zeros((1,), F32)
    pk = lambda a, b_, c_, d_, e, f_, g_: _pack_small(a, b_, c_, d_, e, f_, g_, zero, D)
    sm = _adamw(r_small,
                pk(norm_g, mem_norm_g, final_g, conv_b, conv_ln_g, conv_ln_b, b_f),
                pk(m_norm_g, m_mem_norm_g, m_final_g, m_conv_b, m_conv_ln_g, m_conv_ln_b, m_b_f),
                pk(v_norm_g, v_mem_norm_g, v_final_g, v_conv_b, v_conv_ln_g, v_conv_ln_b, v_b_f), "adamw_small")
    loss = sm[0][4, 0]
    small_out = [_unpack_small(a, D) for a in sm]
    names = ["norm_g", "mem_norm_g", "w_in", "b_f", "conv_w", "conv_b", "conv_ln_g", "conv_ln_b", "w_conv_pw",
             "w_mem_kv", "w_out", "final_g"]
    outs = [loss, grad_x[None]]
    for kind in range(4):
        for nme in names:
            outs.append(res[nme][kind] if nme in res else small_out[kind][nme])
    return tuple(outs)
```

```python
import functools

import jax
import jax.numpy as jnp
from jax import lax
from jax.experimental import pallas as pl
from jax.experimental.pallas import tpu as pltpu

F32 = jnp.float32
_BF = jnp.bfloat16
SDS = jax.ShapeDtypeStruct
MESH = pl.DeviceIdType.MESH

N_DEV = 8
HEAD = 128
N_FOX = 8
N_MEM = 4
CONV_W = 512
CONV_K = 31
FOX_W = N_FOX * HEAD
MEM_W = N_MEM * HEAD
D_IN = 3 * CONV_W + 4 * FOX_W + N_FOX + 2 * MEM_W
EPS = 1e-6
SCALE = HEAD ** -0.5
NEG = -1e30

ADAM_LR = 0.001
ADAM_B1 = 0.9
ADAM_B2 = 0.999
ADAM_EPS = 1e-08
ADAM_WD = 0.01
ADAM_STEP = 10

PART_W = 3584
F_A, F_B, F_GC, F_FG, F_MG, F_FL = 0, 512, 1024, 1536, 2560, 3072
B_Q, B_K, B_V, B_MQ = 0, 1024, 2048, 3072
O_A, O_B, O_GC = 0, 512, 1024
O_Q, O_K, O_V = 1536, 2560, 3584
O_FL, O_FG, O_MQ, O_MG = 4608, 4616, 5640, 6152

HALO = 32
T_ATT = 512
T_ROW = 256
T_CONV = 512
VMEM_LIMIT = 56 * 1024 * 1024
PACK_W = 2048


def _cp(*sem):
    return pltpu.CompilerParams(dimension_semantics=sem, vmem_limit_bytes=VMEM_LIMIT)


def _sigmoid(x):
    return jax.nn.sigmoid(x)


def _dsilu(x, sg):
    return sg * (1.0 + x * (1.0 - sg))


def _dot(a, b, dims):
    return lax.dot_general(a, b, (dims, ((), ())), preferred_element_type=F32)


NN = ((1,), (0,))
NT = ((1,), (1,))
TN = ((0,), (0,))


def _pick(n, pref):
    if n <= pref:
        return n
    t = pref - pref % 128
    while n % t:
        t -= 128
    return t


def _matmul(a, b, mode, out_dtype, name, tm=512, tn=1024, tk=2048, after=None, b_cols=None):
    col0 = 0
    if mode == "nn":
        (M, K), (K2, N) = a.shape, b.shape
        if b_cols is not None:
            col0, N = b_cols
    elif mode == "nt":
        (M, K), (N, K2) = a.shape, b.shape
    else:
        (K, M), (K2, N) = a.shape, b.shape
    assert K == K2
    tn, tk = _pick(N, tn), _pick(K, tk)
    nk = K // tk
    tm = _pick(M, 2 * tm)
    assert M % tm == 0 and N % tn == 0 and K % tk == 0, (name, M, N, K)
    dims = {"nn": NN, "nt": NT, "tn": TN}[mode]
    n_in = 2 if after is None else 3

    def body(*refs):
        a_ref, b_ref = refs[0], refs[1]
        o_ref = refs[n_in]
        p = _dot(a_ref[...].astype(_BF), b_ref[...].astype(_BF), dims)
        if nk == 1:
            o_ref[...] = p.astype(out_dtype)
            return
        acc_ref = refs[n_in + 1]
        k = pl.program_id(2)

        @pl.when(k == 0)
        def _():
            acc_ref[...] = p

        @pl.when(jnp.logical_and(k > 0, k < nk - 1))
        def _():
            acc_ref[...] += p

        @pl.when(k == nk - 1)
        def _():
            o_ref[...] = (acc_ref[...] + p).astype(out_dtype)

    if mode == "nn":
        assert col0 % tn == 0
        jb = col0 // tn
        a_spec = pl.BlockSpec((tm, tk), lambda j, i, k: (i, k))
        b_spec = pl.BlockSpec((tk, tn), lambda j, i, k: (k, j + jb))
    elif mode == "nt":
        a_spec = pl.BlockSpec((tm, tk), lambda j, i, k: (i, k))
        b_spec = pl.BlockSpec((tn, tk), lambda j, i, k: (j, k))
    else:
        a_spec = pl.BlockSpec((tk, tm), lambda j, i, k: (k, i))
        b_spec = pl.BlockSpec((tk, tn), lambda j, i, k: (k, j))
    in_specs, args = [a_spec, b_spec], [a, b]
    if after is not None:
        in_specs.append(pl.BlockSpec(memory_space=pl.ANY))
        args.append(after)
    return pl.pallas_call(
        body, grid=(N // tn, M // tm, nk), in_specs=in_specs,
        out_specs=pl.BlockSpec((tm, tn), lambda j, i, k: (i, j)),
        out_shape=SDS((M, N), out_dtype), scratch_shapes=[] if nk == 1 else [pltpu.VMEM((tm, tn), F32)],
        compiler_params=_cp("parallel", "parallel", "arbitrary"), name=name)(*args)


def _rms_fwd(x, g):
    S, D = x.shape
    tr = min(T_ROW, S)

    def body(x_ref, g_ref, h_ref, r_ref):
        xv = x_ref[...]
        r = lax.rsqrt(jnp.mean(xv * xv, axis=-1, keepdims=True) + EPS)
        h_ref[...] = (xv * r * g_ref[...]).astype(_BF)
        r_ref[...] = r

    return pl.pallas_call(
        body, grid=(S // tr,),
        in_specs=[pl.BlockSpec((tr, D), lambda i: (i, 0)), pl.BlockSpec((1, D), lambda i: (0, 0))],
        out_specs=[pl.BlockSpec((tr, D), lambda i: (i, 0)), pl.BlockSpec((tr, 1), lambda i: (i, 0))],
        out_shape=[SDS((S, D), _BF), SDS((S, 1), F32)],
        compiler_params=_cp("parallel"), name="rms_fwd")(x, g)


def _head_loss(x, z, target, g):
    S, D = x.shape
    tr = min(T_ROW, S)

    def body(x_ref, z_ref, t_ref, g_ref, dx2_ref, gg_ref, loss_ref):
        i = pl.program_id(0)
        x2 = x_ref[...] + z_ref[...]
        r = lax.rsqrt(jnp.mean(x2 * x2, axis=-1, keepdims=True) + EPS)
        xh = x2 * r
        gv = g_ref[...]
        diff = xh * gv - t_ref[...]
        lsum = 0.5 * jnp.sum(jnp.mean(diff * diff, axis=-1, keepdims=True), axis=0, keepdims=True)
        dout = diff * (1.0 / D)
        gd = dout * gv
        dx2_ref[...] = r * (gd - xh * jnp.mean(gd * xh, axis=-1, keepdims=True))
        gg = jnp.sum(dout * xh, axis=0, keepdims=True)

        @pl.when(i == 0)
        def _():
            gg_ref[...] = gg
            loss_ref[...] = jnp.broadcast_to(lsum, (1, 128))

        @pl.when(i > 0)
        def _():
            gg_ref[...] += gg
            loss_ref[...] += jnp.broadcast_to(lsum, (1, 128))

    row = pl.BlockSpec((tr, D), lambda i: (i, 0))
    return pl.pallas_call(
        body, grid=(S // tr,), in_specs=[row, row, row, pl.BlockSpec((1, D), lambda i: (0, 0))],
        out_specs=[row, pl.BlockSpec((1, D), lambda i: (0, 0)), pl.BlockSpec((1, 128), lambda i: (0, 0))],
        out_shape=[SDS((S, D), F32), SDS((1, D), F32), SDS((1, 128), F32)],
        compiler_params=_cp("arbitrary"), name="head_loss")(x, z, target, g)


def _rms_bwd(x, r, dh, dx2, g):
    S, D = x.shape
    tr = min(T_ROW, S)

    def body(x_ref, r_ref, dh_ref, dx2_ref, g_ref, gx_ref, gg_ref):
        i = pl.program_id(0)
        rv = r_ref[...]
        xh = x_ref[...] * rv
        dh_ = dh_ref[...]
        gd = dh_ * g_ref[...]
        gx_ref[...] = dx2_ref[...] + rv * (gd - xh * jnp.mean(gd * xh, axis=-1, keepdims=True))
        gg = jnp.sum(dh_ * xh, axis=0, keepdims=True)

        @pl.when(i == 0)
        def _():
            gg_ref[...] = gg

        @pl.when(i > 0)
        def _():
            gg_ref[...] += gg

    row = pl.BlockSpec((tr, D), lambda i: (i, 0))
    return pl.pallas_call(
        body, grid=(S // tr,),
        in_specs=[row, pl.BlockSpec((tr, 1), lambda i: (i, 0)), row, row, pl.BlockSpec((1, D), lambda i: (0, 0))],
        out_specs=[row, pl.BlockSpec((1, D), lambda i: (0, 0))],
        out_shape=[SDS((S, D), F32), SDS((1, D), F32)],
        compiler_params=_cp("arbitrary"), name="rms_bwd")(x, r, dh, dx2, g)


def _mem_kv_fwd(mem, g, w_kv):
    M, D = mem.shape

    def body(mem_ref, g_ref, w_ref, mkv_ref, mhat_ref):
        mv = mem_ref[...]
        mhat = mv * lax.rsqrt(jnp.mean(mv * mv, axis=-1, keepdims=True) + EPS)
        mhat_ref[...] = mhat
        mkv_ref[...] = _dot((mhat * g_ref[...]).astype(_BF), w_ref[...], NN).astype(_BF)

    return pl.pallas_call(
        body, out_shape=[SDS((M, 2 * MEM_W), _BF), SDS((M, D), F32)],
        compiler_params=pltpu.CompilerParams(vmem_limit_bytes=VMEM_LIMIT), name="mem_kv_fwd")(mem, g, w_kv)


def _mem_kv_bwd(dmkv, mhat, g, w_kv):
    M, D = mhat.shape

    def body(d_ref, mhat_ref, g_ref, w_ref, gw_ref, gg_ref):
        d = d_ref[...].astype(_BF)
        mhat = mhat_ref[...]
        gw_ref[...] = _dot((mhat * g_ref[...]).astype(_BF), d, TN).astype(_BF)
        dmh = _dot(d, w_ref[...], NT)
        gg_ref[...] = jnp.sum(dmh * mhat, axis=0, keepdims=True)

    return pl.pallas_call(
        body, out_shape=[SDS((D, 2 * MEM_W), _BF), SDS((1, D), F32)],
        compiler_params=pltpu.CompilerParams(vmem_limit_bytes=VMEM_LIMIT), name="mem_kv_bwd")(dmkv, mhat, g, w_kv)


def _mem_attn_fwd(p_b, p_f, mkv, y):
    S = p_b.shape[0]
    M = mkv.shape[0]
    tq = min(T_ATT, S)

    def body(q_ref, kv_ref, g_ref, yin_ref, y_ref):
        for hd in range(N_MEM):
            cols = slice(hd * HEAD, (hd + 1) * HEAD)
            s = _dot(q_ref[:, cols], kv_ref[:, cols], NT) * SCALE
            m = jnp.max(s, axis=-1, keepdims=True)
            e = jnp.exp(s - m)
            p = e / jnp.sum(e, axis=-1, keepdims=True)
            o = _dot(p.astype(_BF), kv_ref[:, MEM_W + hd * HEAD:MEM_W + (hd + 1) * HEAD], NN)
            gv = g_ref[:, cols]
            y_ref[:, cols] = (o * (gv * _sigmoid(gv))).astype(_BF)

    return pl.pallas_call(
        body, grid=(S // tq,),
        in_specs=[pl.BlockSpec((tq, MEM_W), lambda i: (i, B_MQ // MEM_W)),
                  pl.BlockSpec((M, 2 * MEM_W), lambda i: (0, 0)),
                  pl.BlockSpec((tq, MEM_W), lambda i: (i, F_MG // MEM_W)), pl.BlockSpec(memory_space=pl.ANY)],
        out_specs=pl.BlockSpec((tq, MEM_W), lambda i: (i, (CONV_W + FOX_W) // MEM_W)),
        out_shape=SDS(y.shape, y.dtype), input_output_aliases={3: 0},
        compiler_params=_cp("parallel"), name="mem_attn_fwd")(p_b, mkv, p_f, y)


def _mem_attn_bwd(p_b, p_f, mkv, dy):
    S = p_b.shape[0]
    M = mkv.shape[0]
    tq = min(T_ATT, S)

    def body(q_ref, kv_ref, g_ref, dy_ref, dq_ref, dg_ref, dk_ref, dv_ref):
        i = pl.program_id(0)

        @pl.when(i == 0)
        def _():
            dk_ref[...] = jnp.zeros_like(dk_ref)
            dv_ref[...] = jnp.zeros_like(dv_ref)

        for hd in range(N_MEM):
            cols = slice(hd * HEAD, (hd + 1) * HEAD)
            q, k = q_ref[:, cols], kv_ref[:, cols]
            v = kv_ref[:, MEM_W + hd * HEAD:MEM_W + (hd + 1) * HEAD]
            s = _dot(q, k, NT) * SCALE
            m = jnp.max(s, axis=-1, keepdims=True)
            e = jnp.exp(s - m)
            p = e / jnp.sum(e, axis=-1, keepdims=True)
            pb = p.astype(_BF)
            o = _dot(pb, v, NN)
            gv = g_ref[:, cols]
            sg = _sigmoid(gv)
            dyv = dy_ref[:, cols]
            do = dyv * (gv * sg)
            dg_ref[:, cols] = (dyv * o * _dsilu(gv, sg)).astype(_BF)
            dob = do.astype(_BF)
            dp = _dot(dob, v, NT)
            ds = p * (dp - jnp.sum(dp * p, axis=-1, keepdims=True)) * SCALE
            dsb = ds.astype(_BF)
            dq_ref[:, cols] = _dot(dsb, k, NN).astype(_BF)
            dk_ref[:, cols] += _dot(dsb, q, TN)
            dv_ref[:, cols] += _dot(pb, dob, TN)

    tile = pl.BlockSpec((tq, MEM_W), lambda i: (i, 0))
    kv = pl.BlockSpec((M, MEM_W), lambda i: (0, 0))
    return pl.pallas_call(
        body, grid=(S // tq,),
        in_specs=[pl.BlockSpec((tq, MEM_W), lambda i: (i, B_MQ // MEM_W)),
                  pl.BlockSpec((M, 2 * MEM_W), lambda i: (0, 0)),
                  pl.BlockSpec((tq, MEM_W), lambda i: (i, F_MG // MEM_W)),
                  pl.BlockSpec((tq, MEM_W), lambda i: (i, (CONV_W + FOX_W) // MEM_W))],
        out_specs=[tile, tile, kv, kv],
        out_shape=[SDS((S, MEM_W), _BF), SDS((S, MEM_W), _BF), SDS((M, MEM_W), F32), SDS((M, MEM_W), F32)],
        compiler_params=_cp("arbitrary"), name="mem_attn_bwd")(p_b, mkv, p_f, dy)


def _fox_cumsum(p_f, b_f_pad):
    S = p_f.shape[0]
    tr = min(256, S)
    fb = F_FL // 128

    def body(z_ref, b_ref, col_ref, row_ref, carry_ref):
        i = pl.program_id(0)

        @pl.when(i == 0)
        def _():
            carry_ref[...] = jnp.zeros_like(carry_ref)

        z = z_ref[...] + b_ref[...]
        lf = jnp.minimum(z, 0.0) - jnp.log1p(jnp.exp(-jnp.abs(z)))
        r = lax.broadcasted_iota(jnp.int32, (tr, tr), 0)
        c = lax.broadcasted_iota(jnp.int32, (tr, tr), 1)
        tri = (c <= r).astype(F32)
        cs = jnp.dot(tri, lf, precision=lax.Precision.HIGHEST, preferred_element_type=F32) + carry_ref[...]
        carry_ref[...] = cs[tr - 1:tr, :]
        cst = cs.T
        for hd in range(N_FOX):
            col_ref[hd] = cs[:, hd:hd + 1]
            row_ref[hd] = cst[hd:hd + 1, :]

    return pl.pallas_call(
        body, grid=(S // tr,),
        in_specs=[pl.BlockSpec((tr, 128), lambda i: (i, fb)), pl.BlockSpec((1, 128), lambda i: (0, 0))],
        out_specs=[pl.BlockSpec((N_FOX, tr, 1), lambda i: (0, i, 0)), pl.BlockSpec((N_FOX, 1, tr), lambda i: (0, 0, i))],
        out_shape=[SDS((N_FOX, S, 1), F32), SDS((N_FOX, 1, S), F32)], scratch_shapes=[pltpu.VMEM((1, 128), F32)],
        compiler_params=_cp("arbitrary"), name="fox_cumsum")(p_f, b_f_pad)


def _fox_dlogf(dr, p_f, b_f_pad, dp):
    S = p_f.shape[0]
    tr = min(256, S)
    nb = S // tr
    fb = F_FL // 128
    wide = PART_W - F_FL

    def body(dr_ref, z_ref, b_ref, dp_ref, dz_ref, gb_ref, carry_ref):
        i = pl.program_id(0)

        @pl.when(i == 0)
        def _():
            carry_ref[...] = jnp.zeros_like(carry_ref)
            gb_ref[...] = jnp.zeros_like(gb_ref)

        heads = [dr_ref[hd, 0:1, :] + dr_ref[hd, 1:2, :] for hd in range(N_FOX)]
        dc = jnp.concatenate(heads + [jnp.zeros((128 - N_FOX, tr), F32)], axis=0).T
        r = lax.broadcasted_iota(jnp.int32, (tr, tr), 0)
        c = lax.broadcasted_iota(jnp.int32, (tr, tr), 1)
        tri = (c >= r).astype(F32)
        rc = jnp.dot(tri, dc, precision=lax.Precision.HIGHEST, preferred_element_type=F32) + carry_ref[...]
        carry_ref[...] = rc[0:1, :]
        z = z_ref[...] + b_ref[...]
        dz = rc * _sigmoid(-z)
        gb_ref[...] += jnp.sum(dz, axis=0, keepdims=True)
        dz_ref[...] = jnp.concatenate([dz.astype(_BF), jnp.zeros((tr, wide - 128), _BF)], axis=1)

    return pl.pallas_call(
        body, grid=(nb,),
        in_specs=[pl.BlockSpec((N_FOX, 8, tr), lambda i: (0, 0, nb - 1 - i)),
                  pl.BlockSpec((tr, 128), lambda i: (nb - 1 - i, fb)), pl.BlockSpec((1, 128), lambda i: (0, 0)),
                  pl.BlockSpec(memory_space=pl.ANY)],
        out_specs=[pl.BlockSpec((tr, wide), lambda i: (nb - 1 - i, F_FL // wide)),
                   pl.BlockSpec((1, 128), lambda i: (0, 0))],
        out_shape=[SDS(dp.shape, dp.dtype), SDS((1, 128), F32)], scratch_shapes=[pltpu.VMEM((1, 128), F32)],
        input_output_aliases={3: 0},
        compiler_params=_cp("arbitrary"), name="fox_dlogf")(dr, p_f, b_f_pad, dp)


def _fox_fwd(p_b, p_f, c_col, c_row, y):
    S = p_b.shape[0]
    t = min(2 * T_ATT, S)
    nq = S // t
    rc = min(256, t)
    qb, kb, vb, gb = B_Q // HEAD, B_K // HEAD, B_V // HEAD, F_FG // HEAD
    kq = SCALE * 1.4426950408889634

    def body(q_ref, k_ref, v_ref, cc_ref, cr_ref, g_ref, yin_ref, o_ref, y_ref, lse_ref, va_ref, ua_ref, ub_ref, m_ref,
             acc_ref):
        i = pl.program_id(1)

        @pl.when(i == 0)
        def _():
            va_ref[:, 0:HEAD] = v_ref[...]
            lane = lax.broadcasted_iota(jnp.int32, (S, HEAD), 1)
            va_ref[:, HEAD:2 * HEAD] = jnp.where(lane == 0, 1.0, 0.0).astype(_BF)

        m_ref[...] = jnp.full_like(m_ref, NEG)
        acc_ref[...] = jnp.zeros_like(acc_ref)

        def scores(b, u_ref):
            off = pl.multiple_of(b * t, t)
            k = k_ref[pl.ds(off, t), :]
            csr = cr_ref[:, pl.ds(off, t)] * (1.0 / SCALE)
            for r in range(0, t, rc):
                u_ref[r:r + rc, :] = _dot(q_ref[r:r + rc, :], k, NT) - csr

        def absorb(b, u_ref, masked):
            va = va_ref[pl.ds(pl.multiple_of(b * t, t), t), :]
            for r in range(0, t, rc):
                u = u_ref[r:r + rc, :]
                if masked:
                    row = lax.broadcasted_iota(jnp.int32, (rc, t), 0) + r
                    col = lax.broadcasted_iota(jnp.int32, (rc, t), 1)
                    u = jnp.where(col <= row, u, NEG)
                m_old = m_ref[r:r + rc, :]
                m_new = jnp.maximum(m_old, jnp.max(u, axis=-1, keepdims=True))
                alpha = jnp.exp2((m_old - m_new) * kq)
                p = jnp.exp2((u - m_new) * kq)
                acc_ref[r:r + rc, :] = alpha * acc_ref[r:r + rc, :] + _dot(p.astype(_BF), va, NN)
                m_ref[r:r + rc, :] = m_new

        scores(0, ua_ref)

        def pair(pi, carry):
            b = 2 * pi
            scores(b + 1, ub_ref)
            absorb(b, ua_ref, False)
            scores(b + 2, ua_ref)
            absorb(b + 1, ub_ref, False)
            return carry

        lax.fori_loop(0, i // 2, pair, 0)

        @pl.when(i % 2 == 1)
        def _():
            scores(i, ub_ref)
            absorb(i - 1, ua_ref, False)
            absorb(i, ub_ref, True)

        @pl.when(i % 2 == 0)
        def _():
            absorb(i, ua_ref, True)

        l = acc_ref[:, HEAD:HEAD + 1]
        o = acc_ref[:, 0:HEAD] / l
        gv = g_ref[...]
        o_ref[...] = o.astype(_BF)
        y_ref[...] = (o * (gv * _sigmoid(gv))).astype(_BF)
        lse_ref[...] = cc_ref[...] + SCALE * m_ref[...] + jnp.log(l)

    tile = pl.BlockSpec((t, HEAD), lambda h, i: (i, h))
    return pl.pallas_call(
        body, grid=(N_FOX, nq),
        in_specs=[pl.BlockSpec((t, HEAD), lambda h, i: (i, qb + h)),
                  pl.BlockSpec((S, HEAD), lambda h, i: (0, kb + h)),
                  pl.BlockSpec((S, HEAD), lambda h, i: (0, vb + h)),
                  pl.BlockSpec((None, t, 1), lambda h, i: (h, i, 0)),
                  pl.BlockSpec((None, 1, S), lambda h, i: (h, 0, 0)),
                  pl.BlockSpec((t, HEAD), lambda h, i: (i, gb + h)), pl.BlockSpec(memory_space=pl.ANY)],
        out_specs=[tile, pl.BlockSpec((t, HEAD), lambda h, i: (i, CONV_W // HEAD + h)),
                   pl.BlockSpec((None, t, 1), lambda h, i: (h, i, 0))],
        out_shape=[SDS((S, FOX_W), _BF), SDS(y.shape, y.dtype), SDS((N_FOX, S, 1), F32)],
        input_output_aliases={6: 1},
        scratch_shapes=[pltpu.VMEM((S, 2 * HEAD), _BF), pltpu.VMEM((t, t), F32), pltpu.VMEM((t, t), F32),
                        pltpu.VMEM((t, 1), F32), pltpu.VMEM((t, 2 * HEAD), F32)],
        compiler_params=_cp("parallel", "arbitrary"), name="fox_fwd")(p_b, p_b, p_b, c_col, c_row, p_f, y)


def _fox_bwd_prep(dy, p_f, o, lse, c_col, dp):
    S = dy.shape[0]
    t = min(T_ATT, S)
    hg = 4
    wd = hg * HEAD

    def body(dy_ref, g_ref, o_ref, lse_ref, cc_ref, dp_ref, do_ref, dg_ref, rows_ref):
        gv = g_ref[...]
        sg = _sigmoid(gv)
        dyv = dy_ref[...]
        ov = o_ref[...].astype(F32)
        do = dyv * (gv * sg)
        do_ref[...] = do.astype(_BF)
        dg_ref[...] = (dyv * ov * _dsilu(gv, sg)).astype(_BF)
        prod = do * ov
        lane = lax.broadcasted_iota(jnp.int32, (t, 128), 1)
        for hd in range(hg):
            delta = jnp.sum(prod[:, hd * HEAD:(hd + 1) * HEAD], axis=-1, keepdims=True)
            a = cc_ref[hd] - lse_ref[hd]
            mat = jnp.where(lane == 0, a, jnp.where(lane == 1, delta, 0.0))
            rows_ref[hd] = mat.T[0:8, :]

    tile = pl.BlockSpec((t, wd), lambda g, i: (i, g))
    col = pl.BlockSpec((hg, t, 1), lambda g, i: (g, i, 0))
    return pl.pallas_call(
        body, grid=(N_FOX // hg, S // t),
        in_specs=[pl.BlockSpec((t, wd), lambda g, i: (i, CONV_W // wd + g)),
                  pl.BlockSpec((t, wd), lambda g, i: (i, F_FG // wd + g)), tile, col, col,
                  pl.BlockSpec(memory_space=pl.ANY)],
        out_specs=[tile, pl.BlockSpec((t, wd), lambda g, i: (i, F_FG // wd + g)),
                   pl.BlockSpec((hg, 8, t), lambda g, i: (g, 0, i))],
        out_shape=[SDS((S, FOX_W), _BF), SDS(dp.shape, dp.dtype), SDS((N_FOX, 8, S), F32)],
        input_output_aliases={5: 1},
        compiler_params=_cp("parallel", "parallel"), name="fox_bwd_prep")(dy, p_f, o, lse, c_col, dp)


def _fox_bwd(p_b, do, rows, c_col):
    S = p_b.shape[0]
    t = min(T_ATT, S)
    nk = S // t
    qb, kb, vb = B_Q // HEAD, B_K // HEAD, B_V // HEAD

    def body(k_ref, v_ref, q_ref, do_ref, rows_ref, cc_ref, dq_ref, dk_ref, dv_ref, dr_ref,
             dqt_ref, dka_ref, dva_ref, dca_ref, dra_ref, sa_ref, pa_ref, sb_ref, pb_ref):
        j = pl.program_id(1)

        @pl.when(j == 0)
        def _():
            dqt_ref[...] = jnp.zeros_like(dqt_ref)
            dra_ref[...] = jnp.zeros_like(dra_ref)

        k = k_ref[...]
        v = v_ref[...]
        kt = k.astype(F32).T.astype(_BF)
        cc = cc_ref[...]
        dka_ref[...] = jnp.zeros_like(dka_ref)
        dva_ref[...] = jnp.zeros_like(dva_ref)
        dca_ref[...] = jnp.zeros_like(dca_ref)

        def scores(i, s_ref, p_ref):
            off = pl.multiple_of(i * t, t)
            s_ref[...] = _dot(k, q_ref[pl.ds(off, t), :], NT) * SCALE + (rows_ref[0:1, pl.ds(off, t)] - cc)
            p_ref[...] = _dot(v, do_ref[pl.ds(off, t), :], NT) - rows_ref[1:2, pl.ds(off, t)]

        def absorb(i, s_ref, p_ref, masked):
            off = pl.multiple_of(i * t, t)
            st = s_ref[...]
            if masked:
                srow = lax.broadcasted_iota(jnp.int32, (t, t), 0)
                tcol = lax.broadcasted_iota(jnp.int32, (t, t), 1)
                st = jnp.where(srow <= tcol, st, NEG)
            pt = jnp.exp(st)
            dva_ref[...] += _dot(pt.astype(_BF), do_ref[pl.ds(off, t), :], NN)
            dst = pt * p_ref[...]
            part = dst[:, 0:128]
            for gidx in range(1, t // 128):
                part = part + dst[:, gidx * 128:(gidx + 1) * 128]
            dca_ref[...] += part
            dra_ref[0:1, pl.ds(off, t)] += jnp.sum(dst, axis=0, keepdims=True)
            dsb = dst.astype(_BF)
            dka_ref[...] += _dot(dsb, q_ref[pl.ds(off, t), :], NN)
            dqt_ref[:, pl.ds(off, t)] += _dot(kt, dsb, NN)

        rest = nk - 1 - j
        scores(j, sa_ref, pa_ref)

        @pl.when(rest == 0)
        def _():
            absorb(j, sa_ref, pa_ref, True)

        @pl.when(rest > 0)
        def _():
            scores(j + 1, sb_ref, pb_ref)
            absorb(j, sa_ref, pa_ref, True)

            def pair(pi, carry):
                c = j + 1 + 2 * pi
                scores(c + 1, sa_ref, pa_ref)
                absorb(c, sb_ref, pb_ref, False)
                scores(c + 2, sb_ref, pb_ref)
                absorb(c + 1, sa_ref, pa_ref, False)
                return carry

            lax.fori_loop(0, (rest - 1) // 2, pair, 0)

            @pl.when(rest % 2 == 1)
            def _():
                absorb(nk - 1, sb_ref, pb_ref, False)

            @pl.when(rest % 2 == 0)
            def _():
                scores(nk - 1, sa_ref, pa_ref)
                absorb(nk - 2, sb_ref, pb_ref, False)
                absorb(nk - 1, sa_ref, pa_ref, False)

        dk_ref[...] = (dka_ref[...] * SCALE).astype(_BF)
        dv_ref[...] = dva_ref[...].astype(_BF)
        dra_ref[1:2, pl.ds(pl.multiple_of(j * t, t), t)] = -jnp.sum(dca_ref[...].T, axis=0, keepdims=True)

        @pl.when(j == nk - 1)
        def _():
            dr_ref[...] = dra_ref[...]
            for ci in range(nk):
                dq_ref[ci * t:(ci + 1) * t, :] = (dqt_ref[:, ci * t:(ci + 1) * t].T * SCALE).astype(_BF)

    tile = pl.BlockSpec((t, HEAD), lambda h, j: (j, h))
    return pl.pallas_call(
        body, grid=(N_FOX, nk),
        in_specs=[pl.BlockSpec((t, HEAD), lambda h, j: (j, kb + h)),
                  pl.BlockSpec((t, HEAD), lambda h, j: (j, vb + h)),
                  pl.BlockSpec((S, HEAD), lambda h, j: (0, qb + h)),
                  pl.BlockSpec((S, HEAD), lambda h, j: (0, h)),
                  pl.BlockSpec((None, 8, S), lambda h, j: (h, 0, 0)),
                  pl.BlockSpec((None, t, 1), lambda h, j: (h, j, 0))],
        out_specs=[pl.BlockSpec((S, HEAD), lambda h, j: (0, h)), tile, tile,
                   pl.BlockSpec((None, 8, S), lambda h, j: (h, 0, 0))],
        out_shape=[SDS((S, FOX_W), _BF), SDS((S, FOX_W), _BF), SDS((S, FOX_W), _BF), SDS((N_FOX, 8, S), F32)],
        scratch_shapes=[pltpu.VMEM((HEAD, S), F32), pltpu.VMEM((t, HEAD), F32), pltpu.VMEM((t, HEAD), F32),
                        pltpu.VMEM((t, 128), F32), pltpu.VMEM((8, S), F32)] + [pltpu.VMEM((t, t), F32)] * 4,
        compiler_params=_cp("parallel", "arbitrary"), name="fox_bwd")(p_b, p_b, p_b, do, rows, c_col)


CHUNK = 32


SUBLANES = 8


def _shifted_windows(win):
    n = win.shape[0]
    return [win] + [pltpu.roll(win, n - s, axis=0) for s in range(1, SUBLANES)]


def _tap(rot, f):
    return rot[f % SUBLANES][f - f % SUBLANES:f - f % SUBLANES + CHUNK, :]


def _conv_taps(ext_ref, w_ref, first, out_fn, n_rows):
    def chunk(c, carry):
        r0 = pl.multiple_of(c * CHUNK, CHUNK)
        rot = _shifted_windows(ext_ref[pl.ds(r0, 2 * CHUNK), :])
        acc = jnp.zeros((CHUNK, CONV_W), F32)
        for k in range(CONV_K):
            acc = acc + w_ref[k:k + 1, :] * _tap(rot, first(k))
        out_fn(r0, acc)
        return carry

    lax.fori_loop(0, n_rows // CHUNK, chunk, 0)


def _conv_fwd(p_f, conv_w, conv_b, ln_g, ln_b, w_pw):
    S = p_f.shape[0]
    tc = min(T_CONV, S)
    hb = tc // HALO

    def body(a_ref, b_ref, gc_ref, ap_ref, bp_ref, w_ref, cb_ref, lg_ref, lb_ref, pw_ref, y_ref, u1_ref, ext_ref):
        i = pl.program_id(0)
        prev = ap_ref[...] * _sigmoid(bp_ref[...])
        ext_ref[0:HALO, :] = jnp.where(i > 0, prev, 0.0)
        ext_ref[HALO:HALO + tc, :] = a_ref[...] * _sigmoid(b_ref[...])
        cb = cb_ref[...]

        def put(r0, acc):
            u1_ref[pl.ds(r0, CHUNK), :] = acc + cb

        _conv_taps(ext_ref, w_ref, lambda k: HALO - (CONV_K - 1) + k, put, tc)
        u1 = u1_ref[...]
        mu = jnp.mean(u1, axis=-1, keepdims=True)
        d = u1 - mu
        rstd = lax.rsqrt(jnp.mean(d * d, axis=-1, keepdims=True) + EPS)
        u2 = d * rstd * lg_ref[...] + lb_ref[...]
        u3 = u2 * _sigmoid(u2)
        pw = _dot(u3.astype(_BF), pw_ref[...], NN)
        gc = gc_ref[...]
        y_ref[...] = (pw * (gc * _sigmoid(gc))).astype(_BF)

    blk = lambda cb_: pl.BlockSpec((tc, CONV_W), lambda i: (i, cb_))
    halo = lambda cb_: pl.BlockSpec((HALO, CONV_W), lambda i: (jnp.maximum(i * hb - 1, 0), cb_))
    vec = pl.BlockSpec((1, CONV_W), lambda i: (0, 0))
    return pl.pallas_call(
        body, grid=(S // tc,),
        in_specs=[blk(0), blk(1), blk(2), halo(0), halo(1), pl.BlockSpec((CONV_K, CONV_W), lambda i: (0, 0)),
                  vec, vec, vec, pl.BlockSpec((CONV_W, CONV_W), lambda i: (0, 0))],
        out_specs=[pl.BlockSpec((tc, CONV_W), lambda i: (i, 0)), pl.BlockSpec((tc, CONV_W), lambda i: (i, 0))],
        out_shape=[SDS((S, CONV_W + FOX_W + MEM_W), _BF), SDS((S, CONV_W), F32)],
        scratch_shapes=[pltpu.VMEM((tc + 2 * HALO, CONV_W), F32)],
        compiler_params=_cp("parallel"), name="conv_fwd")(p_f, p_f, p_f, p_f, p_f, conv_w, conv_b, ln_g, ln_b, w_pw)


def _conv_bwd1(u1, p_f, dy, ln_g, ln_b, w_pw):
    S = u1.shape[0]
    tc = min(T_CONV, S)

    def body(u1_ref, gc_ref, dy_ref, lg_ref, lb_ref, pw_ref, du1_ref, dgc_ref, gpw_ref, glg_ref, glb_ref, gcb_ref):
        i = pl.program_id(0)
        u1v = u1_ref[...]
        mu = jnp.mean(u1v, axis=-1, keepdims=True)
        d = u1v - mu
        rstd = lax.rsqrt(jnp.mean(d * d, axis=-1, keepdims=True) + EPS)
        xh = d * rstd
        lg = lg_ref[...]
        u2 = xh * lg + lb_ref[...]
        sg2 = _sigmoid(u2)
        u3b = (u2 * sg2).astype(_BF)
        w = pw_ref[...]
        pw = _dot(u3b, w, NN)
        gc = gc_ref[...]
        sgc = _sigmoid(gc)
        dyv = dy_ref[...]
        dpw = (dyv * (gc * sgc)).astype(_BF)
        dgc_ref[...] = (dyv * pw * _dsilu(gc, sgc)).astype(_BF)
        gpw = _dot(u3b, dpw, TN)
        du2 = _dot(dpw, w, NT) * _dsilu(u2, sg2)
        glg = jnp.sum(du2 * xh, axis=0, keepdims=True)
        glb = jnp.sum(du2, axis=0, keepdims=True)
        dxh = du2 * lg
        du1 = rstd * (dxh - jnp.mean(dxh, axis=-1, keepdims=True) - xh * jnp.mean(dxh * xh, axis=-1, keepdims=True))
        du1_ref[...] = du1
        gcb = jnp.sum(du1, axis=0, keepdims=True)

        @pl.when(i == 0)
        def _():
            gpw_ref[...] = gpw
            glg_ref[...] = glg
            glb_ref[...] = glb
            gcb_ref[...] = gcb

        @pl.when(i > 0)
        def _():
            gpw_ref[...] += gpw
            glg_ref[...] += glg
            glb_ref[...] += glb
            gcb_ref[...] += gcb

    row = pl.BlockSpec((tc, CONV_W), lambda i: (i, 0))
    vec = pl.BlockSpec((1, CONV_W), lambda i: (0, 0))
    sq = pl.BlockSpec((CONV_W, CONV_W), lambda i: (0, 0))
    return pl.pallas_call(
        body, grid=(S // tc,),
        in_specs=[row, pl.BlockSpec((tc, CONV_W), lambda i: (i, F_GC // CONV_W)), row, vec, vec, sq],
        out_specs=[row, pl.BlockSpec((tc, CONV_W), lambda i: (i, F_GC // CONV_W)), sq, vec, vec, vec],
        out_shape=[SDS((S, CONV_W), F32), SDS((S, 2 * PART_W), _BF), SDS((CONV_W, CONV_W), F32),
                   SDS((1, CONV_W), F32), SDS((1, CONV_W), F32), SDS((1, CONV_W), F32)],
        compiler_params=_cp("arbitrary"), name="conv_bwd1")(u1, p_f, dy, ln_g, ln_b, w_pw)


def _conv_bwd2(du1, p_f, conv_w, dp):
    S = du1.shape[0]
    tc = min(T_CONV, S)
    hb = tc // HALO
    nblk = S // tc
    last_halo = S // HALO - 1

    def body(d_ref, dn_ref, a_ref, b_ref, ap_ref, bp_ref, w_ref, dp_ref, dab_ref, gw_ref, ext_ref, dext_ref, du0_ref,
             gacc_ref):
        i = pl.program_id(0)
        av = a_ref[...]
        sb = _sigmoid(b_ref[...])
        prev = ap_ref[...] * _sigmoid(bp_ref[...])
        ext_ref[0:HALO, :] = jnp.where(i > 0, prev, 0.0)
        ext_ref[HALO:HALO + tc, :] = av * sb
        dext_ref[0:tc, :] = d_ref[...]
        dext_ref[tc:tc + HALO, :] = jnp.where(i < nblk - 1, dn_ref[...], 0.0)

        def put(r0, acc):
            du0_ref[pl.ds(r0, CHUNK), :] = acc

        _conv_taps(dext_ref, w_ref, lambda k: CONV_K - 1 - k, put, tc)
        du0 = du0_ref[...]
        dab_ref[:, 0:CONV_W] = (du0 * sb).astype(_BF)
        dab_ref[:, CONV_W:2 * CONV_W] = (du0 * av * sb * (1.0 - sb)).astype(_BF)

        gacc_ref[...] = jnp.zeros_like(gacc_ref)

        def chunk(c, carry):
            r0 = pl.multiple_of(c * CHUNK, CHUNK)
            rot = _shifted_windows(ext_ref[pl.ds(r0, 2 * CHUNK), :])
            dv = dext_ref[pl.ds(r0, CHUNK), :]
            for k in range(CONV_K):
                prod = dv * _tap(rot, HALO - (CONV_K - 1) + k)
                part = prod[0:SUBLANES]
                for r in range(SUBLANES, CHUNK, SUBLANES):
                    part = part + prod[r:r + SUBLANES]
                gacc_ref[k * SUBLANES:(k + 1) * SUBLANES, :] += part
            return carry

        lax.fori_loop(0, tc // CHUNK, chunk, 0)
        rows = [jnp.sum(gacc_ref[k * SUBLANES:(k + 1) * SUBLANES, :], axis=0, keepdims=True) for k in range(CONV_K)]
        rows.append(jnp.zeros((1, CONV_W), F32))
        gw = jnp.concatenate(rows, axis=0)

        @pl.when(i == 0)
        def _():
            gw_ref[...] = gw

        @pl.when(i > 0)
        def _():
            gw_ref[...] += gw

    row = pl.BlockSpec((tc, CONV_W), lambda i: (i, 0))
    blk = lambda cb_: pl.BlockSpec((tc, CONV_W), lambda i: (i, cb_))
    halo = lambda cb_: pl.BlockSpec((HALO, CONV_W), lambda i: (jnp.maximum(i * hb - 1, 0), cb_))
    nxt = pl.BlockSpec((HALO, CONV_W), lambda i: (jnp.minimum((i + 1) * hb, last_halo), 0))
    return pl.pallas_call(
        body, grid=(nblk,),
        in_specs=[row, nxt, blk(0), blk(1), halo(0), halo(1), pl.BlockSpec((CONV_K, CONV_W), lambda i: (0, 0)),
                  pl.BlockSpec(memory_space=pl.ANY)],
        out_specs=[pl.BlockSpec((tc, 2 * CONV_W), lambda i: (i, 0)), pl.BlockSpec((CONV_K + 1, CONV_W), lambda i: (0, 0))],
        out_shape=[SDS(dp.shape, dp.dtype), SDS((CONV_K + 1, CONV_W), F32)], input_output_aliases={7: 0},
        scratch_shapes=[pltpu.VMEM((tc + 2 * HALO, CONV_W), F32), pltpu.VMEM((tc + 2 * HALO, CONV_W), F32),
                        pltpu.VMEM((tc, CONV_W), F32), pltpu.VMEM((CONV_K * SUBLANES, CONV_W), F32)],
        compiler_params=_cp("arbitrary"), name="conv_bwd2")(du1, du1, p_f, p_f, p_f, p_f, conv_w, dp)


def _exchange(srcs, scatter, name):
    n = len(srcs)
    out_shape = [SDS((N_DEV,) + (s.shape[1:] if sc else s.shape), s.dtype) for s, sc in zip(srcs, scatter)]

    def body(*refs):
        src_refs, dst_refs = refs[:n], refs[n:2 * n]
        send_sems, recv_sems, local_sems = refs[2 * n:]
        x, y, c = lax.axis_index("x"), lax.axis_index("y"), lax.axis_index("c")
        me = 4 * x + 2 * y + c
        copies = []
        for a in range(n):
            for f in range(1, N_DEV):
                px = 1 - x if f & 4 else x
                py = 1 - y if f & 2 else y
                pc = 1 - c if f & 1 else c
                peer = 4 * px + 2 * py + pc
                src = src_refs[a].at[peer] if scatter[a] else src_refs[a]
                cp = pltpu.make_async_remote_copy(
                    src_ref=src, dst_ref=dst_refs[a].at[me], send_sem=send_sems.at[a, f - 1],
                    recv_sem=recv_sems.at[a, f - 1], device_id=(px, py, pc), device_id_type=MESH)
                cp.start()
                copies.append(cp)
            own = src_refs[a].at[me] if scatter[a] else src_refs[a]
            lc = pltpu.make_async_copy(own, dst_refs[a].at[me], local_sems.at[a])
            lc.start()
            copies.append(lc)
        for cp in copies:
            cp.wait()

    anyspec = pl.BlockSpec(memory_space=pl.ANY)
    return pl.pallas_call(
        body, in_specs=[anyspec] * n, out_specs=[anyspec] * n, out_shape=out_shape,
        scratch_shapes=[pltpu.SemaphoreType.DMA((n, N_DEV - 1)), pltpu.SemaphoreType.DMA((n, N_DEV - 1)),
                        pltpu.SemaphoreType.DMA((n,))],
        name=name)(*srcs)


def _flip_peer(f, x, y, c):
    return (1 - x if f & 4 else x, 1 - y if f & 2 else y, 1 - c if f & 1 else c)


def _scatter_start(srcs):
    n = len(srcs)
    lands = [lax.empty((N_DEV - 1,) + s.shape[1:], s.dtype) for s in srcs]

    def body(*refs):
        src_refs, land_refs = refs[:n], refs[n:2 * n]
        send_sems, recv_sems = refs[2 * n:3 * n], refs[3 * n:4 * n]
        token = refs[-1]
        x, y, c = lax.axis_index("x"), lax.axis_index("y"), lax.axis_index("c")
        for a in range(n):
            for f in range(1, N_DEV):
                px, py, pc = _flip_peer(f, x, y, c)
                pltpu.make_async_remote_copy(
                    src_ref=src_refs[a].at[4 * px + 2 * py + pc], dst_ref=land_refs[a].at[f - 1],
                    send_sem=send_sems[a], recv_sem=recv_sems[a], device_id=(px, py, pc), device_id_type=MESH).start()
        token[...] = jnp.zeros_like(token)

    hbm = pl.BlockSpec(memory_space=pltpu.HBM)
    sem = pl.BlockSpec(memory_space=pltpu.SEMAPHORE)
    bufs = [pltpu.with_memory_space_constraint(b, pltpu.HBM) for b in list(srcs) + lands]
    out = pl.pallas_call(
        body, name="scatter_start",
        out_shape=(*[pltpu.SemaphoreType.DMA(())] * (2 * n), *[pltpu.HBM(b.shape, b.dtype) for b in bufs],
                   SDS((8, 128), F32)),
        in_specs=[hbm] * (2 * n),
        out_specs=(*[sem] * (2 * n), *[hbm] * (2 * n), pl.BlockSpec(memory_space=pltpu.VMEM)),
        input_output_aliases={i: 2 * n + i for i in range(2 * n)},
        compiler_params=pltpu.CompilerParams(has_side_effects=pltpu.SideEffectType.DATAFLOW_SIDE_EFFECTING))(*bufs)
    return list(out[:2 * n]), list(out[2 * n:3 * n]), list(out[3 * n:4 * n]), out[-1]


def _scatter_wait(sems, srcs, lands, after):
    n = len(srcs)

    def body(*refs):
        src_refs, land_refs = refs[:n], refs[n:2 * n]
        send_sems, recv_sems = refs[2 * n:3 * n], refs[3 * n:4 * n]
        x, y, c = lax.axis_index("x"), lax.axis_index("y"), lax.axis_index("c")
        for a in range(n):
            seven = pltpu.make_async_remote_copy(
                src_ref=src_refs[a].at[pl.ds(0, N_DEV - 1)], dst_ref=land_refs[a], send_sem=send_sems[a],
                recv_sem=recv_sems[a], device_id=(x, y, c), device_id_type=MESH)
            seven.wait_send()
            seven.wait_recv()

    hbm = pl.BlockSpec(memory_space=pltpu.HBM)
    sem = pl.BlockSpec(memory_space=pltpu.SEMAPHORE)
    bufs = list(srcs) + list(lands)
    out = pl.pallas_call(
        body, name="scatter_wait", out_shape=tuple(pltpu.HBM(b.shape, b.dtype) for b in bufs),
        in_specs=[hbm] * (2 * n) + [sem] * (2 * n) + [pl.BlockSpec(memory_space=pl.ANY)],
        out_specs=tuple([hbm] * (2 * n)), input_output_aliases={i: i for i in range(2 * n)},
        compiler_params=pltpu.CompilerParams(has_side_effects=pltpu.SideEffectType.DATAFLOW_SIDE_EFFECTING))(
            *bufs, *sems, after)
    return list(out[:n]), list(out[n:])


def _gather_two_level(srcs, name):
    n = len(srcs)
    out_shape = [SDS((N_DEV,) + s.shape, s.dtype) for s in srcs]

    def body(*refs):
        src_refs, dst_refs = refs[:n], refs[n:2 * n]
        send_sems, recv_sems, local_sems = refs[2 * n:]
        x, y, c = lax.axis_index("x"), lax.axis_index("y"), lax.axis_index("c")
        sibling = (x, y, 1 - c)
        chips = [(1 - x, y), (x, 1 - y), (1 - x, 1 - y)]

        def slot(a, px, py, pc):
            return dst_refs[a].at[4 * px + 2 * py + pc]

        def copy(a, k, block, to, src=None):
            return pltpu.make_async_remote_copy(
                src_ref=slot(a, *block) if src is None else src, dst_ref=slot(a, *block),
                send_sem=send_sems.at[a, k], recv_sem=recv_sems.at[a, k], device_id=to, device_id_type=MESH)

        own, sends = [], []
        for a in range(n):
            mine = pltpu.make_async_copy(src_refs[a], slot(a, x, y, c), local_sems.at[a])
            mine.start()
            own.append(mine)
            first = [copy(a, 1 + j, (x, y, c), (*chip, c), src=src_refs[a]) for j, chip in enumerate(chips)]
            first.append(copy(a, 0, (x, y, c), sibling, src=src_refs[a]))
            for cp in first:
                cp.start()
            sends += first
        for a in range(n):
            for j, chip in enumerate(chips):
                copy(a, 1 + j, (*chip, c), (x, y, c)).wait_recv()
                fwd = copy(a, 4 + j, (*chip, c), sibling)
                fwd.start()
                sends.append(fwd)
        for a in range(n):
            copy(a, 0, (x, y, 1 - c), (x, y, c)).wait_recv()
            for j, chip in enumerate(chips):
                copy(a, 4 + j, (*chip, 1 - c), (x, y, c)).wait_recv()
        for cp in sends:
            cp.wait_send()
        for cp in own:
            cp.wait()

    anyspec = pl.BlockSpec(memory_space=pl.ANY)
    return pl.pallas_call(
        body, in_specs=[anyspec] * n, out_specs=[anyspec] * n, out_shape=out_shape,
        scratch_shapes=[pltpu.SemaphoreType.DMA((n, N_DEV - 1)), pltpu.SemaphoreType.DMA((n, N_DEV - 1)),
                        pltpu.SemaphoreType.DMA((n,))],
        name=name)(*srcs)


def _adamw(parts, w, m, v, name, tr=256, own=None):
    lead = w.ndim == 3
    R, C = w.shape[-2:]
    tr = tr if R % tr == 0 else R
    n_parts = parts.shape[0]
    first = [] if own is None else [own]

    def body(*refs):
        p_ref, w_ref, m_ref, v_ref, g_ref, d_ref, nm_ref, nv_ref = refs[len(first):]
        terms = [r[0] for r in refs[:len(first)]] + [p_ref[dev] for dev in range(n_parts)]
        g = terms[0].astype(F32)
        for term in terms[1:]:
            g = g + term.astype(F32)
        mn = ADAM_B1 * m_ref[...] + (1.0 - ADAM_B1) * g
        vn = ADAM_B2 * v_ref[...] + (1.0 - ADAM_B2) * (g * g)
        m_hat = mn / (1.0 - ADAM_B1 ** ADAM_STEP)
        v_hat = vn / (1.0 - ADAM_B2 ** ADAM_STEP)
        g_ref[...] = g
        d_ref[...] = -ADAM_LR * (m_hat / (jnp.sqrt(v_hat) + ADAM_EPS) + ADAM_WD * w_ref[...])
        nm_ref[...] = mn
        nv_ref[...] = vn

    blk = pl.BlockSpec((None, tr, C), lambda i: (0, i, 0)) if lead else pl.BlockSpec((tr, C), lambda i: (i, 0))
    return pl.pallas_call(
        body, grid=(R // tr,),
        in_specs=[pl.BlockSpec((1, tr, C), lambda i: (0, i, 0))] * len(first)
        + [pl.BlockSpec((n_parts, tr, C), lambda i: (0, i, 0)), blk, blk, blk],
        out_specs=[blk] * 4, out_shape=[SDS(w.shape, F32)] * 4,
        compiler_params=_cp("parallel"), name=name)(*first, parts, w, m, v)


def _adamw_transposed(parts, own, w_t, m_t, v_t, name, tr=256):
    n_parts, R, C = parts.shape
    tr = tr if R % tr == 0 else R

    def body(o_ref, p_ref, w_ref, m_ref, v_ref, g_ref, d_ref, nm_ref, nv_ref):
        r = lax.broadcasted_iota(jnp.int32, (tr, tr), 0)
        c = lax.broadcasted_iota(jnp.int32, (tr, tr), 1)
        eye = (r == c).astype(parts.dtype)
        g = _dot(o_ref[0], eye, TN)
        for dev in range(n_parts):
            g = g + _dot(p_ref[dev], eye, TN)
        mn = ADAM_B1 * m_ref[...] + (1.0 - ADAM_B1) * g
        vn = ADAM_B2 * v_ref[...] + (1.0 - ADAM_B2) * (g * g)
        m_hat = mn / (1.0 - ADAM_B1 ** ADAM_STEP)
        v_hat = vn / (1.0 - ADAM_B2 ** ADAM_STEP)
        g_ref[...] = g
        d_ref[...] = -ADAM_LR * (m_hat / (jnp.sqrt(v_hat) + ADAM_EPS) + ADAM_WD * w_ref[...])
        nm_ref[...] = mn
        nv_ref[...] = vn

    blk = pl.BlockSpec((C, tr), lambda i: (0, i))
    return pl.pallas_call(
        body, grid=(R // tr,),
        in_specs=[pl.BlockSpec((1, tr, C), lambda i: (0, i, 0)), pl.BlockSpec((n_parts, tr, C), lambda i: (0, i, 0)),
                  blk, blk, blk],
        out_specs=[blk] * 4, out_shape=[SDS((C, R), F32)] * 4,
        compiler_params=_cp("parallel"), name=name)(own, parts, w_t, m_t, v_t)


_SEGMENTS = ((O_A, O_Q, F_A), (O_Q, O_FL, PART_W + B_Q), (O_FL, O_FG, F_FL), (O_FG, O_MQ, F_FG),
             (O_MQ, O_MG, PART_W + B_MQ), (O_MG, D_IN, F_MG))


def _shards_to_aligned(shards):
    n, D, cols = shards.shape
    pieces = []
    for o0, o1, a0 in sorted(_SEGMENTS, key=lambda s: s[2]):
        for d in range(o0 // cols, (o1 - 1) // cols + 1):
            lo, hi = max(o0, d * cols), min(o1, (d + 1) * cols)
            pieces.append(shards[d, :, lo - d * cols:hi - d * cols])
        if a0 == F_FL:
            pieces.append(jnp.zeros((D, PART_W - F_FL - (o1 - o0)), shards.dtype))
    return jnp.concatenate(pieces, axis=1)


def _aligned_to_shards(g_all, cols):
    shards = []
    for d in range(N_DEV):
        pieces = []
        for o0, o1, a0 in _SEGMENTS:
            lo, hi = max(o0, d * cols), min(o1, (d + 1) * cols)
            if lo < hi:
                pieces.append(g_all[:, a0 + lo - o0:a0 + hi - o0])
        shards.append(jnp.concatenate(pieces, axis=1))
    return jnp.stack(shards)


def _pack_small(norm, mem_norm, final, conv_b, ln_g, ln_b, b_f, extra, D):
    width = max(D, PACK_W)
    row3 = jnp.concatenate([conv_b.reshape(-1), ln_g.reshape(-1), ln_b.reshape(-1), b_f.reshape(-1)])
    rows = [norm.reshape(-1), mem_norm.reshape(-1), final.reshape(-1), row3, extra.reshape(-1)]
    rows = [jnp.pad(r, (0, width - r.shape[0])) for r in rows]
    return jnp.concatenate([jnp.stack(rows), jnp.zeros((3, width), F32)], axis=0)


def _unpack_small(p, D):
    c = CONV_W
    return dict(norm_g=p[0:1, :D], mem_norm_g=p[1:2, :D], final_g=p[2, :D], conv_b=p[3:4, 0:c],
                conv_ln_g=p[3:4, c:2 * c], conv_ln_b=p[3:4, 2 * c:3 * c], b_f=p[3:4, 3 * c:3 * c + N_FOX])


def kernel(x, mem, norm_g, mem_norm_g, w_in, b_f, conv_w, conv_b, conv_ln_g, conv_ln_b, w_conv_pw, w_mem_kv, w_out, final_g, loss_target, m_norm_g, m_mem_norm_g, m_w_in, m_b_f, m_conv_w, m_conv_b, m_conv_ln_g, m_conv_ln_b, m_w_conv_pw, m_w_mem_kv, m_w_out, m_final_g, v_norm_g, v_mem_norm_g, v_w_in, v_b_f, v_conv_w, v_conv_b, v_conv_ln_g, v_conv_ln_b, v_w_conv_pw, v_w_mem_kv, v_w_out, v_final_g):
    S, D = x.shape[1], x.shape[2]
    M = mem.shape[1]
    xs, ms, tgt = x[0], mem[0], loss_target[0]
    cols = w_in.shape[2]

    g_in, g_cw, g_pw, g_kv, g_out = _gather_two_level(
        [w_in[0].astype(_BF), conv_w[0], w_conv_pw[0].astype(_BF), w_mem_kv[0].astype(_BF), w_out[0].astype(_BF)],
        "gather_weights")
    w_all = _shards_to_aligned(g_in)
    conv_w_full = jnp.transpose(g_cw, (1, 0, 2)).reshape(CONV_K, CONV_W)
    w_pw_full = g_pw.reshape(CONV_W, CONV_W)
    w_kv_full = g_kv.reshape(D, 2 * MEM_W)
    w_out_full = g_out.reshape(CONV_W + FOX_W + MEM_W, D)
    b_f_pad = jnp.pad(b_f, ((0, 0), (0, 128 - N_FOX)))

    h, r1 = _rms_fwd(xs, norm_g)
    p_f = _matmul(h, w_all, "nn", F32, "proj_f", b_cols=(0, PART_W))
    p_b = _matmul(h, w_all, "nn", _BF, "proj_b", b_cols=(PART_W, PART_W))
    mkv, mhat = _mem_kv_fwd(ms, mem_norm_g, w_kv_full)
    y, u1 = _conv_fwd(p_f, conv_w_full, conv_b, conv_ln_g, conv_ln_b, w_pw_full)
    c_col, c_row = _fox_cumsum(p_f, b_f_pad)
    o_fox, y, lse = _fox_fwd(p_b, p_f, c_col, c_row, y)
    y = _mem_attn_fwd(p_b, p_f, mkv, y)
    z = _matmul(y, w_out_full, "nn", F32, "out_proj")
    dx2, g_final, loss_part = _head_loss(xs, z, tgt, final_g.reshape(1, D))

    dy = _matmul(dx2, w_out_full, "nt", F32, "d_y")
    gw_out = _matmul(y, dx2, "tn", _BF, "gw_out")
    d_mq, d_mg, d_mk, d_mv = _mem_attn_bwd(p_b, p_f, mkv, dy)
    gw_kv, g_mem_norm = _mem_kv_bwd(jnp.concatenate([d_mk, d_mv], axis=1), mhat, mem_norm_g, w_kv_full)
    du1, dp, gw_pw, g_ln_g, g_ln_b, g_cb = _conv_bwd1(u1, p_f, dy, conv_ln_g, conv_ln_b, w_pw_full)
    dp, gw_cw = _conv_bwd2(du1, p_f, conv_w_full, dp)
    d_o, dp, rows = _fox_bwd_prep(dy, p_f, o_fox, lse, c_col, dp)
    d_q, d_k, d_v, dr = _fox_bwd(p_b, d_o, rows, c_col)
    dp, g_bf = _fox_dlogf(dr, p_f, b_f_pad, dp)
    for piece, col in ((d_mg, F_MG), (d_q, PART_W + B_Q), (d_k, PART_W + B_K), (d_v, PART_W + B_V),
                       (d_mq, PART_W + B_MQ)):
        dp = lax.dynamic_update_slice(dp, piece, (0, col))
    gw_all = _matmul(h, dp, "tn", _BF, "gw_in")

    send_in = _aligned_to_shards(gw_all, cols)
    send_cw = jnp.transpose(gw_cw[:CONV_K].reshape(CONV_K, N_DEV, CONV_W // N_DEV), (1, 0, 2))
    send_pw = gw_pw.astype(_BF).reshape(N_DEV, CONV_W // N_DEV, CONV_W)
    send_kv = gw_kv.reshape(N_DEV, D // N_DEV, 2 * MEM_W)
    send_out = gw_out.reshape(N_DEV, (CONV_W + FOX_W + MEM_W) // N_DEV, D)
    sems, sent, lands, token = _scatter_start([send_in, send_cw, send_pw, send_kv, send_out])
    dh = _matmul(dp, w_all, "nt", F32, "d_h", after=token)
    grad_x, g_norm = _rms_bwd(xs, r1, dh, dx2, norm_g)
    sent, lands = _scatter_wait(sems, sent, lands, grad_x)
    me = 4 * lax.axis_index("x") + 2 * lax.axis_index("y") + lax.axis_index("c")
    own = [lax.dynamic_index_in_dim(s_, me, 0, keepdims=True) for s_ in sent]

    small = _pack_small(g_norm, g_mem_norm, g_final, g_cb, g_ln_g, g_ln_b, g_bf[:, :N_FOX], loss_part[0, 0:1], D)
    r_small, = _exchange([small], [False], "exchange_small")

    res = {}
    w_in_t = _adamw_transposed(lands[0], own[0], jnp.transpose(w_in[0]), jnp.transpose(m_w_in[0]),
                               jnp.transpose(v_w_in[0]), "adamw_w_in")
    res["w_in"] = [jnp.transpose(a_)[None] for a_ in w_in_t]
    res["conv_w"] = _adamw(lands[1], conv_w, m_conv_w, v_conv_w, "adamw_conv_w", own=own[1])
    res["w_conv_pw"] = _adamw(lands[2], w_conv_pw, m_w_conv_pw, v_w_conv_pw, "adamw_w_pw", own=own[2])
    res["w_mem_kv"] = _adamw(lands[3], w_mem_kv, m_w_mem_kv, v_w_mem_kv, "adamw_w_kv", own=own[3])
    res["w_out"] = _adamw(lands[4], w_out, m_w_out, v_w_out, "adamw_w_out", own=own[4])
    zero = jnp.zeros((1,), F32)
    pk = lambda a, b_, c_, d_, e, f_, g_: _pack_small(a, b_, c_, d_, e, f_, g_, zero, D)
    sm = _adamw(r_small,
                pk(norm_g, mem_norm_g, final_g, conv_b, conv_ln_g, conv_ln_b, b_f),
                pk(m_norm_g, m_mem_norm_g, m_final_g, m_conv_b, m_conv_ln_g, m_conv_ln_b, m_b_f),
                pk(v_norm_g, v_mem_norm_g, v_final_g, v_conv_b, v_conv_ln_g, v_conv_ln_b, v_b_f), "adamw_small")
    loss = sm[0][4, 0]
    small_out = [_unpack_small(a, D) for a in sm]
    names = ["norm_g", "mem_norm_g", "w_in", "b_f", "conv_w", "conv_b", "conv_ln_g", "conv_ln_b", "w_conv_pw",
             "w_mem_kv", "w_out", "final_g"]
    outs = [loss, grad_x[None]]
    for kind in range(4):
        for nme in names:
            outs.append(res[nme][kind] if nme in res else small_out[kind][nme])
    return tuple(outs)
```

```python
import functools

import jax
import jax.numpy as jnp
from jax import lax
from jax.experimental import pallas as pl
from jax.experimental.pallas import tpu as pltpu

F32 = jnp.float32
_BF = jnp.bfloat16
SDS = jax.ShapeDtypeStruct
MESH = pl.DeviceIdType.MESH

N_DEV = 8
HEAD = 128
N_FOX = 8
N_MEM = 4
CONV_W = 512
CONV_K = 31
FOX_W = N_FOX * HEAD
MEM_W = N_MEM * HEAD
D_IN = 3 * CONV_W + 4 * FOX_W + N_FOX + 2 * MEM_W
EPS = 1e-6
SCALE = HEAD ** -0.5
NEG = -1e30

ADAM_LR = 0.001
ADAM_B1 = 0.9
ADAM_B2 = 0.999
ADAM_EPS = 1e-08
ADAM_WD = 0.01
ADAM_STEP = 10

PART_W = 3584
F_A, F_B, F_GC, F_FG, F_MG, F_FL = 0, 512, 1024, 1536, 2560, 3072
B_Q, B_K, B_V, B_MQ = 0, 1024, 2048, 3072
O_A, O_B, O_GC = 0, 512, 1024
O_Q, O_K, O_V = 1536, 2560, 3584
O_FL, O_FG, O_MQ, O_MG = 4608, 4616, 5640, 6152

HALO = 32
T_ATT = 512
T_ROW = 256
T_CONV = 512
VMEM_LIMIT = 56 * 1024 * 1024
PACK_W = 2048


def _cp(*sem):
    return pltpu.CompilerParams(dimension_semantics=sem, vmem_limit_bytes=VMEM_LIMIT)


def _sigmoid(x):
    return jax.nn.sigmoid(x)


def _dsilu(x, sg):
    return sg * (1.0 + x * (1.0 - sg))


def _dot(a, b, dims):
    return lax.dot_general(a, b, (dims, ((), ())), preferred_element_type=F32)


NN = ((1,), (0,))
NT = ((1,), (1,))
TN = ((0,), (0,))


def _pick(n, pref):
    if n <= pref:
        return n
    t = pref - pref % 128
    while n % t:
        t -= 128
    return t


def _matmul(a, b, mode, out_dtype, name, tm=512, tn=1024, tk=2048, after=None, b_cols=None):
    col0 = 0
    if mode == "nn":
        (M, K), (K2, N) = a.shape, b.shape
        if b_cols is not None:
            col0, N = b_cols
    elif mode == "nt":
        (M, K), (N, K2) = a.shape, b.shape
    else:
        (K, M), (K2, N) = a.shape, b.shape
    assert K == K2
    tn, tk = _pick(N, tn), _pick(K, tk)
    nk = K // tk
    tm = _pick(M, 2 * tm)
    assert M % tm == 0 and N % tn == 0 and K % tk == 0, (name, M, N, K)
    dims = {"nn": NN, "nt": NT, "tn": TN}[mode]
    n_in = 2 if after is None else 3

    def body(*refs):
        a_ref, b_ref = refs[0], refs[1]
        o_ref = refs[n_in]
        p = _dot(a_ref[...].astype(_BF), b_ref[...].astype(_BF), dims)
        if nk == 1:
            o_ref[...] = p.astype(out_dtype)
            return
        acc_ref = refs[n_in + 1]
        k = pl.program_id(2)

        @pl.when(k == 0)
        def _():
            acc_ref[...] = p

        @pl.when(jnp.logical_and(k > 0, k < nk - 1))
        def _():
            acc_ref[...] += p

        @pl.when(k == nk - 1)
        def _():
            o_ref[...] = (acc_ref[...] + p).astype(out_dtype)

    if mode == "nn":
        assert col0 % tn == 0
        jb = col0 // tn
        a_spec = pl.BlockSpec((tm, tk), lambda j, i, k: (i, k))
        b_spec = pl.BlockSpec((tk, tn), lambda j, i, k: (k, j + jb))
    elif mode == "nt":
        a_spec = pl.BlockSpec((tm, tk), lambda j, i, k: (i, k))
        b_spec = pl.BlockSpec((tn, tk), lambda j, i, k: (j, k))
    else:
        a_spec = pl.BlockSpec((tk, tm), lambda j, i, k: (k, i))
        b_spec = pl.BlockSpec((tk, tn), lambda j, i, k: (k, j))
    in_specs, args = [a_spec, b_spec], [a, b]
    if after is not None:
        in_specs.append(pl.BlockSpec(memory_space=pl.ANY))
        args.append(after)
    return pl.pallas_call(
        body, grid=(N // tn, M // tm, nk), in_specs=in_specs,
        out_specs=pl.BlockSpec((tm, tn), lambda j, i, k: (i, j)),
        out_shape=SDS((M, N), out_dtype), scratch_shapes=[] if nk == 1 else [pltpu.VMEM((tm, tn), F32)],
        compiler_params=_cp("parallel", "parallel", "arbitrary"), name=name)(*args)


def _rms_fwd(x, g):
    S, D = x.shape
    tr = min(T_ROW, S)

    def body(x_ref, g_ref, h_ref, r_ref):
        xv = x_ref[...]
        r = lax.rsqrt(jnp.mean(xv * xv, axis=-1, keepdims=True) + EPS)
        h_ref[...] = (xv * r * g_ref[...]).astype(_BF)
        r_ref[...] = r

    return pl.pallas_call(
        body, grid=(S // tr,),
        in_specs=[pl.BlockSpec((tr, D), lambda i: (i, 0)), pl.BlockSpec((1, D), lambda i: (0, 0))],
        out_specs=[pl.BlockSpec((tr, D), lambda i: (i, 0)), pl.BlockSpec((tr, 1), lambda i: (i, 0))],
        out_shape=[SDS((S, D), _BF), SDS((S, 1), F32)],
        compiler_params=_cp("parallel"), name="rms_fwd")(x, g)


def _head_loss(x, z, target, g):
    S, D = x.shape
    tr = min(T_ROW, S)

    def body(x_ref, z_ref, t_ref, g_ref, dx2_ref, gg_ref, loss_ref):
        i = pl.program_id(0)
        x2 = x_ref[...] + z_ref[...]
        r = lax.rsqrt(jnp.mean(x2 * x2, axis=-1, keepdims=True) + EPS)
        xh = x2 * r
        gv = g_ref[...]
        diff = xh * gv - t_ref[...]
        lsum = 0.5 * jnp.sum(jnp.mean(diff * diff, axis=-1, keepdims=True), axis=0, keepdims=True)
        dout = diff * (1.0 / D)
        gd = dout * gv
        dx2_ref[...] = r * (gd - xh * jnp.mean(gd * xh, axis=-1, keepdims=True))
        gg = jnp.sum(dout * xh, axis=0, keepdims=True)

        @pl.when(i == 0)
        def _():
            gg_ref[...] = gg
            loss_ref[...] = jnp.broadcast_to(lsum, (1, 128))

        @pl.when(i > 0)
        def _():
            gg_ref[...] += gg
            loss_ref[...] += jnp.broadcast_to(lsum, (1, 128))

    row = pl.BlockSpec((tr, D), lambda i: (i, 0))
    return pl.pallas_call(
        body, grid=(S // tr,), in_specs=[row, row, row, pl.BlockSpec((1, D), lambda i: (0, 0))],
        out_specs=[row, pl.BlockSpec((1, D), lambda i: (0, 0)), pl.BlockSpec((1, 128), lambda i: (0, 0))],
        out_shape=[SDS((S, D), F32), SDS((1, D), F32), SDS((1, 128), F32)],
        compiler_params=_cp("arbitrary"), name="head_loss")(x, z, target, g)


def _rms_bwd(x, r, dh, dx2, g):
    S, D = x.shape
    tr = min(T_ROW, S)

    def body(x_ref, r_ref, dh_ref, dx2_ref, g_ref, gx_ref, gg_ref):
        i = pl.program_id(0)
        rv = r_ref[...]
        xh = x_ref[...] * rv
        dh_ = dh_ref[...]
        gd = dh_ * g_ref[...]
        gx_ref[...] = dx2_ref[...] + rv * (gd - xh * jnp.mean(gd * xh, axis=-1, keepdims=True))
        gg = jnp.sum(dh_ * xh, axis=0, keepdims=True)

        @pl.when(i == 0)
        def _():
            gg_ref[...] = gg

        @pl.when(i > 0)
        def _():
            gg_ref[...] += gg

    row = pl.BlockSpec((tr, D), lambda i: (i, 0))
    return pl.pallas_call(
        body, grid=(S // tr,),
        in_specs=[row, pl.BlockSpec((tr, 1), lambda i: (i, 0)), row, row, pl.BlockSpec((1, D), lambda i: (0, 0))],
        out_specs=[row, pl.BlockSpec((1, D), lambda i: (0, 0))],
        out_shape=[SDS((S, D), F32), SDS((1, D), F32)],
        compiler_params=_cp("arbitrary"), name="rms_bwd")(x, r, dh, dx2, g)


def _mem_kv_fwd(mem, g, w_kv):
    M, D = mem.shape

    def body(mem_ref, g_ref, w_ref, mkv_ref, mhat_ref):
        mv = mem_ref[...]
        mhat = mv * lax.rsqrt(jnp.mean(mv * mv, axis=-1, keepdims=True) + EPS)
        mhat_ref[...] = mhat
        mkv_ref[...] = _dot((mhat * g_ref[...]).astype(_BF), w_ref[...], NN).astype(_BF)

    return pl.pallas_call(
        body, out_shape=[SDS((M, 2 * MEM_W), _BF), SDS((M, D), F32)],
        compiler_params=pltpu.CompilerParams(vmem_limit_bytes=VMEM_LIMIT), name="mem_kv_fwd")(mem, g, w_kv)


def _mem_kv_bwd(dmkv, mhat, g, w_kv):
    M, D = mhat.shape

    def body(d_ref, mhat_ref, g_ref, w_ref, gw_ref, gg_ref):
        d = d_ref[...].astype(_BF)
        mhat = mhat_ref[...]
        gw_ref[...] = _dot((mhat * g_ref[...]).astype(_BF), d, TN).astype(_BF)
        dmh = _dot(d, w_ref[...], NT)
        gg_ref[...] = jnp.sum(dmh * mhat, axis=0, keepdims=True)

    return pl.pallas_call(
        body, out_shape=[SDS((D, 2 * MEM_W), _BF), SDS((1, D), F32)],
        compiler_params=pltpu.CompilerParams(vmem_limit_bytes=VMEM_LIMIT), name="mem_kv_bwd")(dmkv, mhat, g, w_kv)


def _mem_attn_fwd(p_b, p_f, mkv, y):
    S = p_b.shape[0]
    M = mkv.shape[0]
    tq = min(T_ATT, S)

    def body(q_ref, kv_ref, g_ref, yin_ref, y_ref):
        for hd in range(N_MEM):
            cols = slice(hd * HEAD, (hd + 1) * HEAD)
            s = _dot(q_ref[:, cols], kv_ref[:, cols], NT) * SCALE
            m = jnp.max(s, axis=-1, keepdims=True)
            e = jnp.exp(s - m)
            p = e / jnp.sum(e, axis=-1, keepdims=True)
            o = _dot(p.astype(_BF), kv_ref[:, MEM_W + hd * HEAD:MEM_W + (hd + 1) * HEAD], NN)
            gv = g_ref[:, cols]
            y_ref[:, cols] = (o * (gv * _sigmoid(gv))).astype(_BF)

    return pl.pallas_call(
        body, grid=(S // tq,),
        in_specs=[pl.BlockSpec((tq, MEM_W), lambda i: (i, B_MQ // MEM_W)),
                  pl.BlockSpec((M, 2 * MEM_W), lambda i: (0, 0)),
                  pl.BlockSpec((tq, MEM_W), lambda i: (i, F_MG // MEM_W)), pl.BlockSpec(memory_space=pl.ANY)],
        out_specs=pl.BlockSpec((tq, MEM_W), lambda i: (i, (CONV_W + FOX_W) // MEM_W)),
        out_shape=SDS(y.shape, y.dtype), input_output_aliases={3: 0},
        compiler_params=_cp("parallel"), name="mem_attn_fwd")(p_b, mkv, p_f, y)


def _mem_attn_bwd(p_b, p_f, mkv, dy):
    S = p_b.shape[0]
    M = mkv.shape[0]
    tq = min(T_ATT, S)

    def body(q_ref, kv_ref, g_ref, dy_ref, dq_ref, dg_ref, dk_ref, dv_ref):
        i = pl.program_id(0)

        @pl.when(i == 0)
        def _():
            dk_ref[...] = jnp.zeros_like(dk_ref)
            dv_ref[...] = jnp.zeros_like(dv_ref)

        for hd in range(N_MEM):
            cols = slice(hd * HEAD, (hd + 1) * HEAD)
            q, k = q_ref[:, cols], kv_ref[:, cols]
            v = kv_ref[:, MEM_W + hd * HEAD:MEM_W + (hd + 1) * HEAD]
            s = _dot(q, k, NT) * SCALE
            m = jnp.max(s, axis=-1, keepdims=True)
            e = jnp.exp(s - m)
            p = e / jnp.sum(e, axis=-1, keepdims=True)
            pb = p.astype(_BF)
            o = _dot(pb, v, NN)
            gv = g_ref[:, cols]
            sg = _sigmoid(gv)
            dyv = dy_ref[:, cols]
            do = dyv * (gv * sg)
            dg_ref[:, cols] = (dyv * o * _dsilu(gv, sg)).astype(_BF)
            dob = do.astype(_BF)
            dp = _dot(dob, v, NT)
            ds = p * (dp - jnp.sum(dp * p, axis=-1, keepdims=True)) * SCALE
            dsb = ds.astype(_BF)
            dq_ref[:, cols] = _dot(dsb, k, NN).astype(_BF)
            dk_ref[:, cols] += _dot(dsb, q, TN)
            dv_ref[:, cols] += _dot(pb, dob, TN)

    tile = pl.BlockSpec((tq, MEM_W), lambda i: (i, 0))
    kv = pl.BlockSpec((M, MEM_W), lambda i: (0, 0))
    return pl.pallas_call(
        body, grid=(S // tq,),
        in_specs=[pl.BlockSpec((tq, MEM_W), lambda i: (i, B_MQ // MEM_W)),
                  pl.BlockSpec((M, 2 * MEM_W), lambda i: (0, 0)),
                  pl.BlockSpec((tq, MEM_W), lambda i: (i, F_MG // MEM_W)),
                  pl.BlockSpec((tq, MEM_W), lambda i: (i, (CONV_W + FOX_W) // MEM_W))],
        out_specs=[tile, tile, kv, kv],
        out_shape=[SDS((S, MEM_W), _BF), SDS((S, MEM_W), _BF), SDS((M, MEM_W), F32), SDS((M, MEM_W), F32)],
        compiler_params=_cp("arbitrary"), name="mem_attn_bwd")(p_b, mkv, p_f, dy)


def _fox_cumsum(p_f, b_f_pad):
    S = p_f.shape[0]
    tr = min(256, S)
    fb = F_FL // 128

    def body(z_ref, b_ref, col_ref, row_ref, carry_ref):
        i = pl.program_id(0)

        @pl.when(i == 0)
        def _():
            carry_ref[...] = jnp.zeros_like(carry_ref)

        z = z_ref[...] + b_ref[...]
        lf = jnp.minimum(z, 0.0) - jnp.log1p(jnp.exp(-jnp.abs(z)))
        r = lax.broadcasted_iota(jnp.int32, (tr, tr), 0)
        c = lax.broadcasted_iota(jnp.int32, (tr, tr), 1)
        tri = (c <= r).astype(F32)
        cs = jnp.dot(tri, lf, precision=lax.Precision.HIGHEST, preferred_element_type=F32) + carry_ref[...]
        carry_ref[...] = cs[tr - 1:tr, :]
        cst = cs.T
        for hd in range(N_FOX):
            col_ref[hd] = cs[:, hd:hd + 1]
            row_ref[hd] = cst[hd:hd + 1, :]

    return pl.pallas_call(
        body, grid=(S // tr,),
        in_specs=[pl.BlockSpec((tr, 128), lambda i: (i, fb)), pl.BlockSpec((1, 128), lambda i: (0, 0))],
        out_specs=[pl.BlockSpec((N_FOX, tr, 1), lambda i: (0, i, 0)), pl.BlockSpec((N_FOX, 1, tr), lambda i: (0, 0, i))],
        out_shape=[SDS((N_FOX, S, 1), F32), SDS((N_FOX, 1, S), F32)], scratch_shapes=[pltpu.VMEM((1, 128), F32)],
        compiler_params=_cp("arbitrary"), name="fox_cumsum")(p_f, b_f_pad)


def _fox_dlogf(dr, p_f, b_f_pad, dp):
    S = p_f.shape[0]
    tr = min(256, S)
    nb = S // tr
    fb = F_FL // 128
    wide = PART_W - F_FL

    def body(dr_ref, z_ref, b_ref, dp_ref, dz_ref, gb_ref, carry_ref):
        i = pl.program_id(0)

        @pl.when(i == 0)
        def _():
            carry_ref[...] = jnp.zeros_like(carry_ref)
            gb_ref[...] = jnp.zeros_like(gb_ref)

        heads = [dr_ref[hd, 0:1, :] + dr_ref[hd, 1:2, :] for hd in range(N_FOX)]
        dc = jnp.concatenate(heads + [jnp.zeros((128 - N_FOX, tr), F32)], axis=0).T
        r = lax.broadcasted_iota(jnp.int32, (tr, tr), 0)
        c = lax.broadcasted_iota(jnp.int32, (tr, tr), 1)
        tri = (c >= r).astype(F32)
        rc = jnp.dot(tri, dc, precision=lax.Precision.HIGHEST, preferred_element_type=F32) + carry_ref[...]
        carry_ref[...] = rc[0:1, :]
        z = z_ref[...] + b_ref[...]
        dz = rc * _sigmoid(-z)
        gb_ref[...] += jnp.sum(dz, axis=0, keepdims=True)
        dz_ref[...] = jnp.concatenate([dz.astype(_BF), jnp.zeros((tr, wide - 128), _BF)], axis=1)

    return pl.pallas_call(
        body, grid=(nb,),
        in_specs=[pl.BlockSpec((N_FOX, 8, tr), lambda i: (0, 0, nb - 1 - i)),
                  pl.BlockSpec((tr, 128), lambda i: (nb - 1 - i, fb)), pl.BlockSpec((1, 128), lambda i: (0, 0)),
                  pl.BlockSpec(memory_space=pl.ANY)],
        out_specs=[pl.BlockSpec((tr, wide), lambda i: (nb - 1 - i, F_FL // wide)),
                   pl.BlockSpec((1, 128), lambda i: (0, 0))],
        out_shape=[SDS(dp.shape, dp.dtype), SDS((1, 128), F32)], scratch_shapes=[pltpu.VMEM((1, 128), F32)],
        input_output_aliases={3: 0},
        compiler_params=_cp("arbitrary"), name="fox_dlogf")(dr, p_f, b_f_pad, dp)


def _fox_fwd(p_b, p_f, c_col, c_row, y):
    S = p_b.shape[0]
    t = min(2 * T_ATT, S)
    nq = S // t
    rc = min(256, t)
    qb, kb, vb, gb = B_Q // HEAD, B_K // HEAD, B_V // HEAD, F_FG // HEAD
    kq = SCALE * 1.4426950408889634

    def body(q_ref, k_ref, v_ref, cc_ref, cr_ref, g_ref, yin_ref, o_ref, y_ref, lse_ref, va_ref, ua_ref, ub_ref, m_ref,
             acc_ref):
        i = pl.program_id(1)

        @pl.when(i == 0)
        def _():
            va_ref[:, 0:HEAD] = v_ref[...]
            lane = lax.broadcasted_iota(jnp.int32, (S, HEAD), 1)
            va_ref[:, HEAD:2 * HEAD] = jnp.where(lane == 0, 1.0, 0.0).astype(_BF)

        m_ref[...] = jnp.full_like(m_ref, NEG)
        acc_ref[...] = jnp.zeros_like(acc_ref)

        def scores(b, u_ref):
            off = pl.multiple_of(b * t, t)
            k = k_ref[pl.ds(off, t), :]
            csr = cr_ref[:, pl.ds(off, t)] * (1.0 / SCALE)
            for r in range(0, t, rc):
                u_ref[r:r + rc, :] = _dot(q_ref[r:r + rc, :], k, NT) - csr

        def absorb(b, u_ref, masked):
            va = va_ref[pl.ds(pl.multiple_of(b * t, t), t), :]
            for r in range(0, t, rc):
                u = u_ref[r:r + rc, :]
                if masked:
                    row = lax.broadcasted_iota(jnp.int32, (rc, t), 0) + r
                    col = lax.broadcasted_iota(jnp.int32, (rc, t), 1)
                    u = jnp.where(col <= row, u, NEG)
                m_old = m_ref[r:r + rc, :]
                m_new = jnp.maximum(m_old, jnp.max(u, axis=-1, keepdims=True))
                alpha = jnp.exp2((m_old - m_new) * kq)
                p = jnp.exp2((u - m_new) * kq)
                acc_ref[r:r + rc, :] = alpha * acc_ref[r:r + rc, :] + _dot(p.astype(_BF), va, NN)
                m_ref[r:r + rc, :] = m_new

        scores(0, ua_ref)

        def pair(pi, carry):
            b = 2 * pi
            scores(b + 1, ub_ref)
            absorb(b, ua_ref, False)
            scores(b + 2, ua_ref)
            absorb(b + 1, ub_ref, False)
            return carry

        lax.fori_loop(0, i // 2, pair, 0)

        @pl.when(i % 2 == 1)
        def _():
            scores(i, ub_ref)
            absorb(i - 1, ua_ref, False)
            absorb(i, ub_ref, True)

        @pl.when(i % 2 == 0)
        def _():
            absorb(i, ua_ref, True)

        l = acc_ref[:, HEAD:HEAD + 1]
        o = acc_ref[:, 0:HEAD] / l
        gv = g_ref[...]
        o_ref[...] = o.astype(_BF)
        y_ref[...] = (o * (gv * _sigmoid(gv))).astype(_BF)
        lse_ref[...] = cc_ref[...] + SCALE * m_ref[...] + jnp.log(l)

    tile = pl.BlockSpec((t, HEAD), lambda h, i: (i, h))
    return pl.pallas_call(
        body, grid=(N_FOX, nq),
        in_specs=[pl.BlockSpec((t, HEAD), lambda h, i: (i, qb + h)),
                  pl.BlockSpec((S, HEAD), lambda h, i: (0, kb + h)),
                  pl.BlockSpec((S, HEAD), lambda h, i: (0, vb + h)),
                  pl.BlockSpec((None, t, 1), lambda h, i: (h, i, 0)),
                  pl.BlockSpec((None, 1, S), lambda h, i: (h, 0, 0)),
                  pl.BlockSpec((t, HEAD), lambda h, i: (i, gb + h)), pl.BlockSpec(memory_space=pl.ANY)],
        out_specs=[tile, pl.BlockSpec((t, HEAD), lambda h, i: (i, CONV_W // HEAD + h)),
                   pl.BlockSpec((None, t, 1), lambda h, i: (h, i, 0))],
        out_shape=[SDS((S, FOX_W), _BF), SDS(y.shape, y.dtype), SDS((N_FOX, S, 1), F32)],
        input_output_aliases={6: 1},
        scratch_shapes=[pltpu.VMEM((S, 2 * HEAD), _BF), pltpu.VMEM((t, t), F32), pltpu.VMEM((t, t), F32),
                        pltpu.VMEM((t, 1), F32), pltpu.VMEM((t, 2 * HEAD), F32)],
        compiler_params=_cp("parallel", "arbitrary"), name="fox_fwd")(p_b, p_b, p_b, c_col, c_row, p_f, y)


def _fox_bwd_prep(dy, p_f, o, lse, c_col, dp):
    S = dy.shape[0]
    t = min(T_ATT, S)
    hg = 4
    wd = hg * HEAD

    def body(dy_ref, g_ref, o_ref, lse_ref, cc_ref, dp_ref, do_ref, dg_ref, rows_ref):
        gv = g_ref[...]
        sg = _sigmoid(gv)
        dyv = dy_ref[...]
        ov = o_ref[...].astype(F32)
        do = dyv * (gv * sg)
        do_ref[...] = do.astype(_BF)
        dg_ref[...] = (dyv * ov * _dsilu(gv, sg)).astype(_BF)
        prod = do * ov
        lane = lax.broadcasted_iota(jnp.int32, (t, 128), 1)
        for hd in range(hg):
            delta = jnp.sum(prod[:, hd * HEAD:(hd + 1) * HEAD], axis=-1, keepdims=True)
            a = cc_ref[hd] - lse_ref[hd]
            mat = jnp.where(lane == 0, a, jnp.where(lane == 1, delta, 0.0))
            rows_ref[hd] = mat.T[0:8, :]

    tile = pl.BlockSpec((t, wd), lambda g, i: (i, g))
    col = pl.BlockSpec((hg, t, 1), lambda g, i: (g, i, 0))
    return pl.pallas_call(
        body, grid=(N_FOX // hg, S // t),
        in_specs=[pl.BlockSpec((t, wd), lambda g, i: (i, CONV_W // wd + g)),
                  pl.BlockSpec((t, wd), lambda g, i: (i, F_FG // wd + g)), tile, col, col,
                  pl.BlockSpec(memory_space=pl.ANY)],
        out_specs=[tile, pl.BlockSpec((t, wd), lambda g, i: (i, F_FG // wd + g)),
                   pl.BlockSpec((hg, 8, t), lambda g, i: (g, 0, i))],
        out_shape=[SDS((S, FOX_W), _BF), SDS(dp.shape, dp.dtype), SDS((N_FOX, 8, S), F32)],
        input_output_aliases={5: 1},
        compiler_params=_cp("parallel", "parallel"), name="fox_bwd_prep")(dy, p_f, o, lse, c_col, dp)


def _fox_bwd(p_b, do, rows, c_col):
    S = p_b.shape[0]
    t = min(T_ATT, S)
    nk = S // t
    qb, kb, vb = B_Q // HEAD, B_K // HEAD, B_V // HEAD

    def body(k_ref, v_ref, q_ref, do_ref, rows_ref, cc_ref, dq_ref, dk_ref, dv_ref, dr_ref,
             dqt_ref, dka_ref, dva_ref, dca_ref, dra_ref, sa_ref, pa_ref, sb_ref, pb_ref, ccb_ref):
        j = pl.program_id(1)

        @pl.when(j == 0)
        def _():
            dqt_ref[...] = jnp.zeros_like(dqt_ref)
            dra_ref[...] = jnp.zeros_like(dra_ref)

        k = k_ref[...]
        v = v_ref[...]
        kt = k.astype(F32).T.astype(_BF)
        ccb_ref[...] = jnp.broadcast_to(cc_ref[...], (t, t))
        dka_ref[...] = jnp.zeros_like(dka_ref)
        dva_ref[...] = jnp.zeros_like(dva_ref)
        dca_ref[...] = jnp.zeros_like(dca_ref)

        def scores(i, s_ref, p_ref):
            off = pl.multiple_of(i * t, t)
            s_ref[...] = _dot(k, q_ref[pl.ds(off, t), :], NT) * SCALE + (rows_ref[0:1, pl.ds(off, t)] - ccb_ref[...])
            p_ref[...] = _dot(v, do_ref[pl.ds(off, t), :], NT) - rows_ref[1:2, pl.ds(off, t)]

        def absorb(i, s_ref, p_ref, masked):
            off = pl.multiple_of(i * t, t)
            st = s_ref[...]
            if masked:
                srow = lax.broadcasted_iota(jnp.int32, (t, t), 0)
                tcol = lax.broadcasted_iota(jnp.int32, (t, t), 1)
                st = jnp.where(srow <= tcol, st, NEG)
            pt = jnp.exp(st)
            dva_ref[...] += _dot(pt.astype(_BF), do_ref[pl.ds(off, t), :], NN)
            dst = pt * p_ref[...]
            part = dst[:, 0:128]
            for gidx in range(1, t // 128):
                part = part + dst[:, gidx * 128:(gidx + 1) * 128]
            dca_ref[...] += part
            dra_ref[0:1, pl.ds(off, t)] += jnp.sum(dst, axis=0, keepdims=True)
            dsb = dst.astype(_BF)
            dka_ref[...] += _dot(dsb, q_ref[pl.ds(off, t), :], NN)
            dqt_ref[:, pl.ds(off, t)] += _dot(kt, dsb, NN)

        rest = nk - 1 - j
        scores(j, sa_ref, pa_ref)

        @pl.when(rest == 0)
        def _():
            absorb(j, sa_ref, pa_ref, True)

        @pl.when(rest > 0)
        def _():
            scores(j + 1, sb_ref, pb_ref)
            absorb(j, sa_ref, pa_ref, True)

            def pair(pi, carry):
                c = j + 1 + 2 * pi
                scores(c + 1, sa_ref, pa_ref)
                absorb(c, sb_ref, pb_ref, False)
                scores(c + 2, sb_ref, pb_ref)
                absorb(c + 1, sa_ref, pa_ref, False)
                return carry

            lax.fori_loop(0, (rest - 1) // 2, pair, 0)

            @pl.when(rest % 2 == 1)
            def _():
                absorb(nk - 1, sb_ref, pb_ref, False)

            @pl.when(rest % 2 == 0)
            def _():
                scores(nk - 1, sa_ref, pa_ref)
                absorb(nk - 2, sb_ref, pb_ref, False)
                absorb(nk - 1, sa_ref, pa_ref, False)

        dk_ref[...] = (dka_ref[...] * SCALE).astype(_BF)
        dv_ref[...] = dva_ref[...].astype(_BF)
        dra_ref[1:2, pl.ds(pl.multiple_of(j * t, t), t)] = -jnp.sum(dca_ref[...].T, axis=0, keepdims=True)

        @pl.when(j == nk - 1)
        def _():
            dr_ref[...] = dra_ref[...]
            for ci in range(nk):
                dq_ref[ci * t:(ci + 1) * t, :] = (dqt_ref[:, ci * t:(ci + 1) * t].T * SCALE).astype(_BF)

    tile = pl.BlockSpec((t, HEAD), lambda h, j: (j, h))
    return pl.pallas_call(
        body, grid=(N_FOX, nk),
        in_specs=[pl.BlockSpec((t, HEAD), lambda h, j: (j, kb + h)),
                  pl.BlockSpec((t, HEAD), lambda h, j: (j, vb + h)),
                  pl.BlockSpec((S, HEAD), lambda h, j: (0, qb + h)),
                  pl.BlockSpec((S, HEAD), lambda h, j: (0, h)),
                  pl.BlockSpec((None, 8, S), lambda h, j: (h, 0, 0)),
                  pl.BlockSpec((None, t, 1), lambda h, j: (h, j, 0))],
        out_specs=[pl.BlockSpec((S, HEAD), lambda h, j: (0, h)), tile, tile,
                   pl.BlockSpec((None, 8, S), lambda h, j: (h, 0, 0))],
        out_shape=[SDS((S, FOX_W), _BF), SDS((S, FOX_W), _BF), SDS((S, FOX_W), _BF), SDS((N_FOX, 8, S), F32)],
        scratch_shapes=[pltpu.VMEM((HEAD, S), F32), pltpu.VMEM((t, HEAD), F32), pltpu.VMEM((t, HEAD), F32),
                        pltpu.VMEM((t, 128), F32), pltpu.VMEM((8, S), F32)] + [pltpu.VMEM((t, t), F32)] * 5,
        compiler_params=_cp("parallel", "arbitrary"), name="fox_bwd")(p_b, p_b, p_b, do, rows, c_col)


CHUNK = 32


SUBLANES = 8


def _shifted_windows(win):
    n = win.shape[0]
    return [win] + [pltpu.roll(win, n - s, axis=0) for s in range(1, SUBLANES)]


def _tap(rot, f):
    return rot[f % SUBLANES][f - f % SUBLANES:f - f % SUBLANES + CHUNK, :]


def _conv_taps(ext_ref, w_ref, first, out_fn, n_rows):
    def chunk(c, carry):
        r0 = pl.multiple_of(c * CHUNK, CHUNK)
        rot = _shifted_windows(ext_ref[pl.ds(r0, 2 * CHUNK), :])
        acc = jnp.zeros((CHUNK, CONV_W), F32)
        for k in range(CONV_K):
            acc = acc + w_ref[k:k + 1, :] * _tap(rot, first(k))
        out_fn(r0, acc)
        return carry

    lax.fori_loop(0, n_rows // CHUNK, chunk, 0)


def _conv_fwd(p_f, conv_w, conv_b, ln_g, ln_b, w_pw):
    S = p_f.shape[0]
    tc = min(T_CONV, S)
    hb = tc // HALO

    def body(a_ref, b_ref, gc_ref, ap_ref, bp_ref, w_ref, cb_ref, lg_ref, lb_ref, pw_ref, y_ref, u1_ref, ext_ref):
        i = pl.program_id(0)
        prev = ap_ref[...] * _sigmoid(bp_ref[...])
        ext_ref[0:HALO, :] = jnp.where(i > 0, prev, 0.0)
        ext_ref[HALO:HALO + tc, :] = a_ref[...] * _sigmoid(b_ref[...])
        cb = cb_ref[...]

        def put(r0, acc):
            u1_ref[pl.ds(r0, CHUNK), :] = acc + cb

        _conv_taps(ext_ref, w_ref, lambda k: HALO - (CONV_K - 1) + k, put, tc)
        u1 = u1_ref[...]
        mu = jnp.mean(u1, axis=-1, keepdims=True)
        d = u1 - mu
        rstd = lax.rsqrt(jnp.mean(d * d, axis=-1, keepdims=True) + EPS)
        u2 = d * rstd * lg_ref[...] + lb_ref[...]
        u3 = u2 * _sigmoid(u2)
        pw = _dot(u3.astype(_BF), pw_ref[...], NN)
        gc = gc_ref[...]
        y_ref[...] = (pw * (gc * _sigmoid(gc))).astype(_BF)

    blk = lambda cb_: pl.BlockSpec((tc, CONV_W), lambda i: (i, cb_))
    halo = lambda cb_: pl.BlockSpec((HALO, CONV_W), lambda i: (jnp.maximum(i * hb - 1, 0), cb_))
    vec = pl.BlockSpec((1, CONV_W), lambda i: (0, 0))
    return pl.pallas_call(
        body, grid=(S // tc,),
        in_specs=[blk(0), blk(1), blk(2), halo(0), halo(1), pl.BlockSpec((CONV_K, CONV_W), lambda i: (0, 0)),
                  vec, vec, vec, pl.BlockSpec((CONV_W, CONV_W), lambda i: (0, 0))],
        out_specs=[pl.BlockSpec((tc, CONV_W), lambda i: (i, 0)), pl.BlockSpec((tc, CONV_W), lambda i: (i, 0))],
        out_shape=[SDS((S, CONV_W + FOX_W + MEM_W), _BF), SDS((S, CONV_W), F32)],
        scratch_shapes=[pltpu.VMEM((tc + 2 * HALO, CONV_W), F32)],
        compiler_params=_cp("parallel"), name="conv_fwd")(p_f, p_f, p_f, p_f, p_f, conv_w, conv_b, ln_g, ln_b, w_pw)


def _conv_bwd1(u1, p_f, dy, ln_g, ln_b, w_pw):
    S = u1.shape[0]
    tc = min(T_CONV, S)

    def body(u1_ref, gc_ref, dy_ref, lg_ref, lb_ref, pw_ref, du1_ref, dgc_ref, gpw_ref, glg_ref, glb_ref, gcb_ref):
        i = pl.program_id(0)
        u1v = u1_ref[...]
        mu = jnp.mean(u1v, axis=-1, keepdims=True)
        d = u1v - mu
        rstd = lax.rsqrt(jnp.mean(d * d, axis=-1, keepdims=True) + EPS)
        xh = d * rstd
        lg = lg_ref[...]
        u2 = xh * lg + lb_ref[...]
        sg2 = _sigmoid(u2)
        u3b = (u2 * sg2).astype(_BF)
        w = pw_ref[...]
        pw = _dot(u3b, w, NN)
        gc = gc_ref[...]
        sgc = _sigmoid(gc)
        dyv = dy_ref[...]
        dpw = (dyv * (gc * sgc)).astype(_BF)
        dgc_ref[...] = (dyv * pw * _dsilu(gc, sgc)).astype(_BF)
        gpw = _dot(u3b, dpw, TN)
        du2 = _dot(dpw, w, NT) * _dsilu(u2, sg2)
        glg = jnp.sum(du2 * xh, axis=0, keepdims=True)
        glb = jnp.sum(du2, axis=0, keepdims=True)
        dxh = du2 * lg
        du1 = rstd * (dxh - jnp.mean(dxh, axis=-1, keepdims=True) - xh * jnp.mean(dxh * xh, axis=-1, keepdims=True))
        du1_ref[...] = du1
        gcb = jnp.sum(du1, axis=0, keepdims=True)

        @pl.when(i == 0)
        def _():
            gpw_ref[...] = gpw
            glg_ref[...] = glg
            glb_ref[...] = glb
            gcb_ref[...] = gcb

        @pl.when(i > 0)
        def _():
            gpw_ref[...] += gpw
            glg_ref[...] += glg
            glb_ref[...] += glb
            gcb_ref[...] += gcb

    row = pl.BlockSpec((tc, CONV_W), lambda i: (i, 0))
    vec = pl.BlockSpec((1, CONV_W), lambda i: (0, 0))
    sq = pl.BlockSpec((CONV_W, CONV_W), lambda i: (0, 0))
    return pl.pallas_call(
        body, grid=(S // tc,),
        in_specs=[row, pl.BlockSpec((tc, CONV_W), lambda i: (i, F_GC // CONV_W)), row, vec, vec, sq],
        out_specs=[row, pl.BlockSpec((tc, CONV_W), lambda i: (i, F_GC // CONV_W)), sq, vec, vec, vec],
        out_shape=[SDS((S, CONV_W), F32), SDS((S, 2 * PART_W), _BF), SDS((CONV_W, CONV_W), F32),
                   SDS((1, CONV_W), F32), SDS((1, CONV_W), F32), SDS((1, CONV_W), F32)],
        compiler_params=_cp("arbitrary"), name="conv_bwd1")(u1, p_f, dy, ln_g, ln_b, w_pw)


def _conv_bwd2(du1, p_f, conv_w, dp):
    S = du1.shape[0]
    tc = min(T_CONV, S)
    hb = tc // HALO
    nblk = S // tc
    last_halo = S // HALO - 1

    def body(d_ref, dn_ref, a_ref, b_ref, ap_ref, bp_ref, w_ref, dp_ref, dab_ref, gw_ref, ext_ref, dext_ref, du0_ref,
             gacc_ref):
        i = pl.program_id(0)
        av = a_ref[...]
        sb = _sigmoid(b_ref[...])
        prev = ap_ref[...] * _sigmoid(bp_ref[...])
        ext_ref[0:HALO, :] = jnp.where(i > 0, prev, 0.0)
        ext_ref[HALO:HALO + tc, :] = av * sb
        dext_ref[0:tc, :] = d_ref[...]
        dext_ref[tc:tc + HALO, :] = jnp.where(i < nblk - 1, dn_ref[...], 0.0)

        def put(r0, acc):
            du0_ref[pl.ds(r0, CHUNK), :] = acc

        _conv_taps(dext_ref, w_ref, lambda k: CONV_K - 1 - k, put, tc)
        du0 = du0_ref[...]
        dab_ref[:, 0:CONV_W] = (du0 * sb).astype(_BF)
        dab_ref[:, CONV_W:2 * CONV_W] = (du0 * av * sb * (1.0 - sb)).astype(_BF)

        gacc_ref[...] = jnp.zeros_like(gacc_ref)

        def chunk(c, carry):
            r0 = pl.multiple_of(c * CHUNK, CHUNK)
            rot = _shifted_windows(ext_ref[pl.ds(r0, 2 * CHUNK), :])
            dv = dext_ref[pl.ds(r0, CHUNK), :]
            for k in range(CONV_K):
                prod = dv * _tap(rot, HALO - (CONV_K - 1) + k)
                part = prod[0:SUBLANES]
                for r in range(SUBLANES, CHUNK, SUBLANES):
                    part = part + prod[r:r + SUBLANES]
                gacc_ref[k * SUBLANES:(k + 1) * SUBLANES, :] += part
            return carry

        lax.fori_loop(0, tc // CHUNK, chunk, 0)
        rows = [jnp.sum(gacc_ref[k * SUBLANES:(k + 1) * SUBLANES, :], axis=0, keepdims=True) for k in range(CONV_K)]
        rows.append(jnp.zeros((1, CONV_W), F32))
        gw = jnp.concatenate(rows, axis=0)

        @pl.when(i == 0)
        def _():
            gw_ref[...] = gw

        @pl.when(i > 0)
        def _():
            gw_ref[...] += gw

    row = pl.BlockSpec((tc, CONV_W), lambda i: (i, 0))
    blk = lambda cb_: pl.BlockSpec((tc, CONV_W), lambda i: (i, cb_))
    halo = lambda cb_: pl.BlockSpec((HALO, CONV_W), lambda i: (jnp.maximum(i * hb - 1, 0), cb_))
    nxt = pl.BlockSpec((HALO, CONV_W), lambda i: (jnp.minimum((i + 1) * hb, last_halo), 0))
    return pl.pallas_call(
        body, grid=(nblk,),
        in_specs=[row, nxt, blk(0), blk(1), halo(0), halo(1), pl.BlockSpec((CONV_K, CONV_W), lambda i: (0, 0)),
                  pl.BlockSpec(memory_space=pl.ANY)],
        out_specs=[pl.BlockSpec((tc, 2 * CONV_W), lambda i: (i, 0)), pl.BlockSpec((CONV_K + 1, CONV_W), lambda i: (0, 0))],
        out_shape=[SDS(dp.shape, dp.dtype), SDS((CONV_K + 1, CONV_W), F32)], input_output_aliases={7: 0},
        scratch_shapes=[pltpu.VMEM((tc + 2 * HALO, CONV_W), F32), pltpu.VMEM((tc + 2 * HALO, CONV_W), F32),
                        pltpu.VMEM((tc, CONV_W), F32), pltpu.VMEM((CONV_K * SUBLANES, CONV_W), F32)],
        compiler_params=_cp("arbitrary"), name="conv_bwd2")(du1, du1, p_f, p_f, p_f, p_f, conv_w, dp)


def _exchange(srcs, scatter, name):
    n = len(srcs)
    out_shape = [SDS((N_DEV,) + (s.shape[1:] if sc else s.shape), s.dtype) for s, sc in zip(srcs, scatter)]

    def body(*refs):
        src_refs, dst_refs = refs[:n], refs[n:2 * n]
        send_sems, recv_sems, local_sems = refs[2 * n:]
        x, y, c = lax.axis_index("x"), lax.axis_index("y"), lax.axis_index("c")
        me = 4 * x + 2 * y + c
        copies = []
        for a in range(n):
            for f in range(1, N_DEV):
                px = 1 - x if f & 4 else x
                py = 1 - y if f & 2 else y
                pc = 1 - c if f & 1 else c
                peer = 4 * px + 2 * py + pc
                src = src_refs[a].at[peer] if scatter[a] else src_refs[a]
                cp = pltpu.make_async_remote_copy(
                    src_ref=src, dst_ref=dst_refs[a].at[me], send_sem=send_sems.at[a, f - 1],
                    recv_sem=recv_sems.at[a, f - 1], device_id=(px, py, pc), device_id_type=MESH)
                cp.start()
                copies.append(cp)
            own = src_refs[a].at[me] if scatter[a] else src_refs[a]
            lc = pltpu.make_async_copy(own, dst_refs[a].at[me], local_sems.at[a])
            lc.start()
            copies.append(lc)
        for cp in copies:
            cp.wait()

    anyspec = pl.BlockSpec(memory_space=pl.ANY)
    return pl.pallas_call(
        body, in_specs=[anyspec] * n, out_specs=[anyspec] * n, out_shape=out_shape,
        scratch_shapes=[pltpu.SemaphoreType.DMA((n, N_DEV - 1)), pltpu.SemaphoreType.DMA((n, N_DEV - 1)),
                        pltpu.SemaphoreType.DMA((n,))],
        name=name)(*srcs)


def _flip_peer(f, x, y, c):
    return (1 - x if f & 4 else x, 1 - y if f & 2 else y, 1 - c if f & 1 else c)


def _scatter_start(srcs):
    n = len(srcs)
    lands = [lax.empty((N_DEV - 1,) + s.shape[1:], s.dtype) for s in srcs]

    def body(*refs):
        src_refs, land_refs = refs[:n], refs[n:2 * n]
        send_sems, recv_sems = refs[2 * n:3 * n], refs[3 * n:4 * n]
        token = refs[-1]
        x, y, c = lax.axis_index("x"), lax.axis_index("y"), lax.axis_index("c")
        for a in range(n):
            for f in range(1, N_DEV):
                px, py, pc = _flip_peer(f, x, y, c)
                pltpu.make_async_remote_copy(
                    src_ref=src_refs[a].at[4 * px + 2 * py + pc], dst_ref=land_refs[a].at[f - 1],
                    send_sem=send_sems[a], recv_sem=recv_sems[a], device_id=(px, py, pc), device_id_type=MESH).start()
        token[...] = jnp.zeros_like(token)

    hbm = pl.BlockSpec(memory_space=pltpu.HBM)
    sem = pl.BlockSpec(memory_space=pltpu.SEMAPHORE)
    bufs = [pltpu.with_memory_space_constraint(b, pltpu.HBM) for b in list(srcs) + lands]
    out = pl.pallas_call(
        body, name="scatter_start",
        out_shape=(*[pltpu.SemaphoreType.DMA(())] * (2 * n), *[pltpu.HBM(b.shape, b.dtype) for b in bufs],
                   SDS((8, 128), F32)),
        in_specs=[hbm] * (2 * n),
        out_specs=(*[sem] * (2 * n), *[hbm] * (2 * n), pl.BlockSpec(memory_space=pltpu.VMEM)),
        input_output_aliases={i: 2 * n + i for i in range(2 * n)},
        compiler_params=pltpu.CompilerParams(has_side_effects=pltpu.SideEffectType.DATAFLOW_SIDE_EFFECTING))(*bufs)
    return list(out[:2 * n]), list(out[2 * n:3 * n]), list(out[3 * n:4 * n]), out[-1]


def _scatter_wait(sems, srcs, lands, after):
    n = len(srcs)

    def body(*refs):
        src_refs, land_refs = refs[:n], refs[n:2 * n]
        send_sems, recv_sems = refs[2 * n:3 * n], refs[3 * n:4 * n]
        x, y, c = lax.axis_index("x"), lax.axis_index("y"), lax.axis_index("c")
        for a in range(n):
            seven = pltpu.make_async_remote_copy(
                src_ref=src_refs[a].at[pl.ds(0, N_DEV - 1)], dst_ref=land_refs[a], send_sem=send_sems[a],
                recv_sem=recv_sems[a], device_id=(x, y, c), device_id_type=MESH)
            seven.wait_send()
            seven.wait_recv()

    hbm = pl.BlockSpec(memory_space=pltpu.HBM)
    sem = pl.BlockSpec(memory_space=pltpu.SEMAPHORE)
    bufs = list(srcs) + list(lands)
    out = pl.pallas_call(
        body, name="scatter_wait", out_shape=tuple(pltpu.HBM(b.shape, b.dtype) for b in bufs),
        in_specs=[hbm] * (2 * n) + [sem] * (2 * n) + [pl.BlockSpec(memory_space=pl.ANY)],
        out_specs=tuple([hbm] * (2 * n)), input_output_aliases={i: i for i in range(2 * n)},
        compiler_params=pltpu.CompilerParams(has_side_effects=pltpu.SideEffectType.DATAFLOW_SIDE_EFFECTING))(
            *bufs, *sems, after)
    return list(out[:n]), list(out[n:])


def _gather_two_level(srcs, name):
    n = len(srcs)
    out_shape = [SDS((N_DEV,) + s.shape, s.dtype) for s in srcs]

    def body(*refs):
        src_refs, dst_refs = refs[:n], refs[n:2 * n]
        send_sems, recv_sems, local_sems = refs[2 * n:]
        x, y, c = lax.axis_index("x"), lax.axis_index("y"), lax.axis_index("c")
        sibling = (x, y, 1 - c)
        chips = [(1 - x, y), (x, 1 - y), (1 - x, 1 - y)]

        def slot(a, px, py, pc):
            return dst_refs[a].at[4 * px + 2 * py + pc]

        def copy(a, k, block, to, src=None):
            return pltpu.make_async_remote_copy(
                src_ref=slot(a, *block) if src is None else src, dst_ref=slot(a, *block),
                send_sem=send_sems.at[a, k], recv_sem=recv_sems.at[a, k], device_id=to, device_id_type=MESH)

        own, sends = [], []
        for a in range(n):
            mine = pltpu.make_async_copy(src_refs[a], slot(a, x, y, c), local_sems.at[a])
            mine.start()
            own.append(mine)
            first = [copy(a, 1 + j, (x, y, c), (*chip, c), src=src_refs[a]) for j, chip in enumerate(chips)]
            first.append(copy(a, 0, (x, y, c), sibling, src=src_refs[a]))
            for cp in first:
                cp.start()
            sends += first
        for a in range(n):
            for j, chip in enumerate(chips):
                copy(a, 1 + j, (*chip, c), (x, y, c)).wait_recv()
                fwd = copy(a, 4 + j, (*chip, c), sibling)
                fwd.start()
                sends.append(fwd)
        for a in range(n):
            copy(a, 0, (x, y, 1 - c), (x, y, c)).wait_recv()
            for j, chip in enumerate(chips):
                copy(a, 4 + j, (*chip, 1 - c), (x, y, c)).wait_recv()
        for cp in sends:
            cp.wait_send()
        for cp in own:
            cp.wait()

    anyspec = pl.BlockSpec(memory_space=pl.ANY)
    return pl.pallas_call(
        body, in_specs=[anyspec] * n, out_specs=[anyspec] * n, out_shape=out_shape,
        scratch_shapes=[pltpu.SemaphoreType.DMA((n, N_DEV - 1)), pltpu.SemaphoreType.DMA((n, N_DEV - 1)),
                        pltpu.SemaphoreType.DMA((n,))],
        name=name)(*srcs)


def _adamw(parts, w, m, v, name, tr=256, own=None):
    lead = w.ndim == 3
    R, C = w.shape[-2:]
    tr = tr if R % tr == 0 else R
    n_parts = parts.shape[0]
    first = [] if own is None else [own]

    def body(*refs):
        p_ref, w_ref, m_ref, v_ref, g_ref, d_ref, nm_ref, nv_ref = refs[len(first):]
        terms = [r[0] for r in refs[:len(first)]] + [p_ref[dev] for dev in range(n_parts)]
        g = terms[0].astype(F32)
        for term in terms[1:]:
            g = g + term.astype(F32)
        mn = ADAM_B1 * m_ref[...] + (1.0 - ADAM_B1) * g
        vn = ADAM_B2 * v_ref[...] + (1.0 - ADAM_B2) * (g * g)
        m_hat = mn / (1.0 - ADAM_B1 ** ADAM_STEP)
        v_hat = vn / (1.0 - ADAM_B2 ** ADAM_STEP)
        g_ref[...] = g
        d_ref[...] = -ADAM_LR * (m_hat / (jnp.sqrt(v_hat) + ADAM_EPS) + ADAM_WD * w_ref[...])
        nm_ref[...] = mn
        nv_ref[...] = vn

    blk = pl.BlockSpec((None, tr, C), lambda i: (0, i, 0)) if lead else pl.BlockSpec((tr, C), lambda i: (i, 0))
    return pl.pallas_call(
        body, grid=(R // tr,),
        in_specs=[pl.BlockSpec((1, tr, C), lambda i: (0, i, 0))] * len(first)
        + [pl.BlockSpec((n_parts, tr, C), lambda i: (0, i, 0)), blk, blk, blk],
        out_specs=[blk] * 4, out_shape=[SDS(w.shape, F32)] * 4,
        compiler_params=_cp("parallel"), name=name)(*first, parts, w, m, v)


def _adamw_transposed(parts, own, w_t, m_t, v_t, name, tr=256):
    n_parts, R, C = parts.shape
    tr = tr if R % tr == 0 else R

    def body(o_ref, p_ref, w_ref, m_ref, v_ref, g_ref, d_ref, nm_ref, nv_ref):
        r = lax.broadcasted_iota(jnp.int32, (tr, tr), 0)
        c = lax.broadcasted_iota(jnp.int32, (tr, tr), 1)
        eye = (r == c).astype(parts.dtype)
        g = _dot(o_ref[0], eye, TN)
        for dev in range(n_parts):
            g = g + _dot(p_ref[dev], eye, TN)
        mn = ADAM_B1 * m_ref[...] + (1.0 - ADAM_B1) * g
        vn = ADAM_B2 * v_ref[...] + (1.0 - ADAM_B2) * (g * g)
        m_hat = mn / (1.0 - ADAM_B1 ** ADAM_STEP)
        v_hat = vn / (1.0 - ADAM_B2 ** ADAM_STEP)
        g_ref[...] = g
        d_ref[...] = -ADAM_LR * (m_hat / (jnp.sqrt(v_hat) + ADAM_EPS) + ADAM_WD * w_ref[...])
        nm_ref[...] = mn
        nv_ref[...] = vn

    blk = pl.BlockSpec((C, tr), lambda i: (0, i))
    return pl.pallas_call(
        body, grid=(R // tr,),
        in_specs=[pl.BlockSpec((1, tr, C), lambda i: (0, i, 0)), pl.BlockSpec((n_parts, tr, C), lambda i: (0, i, 0)),
                  blk, blk, blk],
        out_specs=[blk] * 4, out_shape=[SDS((C, R), F32)] * 4,
        compiler_params=_cp("parallel"), name=name)(own, parts, w_t, m_t, v_t)


_SEGMENTS = ((O_A, O_Q, F_A), (O_Q, O_FL, PART_W + B_Q), (O_FL, O_FG, F_FL), (O_FG, O_MQ, F_FG),
             (O_MQ, O_MG, PART_W + B_MQ), (O_MG, D_IN, F_MG))


def _shards_to_aligned(shards):
    n, D, cols = shards.shape
    pieces = []
    for o0, o1, a0 in sorted(_SEGMENTS, key=lambda s: s[2]):
        for d in range(o0 // cols, (o1 - 1) // cols + 1):
            lo, hi = max(o0, d * cols), min(o1, (d + 1) * cols)
            pieces.append(shards[d, :, lo - d * cols:hi - d * cols])
        if a0 == F_FL:
            pieces.append(jnp.zeros((D, PART_W - F_FL - (o1 - o0)), shards.dtype))
    return jnp.concatenate(pieces, axis=1)


def _aligned_to_shards(g_all, cols):
    shards = []
    for d in range(N_DEV):
        pieces = []
        for o0, o1, a0 in _SEGMENTS:
            lo, hi = max(o0, d * cols), min(o1, (d + 1) * cols)
            if lo < hi:
                pieces.append(g_all[:, a0 + lo - o0:a0 + hi - o0])
        shards.append(jnp.concatenate(pieces, axis=1))
    return jnp.stack(shards)


def _pack_small(norm, mem_norm, final, conv_b, ln_g, ln_b, b_f, extra, D):
    width = max(D, PACK_W)
    row3 = jnp.concatenate([conv_b.reshape(-1), ln_g.reshape(-1), ln_b.reshape(-1), b_f.reshape(-1)])
    rows = [norm.reshape(-1), mem_norm.reshape(-1), final.reshape(-1), row3, extra.reshape(-1)]
    rows = [jnp.pad(r, (0, width - r.shape[0])) for r in rows]
    return jnp.concatenate([jnp.stack(rows), jnp.zeros((3, width), F32)], axis=0)


def _unpack_small(p, D):
    c = CONV_W
    return dict(norm_g=p[0:1, :D], mem_norm_g=p[1:2, :D], final_g=p[2, :D], conv_b=p[3:4, 0:c],
                conv_ln_g=p[3:4, c:2 * c], conv_ln_b=p[3:4, 2 * c:3 * c], b_f=p[3:4, 3 * c:3 * c + N_FOX])


def kernel(x, mem, norm_g, mem_norm_g, w_in, b_f, conv_w, conv_b, conv_ln_g, conv_ln_b, w_conv_pw, w_mem_kv, w_out, final_g, loss_target, m_norm_g, m_mem_norm_g, m_w_in, m_b_f, m_conv_w, m_conv_b, m_conv_ln_g, m_conv_ln_b, m_w_conv_pw, m_w_mem_kv, m_w_out, m_final_g, v_norm_g, v_mem_norm_g, v_w_in, v_b_f, v_conv_w, v_conv_b, v_conv_ln_g, v_conv_ln_b, v_w_conv_pw, v_w_mem_kv, v_w_out, v_final_g):
    S, D = x.shape[1], x.shape[2]
    M = mem.shape[1]
    xs, ms, tgt = x[0], mem[0], loss_target[0]
    cols = w_in.shape[2]

    g_in, g_cw, g_pw, g_kv, g_out = _gather_two_level(
        [w_in[0].astype(_BF), conv_w[0], w_conv_pw[0].astype(_BF), w_mem_kv[0].astype(_BF), w_out[0].astype(_BF)],
        "gather_weights")
    w_all = _shards_to_aligned(g_in)
    conv_w_full = jnp.transpose(g_cw, (1, 0, 2)).reshape(CONV_K, CONV_W)
    w_pw_full = g_pw.reshape(CONV_W, CONV_W)
    w_kv_full = g_kv.reshape(D, 2 * MEM_W)
    w_out_full = g_out.reshape(CONV_W + FOX_W + MEM_W, D)
    b_f_pad = jnp.pad(b_f, ((0, 0), (0, 128 - N_FOX)))

    h, r1 = _rms_fwd(xs, norm_g)
    p_f = _matmul(h, w_all, "nn", F32, "proj_f", b_cols=(0, PART_W))
    p_b = _matmul(h, w_all, "nn", _BF, "proj_b", b_cols=(PART_W, PART_W))
    mkv, mhat = _mem_kv_fwd(ms, mem_norm_g, w_kv_full)
    y, u1 = _conv_fwd(p_f, conv_w_full, conv_b, conv_ln_g, conv_ln_b, w_pw_full)
    c_col, c_row = _fox_cumsum(p_f, b_f_pad)
    o_fox, y, lse = _fox_fwd(p_b, p_f, c_col, c_row, y)
    y = _mem_attn_fwd(p_b, p_f, mkv, y)
    z = _matmul(y, w_out_full, "nn", F32, "out_proj")
    dx2, g_final, loss_part = _head_loss(xs, z, tgt, final_g.reshape(1, D))

    dy = _matmul(dx2, w_out_full, "nt", F32, "d_y")
    gw_out = _matmul(y, dx2, "tn", _BF, "gw_out")
    d_mq, d_mg, d_mk, d_mv = _mem_attn_bwd(p_b, p_f, mkv, dy)
    gw_kv, g_mem_norm = _mem_kv_bwd(jnp.concatenate([d_mk, d_mv], axis=1), mhat, mem_norm_g, w_kv_full)
    du1, dp, gw_pw, g_ln_g, g_ln_b, g_cb = _conv_bwd1(u1, p_f, dy, conv_ln_g, conv_ln_b, w_pw_full)
    dp, gw_cw = _conv_bwd2(du1, p_f, conv_w_full, dp)
    d_o, dp, rows = _fox_bwd_prep(dy, p_f, o_fox, lse, c_col, dp)
    d_q, d_k, d_v, dr = _fox_bwd(p_b, d_o, rows, c_col)
    dp, g_bf = _fox_dlogf(dr, p_f, b_f_pad, dp)
    for piece, col in ((d_mg, F_MG), (d_q, PART_W + B_Q), (d_k, PART_W + B_K), (d_v, PART_W + B_V),
                       (d_mq, PART_W + B_MQ)):
        dp = lax.dynamic_update_slice(dp, piece, (0, col))
    gw_all = _matmul(h, dp, "tn", _BF, "gw_in")

    send_in = _aligned_to_shards(gw_all, cols)
    send_cw = jnp.transpose(gw_cw[:CONV_K].reshape(CONV_K, N_DEV, CONV_W // N_DEV), (1, 0, 2))
    send_pw = gw_pw.astype(_BF).reshape(N_DEV, CONV_W // N_DEV, CONV_W)
    send_kv = gw_kv.reshape(N_DEV, D // N_DEV, 2 * MEM_W)
    send_out = gw_out.reshape(N_DEV, (CONV_W + FOX_W + MEM_W) // N_DEV, D)
    sems, sent, lands, token = _scatter_start([send_in, send_cw, send_pw, send_kv, send_out])
    dh = _matmul(dp, w_all, "nt", F32, "d_h", after=token)
    grad_x, g_norm = _rms_bwd(xs, r1, dh, dx2, norm_g)
    sent, lands = _scatter_wait(sems, sent, lands, grad_x)
    me = 4 * lax.axis_index("x") + 2 * lax.axis_index("y") + lax.axis_index("c")
    own = [lax.dynamic_index_in_dim(s_, me, 0, keepdims=True) for s_ in sent]

    small = _pack_small(g_norm, g_mem_norm, g_final, g_cb, g_ln_g, g_ln_b, g_bf[:, :N_FOX], loss_part[0, 0:1], D)
    r_small, = _exchange([small], [False], "exchange_small")

    res = {}
    w_in_t = _adamw_transposed(lands[0], own[0], jnp.transpose(w_in[0]), jnp.transpose(m_w_in[0]),
                               jnp.transpose(v_w_in[0]), "adamw_w_in")
    res["w_in"] = [jnp.transpose(a_)[None] for a_ in w_in_t]
    res["conv_w"] = _adamw(lands[1], conv_w, m_conv_w, v_conv_w, "adamw_conv_w", own=own[1])
    res["w_conv_pw"] = _adamw(lands[2], w_conv_pw, m_w_conv_pw, v_w_conv_pw, "adamw_w_pw", own=own[2])
    res["w_mem_kv"] = _adamw(lands[3], w_mem_kv, m_w_mem_kv, v_w_mem_kv, "adamw_w_kv", own=own[3])
    res["w_out"] = _adamw(lands[4], w_out, m_w_out, v_w_out, "adamw_w_out", own=own[4])
    zero = jnp.zeros((1,), F32)
    pk = lambda a, b_, c_, d_, e, f_, g_: _pack_small(a, b_, c_, d_, e, f_, g_, zero, D)
    sm = _adamw(r_small,
                pk(norm_g, mem_norm_g, final_g, conv_b, conv_ln_g, conv_ln_b, b_f),
                pk(m_norm_g, m_mem_norm_g, m_final_g, m_conv_b, m_conv_ln_g, m_conv_ln_b, m_b_f),
                pk(v_norm_g, v_mem_norm_g, v_final_g, v_conv_b, v_conv_ln_g, v_conv_ln_b, v_b_f), "adamw_small")
    loss = sm[0][4, 0]
    small_out = [_unpack_small(a, D) for a in sm]
    names = ["norm_g", "mem_norm_g", "w_in", "b_f", "conv_w", "conv_b", "conv_ln_g", "conv_ln_b", "w_conv_pw",
             "w_mem_kv", "w_out", "final_g"]
    outs = [loss, grad_x[None]]
    for kind in range(4):
        for nme in names:
            outs.append(res[nme][kind] if nme in res else small_out[kind][nme])
    return tuple(outs)
```
